```python
import math
import jax, jax.numpy as jnp
from jax import lax
import numpy as np

D_MODEL = 1024
BATCH = 16
SEQ = 4096
DEPTH = 2
DEC_BATCH = 32
DEC_SEQ = 16
PAST_LEN = 2048

CHUNK = 64
N_EVEN = (DEPTH + 1) // 2
N_ODD = DEPTH // 2
ALPHA = (2.0 * DEPTH) ** 0.25
BETA_INIT = (8.0 * DEPTH) ** -0.25
LN_EPS = 1e-5
RMS_EPS = 1e-5

D_FF = 2816

S5_WIDTH = D_MODEL // 2
S5_GROUP = 16
S5_GROUPS = S5_WIDTH // S5_GROUP
S5_STATE = 64
DT_MIN = 0.001
DT_MAX = 0.1
SB_HEADS = 8
SB_HEAD_DIM = 64
SB_WIDTH = SB_HEADS * SB_HEAD_DIM
Q_BLOCK = 128
MIX0_IN = S5_WIDTH + 3 * SB_WIDTH
MIX0_OUT = S5_WIDTH + SB_WIDTH

SSD_INNER = 2 * D_MODEL
SSD_HEAD_DIM = 64
SSD_HEADS = SSD_INNER // SSD_HEAD_DIM
SSD_GROUPS = 4
SSD_HPG = SSD_HEADS // SSD_GROUPS
SSD_STATE = 128
SSD_CONV = 4
SSD_GN = SSD_GROUPS * SSD_STATE
SSD_CONV_DIM = SSD_INNER + 2 * SSD_GN
SSD_IN = SSD_INNER + SSD_CONV_DIM + SSD_HEADS

kernel_name = "s5_stickbreak_ssd_streaming_encoder_step"


def layer_norm(x, g, b):
    xf = x.astype(jnp.float32)
    mu = jnp.mean(xf, axis=-1, keepdims=True)
    var = jnp.mean(jnp.square(xf - mu), axis=-1, keepdims=True)
    return ((xf - mu) * lax.rsqrt(var + LN_EPS) * g + b).astype(x.dtype)


def swiglu(x, w_gate, w_up, w_down):
    return (jax.nn.silu(x @ w_gate) * (x @ w_up)) @ w_down


def _linear_combine(e1, e2):
    a1, b1 = e1
    a2, b2 = e2
    return a1 * a2, a2 * b1 + b2


def s5_discretize(a_re, a_im, log_dt, b_re, b_im, c_re, c_im):
    a = lax.complex(a_re.astype(jnp.float32), a_im.astype(jnp.float32))
    dt = jnp.exp(log_dt.astype(jnp.float32))[:, None]
    a_bar = jnp.exp(a * dt)
    b = lax.complex(b_re.astype(jnp.float32), b_im.astype(jnp.float32))
    b_bar = ((a_bar - 1.0) / a)[..., None] * b
    c_mat = lax.complex(c_re.astype(jnp.float32), c_im.astype(jnp.float32))
    return a_bar, b_bar, c_mat


def s5_recurrence(u, h0, a_bar, b_bar, c_mat):
    bsz, length = u.shape[0], u.shape[1]
    c = min(CHUNK, length)
    n = length // c
    u_chunks = jnp.moveaxis(u.reshape(bsz, n, c, S5_GROUPS, S5_GROUP), 1, 0)

    def step(h, u_c):
        bu = jnp.einsum('bcgp,gnp->bcgn', u_c.astype(jnp.complex64), b_bar)
        bu = bu.at[:, 0].add(a_bar * h)
        a_c = jnp.broadcast_to(a_bar, bu.shape)
        _, hs = lax.associative_scan(_linear_combine, (a_c, bu), axis=1)
        y = jnp.einsum('bcgn,gpn->bcgp', hs, c_mat).real
        return hs[:, -1], y

    h_last, ys = lax.scan(step, h0, u_chunks)
    return jnp.moveaxis(ys, 0, 1).reshape(bsz, length, S5_GROUPS, S5_GROUP), h_last


def sb_block(q_blk, q_pos, k, v):
    z = jnp.einsum('bqhd,bkhd->bhqk', q_blk, k).astype(jnp.float32) * (SB_HEAD_DIM ** -0.5)
    k_pos = jnp.arange(k.shape[1])
    mask = k_pos[None, :] < q_pos[:, None]
    sp = jnp.where(mask, jax.nn.softplus(z), 0.0)
    suffix = lax.cumsum(sp, axis=3, reverse=True) - sp
    w = jnp.where(mask, jnp.exp(jax.nn.log_sigmoid(z) - suffix), 0.0)
    return jnp.einsum('bhqk,bkhd->bqhd', w, v)


def stick_breaking(q, k, v, q_offset):
    bsz, lq = q.shape[0], q.shape[1]
    qb = min(Q_BLOCK, lq)
    nb = lq // qb
    q_blocks = jnp.moveaxis(q.reshape(bsz, nb, qb, SB_HEADS, SB_HEAD_DIM), 1, 0)
    pos = (q_offset + jnp.arange(lq)).reshape(nb, qb)
    out = lax.map(lambda args: sb_block(args[0], args[1], k, v), (q_blocks, pos))
    return jnp.moveaxis(out, 0, 1).reshape(bsz, lq, SB_WIDTH)


def mixer_ab(x, w_in, a_re, a_im, log_dt, b_re, b_im, c_re, c_im, d_skip, w_glu, b_glu, w_out,
             h0, k_cache, v_cache):
    bsz, length, _ = x.shape
    proj = x @ w_in
    u = proj[..., :S5_WIDTH].astype(jnp.float32)
    q = proj[..., S5_WIDTH:S5_WIDTH + SB_WIDTH].reshape(bsz, length, SB_HEADS, SB_HEAD_DIM)
    k = proj[..., S5_WIDTH + SB_WIDTH:S5_WIDTH + 2 * SB_WIDTH].reshape(bsz, length, SB_HEADS, SB_HEAD_DIM)
    v = proj[..., S5_WIDTH + 2 * SB_WIDTH:].reshape(bsz, length, SB_HEADS, SB_HEAD_DIM)
    a_bar, b_bar, c_mat = s5_discretize(a_re, a_im, log_dt, b_re, b_im, c_re, c_im)
    if h0 is None:
        h0 = jnp.zeros((bsz, S5_GROUPS, S5_STATE), jnp.complex64)
    ys, h_last = s5_recurrence(u.reshape(bsz, length, S5_GROUPS, S5_GROUP), h0, a_bar, b_bar, c_mat)
    y = ys.reshape(bsz, length, S5_WIDTH) + d_skip * u
    g = jax.nn.gelu(y)
    s5_out = g * jax.nn.sigmoid(g @ w_glu + b_glu)
    if k_cache is None:
        k_all, v_all, offset = k, v, 0
    else:
        k_all = jnp.concatenate([k_cache.astype(k.dtype), k], axis=1)
        v_all = jnp.concatenate([v_cache.astype(v.dtype), v], axis=1)
        offset = k_cache.shape[1]
    sb_out = stick_breaking(q, k_all, v_all, offset)
    out = jnp.concatenate([s5_out.astype(x.dtype), sb_out.astype(x.dtype)], axis=-1) @ w_out
    return out, h_last.real, h_last.imag, k, v


def ssd_recurrence(xs, dt, a, bm, cm, h0):
    bsz, length = xs.shape[0], xs.shape[1]
    c = min(CHUNK, length)
    n = length // c
    split = lambda t: jnp.moveaxis(t.reshape((bsz, n, c) + t.shape[2:]), 1, 0)
    causal = jnp.tril(jnp.ones((c, c), dtype=bool))[None, :, :, None, None]

    def step(h, inp):
        x_c, dt_c, b_c, c_c = inp
        cs = jnp.cumsum(dt_c * a, axis=1)
        seg = cs[:, :, None] - cs[:, None]
        decay = jnp.exp(jnp.where(causal, seg, -jnp.inf))
        cb = jnp.einsum('btgn,bsgn->btsg', c_c, b_c)
        w = cb[..., None] * decay * dt_c[:, None]
        y = jnp.einsum('btsgr,bsgrp->btgrp', w, x_c)
        y = y + jnp.einsum('btgn,bgrpn->btgrp', c_c, h) * jnp.exp(cs)[..., None]
        to_end = jnp.exp(cs[:, -1:] - cs) * dt_c
        h_new = h * jnp.exp(cs[:, -1])[..., None, None] + jnp.einsum('bsgr,bsgn,bsgrp->bgrpn', to_end, b_c, x_c)
        return h_new, y

    h_last, ys = lax.scan(step, h0, (split(xs), split(dt), split(bm), split(cm)))
    return jnp.moveaxis(ys, 0, 1).reshape(xs.shape), h_last


def mixer_ssd(x, w_in, conv_w, conv_b, dt_bias, a_log, d_skip, norm_g, w_out, conv_state, ssm_state):
    bsz, length, _ = x.shape
    proj = x @ w_in
    z = proj[..., :SSD_INNER]
    xbc = proj[..., SSD_INNER:SSD_INNER + SSD_CONV_DIM]
    dt_raw = proj[..., SSD_INNER + SSD_CONV_DIM:]
    if conv_state is None:
        conv_state = jnp.zeros((bsz, SSD_CONV - 1, SSD_CONV_DIM), xbc.dtype)
    ext = jnp.concatenate([conv_state.astype(xbc.dtype), xbc], axis=1)
    new_conv = ext[:, length:]
    conv = conv_b + sum(ext[:, w:w + length] * conv_w[w] for w in range(SSD_CONV))
    xbc = jax.nn.silu(conv)
    xs = xbc[..., :SSD_INNER].reshape(bsz, length, SSD_GROUPS, SSD_HPG, SSD_HEAD_DIM)
    bm = xbc[..., SSD_INNER:SSD_INNER + SSD_GN].reshape(bsz, length, SSD_GROUPS, SSD_STATE)
    cm = xbc[..., SSD_INNER + SSD_GN:].reshape(bsz, length, SSD_GROUPS, SSD_STATE)
    dt = jax.nn.softplus(dt_raw.astype(jnp.float32) + dt_bias).reshape(bsz, length, SSD_GROUPS, SSD_HPG)
    a = -jnp.exp(a_log.astype(jnp.float32)).reshape(SSD_GROUPS, SSD_HPG)
    if ssm_state is None:
        h0 = jnp.zeros((bsz, SSD_GROUPS, SSD_HPG, SSD_HEAD_DIM, SSD_STATE), jnp.float32)
    else:
        h0 = ssm_state.astype(jnp.float32).reshape(bsz, SSD_GROUPS, SSD_HPG, SSD_HEAD_DIM, SSD_STATE)
    y, h_last = ssd_recurrence(xs, dt, a, bm, cm, h0)
    y = y + d_skip.reshape(SSD_GROUPS, SSD_HPG)[..., None] * xs
    gw = SSD_HPG * SSD_HEAD_DIM
    yg = y.reshape(bsz, length, SSD_GROUPS, gw) * jax.nn.silu(z.astype(jnp.float32)).reshape(bsz, length, SSD_GROUPS, gw)
    yg = yg * lax.rsqrt(jnp.mean(jnp.square(yg), axis=-1, keepdims=True) + RMS_EPS) * norm_g.reshape(SSD_GROUPS, gw)
    out = yg.reshape(bsz, length, SSD_INNER).astype(x.dtype) @ w_out
    return out, h_last.reshape(bsz, SSD_HEADS, SSD_HEAD_DIM, SSD_STATE), new_conv


def setup_inputs(seed: int = 0) -> dict:
    key = jax.random.key(seed)
    ks = iter(jax.random.split(key, 48))
    nrm = lambda shape, scale=1.0: scale * jax.random.normal(next(ks), shape, jnp.float32)
    uni = lambda shape, lo, hi: jax.random.uniform(next(ks), shape, jnp.float32, minval=lo, maxval=hi)
    x_prompt = nrm((BATCH, SEQ, D_MODEL))
    x_sample = nrm((DEC_BATCH, DEC_SEQ, D_MODEL))
    state_s5_re = nrm((N_EVEN, DEC_BATCH, S5_GROUPS, S5_STATE), 0.1)
    state_s5_im = nrm((N_EVEN, DEC_BATCH, S5_GROUPS, S5_STATE), 0.1)
    cache_sb_k = nrm((N_EVEN, DEC_BATCH, PAST_LEN, SB_HEADS, SB_HEAD_DIM))
    cache_sb_v = nrm((N_EVEN, DEC_BATCH, PAST_LEN, SB_HEADS, SB_HEAD_DIM))
    state_ssd = nrm((N_ODD, DEC_BATCH, SSD_HEADS, SSD_HEAD_DIM, SSD_STATE), 0.1)
    state_conv = nrm((N_ODD, DEC_BATCH, SSD_CONV - 1, SSD_CONV_DIM))
    ln_g = 1.0 + nrm((DEPTH, 3, D_MODEL), 0.01)
    ln_b = nrm((DEPTH, 3, D_MODEL), 0.01)
    ffn_w_gate = nrm((DEPTH, 2, D_MODEL, D_FF), D_MODEL ** -0.5)
    ffn_w_up = nrm((DEPTH, 2, D_MODEL, D_FF), D_MODEL ** -0.5)
    ffn_w_down = nrm((DEPTH, 2, D_FF, D_MODEL), BETA_INIT * D_FF ** -0.5)
    mix0_w_in = nrm((N_EVEN, D_MODEL, MIX0_IN), D_MODEL ** -0.5)
    s5_a_re = -0.5 + nrm((N_EVEN, S5_GROUPS, S5_STATE), 0.01)
    s5_a_im = jnp.pi * jnp.arange(S5_STATE, dtype=jnp.float32) + nrm((N_EVEN, S5_GROUPS, S5_STATE), 0.01)
    s5_log_dt = uni((N_EVEN, S5_GROUPS), math.log(DT_MIN), math.log(DT_MAX))
    s5_b_re = nrm((N_EVEN, S5_GROUPS, S5_STATE, S5_GROUP), (2.0 * S5_GROUP) ** -0.5)
    s5_b_im = nrm((N_EVEN, S5_GROUPS, S5_STATE, S5_GROUP), (2.0 * S5_GROUP) ** -0.5)
    s5_c_re = nrm((N_EVEN, S5_GROUPS, S5_GROUP, S5_STATE), S5_STATE ** -0.5)
    s5_c_im = nrm((N_EVEN, S5_GROUPS, S5_GROUP, S5_STATE), S5_STATE ** -0.5)
    s5_d = nrm((N_EVEN, S5_WIDTH))
    s5_w_glu = nrm((N_EVEN, S5_WIDTH, S5_WIDTH), S5_WIDTH ** -0.5)
    s5_b_glu = nrm((N_EVEN, S5_WIDTH), 0.01)
    mix0_w_out = nrm((N_EVEN, MIX0_OUT, D_MODEL), BETA_INIT * MIX0_OUT ** -0.5)
    ssd_w_in = nrm((N_ODD, D_MODEL, SSD_IN), D_MODEL ** -0.5)
    ssd_conv_w = nrm((N_ODD, SSD_CONV, SSD_CONV_DIM), SSD_CONV ** -0.5)
    ssd_conv_b = nrm((N_ODD, SSD_CONV_DIM), 0.01)
    dt0 = jnp.exp(uni((N_ODD, SSD_HEADS), math.log(DT_MIN), math.log(DT_MAX)))
    ssd_dt_bias = dt0 + jnp.log(-jnp.expm1(-dt0))
    ssd_a_log = jnp.log(uni((N_ODD, SSD_HEADS), 1.0, 16.0))
    ssd_d = 1.0 + nrm((N_ODD, SSD_HEADS), 0.01)
    ssd_norm_g = 1.0 + nrm((N_ODD, SSD_INNER), 0.01)
    ssd_w_out = nrm((N_ODD, SSD_INNER, D_MODEL), BETA_INIT * SSD_INNER ** -0.5)
    return {"x_prompt": x_prompt, "x_sample": x_sample,
            "state_s5_re": state_s5_re, "state_s5_im": state_s5_im,
            "cache_sb_k": cache_sb_k, "cache_sb_v": cache_sb_v,
            "state_ssd": state_ssd, "state_conv": state_conv,
            "ln_g": ln_g, "ln_b": ln_b,
            "ffn_w_gate": ffn_w_gate, "ffn_w_up": ffn_w_up, "ffn_w_down": ffn_w_down,
            "mix0_w_in": mix0_w_in, "s5_a_re": s5_a_re, "s5_a_im": s5_a_im, "s5_log_dt": s5_log_dt,
            "s5_b_re": s5_b_re, "s5_b_im": s5_b_im, "s5_c_re": s5_c_re, "s5_c_im": s5_c_im,
            "s5_d": s5_d, "s5_w_glu": s5_w_glu, "s5_b_glu": s5_b_glu, "mix0_w_out": mix0_w_out,
            "ssd_w_in": ssd_w_in, "ssd_conv_w": ssd_conv_w, "ssd_conv_b": ssd_conv_b,
            "ssd_dt_bias": ssd_dt_bias, "ssd_a_log": ssd_a_log, "ssd_d": ssd_d,
            "ssd_norm_g": ssd_norm_g, "ssd_w_out": ssd_w_out}


def reference(x_prompt, x_sample, state_s5_re, state_s5_im, cache_sb_k, cache_sb_v, state_ssd, state_conv,
              ln_g, ln_b, ffn_w_gate, ffn_w_up, ffn_w_down,
              mix0_w_in, s5_a_re, s5_a_im, s5_log_dt, s5_b_re, s5_b_im, s5_c_re, s5_c_im,
              s5_d, s5_w_glu, s5_b_glu, mix0_w_out,
              ssd_w_in, ssd_conv_w, ssd_conv_b, ssd_dt_bias, ssd_a_log, ssd_d, ssd_norm_g, ssd_w_out):

    def trunk(x, states):
        s5_re_l, s5_im_l, k_l, v_l, ssd_l, conv_l = [], [], [], [], [], []
        for l in range(DEPTH):
            x = layer_norm(ALPHA * x + 0.5 * swiglu(x, ffn_w_gate[l, 0], ffn_w_up[l, 0], ffn_w_down[l, 0]),
                           ln_g[l, 0], ln_b[l, 0])
            i = l // 2
            if l % 2 == 0:
                if states is None:
                    h0, kc, vc = None, None, None
                else:
                    h0 = lax.complex(states[0][i].astype(jnp.float32), states[1][i].astype(jnp.float32))
                    kc, vc = states[2][i], states[3][i]
                m, h_re, h_im, k_new, v_new = mixer_ab(
                    x, mix0_w_in[i], s5_a_re[i], s5_a_im[i], s5_log_dt[i], s5_b_re[i], s5_b_im[i],
                    s5_c_re[i], s5_c_im[i], s5_d[i], s5_w_glu[i], s5_b_glu[i], mix0_w_out[i], h0, kc, vc)
                s5_re_l.append(h_re)
                s5_im_l.append(h_im)
                k_l.append(k_new)
                v_l.append(v_new)
            else:
                if states is None:
                    cs, ss = None, None
                else:
                    cs, ss = states[5][i], states[4][i]
                m, h_ssd, c_new = mixer_ssd(
                    x, ssd_w_in[i], ssd_conv_w[i], ssd_conv_b[i], ssd_dt_bias[i], ssd_a_log[i], ssd_d[i],
                    ssd_norm_g[i], ssd_w_out[i], cs, ss)
                ssd_l.append(h_ssd)
                conv_l.append(c_new)
            x = layer_norm(ALPHA * x + m, ln_g[l, 1], ln_b[l, 1])
            x = layer_norm(ALPHA * x + 0.5 * swiglu(x, ffn_w_gate[l, 1], ffn_w_up[l, 1], ffn_w_down[l, 1]),
                           ln_g[l, 2], ln_b[l, 2])
        return (x, jnp.stack(s5_re_l), jnp.stack(s5_im_l), jnp.stack(k_l), jnp.stack(v_l),
                jnp.stack(ssd_l), jnp.stack(conv_l))

    y_prompt, s5_re_p, s5_im_p, k_p, v_p, ssd_p, conv_p = trunk(x_prompt, None)
    y_sample, s5_re_s, s5_im_s, k_s, v_s, ssd_s, conv_s = trunk(
        x_sample, (state_s5_re, state_s5_im, cache_sb_k, cache_sb_v, state_ssd, state_conv))
    return (y_prompt, y_sample, s5_re_p, s5_im_p, k_p, v_p, ssd_p, conv_p,
            s5_re_s, s5_im_s, k_s, v_s, ssd_s, conv_s)
```

```python
import functools
import math

import jax
import jax.numpy as jnp
from jax import lax
from jax.experimental import pallas as pl
from jax.experimental.pallas import tpu as pltpu

F32 = jnp.float32
BF16 = jnp.bfloat16

D_MODEL = 1024
DEPTH = 2
ALPHA = (2.0 * DEPTH) ** 0.25
LN_EPS = 1e-5
RMS_EPS = 1e-5
D_FF = 2816

S5_WIDTH = 512
S5_GROUP = 16
S5_GROUPS = 32
S5_STATE = 64
S5_CHUNK = 16
S5_ROW = S5_CHUNK * S5_GROUP
SB_HEADS = 8
SB_HEAD_DIM = 64
SB_WIDTH = 512
MIX0_IN = 2048

SSD_INNER = 2048
SSD_HEAD_DIM = 64
SSD_HEADS = 32
SSD_GROUPS = 4
SSD_STATE = 128
SSD_CONV = 4
SSD_GN = 512
SSD_CONV_DIM = 3072
SSD_PAIRS = SSD_HEADS // 2
CONV_PAD = 8

VMEM_LIMIT = 56 * 1024 * 1024


def _cparams(sem):
    return pltpu.CompilerParams(dimension_semantics=sem, vmem_limit_bytes=VMEM_LIMIT)


def _sigmoid(x):
    return 1.0 / (1.0 + jnp.exp(-x))


def _softplus(x):
    return jnp.maximum(x, 0.0) + jnp.log1p(jnp.exp(-jnp.abs(x)))


def _layer_norm(y, g, b):
    mu = jnp.mean(y, axis=-1, keepdims=True)
    d = y - mu
    var = jnp.mean(d * d, axis=-1, keepdims=True)
    return d * lax.rsqrt(var + LN_EPS) * g + b


def _dot(a, b):
    return jnp.dot(a, b, preferred_element_type=F32)


def _dot_nt(a, b):
    return lax.dot_general(a, b, (((1,), (1,)), ((), ())), preferred_element_type=F32)


def _dot_tn(a, b):
    return lax.dot_general(a, b, (((0,), (0,)), ((), ())), preferred_element_type=F32)


def _split3(x):
    hi = x.astype(BF16)
    r = x - hi.astype(F32)
    mid = r.astype(BF16)
    lo = (r - mid.astype(F32)).astype(BF16)
    return hi, mid, lo


def _sum_right(x, ones_mat):
    hi, mid, lo = _split3(x)
    return _dot(hi, ones_mat) + _dot(mid, ones_mat) + _dot(lo, ones_mat)


def _sum_left(ones_mat, x):
    hi, mid, lo = _split3(x)
    return _dot(ones_mat, hi) + _dot(ones_mat, mid) + _dot(ones_mat, lo)


def _row_tile(t):
    for tm in (512, 256, 128, 64, 32, 16, 8):
        if t % tm == 0:
            return tm
    raise ValueError(f"token count {t} is not a multiple of 8")


def _full(shape):
    return pl.BlockSpec(shape, lambda *_: (0,) * len(shape), pipeline_mode=pl.Buffered(1))


FFN_TF = 1408


def _ffn_kernel(x_ref, wg_ref, wu_ref, wd_ref, g_ref, b_ref, o_ref):
    x = x_ref[...]
    xb = x.astype(BF16)
    acc = None
    for c in range(D_FF // FFN_TF):
        sl = slice(c * FFN_TF, (c + 1) * FFN_TF)
        gate = _dot(xb, wg_ref[:, sl])
        up = _dot(xb, wu_ref[:, sl])
        h = (gate * _sigmoid(gate) * up).astype(BF16)
        part = _dot(h, wd_ref[sl, :])
        acc = part if acc is None else acc + part
    o_ref[...] = _layer_norm(ALPHA * x + 0.5 * acc, g_ref[...], b_ref[...])


def ffn_ln(x, wg, wu, wd, g, b):
    t = x.shape[0]
    tm = _row_tile(t)
    return pl.pallas_call(
        _ffn_kernel,
        grid=(t // tm,),
        in_specs=[pl.BlockSpec((tm, D_MODEL), lambda i: (i, 0)),
                  _full((D_MODEL, D_FF)), _full((D_MODEL, D_FF)), _full((D_FF, D_MODEL)),
                  _full((1, D_MODEL)), _full((1, D_MODEL))],
        out_specs=pl.BlockSpec((tm, D_MODEL), lambda i: (i, 0)),
        out_shape=jax.ShapeDtypeStruct((t, D_MODEL), F32),
        compiler_params=_cparams(("parallel",)),
        name="ffn_ln",
    )(x, wg, wu, wd, g.reshape(1, D_MODEL), b.reshape(1, D_MODEL))


def _proj_kernel(x_ref, w_ref, *o_refs, plan):
    xb = x_ref[...].astype(BF16)
    refs = iter(o_refs)
    for lo, hi, outs in plan:
        y = _dot(xb, w_ref[:, lo:hi])
        for _, scale in outs:
            o_ref = next(refs)
            o_ref[...] = (y if scale == 1.0 else y * scale).astype(o_ref.dtype)


def proj(x, w, plan):
    t = x.shape[0]
    tm = _row_tile(t)
    n = w.shape[1]
    flat = [(hi - lo, dt) for lo, hi, outs in plan for dt, _ in outs]
    return pl.pallas_call(
        functools.partial(_proj_kernel, plan=plan),
        grid=(t // tm,),
        in_specs=[pl.BlockSpec((tm, D_MODEL), lambda i: (i, 0)), _full((D_MODEL, n))],
        out_specs=[pl.BlockSpec((tm, width), lambda i: (i, 0)) for width, _ in flat],
        out_shape=[jax.ShapeDtypeStruct((t, width), dt) for width, dt in flat],
        compiler_params=_cparams(("parallel",)),
        name="proj",
    )(x, w)


def _s5_prep_kernel(a_re_ref, a_im_ref, ldt_ref, bt1_ref, bt2_ref, cq1_ref, cq2_ref, cm_ref,
                    m_ref, p_ref, psw_ref, qt_ref, a1_ref, a2_ref, ppad_ref):
    ar, ai = a_re_ref[...], a_im_ref[...]
    dt = jnp.exp(ldt_ref[...])

    def cpow(k):
        mag = jnp.exp(k * (ar * dt))
        ang = k * (ai * dt)
        return mag * jnp.cos(ang), mag * jnp.sin(ang)

    abr, abi = cpow(1.0)
    nr, ni = abr - 1.0, abi
    den = ar * ar + ai * ai
    fr = (nr * ar + ni * ai) / den
    fi = (ni * ar - nr * ai) / den
    bt1, bt2 = bt1_ref[...], bt2_ref[...]
    bb = fr * bt1 + fi * bt2
    bbsw = fr * bt2 - fi * bt1
    group_shift = S5_GROUP.bit_length() - 1
    step = jnp.right_shift(lax.broadcasted_iota(jnp.int32, (S5_ROW, 1), 0), group_shift).astype(F32)
    pwr, pwi = cpow(float(S5_CHUNK - 1) - step)
    p = pwr * bb + pwi * bbsw
    p_ref[...] = p.astype(BF16)
    psw_ref[...] = pltpu.roll(p, S5_STATE, 1).astype(BF16)
    qwr, qwi = cpow(step + 1.0)
    qt_ref[...] = (qwr * cq1_ref[...] + qwi * cq2_ref[...]).astype(BF16)
    a16r, a16i = cpow(float(S5_CHUNK))
    lane = lax.broadcasted_iota(jnp.int32, (1, 2 * S5_STATE), 1)
    a1_ref[...] = a16r
    a2_ref[...] = jnp.where(lane < S5_STATE, -a16i, a16i)

    ppad_ref[0:S5_ROW, :] = p
    ppad_ref[S5_ROW:2 * S5_ROW, :] = jnp.zeros((S5_ROW, 2 * S5_STATE), F32)
    cm = cm_ref[...]
    col_step = jnp.right_shift(lax.broadcasted_iota(jnp.int32, (2 * S5_STATE, S5_ROW), 1), group_shift)
    acc = jnp.zeros((S5_ROW, S5_ROW), F32)
    for t in range(S5_CHUNK):
        off = S5_GROUP * (S5_CHUNK - 1 - t)
        w_t = ppad_ref[off:off + S5_ROW, :].astype(BF16)
        c_t = jnp.where(col_step == t, cm, 0.0).astype(BF16)
        acc = acc + _dot(w_t, c_t)
    m_ref[...] = acc.astype(BF16)


def s5_prep(a_re, a_im, log_dt, b_re, b_im, c_re, c_im):
    g, n, r = S5_GROUPS, S5_STATE, S5_ROW
    dup = lambda v: jnp.concatenate([v, v], axis=-1).reshape(g, 1, 2 * n)
    ldt = jnp.broadcast_to(log_dt.reshape(g, 1, 1), (g, 1, 2 * n))
    btr = jnp.tile(jnp.swapaxes(b_re, 1, 2), (1, S5_CHUNK, 1))
    bti = jnp.tile(jnp.swapaxes(b_im, 1, 2), (1, S5_CHUNK, 1))
    bt1 = jnp.concatenate([btr, bti], axis=-1)
    bt2 = jnp.concatenate([-bti, btr], axis=-1)
    cqr = jnp.tile(c_re, (1, S5_CHUNK, 1))
    cqi = jnp.tile(c_im, (1, S5_CHUNK, 1))
    cq1 = jnp.concatenate([cqr, -cqi], axis=-1)
    cq2 = jnp.concatenate([-cqi, -cqr], axis=-1)
    cm = jnp.concatenate([jnp.tile(jnp.swapaxes(c_re, 1, 2), (1, 1, S5_CHUNK)),
                          jnp.tile(-jnp.swapaxes(c_im, 1, 2), (1, 1, S5_CHUNK))], axis=1)
    grp = lambda *shape: pl.BlockSpec((None,) + shape, lambda i: (i, 0, 0))
    return pl.pallas_call(
        _s5_prep_kernel,
        grid=(g,),
        in_specs=[grp(1, 2 * n), grp(1, 2 * n), grp(1, 2 * n), grp(r, 2 * n), grp(r, 2 * n),
                  grp(r, 2 * n), grp(r, 2 * n), grp(2 * n, r)],
        out_specs=[grp(r, r), grp(r, 2 * n), grp(r, 2 * n), grp(r, 2 * n), grp(1, 2 * n), grp(1, 2 * n)],
        out_shape=[jax.ShapeDtypeStruct((g, r, r), BF16), jax.ShapeDtypeStruct((g, r, 2 * n), BF16),
                   jax.ShapeDtypeStruct((g, r, 2 * n), BF16), jax.ShapeDtypeStruct((g, r, 2 * n), BF16),
                   jax.ShapeDtypeStruct((g, 1, 2 * n), F32), jax.ShapeDtypeStruct((g, 1, 2 * n), F32)],
        scratch_shapes=[pltpu.VMEM((2 * r, 2 * n), F32)],
        compiler_params=_cparams(("parallel",)),
        name="s5_prep",
    )(dup(a_re), dup(a_im), ldt, bt1, bt2, cq1, cq2, cm)


def _s5_kernel(u_ref, h0_ref, h0sw_ref, m_ref, p_ref, psw_ref, qt_ref, a1_ref, a2_ref,
               y_ref, hl_ref, s_ref, ssw_ref, hin_ref, *, bn, nj):
    u = u_ref[...]
    s_ref[...] = _dot(u, p_ref[...])
    ssw_ref[...] = _dot(u, psw_ref[...])
    a1, a2 = a1_ref[...], a2_ref[...]

    def body(j, carry):
        h, hsw = carry
        rows = pl.ds(pl.multiple_of(j * bn, bn), bn)
        hin_ref[rows, :] = h
        return a1 * h + a2 * hsw + s_ref[rows, :], a1 * hsw - a2 * h + ssw_ref[rows, :]

    h, _ = lax.fori_loop(0, nj, body, (h0_ref[...], h0sw_ref[...]))
    hl_ref[...] = h
    y_ref[...] = _dot(u, m_ref[...]) + _dot_nt(hin_ref[...].astype(BF16), qt_ref[...])


def s5_scan(u_g, h0, h0sw, mats, bn, nj):
    m, p, psw, qt, a1, a2 = mats
    g, n, r = S5_GROUPS, S5_STATE, S5_ROW
    rows = bn * nj
    grp = lambda *shape: pl.BlockSpec((None,) + shape, lambda i: (i, 0, 0))
    return pl.pallas_call(
        functools.partial(_s5_kernel, bn=bn, nj=nj),
        grid=(g,),
        in_specs=[grp(rows, r), grp(bn, 2 * n), grp(bn, 2 * n), grp(r, r), grp(r, 2 * n), grp(r, 2 * n),
                  grp(r, 2 * n), grp(1, 2 * n), grp(1, 2 * n)],
        out_specs=[grp(rows, r), grp(bn, 2 * n)],
        out_shape=[jax.ShapeDtypeStruct((g, rows, r), F32), jax.ShapeDtypeStruct((g, bn, 2 * n), F32)],
        scratch_shapes=[pltpu.VMEM((rows, 2 * n), F32)] * 3,
        compiler_params=_cparams(("parallel",)),
        name="s5_scan",
    )(u_g, h0, h0sw, m, p, psw, qt, a1, a2)


SB_TK = 256


def _sb_kernel(q_ref, kd_ref, vd_ref, kp_ref, vp_ref, ud_ref, up_ref, o_ref, *, tq, n_past):
    qi = pl.program_id(2)
    q = q_ref[...]
    kd = kd_ref[...].astype(BF16)
    vd = vd_ref[...].astype(BF16)
    lane_lo = lax.broadcasted_iota(jnp.int32, (tq, 2 * SB_HEAD_DIM), 1) < SB_HEAD_DIM
    row = lax.broadcasted_iota(jnp.int32, (tq, tq), 0)
    col = lax.broadcasted_iota(jnp.int32, (tq, tq), 1)
    earlier = col < row
    blocks = n_past(qi)
    outs = []
    for half in range(2):
        qm = jnp.where(lane_lo if half == 0 else jnp.logical_not(lane_lo), q, jnp.zeros_like(q))
        z = _dot_nt(qm, kd)
        sp = jnp.where(earlier, _softplus(z), 0.0)
        suffix = _sum_right(sp, ud_ref[...])
        w = jnp.where(earlier, jnp.exp(z - sp - suffix), 0.0)
        acc0 = _dot(w.astype(BF16), vd)
        carry0 = jnp.sum(sp, axis=-1, keepdims=True)

        def body(i, c, qm=qm):
            acc, carry = c
            rows = pl.ds(pl.multiple_of((blocks - 1 - i) * SB_TK, SB_TK), SB_TK)
            k = kp_ref[rows, :].astype(BF16)
            v = vp_ref[rows, :].astype(BF16)
            z = _dot_nt(qm, k)
            sp = _softplus(z)
            suffix = _sum_right(sp, up_ref[...])
            w = jnp.exp(z - sp - suffix - carry)
            return acc + _dot(w.astype(BF16), v), carry + jnp.sum(sp, axis=-1, keepdims=True)

        acc, _ = lax.fori_loop(0, blocks, body, (acc0, carry0))
        outs.append(acc)
    o_ref[...] = jnp.where(lane_lo, outs[0], outs[1])


def _strict_lower_ones(n):
    r = lax.broadcasted_iota(jnp.int32, (n, n), 0)
    c = lax.broadcasted_iota(jnp.int32, (n, n), 1)
    return (r > c).astype(BF16)


def stick_breaking(q, k_new, v_new, k_past, v_past, bn, lq, tq, past_len):
    lp = k_past.shape[1]
    nq = lq // tq
    lanes = 2 * SB_HEAD_DIM
    if past_len is None:
        assert tq % SB_TK == 0
        n_past = lambda qi: qi * (tq // SB_TK)
    else:
        assert past_len == lp and lp % SB_TK == 0
        n_past = lambda qi: lp // SB_TK
    return pl.pallas_call(
        functools.partial(_sb_kernel, tq=tq, n_past=n_past),
        grid=(bn, SB_HEADS // 2, nq),
        in_specs=[pl.BlockSpec((None, tq, lanes), lambda b, h, i: (b, i, h)),
                  pl.BlockSpec((None, tq, lanes), lambda b, h, i: (b, i, h)),
                  pl.BlockSpec((None, tq, lanes), lambda b, h, i: (b, i, h)),
                  pl.BlockSpec((None, lp, lanes), lambda b, h, i: (b, 0, h)),
                  pl.BlockSpec((None, lp, lanes), lambda b, h, i: (b, 0, h)),
                  _full((tq, tq)), _full((SB_TK, SB_TK))],
        out_specs=pl.BlockSpec((None, tq, lanes), lambda b, h, i: (b, i, h)),
        out_shape=jax.ShapeDtypeStruct((bn, lq, SB_WIDTH), F32),
        compiler_params=_cparams(("parallel", "parallel", "arbitrary")),
        name="stick_breaking",
    )(q, k_new, v_new, k_past, v_past, _strict_lower_ones(tq), _strict_lower_ones(SB_TK))


def _gelu(x):
    return 0.5 * x * (1.0 + jnp.tanh(math.sqrt(2.0 / math.pi) * (x + 0.044715 * (x * x * x))))


def _mix0_out_kernel(x_ref, ys_ref, u_ref, sb_ref, d_ref, wglu_ref, bglu_ref, wo1_ref, wo2_ref, g_ref, b_ref, o_ref):
    x = x_ref[...]
    gl = _gelu(ys_ref[...] + d_ref[...] * u_ref[...])
    gate = _sigmoid(_dot(gl.astype(BF16), wglu_ref[...]) + bglu_ref[...])
    m = _dot((gl * gate).astype(BF16), wo1_ref[...]) + _dot(sb_ref[...].astype(BF16), wo2_ref[...])
    o_ref[...] = _layer_norm(ALPHA * x + m, g_ref[...], b_ref[...])


def mix0_out(x, ys, u, sb, d, wglu, bglu, wo1, wo2, g, b):
    t = x.shape[0]
    tm = _row_tile(t)
    rowblk = lambda n: pl.BlockSpec((tm, n), lambda i: (i, 0))
    return pl.pallas_call(
        _mix0_out_kernel,
        grid=(t // tm,),
        in_specs=[rowblk(D_MODEL), rowblk(S5_WIDTH), rowblk(S5_WIDTH), rowblk(SB_WIDTH),
                  _full((1, S5_WIDTH)), _full((S5_WIDTH, S5_WIDTH)), _full((1, S5_WIDTH)),
                  _full((S5_WIDTH, D_MODEL)), _full((SB_WIDTH, D_MODEL)),
                  _full((1, D_MODEL)), _full((1, D_MODEL))],
        out_specs=rowblk(D_MODEL),
        out_shape=jax.ShapeDtypeStruct((t, D_MODEL), F32),
        compiler_params=_cparams(("parallel",)),
        name="mix0_out",
    )(x, ys, u, sb, d.reshape(1, -1), wglu, bglu.reshape(1, -1), wo1, wo2, g.reshape(1, -1), b.reshape(1, -1))


def _ssd_kernel(xbc_ref, z_ref, dt_ref, dtt_ref, conv0_ref, h0_ref, cw_ref, cb_ref, dtb_ref, dtbt_ref,
                a_ref, at_ref, dsk_ref, ng_ref, tril_ref, triu_ref,
                y_ref, hl_ref, ext_ref, h_ref, *, lc, nc):
    c = pl.program_id(1)

    @pl.when(c == 0)
    def _():
        ext_ref[0:CONV_PAD, :] = conv0_ref[...]
        h_ref[...] = h0_ref[...]

    ext_ref[CONV_PAD:CONV_PAD + lc, :] = xbc_ref[...]
    conv = cb_ref[...]
    first = CONV_PAD - (SSD_CONV - 1)
    for w in range(SSD_CONV):
        conv = conv + ext_ref[first + w:first + w + lc, :] * cw_ref[w:w + 1, :]
    ext_ref[0:CONV_PAD, :] = ext_ref[lc:lc + CONV_PAD, :]
    act = conv * _sigmoid(conv)

    dt = _softplus(dt_ref[...] + dtb_ref[...])
    dtt = _softplus(dtt_ref[...] + dtbt_ref[...])
    cs = _sum_left(tril_ref[...], dt * a_ref[...])
    cst = _sum_right(dtt * at_ref[...], triu_ref[...])
    cs_last = cs[lc - 1:lc, :]
    ecs = jnp.exp(cs)
    to_end = jnp.exp(cs_last - cs) * dt
    e_last = jnp.exp(cs_last)

    row = lax.broadcasted_iota(jnp.int32, (lc, lc), 0)
    col = lax.broadcasted_iota(jnp.int32, (lc, lc), 1)
    causal = col <= row
    lanes = 2 * SSD_HEAD_DIM
    lane_lo = lax.broadcasted_iota(jnp.int32, (lc, lanes), 1) < SSD_HEAD_DIM
    row_lo = lax.broadcasted_iota(jnp.int32, (lanes, lanes), 0) < SSD_HEAD_DIM
    pairs_per_group = SSD_PAIRS // SSD_GROUPS

    for g in range(SSD_GROUPS):
        bm = act[:, SSD_INNER + g * SSD_STATE:SSD_INNER + (g + 1) * SSD_STATE].astype(BF16)
        cm = act[:, SSD_INNER + SSD_GN + g * SSD_STATE:SSD_INNER + SSD_GN + (g + 1) * SSD_STATE].astype(BF16)
        cb = _dot_nt(cm, bm)
        gated, sq = [], jnp.zeros((lc, 1), F32)
        for k in range(g * pairs_per_group, (g + 1) * pairs_per_group):
            sl = slice(k * lanes, (k + 1) * lanes)
            xp = act[:, sl]
            xpb = xp.astype(BF16)
            ys = []
            for h in (2 * k, 2 * k + 1):
                seg = cs[:, h:h + 1] - cst[h:h + 1, :]
                decay = jnp.exp(jnp.where(causal, seg, -jnp.inf))
                w = cb * decay * dtt[h:h + 1, :]
                ys.append(_dot(w.astype(BF16), xpb))
            y = jnp.where(lane_lo, ys[0], ys[1])
            ha, hb = 2 * k, 2 * k + 1
            hp = h_ref[k]
            y = y + _dot_nt(cm, hp.astype(BF16)) * jnp.where(lane_lo, ecs[:, ha:ha + 1], ecs[:, hb:hb + 1])
            xw = (xp * jnp.where(lane_lo, to_end[:, ha:ha + 1], to_end[:, hb:hb + 1])).astype(BF16)
            h_ref[k] = hp * jnp.where(row_lo, e_last[:, ha:ha + 1], e_last[:, hb:hb + 1]) + _dot_tn(xw, bm)
            y = y + dsk_ref[:, sl] * xp
            zz = z_ref[:, sl]
            yg = y * (zz * _sigmoid(zz))
            sq = sq + jnp.sum(yg * yg, axis=-1, keepdims=True)
            gated.append(yg)
        scale = lax.rsqrt(sq * (1.0 / (pairs_per_group * lanes)) + RMS_EPS)
        for k, yg in zip(range(g * pairs_per_group, (g + 1) * pairs_per_group), gated):
            sl = slice(k * lanes, (k + 1) * lanes)
            y_ref[:, sl] = yg * scale * ng_ref[:, sl]

    @pl.when(c == nc - 1)
    def _():
        hl_ref[...] = h_ref[...]


def ssd_mix(xbc, z, dt_raw, conv0, h0, conv_w, conv_b, dt_bias, a_log, d_skip, norm_g, bn, length, lc):
    nc = length // lc
    t = bn * length
    lanes = 2 * SSD_HEAD_DIM
    dtt = jnp.swapaxes(dt_raw.reshape(bn * nc, lc, SSD_HEADS), 1, 2)
    conv0p = jnp.pad(conv0, ((0, 0), (CONV_PAD - (SSD_CONV - 1), 0), (0, 0)))
    a = -jnp.exp(a_log.astype(F32))
    r = lax.broadcasted_iota(jnp.int32, (lc, lc), 0)
    cc = lax.broadcasted_iota(jnp.int32, (lc, lc), 1)
    tril = (cc <= r).astype(BF16)
    triu = (r <= cc).astype(BF16)
    tok = lambda n: pl.BlockSpec((lc, n), lambda b, c: (b * nc + c, 0))
    y, hl = pl.pallas_call(
        functools.partial(_ssd_kernel, lc=lc, nc=nc),
        grid=(bn, nc),
        in_specs=[tok(SSD_CONV_DIM), tok(SSD_INNER), tok(SSD_HEADS),
                  pl.BlockSpec((None, SSD_HEADS, lc), lambda b, c: (b * nc + c, 0, 0)),
                  pl.BlockSpec((None, CONV_PAD, SSD_CONV_DIM), lambda b, c: (b, 0, 0)),
                  pl.BlockSpec((None, SSD_PAIRS, lanes, SSD_STATE), lambda b, c: (b, 0, 0, 0)),
                  _full((SSD_CONV, SSD_CONV_DIM)), _full((1, SSD_CONV_DIM)),
                  _full((1, SSD_HEADS)), _full((SSD_HEADS, 1)), _full((1, SSD_HEADS)), _full((SSD_HEADS, 1)),
                  _full((1, SSD_INNER)), _full((1, SSD_INNER)), _full((lc, lc)), _full((lc, lc))],
        out_specs=[tok(SSD_INNER),
                   pl.BlockSpec((None, SSD_PAIRS, lanes, SSD_STATE), lambda b, c: (b, 0, 0, 0))],
        out_shape=[jax.ShapeDtypeStruct((t, SSD_INNER), F32),
                   jax.ShapeDtypeStruct((bn, SSD_PAIRS, lanes, SSD_STATE), F32)],
        scratch_shapes=[pltpu.VMEM((CONV_PAD + lc, SSD_CONV_DIM), F32),
                        pltpu.VMEM((SSD_PAIRS, lanes, SSD_STATE), F32)],
        compiler_params=_cparams(("parallel", "arbitrary")),
        name="ssd_mix",
    )(xbc, z, dt_raw, dtt, conv0p, h0.reshape(bn, SSD_PAIRS, lanes, SSD_STATE),
      conv_w, conv_b.reshape(1, -1), dt_bias.reshape(1, -1), dt_bias.reshape(-1, 1),
      a.reshape(1, -1), a.reshape(-1, 1), jnp.repeat(d_skip, SSD_HEAD_DIM).reshape(1, -1),
      norm_g.reshape(1, -1), tril, triu)
    return y, hl.reshape(bn, SSD_HEADS, SSD_HEAD_DIM, SSD_STATE)


def _out_ln_kernel(x_ref, a_ref, w_ref, g_ref, b_ref, o_ref):
    x = x_ref[...]
    m = _dot(a_ref[...].astype(BF16), w_ref[...])
    o_ref[...] = _layer_norm(ALPHA * x + m, g_ref[...], b_ref[...])


def out_ln(x, a, w, g, b):
    t = x.shape[0]
    tm = _row_tile(t)
    k = a.shape[1]
    return pl.pallas_call(
        _out_ln_kernel,
        grid=(t // tm,),
        in_specs=[pl.BlockSpec((tm, D_MODEL), lambda i: (i, 0)), pl.BlockSpec((tm, k), lambda i: (i, 0)),
                  _full((k, D_MODEL)), _full((1, D_MODEL)), _full((1, D_MODEL))],
        out_specs=pl.BlockSpec((tm, D_MODEL), lambda i: (i, 0)),
        out_shape=jax.ShapeDtypeStruct((t, D_MODEL), F32),
        compiler_params=_cparams(("parallel",)),
        name="out_ln",
    )(x, a, w, g.reshape(1, -1), b.reshape(1, -1))


def _to_groups(u, bn, nj):
    v = u.reshape(bn, nj, S5_CHUNK, S5_GROUPS, S5_GROUP)
    return jnp.transpose(v, (3, 1, 0, 2, 4)).reshape(S5_GROUPS, nj * bn, S5_ROW)


def _from_groups(y, bn, nj):
    v = y.reshape(S5_GROUPS, nj, bn, S5_CHUNK, S5_GROUP)
    return jnp.transpose(v, (2, 1, 3, 0, 4)).reshape(bn * nj * S5_CHUNK, S5_WIDTH)


def _trunk(x, bn, length, states, w):
    t = bn * length
    x = x.reshape(t, D_MODEL)
    x = ffn_ln(x, w["wg"][0][0], w["wu"][0][0], w["wd"][0][0], w["ln_g"][0, 0], w["ln_b"][0, 0])

    sw = S5_WIDTH
    both = ((F32, 1.0), (BF16, 1.0))
    u, u_bf, q_bf, k, k_bf, v, v_bf = proj(x, w["mix0_in"], (
        (0, sw, both), (sw, 2 * sw, ((BF16, SB_HEAD_DIM ** -0.5),)),
        (2 * sw, 3 * sw, both), (3 * sw, 4 * sw, both)))
    nj = length // S5_CHUNK
    if states is None:
        h0 = jnp.zeros((S5_GROUPS, bn, 2 * S5_STATE), F32)
        h0sw = h0
    else:
        re = jnp.swapaxes(states[0][0].astype(F32), 0, 1)
        im = jnp.swapaxes(states[1][0].astype(F32), 0, 1)
        h0 = jnp.concatenate([re, im], axis=-1)
        h0sw = jnp.concatenate([im, re], axis=-1)
    y_g, h_last = s5_scan(_to_groups(u_bf, bn, nj), h0, h0sw, w["s5_mats"], bn, nj)
    y_s5 = _from_groups(y_g, bn, nj)
    s5_re = jnp.swapaxes(h_last[..., :S5_STATE], 0, 1)[None]
    s5_im = jnp.swapaxes(h_last[..., S5_STATE:], 0, 1)[None]

    q3 = q_bf.reshape(bn, length, SB_WIDTH)
    if states is None:
        tq = min(SB_TK, length)
        sb = stick_breaking(q3, k_bf.reshape(bn, length, SB_WIDTH), v_bf.reshape(bn, length, SB_WIDTH),
                            k_bf.reshape(bn, length, SB_WIDTH), v_bf.reshape(bn, length, SB_WIDTH),
                            bn, length, tq, None)
    else:
        past = states[2][0].shape[1]
        sb = stick_breaking(q3, k_bf.reshape(bn, length, SB_WIDTH), v_bf.reshape(bn, length, SB_WIDTH),
                            states[2][0].reshape(bn, past, SB_WIDTH), states[3][0].reshape(bn, past, SB_WIDTH),
                            bn, length, length, past)
    x = mix0_out(x, y_s5, u, sb.reshape(t, SB_WIDTH), w["s5_d"], w["s5_wglu"], w["s5_bglu"],
                 w["mix0_out"][:S5_WIDTH], w["mix0_out"][S5_WIDTH:], w["ln_g"][0, 1], w["ln_b"][0, 1])
    x = ffn_ln(x, w["wg"][0][1], w["wu"][0][1], w["wd"][0][1], w["ln_g"][0, 2], w["ln_b"][0, 2])

    x = ffn_ln(x, w["wg"][1][0], w["wu"][1][0], w["wd"][1][0], w["ln_g"][1, 0], w["ln_b"][1, 0])
    one = ((F32, 1.0),)
    z, xbc, dt_raw = proj(x, w["ssd_in"], (
        (0, SSD_INNER, one), (SSD_INNER, SSD_INNER + SSD_CONV_DIM, one),
        (SSD_INNER + SSD_CONV_DIM, SSD_INNER + SSD_CONV_DIM + SSD_HEADS, one)))
    if states is None:
        conv0 = jnp.zeros((bn, SSD_CONV - 1, SSD_CONV_DIM), F32)
        hs0 = jnp.zeros((bn, SSD_HEADS, SSD_HEAD_DIM, SSD_STATE), F32)
    else:
        conv0 = states[5][0].astype(F32)
        hs0 = states[4][0].astype(F32)
    lc = min(128, length)
    yg, h_ssd = ssd_mix(xbc, z, dt_raw, conv0, hs0, w["ssd_conv_w"], w["ssd_conv_b"], w["ssd_dt_bias"],
                        w["ssd_a_log"], w["ssd_d"], w["ssd_norm_g"], bn, length, lc)
    ext = jnp.concatenate([conv0, xbc.reshape(bn, length, SSD_CONV_DIM)[:, length - (SSD_CONV - 1):]], axis=1)
    new_conv = ext[:, ext.shape[1] - (SSD_CONV - 1):]
    x = out_ln(x, yg, w["ssd_out"], w["ln_g"][1, 1], w["ln_b"][1, 1])
    x = ffn_ln(x, w["wg"][1][1], w["wu"][1][1], w["wd"][1][1], w["ln_g"][1, 2], w["ln_b"][1, 2])

    return (x.reshape(bn, length, D_MODEL), s5_re, s5_im,
            k.reshape(1, bn, length, SB_HEADS, SB_HEAD_DIM), v.reshape(1, bn, length, SB_HEADS, SB_HEAD_DIM),
            h_ssd[None], new_conv[None])


def kernel(x_prompt, x_sample, state_s5_re, state_s5_im, cache_sb_k, cache_sb_v, state_ssd, state_conv, ln_g, ln_b, ffn_w_gate, ffn_w_up, ffn_w_down, mix0_w_in, s5_a_re, s5_a_im, s5_log_dt, s5_b_re, s5_b_im, s5_c_re, s5_c_im, s5_d, s5_w_glu, s5_b_glu, mix0_w_out, ssd_w_in, ssd_conv_w, ssd_conv_b, ssd_dt_bias, ssd_a_log, ssd_d, ssd_norm_g, ssd_w_out):
    bf = lambda a: a.astype(BF16)
    w = {
        "wg": [[bf(ffn_w_gate[l, s]) for s in range(2)] for l in range(DEPTH)],
        "wu": [[bf(ffn_w_up[l, s]) for s in range(2)] for l in range(DEPTH)],
        "wd": [[bf(ffn_w_down[l, s]) for s in range(2)] for l in range(DEPTH)],
        "ln_g": ln_g, "ln_b": ln_b,
        "mix0_in": bf(mix0_w_in[0]),
        "s5_mats": s5_prep(s5_a_re[0], s5_a_im[0], s5_log_dt[0], s5_b_re[0], s5_b_im[0], s5_c_re[0], s5_c_im[0]),
        "s5_d": s5_d[0], "s5_wglu": bf(s5_w_glu[0]), "s5_bglu": s5_b_glu[0], "mix0_out": bf(mix0_w_out[0]),
        "ssd_in": bf(ssd_w_in[0]), "ssd_conv_w": ssd_conv_w[0], "ssd_conv_b": ssd_conv_b[0],
        "ssd_dt_bias": ssd_dt_bias[0], "ssd_a_log": ssd_a_log[0], "ssd_d": ssd_d[0],
        "ssd_norm_g": ssd_norm_g[0], "ssd_out": bf(ssd_w_out[0]),
    }
    bp, lp = x_prompt.shape[0], x_prompt.shape[1]
    bs, ls = x_sample.shape[0], x_sample.shape[1]
    yp, s5rp, s5ip, kp, vp, ssdp, convp = _trunk(x_prompt, bp, lp, None, w)
    ys, s5rs, s5is, ks, vs, ssds, convs = _trunk(
        x_sample, bs, ls, (state_s5_re, state_s5_im, cache_sb_k, cache_sb_v, state_ssd, state_conv), w)
    return (yp, ys, s5rp, s5ip, kp, vp, ssdp, convp, s5rs, s5is, ks, vs, ssds, convs)
```

```python
import functools
import math

import jax
import jax.numpy as jnp
from jax import lax
from jax.experimental import pallas as pl
from jax.experimental.pallas import tpu as pltpu

F32 = jnp.float32
BF16 = jnp.bfloat16

D_MODEL = 1024
DEPTH = 2
ALPHA = (2.0 * DEPTH) ** 0.25
LN_EPS = 1e-5
RMS_EPS = 1e-5
D_FF = 2816

S5_WIDTH = 512
S5_GROUP = 16
S5_GROUPS = 32
S5_STATE = 64
S5_CHUNK = 16
S5_ROW = S5_CHUNK * S5_GROUP
SB_HEADS = 8
SB_HEAD_DIM = 64
SB_WIDTH = 512
MIX0_IN = 2048

SSD_INNER = 2048
SSD_HEAD_DIM = 64
SSD_HEADS = 32
SSD_GROUPS = 4
SSD_STATE = 128
SSD_CONV = 4
SSD_GN = 512
SSD_CONV_DIM = 3072
SSD_PAIRS = SSD_HEADS // 2
CONV_PAD = 8

VMEM_LIMIT = 56 * 1024 * 1024


def _cparams(sem):
    return pltpu.CompilerParams(dimension_semantics=sem, vmem_limit_bytes=VMEM_LIMIT)


def _sigmoid(x):
    return 1.0 / (1.0 + jnp.exp2(x * -math.log2(math.e)))


def _softplus(x):
    return jnp.maximum(x, 0.0) + jnp.log1p(jnp.exp(-jnp.abs(x)))


def _layer_norm(y, g, b):
    mu = jnp.mean(y, axis=-1, keepdims=True)
    d = y - mu
    var = jnp.mean(d * d, axis=-1, keepdims=True)
    return d * lax.rsqrt(var + LN_EPS) * g + b


def _dot(a, b):
    return jnp.dot(a, b, preferred_element_type=F32)


def _dot_nt(a, b):
    return lax.dot_general(a, b, (((1,), (1,)), ((), ())), preferred_element_type=F32)


def _dot_tn(a, b):
    return lax.dot_general(a, b, (((0,), (0,)), ((), ())), preferred_element_type=F32)


def _split3(x):
    hi = x.astype(BF16)
    r = x - hi.astype(F32)
    mid = r.astype(BF16)
    lo = (r - mid.astype(F32)).astype(BF16)
    return hi, mid, lo


def _sum_right(x, ones_mat):
    hi, mid, lo = _split3(x)
    return _dot(hi, ones_mat) + _dot(mid, ones_mat) + _dot(lo, ones_mat)


def _sum_left(ones_mat, x):
    hi, mid, lo = _split3(x)
    return _dot(ones_mat, hi) + _dot(ones_mat, mid) + _dot(ones_mat, lo)


def _row_tile(t):
    for tm in (512, 256, 128, 64, 32, 16, 8):
        if t % tm == 0:
            return tm
    raise ValueError(f"token count {t} is not a multiple of 8")


def _full(shape):
    return pl.BlockSpec(shape, lambda *_: (0,) * len(shape), pipeline_mode=pl.Buffered(1))


FFN_TF = 1408


def _ffn_kernel(x_ref, wg_ref, wu_ref, wd_ref, g_ref, b_ref, o_ref):
    x = x_ref[...]
    xb = x.astype(BF16)
    acc = None
    for c in range(D_FF // FFN_TF):
        sl = slice(c * FFN_TF, (c + 1) * FFN_TF)
        gate = _dot(xb, wg_ref[:, sl])
        up = _dot(xb, wu_ref[:, sl])
        h = (gate * _sigmoid(gate) * up).astype(BF16)
        part = _dot(h, wd_ref[sl, :])
        acc = part if acc is None else acc + part
    o_ref[...] = _layer_norm(ALPHA * x + 0.5 * acc, g_ref[...], b_ref[...])


def ffn_ln(x, wg, wu, wd, g, b):
    t = x.shape[0]
    tm = _row_tile(t)
    return pl.pallas_call(
        _ffn_kernel,
        grid=(t // tm,),
        in_specs=[pl.BlockSpec((tm, D_MODEL), lambda i: (i, 0)),
                  _full((D_MODEL, D_FF)), _full((D_MODEL, D_FF)), _full((D_FF, D_MODEL)),
                  _full((1, D_MODEL)), _full((1, D_MODEL))],
        out_specs=pl.BlockSpec((tm, D_MODEL), lambda i: (i, 0)),
        out_shape=jax.ShapeDtypeStruct((t, D_MODEL), F32),
        compiler_params=_cparams(("parallel",)),
        name="ffn_ln",
    )(x, wg, wu, wd, g.reshape(1, D_MODEL), b.reshape(1, D_MODEL))


def _proj_kernel(x_ref, w_ref, *o_refs, plan):
    xb = x_ref[...].astype(BF16)
    refs = iter(o_refs)
    for lo, hi, outs in plan:
        y = _dot(xb, w_ref[:, lo:hi])
        for _, scale in outs:
            o_ref = next(refs)
            o_ref[...] = (y if scale == 1.0 else y * scale).astype(o_ref.dtype)


def proj(x, w, plan):
    t = x.shape[0]
    tm = _row_tile(t)
    n = w.shape[1]
    flat = [(hi - lo, dt) for lo, hi, outs in plan for dt, _ in outs]
    return pl.pallas_call(
        functools.partial(_proj_kernel, plan=plan),
        grid=(t // tm,),
        in_specs=[pl.BlockSpec((tm, D_MODEL), lambda i: (i, 0)), _full((D_MODEL, n))],
        out_specs=[pl.BlockSpec((tm, width), lambda i: (i, 0)) for width, _ in flat],
        out_shape=[jax.ShapeDtypeStruct((t, width), dt) for width, dt in flat],
        compiler_params=_cparams(("parallel",)),
        name="proj",
    )(x, w)


def _s5_prep_kernel(a_re_ref, a_im_ref, ldt_ref, bt1_ref, bt2_ref, cq1_ref, cq2_ref, cm_ref,
                    m_ref, p_ref, psw_ref, qt_ref, a1_ref, a2_ref, ppad_ref):
    ar, ai = a_re_ref[...], a_im_ref[...]
    dt = jnp.exp(ldt_ref[...])

    def cpow(k):
        mag = jnp.exp(k * (ar * dt))
        ang = k * (ai * dt)
        return mag * jnp.cos(ang), mag * jnp.sin(ang)

    abr, abi = cpow(1.0)
    nr, ni = abr - 1.0, abi
    den = ar * ar + ai * ai
    fr = (nr * ar + ni * ai) / den
    fi = (ni * ar - nr * ai) / den
    bt1, bt2 = bt1_ref[...], bt2_ref[...]
    bb = fr * bt1 + fi * bt2
    bbsw = fr * bt2 - fi * bt1
    group_shift = S5_GROUP.bit_length() - 1
    step = jnp.right_shift(lax.broadcasted_iota(jnp.int32, (S5_ROW, 1), 0), group_shift).astype(F32)
    pwr, pwi = cpow(float(S5_CHUNK - 1) - step)
    p = pwr * bb + pwi * bbsw
    p_ref[...] = p.astype(BF16)
    psw_ref[...] = pltpu.roll(p, S5_STATE, 1).astype(BF16)
    qwr, qwi = cpow(step + 1.0)
    qt_ref[...] = (qwr * cq1_ref[...] + qwi * cq2_ref[...]).astype(BF16)
    a16r, a16i = cpow(float(S5_CHUNK))
    lane = lax.broadcasted_iota(jnp.int32, (1, 2 * S5_STATE), 1)
    a1_ref[...] = a16r
    a2_ref[...] = jnp.where(lane < S5_STATE, -a16i, a16i)

    ppad_ref[0:S5_ROW, :] = p
    ppad_ref[S5_ROW:2 * S5_ROW, :] = jnp.zeros((S5_ROW, 2 * S5_STATE), F32)
    cm = cm_ref[...]
    col_step = jnp.right_shift(lax.broadcasted_iota(jnp.int32, (2 * S5_STATE, S5_ROW), 1), group_shift)
    acc = jnp.zeros((S5_ROW, S5_ROW), F32)
    for t in range(S5_CHUNK):
        off = S5_GROUP * (S5_CHUNK - 1 - t)
        w_t = ppad_ref[off:off + S5_ROW, :].astype(BF16)
        c_t = jnp.where(col_step == t, cm, 0.0).astype(BF16)
        acc = acc + _dot(w_t, c_t)
    m_ref[...] = acc.astype(BF16)


def s5_prep(a_re, a_im, log_dt, b_re, b_im, c_re, c_im):
    g, n, r = S5_GROUPS, S5_STATE, S5_ROW
    dup = lambda v: jnp.concatenate([v, v], axis=-1).reshape(g, 1, 2 * n)
    ldt = jnp.broadcast_to(log_dt.reshape(g, 1, 1), (g, 1, 2 * n))
    btr = jnp.tile(jnp.swapaxes(b_re, 1, 2), (1, S5_CHUNK, 1))
    bti = jnp.tile(jnp.swapaxes(b_im, 1, 2), (1, S5_CHUNK, 1))
    bt1 = jnp.concatenate([btr, bti], axis=-1)
    bt2 = jnp.concatenate([-bti, btr], axis=-1)
    cqr = jnp.tile(c_re, (1, S5_CHUNK, 1))
    cqi = jnp.tile(c_im, (1, S5_CHUNK, 1))
    cq1 = jnp.concatenate([cqr, -cqi], axis=-1)
    cq2 = jnp.concatenate([-cqi, -cqr], axis=-1)
    cm = jnp.concatenate([jnp.tile(jnp.swapaxes(c_re, 1, 2), (1, 1, S5_CHUNK)),
                          jnp.tile(-jnp.swapaxes(c_im, 1, 2), (1, 1, S5_CHUNK))], axis=1)
    grp = lambda *shape: pl.BlockSpec((None,) + shape, lambda i: (i, 0, 0))
    return pl.pallas_call(
        _s5_prep_kernel,
        grid=(g,),
        in_specs=[grp(1, 2 * n), grp(1, 2 * n), grp(1, 2 * n), grp(r, 2 * n), grp(r, 2 * n),
                  grp(r, 2 * n), grp(r, 2 * n), grp(2 * n, r)],
        out_specs=[grp(r, r), grp(r, 2 * n), grp(r, 2 * n), grp(r, 2 * n), grp(1, 2 * n), grp(1, 2 * n)],
        out_shape=[jax.ShapeDtypeStruct((g, r, r), BF16), jax.ShapeDtypeStruct((g, r, 2 * n), BF16),
                   jax.ShapeDtypeStruct((g, r, 2 * n), BF16), jax.ShapeDtypeStruct((g, r, 2 * n), BF16),
                   jax.ShapeDtypeStruct((g, 1, 2 * n), F32), jax.ShapeDtypeStruct((g, 1, 2 * n), F32)],
        scratch_shapes=[pltpu.VMEM((2 * r, 2 * n), F32)],
        compiler_params=_cparams(("parallel",)),
        name="s5_prep",
    )(dup(a_re), dup(a_im), ldt, bt1, bt2, cq1, cq2, cm)


def _s5_kernel(u_ref, h0_ref, h0sw_ref, m_ref, p_ref, psw_ref, qt_ref, a1_ref, a2_ref,
               y_ref, hl_ref, s_ref, ssw_ref, hin_ref, *, bn, nj):
    u = u_ref[...]
    s_ref[...] = _dot(u, p_ref[...])
    ssw_ref[...] = _dot(u, psw_ref[...])
    a1, a2 = a1_ref[...], a2_ref[...]

    def body(j, carry):
        h, hsw = carry
        rows = pl.ds(pl.multiple_of(j * bn, bn), bn)
        hin_ref[rows, :] = h
        return a1 * h + a2 * hsw + s_ref[rows, :], a1 * hsw - a2 * h + ssw_ref[rows, :]

    h, _ = lax.fori_loop(0, nj, body, (h0_ref[...], h0sw_ref[...]))
    hl_ref[...] = h
    y_ref[...] = _dot(u, m_ref[...]) + _dot_nt(hin_ref[...].astype(BF16), qt_ref[...])


def s5_scan(u_g, h0, h0sw, mats, bn, nj):
    m, p, psw, qt, a1, a2 = mats
    g, n, r = S5_GROUPS, S5_STATE, S5_ROW
    rows = bn * nj
    grp = lambda *shape: pl.BlockSpec((None,) + shape, lambda i: (i, 0, 0))
    return pl.pallas_call(
        functools.partial(_s5_kernel, bn=bn, nj=nj),
        grid=(g,),
        in_specs=[grp(rows, r), grp(bn, 2 * n), grp(bn, 2 * n), grp(r, r), grp(r, 2 * n), grp(r, 2 * n),
                  grp(r, 2 * n), grp(1, 2 * n), grp(1, 2 * n)],
        out_specs=[grp(rows, r), grp(bn, 2 * n)],
        out_shape=[jax.ShapeDtypeStruct((g, rows, r), F32), jax.ShapeDtypeStruct((g, bn, 2 * n), F32)],
        scratch_shapes=[pltpu.VMEM((rows, 2 * n), F32)] * 3,
        compiler_params=_cparams(("parallel",)),
        name="s5_scan",
    )(u_g, h0, h0sw, m, p, psw, qt, a1, a2)


SB_TK = 256
LOG2E = math.log2(math.e)
SB_SKIP_LOG2 = 152.0


def _sb_block(q2, k, v, ones, carry, mask):
    half = q2.shape[0] // 2
    halves = (slice(0, half), slice(half, 2 * half))
    zs = [_dot_nt(q2[h], k) for h in halves]
    tails = [jnp.log2(1.0 + jnp.exp2(-jnp.abs(z))) for z in zs]
    sps = [jnp.maximum(z, 0.0) + t for z, t in zip(zs, tails)]
    if mask is not None:
        sps = [jnp.where(mask[h], sp, 0.0) for sp, h in zip(sps, halves)]
    his = [sp.astype(BF16) for sp in sps]
    los = [(sp - hi.astype(F32)).astype(BF16) for sp, hi in zip(sps, his)]
    sums = [_dot(jnp.concatenate([hi, lo], axis=0), ones) for hi, lo in zip(his, los)]
    outs = []
    for i, h in enumerate(halves):
        suffix = sums[i][:half] + sums[i][half:]
        logw = jnp.minimum(zs[i], 0.0) - tails[i] - suffix
        if carry is not None:
            logw = logw - carry[h]
        w = jnp.exp2(logw)
        if mask is not None:
            w = jnp.where(mask[h], w, 0.0)
        outs.append(_dot(w.astype(BF16), v))
    mass = [jnp.sum(sp, axis=-1, keepdims=True) for sp in sps]
    return jnp.concatenate(outs, axis=0), jnp.concatenate(mass, axis=0)


def _sb_kernel(q_ref, kd_ref, vd_ref, kp_ref, vp_ref, ud_ref, up_ref, o_ref, *, tq, n_past):
    qi = pl.program_id(2)
    q = q_ref[...]
    kd = kd_ref[...].astype(BF16)
    vd = vd_ref[...].astype(BF16)
    lane_lo = lax.broadcasted_iota(jnp.int32, (tq, 2 * SB_HEAD_DIM), 1) < SB_HEAD_DIM
    row = lax.broadcasted_iota(jnp.int32, (2 * tq, tq), 0)
    col = lax.broadcasted_iota(jnp.int32, (2 * tq, tq), 1)
    earlier = col < jnp.where(row >= tq, row - tq, row)
    blocks = n_past(qi)
    zero = jnp.zeros_like(q)
    q2 = jnp.concatenate([jnp.where(lane_lo, q, zero), jnp.where(lane_lo, zero, q)], axis=0)
    acc0, mass0 = _sb_block(q2, kd, vd, ud_ref[...], None, earlier)

    def cond(s):
        return jnp.logical_and(s[0] < blocks, s[1] < SB_SKIP_LOG2)

    def body(s):
        i, _, acc, carry = s
        rows = pl.ds(pl.multiple_of((blocks - 1 - i) * SB_TK, SB_TK), SB_TK)
        k = kp_ref[rows, :].astype(BF16)
        v = vp_ref[rows, :].astype(BF16)
        d, m = _sb_block(q2, k, v, up_ref[...], carry, None)
        carry = carry + m
        return i + 1, jnp.min(carry), acc + d, carry

    s = lax.while_loop(cond, body, (jnp.int32(0), jnp.min(mass0), acc0, mass0))
    o_ref[...] = jnp.where(lane_lo, s[2][:tq], s[2][tq:])


def _strict_lower_ones(n):
    r = lax.broadcasted_iota(jnp.int32, (n, n), 0)
    c = lax.broadcasted_iota(jnp.int32, (n, n), 1)
    return (r > c).astype(BF16)


def stick_breaking(q, k_new, v_new, k_past, v_past, bn, lq, tq, past_len):
    lp = k_past.shape[1]
    nq = lq // tq
    lanes = 2 * SB_HEAD_DIM
    if past_len is None:
        assert tq % SB_TK == 0
        n_past = lambda qi: qi * (tq // SB_TK)
    else:
        assert past_len == lp and lp % SB_TK == 0
        n_past = lambda qi: lp // SB_TK
    return pl.pallas_call(
        functools.partial(_sb_kernel, tq=tq, n_past=n_past),
        grid=(bn, SB_HEADS // 2, nq),
        in_specs=[pl.BlockSpec((None, tq, lanes), lambda b, h, i: (b, i, h)),
                  pl.BlockSpec((None, tq, lanes), lambda b, h, i: (b, i, h)),
                  pl.BlockSpec((None, tq, lanes), lambda b, h, i: (b, i, h)),
                  pl.BlockSpec((None, lp, lanes), lambda b, h, i: (b, 0, h)),
                  pl.BlockSpec((None, lp, lanes), lambda b, h, i: (b, 0, h)),
                  _full((tq, tq)), _full((SB_TK, SB_TK))],
        out_specs=pl.BlockSpec((None, tq, lanes), lambda b, h, i: (b, i, h)),
        out_shape=jax.ShapeDtypeStruct((bn, lq, SB_WIDTH), F32),
        compiler_params=_cparams(("parallel", "parallel", "arbitrary")),
        name="stick_breaking",
    )(q, k_new, v_new, k_past, v_past, _strict_lower_ones(tq), _strict_lower_ones(SB_TK))


def _gelu(x):
    return 0.5 * x * (1.0 + jnp.tanh(math.sqrt(2.0 / math.pi) * (x + 0.044715 * (x * x * x))))


def _mix0_out_kernel(x_ref, ys_ref, u_ref, sb_ref, d_ref, wglu_ref, bglu_ref, wo1_ref, wo2_ref, g_ref, b_ref, o_ref):
    x = x_ref[...]
    gl = _gelu(ys_ref[...] + d_ref[...] * u_ref[...])
    gate = _sigmoid(_dot(gl.astype(BF16), wglu_ref[...]) + bglu_ref[...])
    m = _dot((gl * gate).astype(BF16), wo1_ref[...]) + _dot(sb_ref[...].astype(BF16), wo2_ref[...])
    o_ref[...] = _layer_norm(ALPHA * x + m, g_ref[...], b_ref[...])


def mix0_out(x, ys, u, sb, d, wglu, bglu, wo1, wo2, g, b):
    t = x.shape[0]
    tm = _row_tile(t)
    rowblk = lambda n: pl.BlockSpec((tm, n), lambda i: (i, 0))
    return pl.pallas_call(
        _mix0_out_kernel,
        grid=(t // tm,),
        in_specs=[rowblk(D_MODEL), rowblk(S5_WIDTH), rowblk(S5_WIDTH), rowblk(SB_WIDTH),
                  _full((1, S5_WIDTH)), _full((S5_WIDTH, S5_WIDTH)), _full((1, S5_WIDTH)),
                  _full((S5_WIDTH, D_MODEL)), _full((SB_WIDTH, D_MODEL)),
                  _full((1, D_MODEL)), _full((1, D_MODEL))],
        out_specs=rowblk(D_MODEL),
        out_shape=jax.ShapeDtypeStruct((t, D_MODEL), F32),
        compiler_params=_cparams(("parallel",)),
        name="mix0_out",
    )(x, ys, u, sb, d.reshape(1, -1), wglu, bglu.reshape(1, -1), wo1, wo2, g.reshape(1, -1), b.reshape(1, -1))


def _ssd_kernel(xbc_ref, z_ref, dt_ref, dtt_ref, conv0_ref, h0_ref, cw_ref, cb_ref, dtb_ref, dtbt_ref,
                alog_ref, alogt_ref, dsk_ref, ng_ref, tril_ref, triu_ref,
                y_ref, hl_ref, ext_ref, h_ref, *, lc, nc):
    c = pl.program_id(1)

    @pl.when(c == 0)
    def _():
        ext_ref[0:CONV_PAD, :] = conv0_ref[...]
        h_ref[...] = h0_ref[...]

    ext_ref[CONV_PAD:CONV_PAD + lc, :] = xbc_ref[...]
    ext = ext_ref[...]
    conv = cb_ref[...] + ext[CONV_PAD:, :] * cw_ref[SSD_CONV - 1:SSD_CONV, :]
    for back in range(1, SSD_CONV):
        tap = SSD_CONV - 1 - back
        conv = conv + pltpu.roll(ext, back, 0)[CONV_PAD:, :] * cw_ref[tap:tap + 1, :]
    ext_ref[0:CONV_PAD, :] = ext[lc:lc + CONV_PAD, :]
    act = conv * _sigmoid(conv)

    dt = _softplus(dt_ref[...] + dtb_ref[...])
    dtt = _softplus(dtt_ref[...] + dtbt_ref[...])
    cs = _sum_left(tril_ref[...], dt * (-LOG2E * jnp.exp(alog_ref[...])))
    cst = _sum_right(dtt * (-LOG2E * jnp.exp(alogt_ref[...])), triu_ref[...])
    cs_last = cs[lc - 1:lc, :]

    row = lax.broadcasted_iota(jnp.int32, (lc, lc), 0)
    col = lax.broadcasted_iota(jnp.int32, (lc, lc), 1)
    causal = col <= row
    lanes = 2 * SSD_HEAD_DIM
    lane_lo = lax.broadcasted_iota(jnp.int32, (lc, lanes), 1) < SSD_HEAD_DIM
    pairs_per_group = SSD_PAIRS // SSD_GROUPS

    for g in range(SSD_GROUPS):
        bm = act[:, SSD_INNER + g * SSD_STATE:SSD_INNER + (g + 1) * SSD_STATE].astype(BF16)
        cm = act[:, SSD_INNER + SSD_GN + g * SSD_STATE:SSD_INNER + SSD_GN + (g + 1) * SSD_STATE].astype(BF16)
        cb = _dot_nt(cm, bm)
        gated, sq = [], jnp.zeros((lc, lanes), F32)
        for k in range(g * pairs_per_group, (g + 1) * pairs_per_group):
            sl = slice(k * lanes, (k + 1) * lanes)
            xp = act[:, sl]
            heads = (2 * k, 2 * k + 1)
            pair = lambda v: jnp.where(lane_lo[:v[0].shape[0]], v[0], v[1])
            xdt = xp * pair([dt[:, h:h + 1] for h in heads])
            xdtb = xdt.astype(BF16)
            csb = [jnp.broadcast_to(cs[:, h:h + 1], (lc, lanes)) for h in heads]
            ys = []
            for h, csh in zip(heads, csb):
                seg = csh[:, :lc] - cst[h:h + 1, :]
                w = cb * jnp.exp2(jnp.where(causal, seg, -jnp.inf))
                ys.append(_dot(w.astype(BF16), xdtb))
            y = pair(ys)
            cs_pair = pair(csb)
            last_pair = pair([cs_last[:, h:h + 1] for h in heads])
            ht = h_ref[k]
            y = y + _dot(cm, ht.astype(BF16)) * jnp.exp2(cs_pair)
            xw = (xdt * jnp.exp2(last_pair - cs_pair)).astype(BF16)
            h_ref[k] = ht * jnp.exp2(last_pair) + _dot_tn(bm, xw)
            y = y + dsk_ref[:, sl] * xp
            zz = z_ref[:, sl]
            yg = y * (zz * _sigmoid(zz))
            sq = sq + yg * yg
            gated.append(yg)
        mean_sq = jnp.sum(sq, axis=-1, keepdims=True) * (1.0 / (pairs_per_group * lanes))
        scale = lax.rsqrt(mean_sq + RMS_EPS)
        for k, yg in zip(range(g * pairs_per_group, (g + 1) * pairs_per_group), gated):
            sl = slice(k * lanes, (k + 1) * lanes)
            y_ref[:, sl] = yg * scale * ng_ref[:, sl]

    @pl.when(c == nc - 1)
    def _():
        hl_ref[...] = h_ref[...]


def ssd_mix(xbc, z, dt_raw, conv0, h0, conv_w, conv_b, dt_bias, a_log, d_skip, norm_g, bn, length, lc):
    nc = length // lc
    t = bn * length
    lanes = 2 * SSD_HEAD_DIM
    dtt = jnp.swapaxes(dt_raw.reshape(bn * nc, lc, SSD_HEADS), 1, 2)
    conv0p = jnp.pad(conv0, ((0, 0), (CONV_PAD - (SSD_CONV - 1), 0), (0, 0)))
    a_log = a_log.astype(F32)
    r = lax.broadcasted_iota(jnp.int32, (lc, lc), 0)
    cc = lax.broadcasted_iota(jnp.int32, (lc, lc), 1)
    tril = (cc <= r).astype(BF16)
    triu = (r <= cc).astype(BF16)
    tok = lambda n: pl.BlockSpec((lc, n), lambda b, c: (b * nc + c, 0))
    y, hl = pl.pallas_call(
        functools.partial(_ssd_kernel, lc=lc, nc=nc),
        grid=(bn, nc),
        in_specs=[tok(SSD_CONV_DIM), tok(SSD_INNER), tok(SSD_HEADS),
                  pl.BlockSpec((None, SSD_HEADS, lc), lambda b, c: (b * nc + c, 0, 0)),
                  pl.BlockSpec((None, CONV_PAD, SSD_CONV_DIM), lambda b, c: (b, 0, 0)),
                  pl.BlockSpec((None, SSD_PAIRS, lanes, SSD_STATE), lambda b, c: (b, 0, 0, 0)),
                  _full((SSD_CONV, SSD_CONV_DIM)), _full((1, SSD_CONV_DIM)),
                  _full((1, SSD_HEADS)), _full((SSD_HEADS, 1)), _full((1, SSD_HEADS)), _full((SSD_HEADS, 1)),
                  _full((1, SSD_INNER)), _full((1, SSD_INNER)), _full((lc, lc)), _full((lc, lc))],
        out_specs=[tok(SSD_INNER),
                   pl.BlockSpec((None, SSD_PAIRS, lanes, SSD_STATE), lambda b, c: (b, 0, 0, 0))],
        out_shape=[jax.ShapeDtypeStruct((t, SSD_INNER), F32),
                   jax.ShapeDtypeStruct((bn, SSD_PAIRS, lanes, SSD_STATE), F32)],
        scratch_shapes=[pltpu.VMEM((CONV_PAD + lc, SSD_CONV_DIM), F32),
                        pltpu.VMEM((SSD_PAIRS, lanes, SSD_STATE), F32)],
        compiler_params=_cparams(("parallel", "arbitrary")),
        name="ssd_mix",
    )(xbc, z, dt_raw, dtt, conv0p, jnp.swapaxes(h0.reshape(bn, SSD_PAIRS, lanes, SSD_STATE), 2, 3),
      conv_w, conv_b.reshape(1, -1), dt_bias.reshape(1, -1), dt_bias.reshape(-1, 1),
      a_log.reshape(1, -1), a_log.reshape(-1, 1), jnp.repeat(d_skip, SSD_HEAD_DIM).reshape(1, -1),
      norm_g.reshape(1, -1), tril, triu)
    return y, jnp.swapaxes(hl, 2, 3).reshape(bn, SSD_HEADS, SSD_HEAD_DIM, SSD_STATE)


def _out_ln_kernel(x_ref, a_ref, w_ref, g_ref, b_ref, o_ref):
    x = x_ref[...]
    m = _dot(a_ref[...].astype(BF16), w_ref[...])
    o_ref[...] = _layer_norm(ALPHA * x + m, g_ref[...], b_ref[...])


def out_ln(x, a, w, g, b):
    t = x.shape[0]
    tm = _row_tile(t)
    k = a.shape[1]
    return pl.pallas_call(
        _out_ln_kernel,
        grid=(t // tm,),
        in_specs=[pl.BlockSpec((tm, D_MODEL), lambda i: (i, 0)), pl.BlockSpec((tm, k), lambda i: (i, 0)),
                  _full((k, D_MODEL)), _full((1, D_MODEL)), _full((1, D_MODEL))],
        out_specs=pl.BlockSpec((tm, D_MODEL), lambda i: (i, 0)),
        out_shape=jax.ShapeDtypeStruct((t, D_MODEL), F32),
        compiler_params=_cparams(("parallel",)),
        name="out_ln",
    )(x, a, w, g.reshape(1, -1), b.reshape(1, -1))


def _to_groups(u, bn, nj):
    v = u.reshape(bn, nj, S5_CHUNK, S5_GROUPS, S5_GROUP)
    return jnp.transpose(v, (3, 1, 0, 2, 4)).reshape(S5_GROUPS, nj * bn, S5_ROW)


def _from_groups(y, bn, nj):
    v = y.reshape(S5_GROUPS, nj, bn, S5_CHUNK, S5_GROUP)
    return jnp.transpose(v, (2, 1, 3, 0, 4)).reshape(bn * nj * S5_CHUNK, S5_WIDTH)


def _trunk(x, bn, length, states, w):
    t = bn * length
    x = x.reshape(t, D_MODEL)
    x = ffn_ln(x, w["wg"][0][0], w["wu"][0][0], w["wd"][0][0], w["ln_g"][0, 0], w["ln_b"][0, 0])

    sw = S5_WIDTH
    both = ((F32, 1.0), (BF16, 1.0))
    u, u_bf, q_bf, k, k_bf, v, v_bf = proj(x, w["mix0_in"], (
        (0, sw, both), (sw, 2 * sw, ((BF16, SB_HEAD_DIM ** -0.5 * LOG2E),)),
        (2 * sw, 3 * sw, both), (3 * sw, 4 * sw, both)))
    nj = length // S5_CHUNK
    if states is None:
        h0 = jnp.zeros((S5_GROUPS, bn, 2 * S5_STATE), F32)
        h0sw = h0
    else:
        re = jnp.swapaxes(states[0][0].astype(F32), 0, 1)
        im = jnp.swapaxes(states[1][0].astype(F32), 0, 1)
        h0 = jnp.concatenate([re, im], axis=-1)
        h0sw = jnp.concatenate([im, re], axis=-1)
    y_g, h_last = s5_scan(_to_groups(u_bf, bn, nj), h0, h0sw, w["s5_mats"], bn, nj)
    y_s5 = _from_groups(y_g, bn, nj)
    s5_re = jnp.swapaxes(h_last[..., :S5_STATE], 0, 1)[None]
    s5_im = jnp.swapaxes(h_last[..., S5_STATE:], 0, 1)[None]

    q3 = q_bf.reshape(bn, length, SB_WIDTH)
    if states is None:
        tq = min(SB_TK, length)
        sb = stick_breaking(q3, k_bf.reshape(bn, length, SB_WIDTH), v_bf.reshape(bn, length, SB_WIDTH),
                            k_bf.reshape(bn, length, SB_WIDTH), v_bf.reshape(bn, length, SB_WIDTH),
                            bn, length, tq, None)
    else:
        past = states[2][0].shape[1]
        sb = stick_breaking(q3, k_bf.reshape(bn, length, SB_WIDTH), v_bf.reshape(bn, length, SB_WIDTH),
                            states[2][0].reshape(bn, past, SB_WIDTH), states[3][0].reshape(bn, past, SB_WIDTH),
                            bn, length, length, past)
    x = mix0_out(x, y_s5, u, sb.reshape(t, SB_WIDTH), w["s5_d"], w["s5_wglu"], w["s5_bglu"],
                 w["mix0_out"][:S5_WIDTH], w["mix0_out"][S5_WIDTH:], w["ln_g"][0, 1], w["ln_b"][0, 1])
    x = ffn_ln(x, w["wg"][0][1], w["wu"][0][1], w["wd"][0][1], w["ln_g"][0, 2], w["ln_b"][0, 2])

    x = ffn_ln(x, w["wg"][1][0], w["wu"][1][0], w["wd"][1][0], w["ln_g"][1, 0], w["ln_b"][1, 0])
    one = ((F32, 1.0),)
    z, xbc, dt_raw = proj(x, w["ssd_in"], (
        (0, SSD_INNER, one), (SSD_INNER, SSD_INNER + SSD_CONV_DIM, one),
        (SSD_INNER + SSD_CONV_DIM, SSD_INNER + SSD_CONV_DIM + SSD_HEADS, one)))
    if states is None:
        conv0 = jnp.zeros((bn, SSD_CONV - 1, SSD_CONV_DIM), F32)
        hs0 = jnp.zeros((bn, SSD_HEADS, SSD_HEAD_DIM, SSD_STATE), F32)
    else:
        conv0 = states[5][0].astype(F32)
        hs0 = states[4][0].astype(F32)
    lc = min(128, length)
    yg, h_ssd = ssd_mix(xbc, z, dt_raw, conv0, hs0, w["ssd_conv_w"], w["ssd_conv_b"], w["ssd_dt_bias"],
                        w["ssd_a_log"], w["ssd_d"], w["ssd_norm_g"], bn, length, lc)
    ext = jnp.concatenate([conv0, xbc.reshape(bn, length, SSD_CONV_DIM)[:, length - (SSD_CONV - 1):]], axis=1)
    new_conv = ext[:, ext.shape[1] - (SSD_CONV - 1):]
    x = out_ln(x, yg, w["ssd_out"], w["ln_g"][1, 1], w["ln_b"][1, 1])
    x = ffn_ln(x, w["wg"][1][1], w["wu"][1][1], w["wd"][1][1], w["ln_g"][1, 2], w["ln_b"][1, 2])

    return (x.reshape(bn, length, D_MODEL), s5_re, s5_im,
            k.reshape(1, bn, length, SB_HEADS, SB_HEAD_DIM), v.reshape(1, bn, length, SB_HEADS, SB_HEAD_DIM),
            h_ssd[None], new_conv[None])


def kernel(x_prompt, x_sample, state_s5_re, state_s5_im, cache_sb_k, cache_sb_v, state_ssd, state_conv, ln_g, ln_b, ffn_w_gate, ffn_w_up, ffn_w_down, mix0_w_in, s5_a_re, s5_a_im, s5_log_dt, s5_b_re, s5_b_im, s5_c_re, s5_c_im, s5_d, s5_w_glu, s5_b_glu, mix0_w_out, ssd_w_in, ssd_conv_w, ssd_conv_b, ssd_dt_bias, ssd_a_log, ssd_d, ssd_norm_g, ssd_w_out):
    bf = lambda a: a.astype(BF16)
    w = {
        "wg": [[bf(ffn_w_gate[l, s]) for s in range(2)] for l in range(DEPTH)],
        "wu": [[bf(ffn_w_up[l, s]) for s in range(2)] for l in range(DEPTH)],
        "wd": [[bf(ffn_w_down[l, s]) for s in range(2)] for l in range(DEPTH)],
        "ln_g": ln_g, "ln_b": ln_b,
        "mix0_in": bf(mix0_w_in[0]),
        "s5_mats": s5_prep(s5_a_re[0], s5_a_im[0], s5_log_dt[0], s5_b_re[0], s5_b_im[0], s5_c_re[0], s5_c_im[0]),
        "s5_d": s5_d[0], "s5_wglu": bf(s5_w_glu[0]), "s5_bglu": s5_b_glu[0], "mix0_out": bf(mix0_w_out[0]),
        "ssd_in": bf(ssd_w_in[0]), "ssd_conv_w": ssd_conv_w[0], "ssd_conv_b": ssd_conv_b[0],
        "ssd_dt_bias": ssd_dt_bias[0], "ssd_a_log": ssd_a_log[0], "ssd_d": ssd_d[0],
        "ssd_norm_g": ssd_norm_g[0], "ssd_out": bf(ssd_w_out[0]),
    }
    bp, lp = x_prompt.shape[0], x_prompt.shape[1]
    bs, ls = x_sample.shape[0], x_sample.shape[1]
    yp, s5rp, s5ip, kp, vp, ssdp, convp = _trunk(x_prompt, bp, lp, None, w)
    ys, s5rs, s5is, ks, vs, ssds, convs = _trunk(
        x_sample, bs, ls, (state_s5_re, state_s5_im, cache_sb_k, cache_sb_v, state_ssd, state_conv), w)
    return (yp, ys, s5rp, s5ip, kp, vp, ssdp, convp, s5rs, s5is, ks, vs, ssds, convs)
```

```python
import functools
import math

import jax
import jax.numpy as jnp
from jax import lax
from jax.experimental import pallas as pl
from jax.experimental.pallas import tpu as pltpu

F32 = jnp.float32
BF16 = jnp.bfloat16

D_MODEL = 1024
DEPTH = 2
ALPHA = (2.0 * DEPTH) ** 0.25
LN_EPS = 1e-5
RMS_EPS = 1e-5
D_FF = 2816

S5_WIDTH = 512
S5_GROUP = 16
S5_GROUPS = 32
S5_STATE = 64
S5_CHUNK = 16
S5_ROW = S5_CHUNK * S5_GROUP
SB_HEADS = 8
SB_HEAD_DIM = 64
SB_WIDTH = 512
MIX0_IN = 2048

SSD_INNER = 2048
SSD_HEAD_DIM = 64
SSD_HEADS = 32
SSD_GROUPS = 4
SSD_STATE = 128
SSD_CONV = 4
SSD_GN = 512
SSD_CONV_DIM = 3072
SSD_PAIRS = SSD_HEADS // 2
CONV_PAD = 8

LANES = 128
VMEM_LIMIT = 56 * 1024 * 1024


def _cparams(sem):
    return pltpu.CompilerParams(dimension_semantics=sem, vmem_limit_bytes=VMEM_LIMIT)


def _sigmoid(x):
    return 1.0 / (1.0 + jnp.exp2(x * -math.log2(math.e)))


def _softplus(x):
    return jnp.maximum(x, 0.0) + jnp.log1p(jnp.exp(-jnp.abs(x)))


def _layer_norm(y, g, b):
    mu = jnp.mean(y, axis=-1, keepdims=True)
    d = y - mu
    var = jnp.mean(d * d, axis=-1, keepdims=True)
    return d * lax.rsqrt(var + LN_EPS) * g + b


def _dot(a, b):
    return jnp.dot(a, b, preferred_element_type=F32)


def _dot_nt(a, b):
    return lax.dot_general(a, b, (((1,), (1,)), ((), ())), preferred_element_type=F32)


def _dot_tn(a, b):
    return lax.dot_general(a, b, (((0,), (0,)), ((), ())), preferred_element_type=F32)


def _split3(x):
    hi = x.astype(BF16)
    r = x - hi.astype(F32)
    mid = r.astype(BF16)
    lo = (r - mid.astype(F32)).astype(BF16)
    return hi, mid, lo


def _sum_right(x, ones_mat):
    hi, mid, lo = _split3(x)
    return _dot(hi, ones_mat) + _dot(mid, ones_mat) + _dot(lo, ones_mat)


def _sum_left(ones_mat, x):
    hi, mid, lo = _split3(x)
    return _dot(ones_mat, hi) + _dot(ones_mat, mid) + _dot(ones_mat, lo)


def _row_tile(t):
    for tm in (512, 256, 128, 64, 32, 16, 8):
        if t % tm == 0:
            return tm
    raise ValueError(f"token count {t} is not a multiple of 8")


def _full(shape):
    return pl.BlockSpec(shape, lambda *_: (0,) * len(shape), pipeline_mode=pl.Buffered(1))


FFN_TF = 1408


def _ffn_kernel(x_ref, wg_ref, wu_ref, wd_ref, g_ref, b_ref, o_ref):
    x = x_ref[...]
    xb = x.astype(BF16)
    acc = None
    for c in range(D_FF // FFN_TF):
        sl = slice(c * FFN_TF, (c + 1) * FFN_TF)
        gate = _dot(xb, wg_ref[:, sl])
        up = _dot(xb, wu_ref[:, sl])
        h = (gate * _sigmoid(gate) * up).astype(BF16)
        part = _dot(h, wd_ref[sl, :])
        acc = part if acc is None else acc + part
    o_ref[...] = _layer_norm(ALPHA * x + 0.5 * acc, g_ref[...], b_ref[...])


def ffn_ln(x, wg, wu, wd, g, b):
    t = x.shape[0]
    tm = _row_tile(t)
    return pl.pallas_call(
        _ffn_kernel,
        grid=(t // tm,),
        in_specs=[pl.BlockSpec((tm, D_MODEL), lambda i: (i, 0)),
                  _full((D_MODEL, D_FF)), _full((D_MODEL, D_FF)), _full((D_FF, D_MODEL)),
                  _full((1, D_MODEL)), _full((1, D_MODEL))],
        out_specs=pl.BlockSpec((tm, D_MODEL), lambda i: (i, 0)),
        out_shape=jax.ShapeDtypeStruct((t, D_MODEL), F32),
        compiler_params=_cparams(("parallel",)),
        name="ffn_ln",
    )(x, wg, wu, wd, g.reshape(1, D_MODEL), b.reshape(1, D_MODEL))


def _proj_kernel(x_ref, w_ref, *o_refs, plan):
    xb = x_ref[...].astype(BF16)
    refs = iter(o_refs)
    for lo, hi, outs in plan:
        y = _dot(xb, w_ref[:, lo:hi])
        for _, scale in outs:
            o_ref = next(refs)
            o_ref[...] = (y if scale == 1.0 else y * scale).astype(o_ref.dtype)


def proj(x, w, plan):
    t = x.shape[0]
    tm = _row_tile(t)
    n = w.shape[1]
    flat = [(hi - lo, dt) for lo, hi, outs in plan for dt, _ in outs]
    return pl.pallas_call(
        functools.partial(_proj_kernel, plan=plan),
        grid=(t // tm,),
        in_specs=[pl.BlockSpec((tm, D_MODEL), lambda i: (i, 0)), _full((D_MODEL, n))],
        out_specs=[pl.BlockSpec((tm, width), lambda i: (i, 0)) for width, _ in flat],
        out_shape=[jax.ShapeDtypeStruct((t, width), dt) for width, dt in flat],
        compiler_params=_cparams(("parallel",)),
        name="proj",
    )(x, w)


def _mix0_in_kernel(x_ref, w_ref, u_ref, us_ref, q_ref, k_ref, kb_ref, v_ref, vb_ref, u_scr, *, q_scale):
    xb = x_ref[...].astype(BF16)
    sw = S5_WIDTH
    u = _dot(xb, w_ref[:, 0:sw])
    u_ref[...] = u
    for c in range(sw // LANES):
        u_scr[c] = u[:, c * LANES:(c + 1) * LANES]
    chunks = u.shape[0] // S5_CHUNK
    per_slab = S5_ROW // LANES
    for s in range(S5_CHUNK):
        for half in range(2):
            for c in range(per_slab):
                rows = u_scr[half * per_slab + c, pl.ds(s, chunks, stride=S5_CHUNK), :]
                us_ref[2 * s + half, :, c * LANES:(c + 1) * LANES] = rows.astype(BF16)
    q_ref[...] = (_dot(xb, w_ref[:, sw:2 * sw]) * q_scale).astype(BF16)
    k = _dot(xb, w_ref[:, 2 * sw:3 * sw])
    k_ref[...] = k
    kb_ref[...] = k.astype(BF16)
    v = _dot(xb, w_ref[:, 3 * sw:4 * sw])
    v_ref[...] = v
    vb_ref[...] = v.astype(BF16)


def mix0_in(x, w, q_scale):
    t = x.shape[0]
    tm = _row_tile(t)
    sw = S5_WIDTH
    tok = pl.BlockSpec((tm, sw), lambda i: (i, 0))
    return pl.pallas_call(
        functools.partial(_mix0_in_kernel, q_scale=q_scale),
        grid=(t // tm,),
        in_specs=[pl.BlockSpec((tm, D_MODEL), lambda i: (i, 0)), _full((D_MODEL, MIX0_IN))],
        out_specs=[tok, pl.BlockSpec((S5_SLABS, tm // S5_CHUNK, S5_ROW), lambda i: (0, i, 0)),
                   tok, tok, tok, tok, tok],
        out_shape=[jax.ShapeDtypeStruct((t, sw), F32),
                   jax.ShapeDtypeStruct((S5_SLABS, t // S5_CHUNK, S5_ROW), BF16),
                   jax.ShapeDtypeStruct((t, sw), BF16), jax.ShapeDtypeStruct((t, sw), F32),
                   jax.ShapeDtypeStruct((t, sw), BF16), jax.ShapeDtypeStruct((t, sw), F32),
                   jax.ShapeDtypeStruct((t, sw), BF16)],
        scratch_shapes=[pltpu.VMEM((sw // LANES, tm, LANES), F32)],
        compiler_params=_cparams(("parallel",)),
        name="mix0_in",
    )(x, w)


def _s5_prep_kernel(a_re_ref, a_im_ref, ldt_ref, bt1_ref, bt2_ref, cq1_ref, cq2_ref, cm_ref,
                    m_ref, p_ref, qt_ref, a16r_ref, a16i_ref, ppad_ref):
    ar, ai = a_re_ref[...], a_im_ref[...]
    dt = jnp.exp(ldt_ref[...])

    def cpow(k):
        mag = jnp.exp(k * (ar * dt))
        ang = k * (ai * dt)
        return mag * jnp.cos(ang), mag * jnp.sin(ang)

    abr, abi = cpow(1.0)
    nr, ni = abr - 1.0, abi
    den = ar * ar + ai * ai
    fr = (nr * ar + ni * ai) / den
    fi = (ni * ar - nr * ai) / den
    bt1, bt2 = bt1_ref[...], bt2_ref[...]
    bb = fr * bt1 + fi * bt2
    bbsw = fr * bt2 - fi * bt1
    group_shift = S5_GROUP.bit_length() - 1
    step = jnp.right_shift(lax.broadcasted_iota(jnp.int32, (S5_ROW, 1), 0), group_shift).astype(F32)
    pwr, pwi = cpow(float(S5_CHUNK - 1) - step)
    p = pwr * bb + pwi * bbsw
    p_ref[...] = p.astype(BF16)
    qwr, qwi = cpow(step + 1.0)
    qt_ref[...] = (qwr * cq1_ref[...] + qwi * cq2_ref[...]).astype(BF16)
    a16r_ref[...], a16i_ref[...] = cpow(float(S5_CHUNK))

    ppad_ref[0:S5_ROW, :] = p
    ppad_ref[S5_ROW:2 * S5_ROW, :] = jnp.zeros((S5_ROW, 2 * S5_STATE), F32)
    cm = cm_ref[...]
    col_step = jnp.right_shift(lax.broadcasted_iota(jnp.int32, (2 * S5_STATE, S5_ROW), 1), group_shift)
    acc = jnp.zeros((S5_ROW, S5_ROW), F32)
    for t in range(S5_CHUNK):
        off = S5_GROUP * (S5_CHUNK - 1 - t)
        w_t = ppad_ref[off:off + S5_ROW, :].astype(BF16)
        c_t = jnp.where(col_step == t, cm, 0.0).astype(BF16)
        acc = acc + _dot(w_t, c_t)
    m_ref[...] = acc.astype(BF16)


def s5_prep(a_re, a_im, log_dt, b_re, b_im, c_re, c_im):
    g, n, r = S5_GROUPS, S5_STATE, S5_ROW
    dup = lambda v: jnp.concatenate([v, v], axis=-1).reshape(g, 1, 2 * n)
    ldt = jnp.broadcast_to(log_dt.reshape(g, 1, 1), (g, 1, 2 * n))
    btr = jnp.tile(jnp.swapaxes(b_re, 1, 2), (1, S5_CHUNK, 1))
    bti = jnp.tile(jnp.swapaxes(b_im, 1, 2), (1, S5_CHUNK, 1))
    bt1 = jnp.concatenate([btr, bti], axis=-1)
    bt2 = jnp.concatenate([-bti, btr], axis=-1)
    cqr = jnp.tile(c_re, (1, S5_CHUNK, 1))
    cqi = jnp.tile(c_im, (1, S5_CHUNK, 1))
    cq1 = jnp.concatenate([cqr, -cqi], axis=-1)
    cq2 = jnp.concatenate([-cqi, -cqr], axis=-1)
    cm = jnp.concatenate([jnp.tile(jnp.swapaxes(c_re, 1, 2), (1, 1, S5_CHUNK)),
                          jnp.tile(-jnp.swapaxes(c_im, 1, 2), (1, 1, S5_CHUNK))], axis=1)
    grp = lambda *shape: pl.BlockSpec((None,) + shape, lambda i: (i, 0, 0))
    return pl.pallas_call(
        _s5_prep_kernel,
        grid=(g,),
        in_specs=[grp(1, 2 * n), grp(1, 2 * n), grp(1, 2 * n), grp(r, 2 * n), grp(r, 2 * n),
                  grp(r, 2 * n), grp(r, 2 * n), grp(2 * n, r)],
        out_specs=[grp(r, r), grp(r, 2 * n), grp(r, 2 * n), grp(1, 2 * n), grp(1, 2 * n)],
        out_shape=[jax.ShapeDtypeStruct((g, r, r), BF16), jax.ShapeDtypeStruct((g, r, 2 * n), BF16),
                   jax.ShapeDtypeStruct((g, r, 2 * n), BF16),
                   jax.ShapeDtypeStruct((g, 1, 2 * n), F32), jax.ShapeDtypeStruct((g, 1, 2 * n), F32)],
        scratch_shapes=[pltpu.VMEM((2 * r, 2 * n), F32)],
        compiler_params=_cparams(("parallel",)),
        name="s5_prep",
    )(dup(a_re), dup(a_im), ldt, bt1, bt2, cq1, cq2, cm)


S5_SLABS = 2 * S5_CHUNK
S5_HALF = (S5_GROUPS // 2) * S5_STATE
S5_CARRY_LANES = LANES


def s5_tiles(m, p, qt, a16r, a16i):
    gh = S5_GROUPS // 2
    eye = jnp.eye(gh, dtype=jnp.bool_)
    zero = jnp.zeros((), BF16)
    p6 = p.reshape(2, gh, S5_CHUNK, S5_GROUP, 2, S5_STATE)
    pt = jnp.where(eye[None, :, None, None, None, :, None], p6[:, :, :, :, :, None, :], zero)
    pt = jnp.transpose(pt, (0, 2, 1, 3, 4, 5, 6)).reshape(2, S5_CHUNK * S5_ROW, 2 * S5_HALF)
    q6 = qt.reshape(2, gh, S5_CHUNK, S5_GROUP, 2, S5_STATE)
    qq = jnp.where(eye[None, :, None, None, None, None, :], q6[..., None], zero)
    qq = jnp.transpose(qq, (0, 2, 4, 1, 5, 6, 3)).reshape(2, S5_CHUNK, 2 * S5_HALF, S5_ROW)
    m0 = m.reshape(2, gh, S5_CHUNK, S5_GROUP, S5_CHUNK, S5_GROUP)[:, :, 0]
    kt = jnp.where(eye[None, :, None, None, :, None], m0[:, :, :, :, None, :], zero)
    kt = jnp.transpose(kt, (0, 3, 1, 2, 4, 5))[:, ::-1]
    kt = kt.reshape(2, S5_CHUNK * S5_ROW, S5_ROW)
    ar = a16r[:, 0, :S5_STATE].reshape(1, S5_GROUPS * S5_STATE)
    ai = a16i[:, 0, :S5_STATE].reshape(1, S5_GROUPS * S5_STATE)
    return kt, pt, qq, ar, ai


def _s5_state_kernel(us_ref, p_ref, sre_ref, sim_ref):
    steps = jnp.concatenate([us_ref[s] for s in range(S5_CHUNK)], axis=1)
    part = _dot(steps, p_ref[...])
    sre_ref[...] = part[:, :S5_HALF]
    sim_ref[...] = part[:, S5_HALF:]


def _s5_carry_kernel(sre_ref, sim_ref, h0re_ref, h0im_ref, ar_ref, ai_ref,
                     hre_ref, him_ref, lre_ref, lim_ref, *, bn, nj):
    ar, ai = ar_ref[...], ai_ref[...]

    def body(j, carry):
        hr, hi = carry
        rows = pl.ds(j, bn, stride=nj)
        hre_ref[rows, :] = hr
        him_ref[rows, :] = hi
        return ar * hr - ai * hi + sre_ref[rows, :], ar * hi + ai * hr + sim_ref[rows, :]

    hr, hi = lax.fori_loop(0, nj, body, (h0re_ref[...], h0im_ref[...]))
    lre_ref[...] = hr
    lim_ref[...] = hi


def _s5_out_kernel(us_ref, k_ref, hre_ref, him_ref, q_ref, y_ref):
    hin = jnp.concatenate([hre_ref[...].astype(BF16), him_ref[...].astype(BF16)], axis=1)
    for t in range(S5_CHUNK):
        steps = jnp.concatenate([us_ref[s] for s in range(t + 1)], axis=1)
        lags = k_ref[(S5_CHUNK - 1 - t) * S5_ROW:, :]
        y_ref[t] = _dot(steps, lags) + _dot(hin, q_ref[t])


def s5_mix(us, h0re, h0im, tiles, bn, nj):
    kt, pt, qq, ar, ai = tiles
    r = bn * nj
    nstate = S5_GROUPS * S5_STATE
    us4 = us.reshape(S5_CHUNK, 2, r, S5_ROW)
    half_once = lambda *shape: pl.BlockSpec((None,) + shape, lambda h, i: (h,) + (0,) * len(shape),
                                            pipeline_mode=pl.Buffered(1))
    rows = min(r, 512)
    sre, sim = pl.pallas_call(
        _s5_state_kernel,
        grid=(2, r // rows),
        in_specs=[pl.BlockSpec((S5_CHUNK, None, rows, S5_ROW), lambda h, i: (0, h, i, 0)),
                  half_once(S5_CHUNK * S5_ROW, 2 * S5_HALF)],
        out_specs=[pl.BlockSpec((rows, S5_HALF), lambda h, i: (i, h))] * 2,
        out_shape=[jax.ShapeDtypeStruct((r, nstate), F32)] * 2,
        compiler_params=_cparams(("arbitrary", "arbitrary")),
        name="s5_state",
    )(us4, pt)
    lanes = S5_CARRY_LANES
    col = lambda n: pl.BlockSpec((n, lanes), lambda i: (0, i))
    hre, him, lre, lim = pl.pallas_call(
        functools.partial(_s5_carry_kernel, bn=bn, nj=nj),
        grid=(nstate // lanes,),
        in_specs=[col(r), col(r), col(bn), col(bn), col(1), col(1)],
        out_specs=[col(r), col(r), col(bn), col(bn)],
        out_shape=[jax.ShapeDtypeStruct((r, nstate), F32)] * 2 + [jax.ShapeDtypeStruct((bn, nstate), F32)] * 2,
        compiler_params=_cparams(("parallel",)),
        name="s5_carry",
    )(sre, sim, h0re, h0im, ar, ai)
    rows = min(r, 256)
    slabs = pl.BlockSpec((S5_CHUNK, None, rows, S5_ROW), lambda h, i: (0, h, i, 0))
    y = pl.pallas_call(
        _s5_out_kernel,
        grid=(2, r // rows),
        in_specs=[slabs, half_once(S5_CHUNK * S5_ROW, S5_ROW),
                  pl.BlockSpec((rows, S5_HALF), lambda h, i: (i, h)),
                  pl.BlockSpec((rows, S5_HALF), lambda h, i: (i, h)),
                  half_once(S5_CHUNK, 2 * S5_HALF, S5_ROW)],
        out_specs=slabs,
        out_shape=jax.ShapeDtypeStruct((S5_CHUNK, 2, r, S5_ROW), F32),
        compiler_params=_cparams(("arbitrary", "arbitrary")),
        name="s5_out",
    )(us4, kt, hre, him, qq)
    return y.reshape(S5_SLABS, r, S5_ROW), lre, lim


SB_TK = 256
LOG2E = math.log2(math.e)
SB_SKIP_LOG2 = 152.0


def _sb_block(q2, k, v, ones, carry, mask):
    half = q2.shape[0] // 2
    halves = (slice(0, half), slice(half, 2 * half))
    zs = [_dot_nt(q2[h], k) for h in halves]
    tails = [jnp.log2(1.0 + jnp.exp2(-jnp.abs(z))) for z in zs]
    sps = [jnp.maximum(z, 0.0) + t for z, t in zip(zs, tails)]
    if mask is not None:
        sps = [jnp.where(mask[h], sp, 0.0) for sp, h in zip(sps, halves)]
    his = [sp.astype(BF16) for sp in sps]
    los = [(sp - hi.astype(F32)).astype(BF16) for sp, hi in zip(sps, his)]
    sums = [_dot(jnp.concatenate([hi, lo], axis=0), ones) for hi, lo in zip(his, los)]
    outs = []
    for i, h in enumerate(halves):
        suffix = sums[i][:half] + sums[i][half:]
        logw = jnp.minimum(zs[i], 0.0) - tails[i] - suffix
        if carry is not None:
            logw = logw - carry[h]
        w = jnp.exp2(logw)
        if mask is not None:
            w = jnp.where(mask[h], w, 0.0)
        outs.append(_dot(w.astype(BF16), v))
    mass = [jnp.sum(sp, axis=-1, keepdims=True) for sp in sps]
    return jnp.concatenate(outs, axis=0), jnp.concatenate(mass, axis=0)


def _sb_kernel(q_ref, kd_ref, vd_ref, kp_ref, vp_ref, ud_ref, up_ref, o_ref, *, tq, n_past):
    qi = pl.program_id(2)
    q = q_ref[...]
    kd = kd_ref[...].astype(BF16)
    vd = vd_ref[...].astype(BF16)
    lane_lo = lax.broadcasted_iota(jnp.int32, (tq, 2 * SB_HEAD_DIM), 1) < SB_HEAD_DIM
    row = lax.broadcasted_iota(jnp.int32, (2 * tq, tq), 0)
    col = lax.broadcasted_iota(jnp.int32, (2 * tq, tq), 1)
    earlier = col < jnp.where(row >= tq, row - tq, row)
    blocks = n_past(qi)
    zero = jnp.zeros_like(q)
    q2 = jnp.concatenate([jnp.where(lane_lo, q, zero), jnp.where(lane_lo, zero, q)], axis=0)
    acc0, mass0 = _sb_block(q2, kd, vd, ud_ref[...], None, earlier)

    def cond(s):
        return jnp.logical_and(s[0] < blocks, s[1] < SB_SKIP_LOG2)

    def body(s):
        i, _, acc, carry = s
        rows = pl.ds(pl.multiple_of((blocks - 1 - i) * SB_TK, SB_TK), SB_TK)
        k = kp_ref[rows, :].astype(BF16)
        v = vp_ref[rows, :].astype(BF16)
        d, m = _sb_block(q2, k, v, up_ref[...], carry, None)
        carry = carry + m
        return i + 1, jnp.min(carry), acc + d, carry

    s = lax.while_loop(cond, body, (jnp.int32(0), jnp.min(mass0), acc0, mass0))
    o_ref[...] = jnp.where(lane_lo, s[2][:tq], s[2][tq:])


def _strict_lower_ones(n):
    r = lax.broadcasted_iota(jnp.int32, (n, n), 0)
    c = lax.broadcasted_iota(jnp.int32, (n, n), 1)
    return (r > c).astype(BF16)


def stick_breaking(q, k_new, v_new, k_past, v_past, bn, lq, tq, past_len):
    lp = k_past.shape[1]
    nq = lq // tq
    lanes = 2 * SB_HEAD_DIM
    if past_len is None:
        assert tq % SB_TK == 0
        n_past = lambda qi: qi * (tq // SB_TK)
    else:
        assert past_len == lp and lp % SB_TK == 0
        n_past = lambda qi: lp // SB_TK
    return pl.pallas_call(
        functools.partial(_sb_kernel, tq=tq, n_past=n_past),
        grid=(bn, SB_HEADS // 2, nq),
        in_specs=[pl.BlockSpec((None, tq, lanes), lambda b, h, i: (b, i, h)),
                  pl.BlockSpec((None, tq, lanes), lambda b, h, i: (b, i, h)),
                  pl.BlockSpec((None, tq, lanes), lambda b, h, i: (b, i, h)),
                  pl.BlockSpec((None, lp, lanes), lambda b, h, i: (b, 0, h)),
                  pl.BlockSpec((None, lp, lanes), lambda b, h, i: (b, 0, h)),
                  _full((tq, tq)), _full((SB_TK, SB_TK))],
        out_specs=pl.BlockSpec((None, tq, lanes), lambda b, h, i: (b, i, h)),
        out_shape=jax.ShapeDtypeStruct((bn, lq, SB_WIDTH), F32),
        compiler_params=_cparams(("parallel", "parallel", "arbitrary")),
        name="stick_breaking",
    )(q, k_new, v_new, k_past, v_past, _strict_lower_ones(tq), _strict_lower_ones(SB_TK))


def _gelu(x):
    return 0.5 * x * (1.0 + jnp.tanh(math.sqrt(2.0 / math.pi) * (x + 0.044715 * (x * x * x))))


def _mix0_out_kernel(x_ref, ys_ref, u_ref, sb_ref, d_ref, wglu_ref, bglu_ref, wo1_ref, wo2_ref, g_ref, b_ref,
                     o_ref, y_scr):
    x = x_ref[...]
    chunks = y_scr.shape[1] // S5_CHUNK
    per_slab = S5_ROW // LANES
    for s in range(S5_CHUNK):
        for half in range(2):
            for c in range(per_slab):
                y_scr[half * per_slab + c, pl.ds(s, chunks, stride=S5_CHUNK), :] = (
                    ys_ref[2 * s + half, :, c * LANES:(c + 1) * LANES])
    ys = jnp.concatenate([y_scr[c] for c in range(S5_WIDTH // LANES)], axis=1)
    gl = _gelu(ys + d_ref[...] * u_ref[...])
    gate = _sigmoid(_dot(gl.astype(BF16), wglu_ref[...]) + bglu_ref[...])
    m = _dot((gl * gate).astype(BF16), wo1_ref[...]) + _dot(sb_ref[...].astype(BF16), wo2_ref[...])
    o_ref[...] = _layer_norm(ALPHA * x + m, g_ref[...], b_ref[...])


def mix0_out(x, ys, u, sb, d, wglu, bglu, wo1, wo2, g, b):
    t = x.shape[0]
    tm = _row_tile(t)
    rowblk = lambda n: pl.BlockSpec((tm, n), lambda i: (i, 0))
    return pl.pallas_call(
        _mix0_out_kernel,
        grid=(t // tm,),
        in_specs=[rowblk(D_MODEL),
                  pl.BlockSpec((S5_SLABS, tm // S5_CHUNK, S5_ROW), lambda i: (0, i, 0)),
                  rowblk(S5_WIDTH), rowblk(SB_WIDTH),
                  _full((1, S5_WIDTH)), _full((S5_WIDTH, S5_WIDTH)), _full((1, S5_WIDTH)),
                  _full((S5_WIDTH, D_MODEL)), _full((SB_WIDTH, D_MODEL)),
                  _full((1, D_MODEL)), _full((1, D_MODEL))],
        out_specs=rowblk(D_MODEL),
        out_shape=jax.ShapeDtypeStruct((t, D_MODEL), F32),
        scratch_shapes=[pltpu.VMEM((S5_WIDTH // LANES, tm, LANES), F32)],
        compiler_params=_cparams(("parallel",)),
        name="mix0_out",
    )(x, ys, u, sb, d.reshape(1, -1), wglu, bglu.reshape(1, -1), wo1, wo2, g.reshape(1, -1), b.reshape(1, -1))


def _ssd_kernel(xbc_ref, z_ref, dt_ref, dtt_ref, conv0_ref, h0_ref, cw_ref, cb_ref, dtb_ref, dtbt_ref,
                alog_ref, alogt_ref, dsk_ref, ng_ref, tril_ref, triu_ref,
                y_ref, hl_ref, ext_ref, h_ref, *, lc, nc):
    c = pl.program_id(1)

    @pl.when(c == 0)
    def _():
        ext_ref[0:CONV_PAD, :] = conv0_ref[...]
        h_ref[...] = h0_ref[...]

    ext_ref[CONV_PAD:CONV_PAD + lc, :] = xbc_ref[...]
    ext = ext_ref[...]
    conv = cb_ref[...] + ext[CONV_PAD:, :] * cw_ref[SSD_CONV - 1:SSD_CONV, :]
    for back in range(1, SSD_CONV):
        tap = SSD_CONV - 1 - back
        conv = conv + pltpu.roll(ext, back, 0)[CONV_PAD:, :] * cw_ref[tap:tap + 1, :]
    ext_ref[0:CONV_PAD, :] = ext[lc:lc + CONV_PAD, :]
    act = conv * _sigmoid(conv)

    dt = _softplus(dt_ref[...] + dtb_ref[...])
    dtt = _softplus(dtt_ref[...] + dtbt_ref[...])
    cs = _sum_left(tril_ref[...], dt * (-LOG2E * jnp.exp(alog_ref[...])))
    cst = _sum_right(dtt * (-LOG2E * jnp.exp(alogt_ref[...])), triu_ref[...])
    cs_last = cs[lc - 1:lc, :]

    row = lax.broadcasted_iota(jnp.int32, (lc, lc), 0)
    col = lax.broadcasted_iota(jnp.int32, (lc, lc), 1)
    causal = col <= row
    lanes = 2 * SSD_HEAD_DIM
    lane_lo = lax.broadcasted_iota(jnp.int32, (lc, lanes), 1) < SSD_HEAD_DIM
    pairs_per_group = SSD_PAIRS // SSD_GROUPS

    for g in range(SSD_GROUPS):
        bm = act[:, SSD_INNER + g * SSD_STATE:SSD_INNER + (g + 1) * SSD_STATE].astype(BF16)
        cm = act[:, SSD_INNER + SSD_GN + g * SSD_STATE:SSD_INNER + SSD_GN + (g + 1) * SSD_STATE].astype(BF16)
        cb = _dot_nt(cm, bm)
        gated, sq = [], jnp.zeros((lc, lanes), F32)
        for k in range(g * pairs_per_group, (g + 1) * pairs_per_group):
            sl = slice(k * lanes, (k + 1) * lanes)
            xp = act[:, sl]
            heads = (2 * k, 2 * k + 1)
            pair = lambda v: jnp.where(lane_lo[:v[0].shape[0]], v[0], v[1])
            xdt = xp * pair([dt[:, h:h + 1] for h in heads])
            xdtb = xdt.astype(BF16)
            csb = [jnp.broadcast_to(cs[:, h:h + 1], (lc, lanes)) for h in heads]
            ys = []
            for h, csh in zip(heads, csb):
                seg = csh[:, :lc] - cst[h:h + 1, :]
                w = cb * jnp.exp2(jnp.where(causal, seg, -jnp.inf))
                ys.append(_dot(w.astype(BF16), xdtb))
            y = pair(ys)
            cs_pair = pair(csb)
            last_pair = pair([cs_last[:, h:h + 1] for h in heads])
            ht = h_ref[k]
            y = y + _dot(cm, ht.astype(BF16)) * jnp.exp2(cs_pair)
            xw = (xdt * jnp.exp2(last_pair - cs_pair)).astype(BF16)
            h_ref[k] = ht * jnp.exp2(last_pair) + _dot_tn(bm, xw)
            y = y + dsk_ref[:, sl] * xp
            zz = z_ref[:, sl]
            yg = y * (zz * _sigmoid(zz))
            sq = sq + yg * yg
            gated.append(yg)
        mean_sq = jnp.sum(sq, axis=-1, keepdims=True) * (1.0 / (pairs_per_group * lanes))
        scale = lax.rsqrt(mean_sq + RMS_EPS)
        for k, yg in zip(range(g * pairs_per_group, (g + 1) * pairs_per_group), gated):
            sl = slice(k * lanes, (k + 1) * lanes)
            y_ref[:, sl] = yg * scale * ng_ref[:, sl]

    @pl.when(c == nc - 1)
    def _():
        hl_ref[...] = h_ref[...]


def ssd_mix(xbc, z, dt_raw, conv0, h0, conv_w, conv_b, dt_bias, a_log, d_skip, norm_g, bn, length, lc):
    nc = length // lc
    t = bn * length
    lanes = 2 * SSD_HEAD_DIM
    dtt = jnp.swapaxes(dt_raw.reshape(bn * nc, lc, SSD_HEADS), 1, 2)
    conv0p = jnp.pad(conv0, ((0, 0), (CONV_PAD - (SSD_CONV - 1), 0), (0, 0)))
    a_log = a_log.astype(F32)
    r = lax.broadcasted_iota(jnp.int32, (lc, lc), 0)
    cc = lax.broadcasted_iota(jnp.int32, (lc, lc), 1)
    tril = (cc <= r).astype(BF16)
    triu = (r <= cc).astype(BF16)
    tok = lambda n: pl.BlockSpec((lc, n), lambda b, c: (b * nc + c, 0))
    y, hl = pl.pallas_call(
        functools.partial(_ssd_kernel, lc=lc, nc=nc),
        grid=(bn, nc),
        in_specs=[tok(SSD_CONV_DIM), tok(SSD_INNER), tok(SSD_HEADS),
                  pl.BlockSpec((None, SSD_HEADS, lc), lambda b, c: (b * nc + c, 0, 0)),
                  pl.BlockSpec((None, CONV_PAD, SSD_CONV_DIM), lambda b, c: (b, 0, 0)),
                  pl.BlockSpec((None, SSD_PAIRS, lanes, SSD_STATE), lambda b, c: (b, 0, 0, 0)),
                  _full((SSD_CONV, SSD_CONV_DIM)), _full((1, SSD_CONV_DIM)),
                  _full((1, SSD_HEADS)), _full((SSD_HEADS, 1)), _full((1, SSD_HEADS)), _full((SSD_HEADS, 1)),
                  _full((1, SSD_INNER)), _full((1, SSD_INNER)), _full((lc, lc)), _full((lc, lc))],
        out_specs=[tok(SSD_INNER),
                   pl.BlockSpec((None, SSD_PAIRS, lanes, SSD_STATE), lambda b, c: (b, 0, 0, 0))],
        out_shape=[jax.ShapeDtypeStruct((t, SSD_INNER), F32),
                   jax.ShapeDtypeStruct((bn, SSD_PAIRS, lanes, SSD_STATE), F32)],
        scratch_shapes=[pltpu.VMEM((CONV_PAD + lc, SSD_CONV_DIM), F32),
                        pltpu.VMEM((SSD_PAIRS, lanes, SSD_STATE), F32)],
        compiler_params=_cparams(("parallel", "arbitrary")),
        name="ssd_mix",
    )(xbc, z, dt_raw, dtt, conv0p, jnp.swapaxes(h0.reshape(bn, SSD_PAIRS, lanes, SSD_STATE), 2, 3),
      conv_w, conv_b.reshape(1, -1), dt_bias.reshape(1, -1), dt_bias.reshape(-1, 1),
      a_log.reshape(1, -1), a_log.reshape(-1, 1), jnp.repeat(d_skip, SSD_HEAD_DIM).reshape(1, -1),
      norm_g.reshape(1, -1), tril, triu)
    return y, jnp.swapaxes(hl, 2, 3).reshape(bn, SSD_HEADS, SSD_HEAD_DIM, SSD_STATE)


def _out_ln_kernel(x_ref, a_ref, w_ref, g_ref, b_ref, o_ref):
    x = x_ref[...]
    m = _dot(a_ref[...].astype(BF16), w_ref[...])
    o_ref[...] = _layer_norm(ALPHA * x + m, g_ref[...], b_ref[...])


def out_ln(x, a, w, g, b):
    t = x.shape[0]
    tm = _row_tile(t)
    k = a.shape[1]
    return pl.pallas_call(
        _out_ln_kernel,
        grid=(t // tm,),
        in_specs=[pl.BlockSpec((tm, D_MODEL), lambda i: (i, 0)), pl.BlockSpec((tm, k), lambda i: (i, 0)),
                  _full((k, D_MODEL)), _full((1, D_MODEL)), _full((1, D_MODEL))],
        out_specs=pl.BlockSpec((tm, D_MODEL), lambda i: (i, 0)),
        out_shape=jax.ShapeDtypeStruct((t, D_MODEL), F32),
        compiler_params=_cparams(("parallel",)),
        name="out_ln",
    )(x, a, w, g.reshape(1, -1), b.reshape(1, -1))


def _trunk(x, bn, length, states, w):
    t = bn * length
    x = x.reshape(t, D_MODEL)
    x = ffn_ln(x, w["wg"][0][0], w["wu"][0][0], w["wd"][0][0], w["ln_g"][0, 0], w["ln_b"][0, 0])

    u, u_slabs, q_bf, k, k_bf, v, v_bf = mix0_in(x, w["mix0_in"], SB_HEAD_DIM ** -0.5 * LOG2E)
    nj = length // S5_CHUNK
    nstate = S5_GROUPS * S5_STATE
    if states is None:
        h0re = h0im = jnp.zeros((bn, nstate), F32)
    else:
        h0re = states[0][0].astype(F32).reshape(bn, nstate)
        h0im = states[1][0].astype(F32).reshape(bn, nstate)
    y_s5, h_re, h_im = s5_mix(u_slabs, h0re, h0im, w["s5_tiles"], bn, nj)
    s5_re = h_re.reshape(1, bn, S5_GROUPS, S5_STATE)
    s5_im = h_im.reshape(1, bn, S5_GROUPS, S5_STATE)

    q3 = q_bf.reshape(bn, length, SB_WIDTH)
    if states is None:
        tq = min(SB_TK, length)
        sb = stick_breaking(q3, k_bf.reshape(bn, length, SB_WIDTH), v_bf.reshape(bn, length, SB_WIDTH),
                            k_bf.reshape(bn, length, SB_WIDTH), v_bf.reshape(bn, length, SB_WIDTH),
                            bn, length, tq, None)
    else:
        past = states[2][0].shape[1]
        sb = stick_breaking(q3, k_bf.reshape(bn, length, SB_WIDTH), v_bf.reshape(bn, length, SB_WIDTH),
                            states[2][0].reshape(bn, past, SB_WIDTH), states[3][0].reshape(bn, past, SB_WIDTH),
                            bn, length, length, past)
    x = mix0_out(x, y_s5, u, sb.reshape(t, SB_WIDTH), w["s5_d"], w["s5_wglu"], w["s5_bglu"],
                 w["mix0_out"][:S5_WIDTH], w["mix0_out"][S5_WIDTH:], w["ln_g"][0, 1], w["ln_b"][0, 1])
    x = ffn_ln(x, w["wg"][0][1], w["wu"][0][1], w["wd"][0][1], w["ln_g"][0, 2], w["ln_b"][0, 2])

    x = ffn_ln(x, w["wg"][1][0], w["wu"][1][0], w["wd"][1][0], w["ln_g"][1, 0], w["ln_b"][1, 0])
    one = ((F32, 1.0),)
    z, xbc, dt_raw = proj(x, w["ssd_in"], (
        (0, SSD_INNER, one), (SSD_INNER, SSD_INNER + SSD_CONV_DIM, one),
        (SSD_INNER + SSD_CONV_DIM, SSD_INNER + SSD_CONV_DIM + SSD_HEADS, one)))
    if states is None:
        conv0 = jnp.zeros((bn, SSD_CONV - 1, SSD_CONV_DIM), F32)
        hs0 = jnp.zeros((bn, SSD_HEADS, SSD_HEAD_DIM, SSD_STATE), F32)
    else:
        conv0 = states[5][0].astype(F32)
        hs0 = states[4][0].astype(F32)
    lc = min(128, length)
    yg, h_ssd = ssd_mix(xbc, z, dt_raw, conv0, hs0, w["ssd_conv_w"], w["ssd_conv_b"], w["ssd_dt_bias"],
                        w["ssd_a_log"], w["ssd_d"], w["ssd_norm_g"], bn, length, lc)
    ext = jnp.concatenate([conv0, xbc.reshape(bn, length, SSD_CONV_DIM)[:, length - (SSD_CONV - 1):]], axis=1)
    new_conv = ext[:, ext.shape[1] - (SSD_CONV - 1):]
    x = out_ln(x, yg, w["ssd_out"], w["ln_g"][1, 1], w["ln_b"][1, 1])
    x = ffn_ln(x, w["wg"][1][1], w["wu"][1][1], w["wd"][1][1], w["ln_g"][1, 2], w["ln_b"][1, 2])

    return (x.reshape(bn, length, D_MODEL), s5_re, s5_im,
            k.reshape(1, bn, length, SB_HEADS, SB_HEAD_DIM), v.reshape(1, bn, length, SB_HEADS, SB_HEAD_DIM),
            h_ssd[None], new_conv[None])


def kernel(x_prompt, x_sample, state_s5_re, state_s5_im, cache_sb_k, cache_sb_v, state_ssd, state_conv, ln_g, ln_b, ffn_w_gate, ffn_w_up, ffn_w_down, mix0_w_in, s5_a_re, s5_a_im, s5_log_dt, s5_b_re, s5_b_im, s5_c_re, s5_c_im, s5_d, s5_w_glu, s5_b_glu, mix0_w_out, ssd_w_in, ssd_conv_w, ssd_conv_b, ssd_dt_bias, ssd_a_log, ssd_d, ssd_norm_g, ssd_w_out):
    bf = lambda a: a.astype(BF16)
    w = {
        "wg": [[bf(ffn_w_gate[l, s]) for s in range(2)] for l in range(DEPTH)],
        "wu": [[bf(ffn_w_up[l, s]) for s in range(2)] for l in range(DEPTH)],
        "wd": [[bf(ffn_w_down[l, s]) for s in range(2)] for l in range(DEPTH)],
        "ln_g": ln_g, "ln_b": ln_b,
        "mix0_in": bf(mix0_w_in[0]),
        "s5_tiles": s5_tiles(*s5_prep(s5_a_re[0], s5_a_im[0], s5_log_dt[0], s5_b_re[0], s5_b_im[0],
                                      s5_c_re[0], s5_c_im[0])),
        "s5_d": s5_d[0], "s5_wglu": bf(s5_w_glu[0]), "s5_bglu": s5_b_glu[0], "mix0_out": bf(mix0_w_out[0]),
        "ssd_in": bf(ssd_w_in[0]), "ssd_conv_w": ssd_conv_w[0], "ssd_conv_b": ssd_conv_b[0],
        "ssd_dt_bias": ssd_dt_bias[0], "ssd_a_log": ssd_a_log[0], "ssd_d": ssd_d[0],
        "ssd_norm_g": ssd_norm_g[0], "ssd_out": bf(ssd_w_out[0]),
    }
    bp, lp = x_prompt.shape[0], x_prompt.shape[1]
    bs, ls = x_sample.shape[0], x_sample.shape[1]
    yp, s5rp, s5ip, kp, vp, ssdp, convp = _trunk(x_prompt, bp, lp, None, w)
    ys, s5rs, s5is, ks, vs, ssds, convs = _trunk(
        x_sample, bs, ls, (state_s5_re, state_s5_im, cache_sb_k, cache_sb_v, state_ssd, state_conv), w)
    return (yp, ys, s5rp, s5ip, kp, vp, ssdp, convp, s5rs, s5is, ks, vs, ssds, convs)
```

```python
import functools
import math

import jax
import jax.numpy as jnp
from jax import lax
from jax.experimental import pallas as pl
from jax.experimental.pallas import tpu as pltpu

F32 = jnp.float32
BF16 = jnp.bfloat16

D_MODEL = 1024
DEPTH = 2
ALPHA = (2.0 * DEPTH) ** 0.25
LN_EPS = 1e-5
RMS_EPS = 1e-5
D_FF = 2816

S5_WIDTH = 512
S5_GROUP = 16
S5_GROUPS = 32
S5_STATE = 64
S5_CHUNK = 16
S5_ROW = S5_CHUNK * S5_GROUP
SB_HEADS = 8
SB_HEAD_DIM = 64
SB_WIDTH = 512
MIX0_IN = 2048

SSD_INNER = 2048
SSD_HEAD_DIM = 64
SSD_HEADS = 32
SSD_GROUPS = 4
SSD_STATE = 128
SSD_CONV = 4
SSD_GN = 512
SSD_CONV_DIM = 3072
SSD_PAIRS = SSD_HEADS // 2
CONV_PAD = 8

LANES = 128
VMEM_LIMIT = 56 * 1024 * 1024


def _cparams(sem):
    return pltpu.CompilerParams(dimension_semantics=sem, vmem_limit_bytes=VMEM_LIMIT)


def _sigmoid(x):
    return 1.0 / (1.0 + jnp.exp2(x * -math.log2(math.e)))


def _softplus(x):
    return jnp.maximum(x, 0.0) + jnp.log1p(jnp.exp(-jnp.abs(x)))


def _layer_norm(y, g, b):
    mu = jnp.mean(y, axis=-1, keepdims=True)
    d = y - mu
    var = jnp.mean(d * d, axis=-1, keepdims=True)
    return d * lax.rsqrt(var + LN_EPS) * g + b


def _dot(a, b):
    return jnp.dot(a, b, preferred_element_type=F32)


def _dot_nt(a, b):
    return lax.dot_general(a, b, (((1,), (1,)), ((), ())), preferred_element_type=F32)


def _dot_tn(a, b):
    return lax.dot_general(a, b, (((0,), (0,)), ((), ())), preferred_element_type=F32)


def _split3(x):
    hi = x.astype(BF16)
    r = x - hi.astype(F32)
    mid = r.astype(BF16)
    lo = (r - mid.astype(F32)).astype(BF16)
    return hi, mid, lo


def _sum_right(x, ones_mat):
    hi, mid, lo = _split3(x)
    return _dot(hi, ones_mat) + _dot(mid, ones_mat) + _dot(lo, ones_mat)


def _sum_left(ones_mat, x):
    hi, mid, lo = _split3(x)
    return _dot(ones_mat, hi) + _dot(ones_mat, mid) + _dot(ones_mat, lo)


def _row_tile(t):
    for tm in (512, 256, 128, 64, 32, 16, 8):
        if t % tm == 0:
            return tm
    raise ValueError(f"token count {t} is not a multiple of 8")


def _full(shape):
    return pl.BlockSpec(shape, lambda *_: (0,) * len(shape), pipeline_mode=pl.Buffered(1))


FFN_TF = 1408


def _ffn_kernel(x_ref, wg_ref, wu_ref, wd_ref, g_ref, b_ref, o_ref):
    x = x_ref[...]
    xb = x.astype(BF16)
    acc = None
    for c in range(D_FF // FFN_TF):
        sl = slice(c * FFN_TF, (c + 1) * FFN_TF)
        gate = _dot(xb, wg_ref[:, sl])
        up = _dot(xb, wu_ref[:, sl])
        h = (gate * _sigmoid(gate) * up).astype(BF16)
        part = _dot(h, wd_ref[sl, :])
        acc = part if acc is None else acc + part
    o_ref[...] = _layer_norm(ALPHA * x + 0.5 * acc, g_ref[...], b_ref[...])


def ffn_ln(x, wg, wu, wd, g, b):
    t = x.shape[0]
    tm = _row_tile(t)
    return pl.pallas_call(
        _ffn_kernel,
        grid=(t // tm,),
        in_specs=[pl.BlockSpec((tm, D_MODEL), lambda i: (i, 0)),
                  _full((D_MODEL, D_FF)), _full((D_MODEL, D_FF)), _full((D_FF, D_MODEL)),
                  _full((1, D_MODEL)), _full((1, D_MODEL))],
        out_specs=pl.BlockSpec((tm, D_MODEL), lambda i: (i, 0)),
        out_shape=jax.ShapeDtypeStruct((t, D_MODEL), F32),
        compiler_params=_cparams(("parallel",)),
        name="ffn_ln",
    )(x, wg, wu, wd, g.reshape(1, D_MODEL), b.reshape(1, D_MODEL))


def _proj_kernel(x_ref, w_ref, *o_refs, plan):
    xb = x_ref[...].astype(BF16)
    refs = iter(o_refs)
    for lo, hi, outs in plan:
        y = _dot(xb, w_ref[:, lo:hi])
        for _, scale in outs:
            o_ref = next(refs)
            o_ref[...] = (y if scale == 1.0 else y * scale).astype(o_ref.dtype)


def proj(x, w, plan):
    t = x.shape[0]
    tm = _row_tile(t)
    n = w.shape[1]
    flat = [(hi - lo, dt) for lo, hi, outs in plan for dt, _ in outs]
    return pl.pallas_call(
        functools.partial(_proj_kernel, plan=plan),
        grid=(t // tm,),
        in_specs=[pl.BlockSpec((tm, D_MODEL), lambda i: (i, 0)), _full((D_MODEL, n))],
        out_specs=[pl.BlockSpec((tm, width), lambda i: (i, 0)) for width, _ in flat],
        out_shape=[jax.ShapeDtypeStruct((t, width), dt) for width, dt in flat],
        compiler_params=_cparams(("parallel",)),
        name="proj",
    )(x, w)


def _mix0_in_kernel(x_ref, w_ref, u_ref, us_ref, q_ref, k_ref, kb_ref, v_ref, vb_ref, u_scr, *, q_scale):
    xb = x_ref[...].astype(BF16)
    sw = S5_WIDTH
    u = _dot(xb, w_ref[:, 0:sw])
    u_ref[...] = u
    for c in range(sw // LANES):
        u_scr[c] = u[:, c * LANES:(c + 1) * LANES]
    chunks = u.shape[0] // S5_CHUNK
    per_slab = S5_ROW // LANES
    for s in range(S5_CHUNK):
        for half in range(2):
            for c in range(per_slab):
                rows = u_scr[half * per_slab + c, pl.ds(s, chunks, stride=S5_CHUNK), :]
                us_ref[2 * s + half, :, c * LANES:(c + 1) * LANES] = rows.astype(BF16)
    q_ref[...] = (_dot(xb, w_ref[:, sw:2 * sw]) * q_scale).astype(BF16)
    k = _dot(xb, w_ref[:, 2 * sw:3 * sw])
    k_ref[...] = k
    kb_ref[...] = k.astype(BF16)
    v = _dot(xb, w_ref[:, 3 * sw:4 * sw])
    v_ref[...] = v
    vb_ref[...] = v.astype(BF16)


def mix0_in(x, w, q_scale):
    t = x.shape[0]
    tm = _row_tile(t)
    sw = S5_WIDTH
    tok = pl.BlockSpec((tm, sw), lambda i: (i, 0))
    return pl.pallas_call(
        functools.partial(_mix0_in_kernel, q_scale=q_scale),
        grid=(t // tm,),
        in_specs=[pl.BlockSpec((tm, D_MODEL), lambda i: (i, 0)), _full((D_MODEL, MIX0_IN))],
        out_specs=[tok, pl.BlockSpec((S5_SLABS, tm // S5_CHUNK, S5_ROW), lambda i: (0, i, 0)),
                   tok, tok, tok, tok, tok],
        out_shape=[jax.ShapeDtypeStruct((t, sw), F32),
                   jax.ShapeDtypeStruct((S5_SLABS, t // S5_CHUNK, S5_ROW), BF16),
                   jax.ShapeDtypeStruct((t, sw), BF16), jax.ShapeDtypeStruct((t, sw), F32),
                   jax.ShapeDtypeStruct((t, sw), BF16), jax.ShapeDtypeStruct((t, sw), F32),
                   jax.ShapeDtypeStruct((t, sw), BF16)],
        scratch_shapes=[pltpu.VMEM((sw // LANES, tm, LANES), F32)],
        compiler_params=_cparams(("parallel",)),
        name="mix0_in",
    )(x, w)


S5_SLABS = 2 * S5_CHUNK
S5_GH = S5_GROUPS // 2
S5_HALF = S5_GH * S5_STATE
S5_CARRY_LANES = LANES


def _s5_tiles_kernel(a_re_ref, a_im_ref, ldt_ref, ar_rows_ref, ai_rows_ref, ldt_rows_ref, b1_ref, b2_ref,
                     bt_re_ref, bt_im_ref, cq_re_ref, cq_im_ref, cm_ref,
                     p_ref, qt_ref, k_ref, a16r_ref, a16i_ref):
    s = pl.program_id(1).astype(F32)

    def cpow(m, ar, ai, dt):
        mag = jnp.exp(m * (ar * dt))
        ang = m * (ai * dt)
        return mag * jnp.cos(ang), mag * jnp.sin(ang)

    def zoh(ar, ai, dt):
        abr, abi = cpow(1.0, ar, ai, dt)
        nr, ni, den = abr - 1.0, abi, ar * ar + ai * ai
        return (nr * ar + ni * ai) / den, (ni * ar - nr * ai) / den

    ar, ai, dt = a_re_ref[...], a_im_ref[...], jnp.exp(ldt_ref[...])
    fr, fi = zoh(ar, ai, dt)
    pwr, pwi = cpow(float(S5_CHUNK - 1) - s, ar, ai, dt)
    cr, ci = pwr * fr - pwi * fi, pwr * fi + pwi * fr
    shape = (S5_ROW, S5_HALF)
    same = (jnp.right_shift(lax.broadcasted_iota(jnp.int32, shape, 0), S5_GROUP.bit_length() - 1)
            == jnp.right_shift(lax.broadcasted_iota(jnp.int32, shape, 1), S5_STATE.bit_length() - 1))
    btr, bti = bt_re_ref[...], bt_im_ref[...]
    p_ref[:, :S5_HALF] = jnp.where(same, cr * btr - ci * bti, 0.0).astype(BF16)
    p_ref[:, S5_HALF:] = jnp.where(same, cr * bti + ci * btr, 0.0).astype(BF16)
    qwr, qwi = cpow(s + 1.0, ar, ai, dt)
    cqr, cqi = cq_re_ref[...], cq_im_ref[...]
    qt_ref[:, :S5_HALF] = jnp.where(same, cqr * qwr - cqi * qwi, 0.0).astype(BF16)
    qt_ref[:, S5_HALF:] = jnp.where(same, -(cqr * qwi + cqi * qwr), 0.0).astype(BF16)
    a16r_ref[...], a16i_ref[...] = cpow(float(S5_CHUNK), ar, ai, dt)

    ar2, ai2, dt2 = ar_rows_ref[...], ai_rows_ref[...], jnp.exp(ldt_rows_ref[...])
    fr2, fi2 = zoh(ar2, ai2, dt2)
    b1, b2 = b1_ref[...], b2_ref[...]
    bb, bbsw = fr2 * b1 + fi2 * b2, fr2 * b2 - fi2 * b1
    lwr, lwi = cpow(s, ar2, ai2, dt2)
    w = (lwr * bb + lwi * bbsw).astype(BF16)
    lag = _dot(w, cm_ref[...].astype(BF16))
    kshape = (S5_ROW, S5_ROW)
    gsh = S5_GROUP.bit_length() - 1
    same_k = (jnp.right_shift(lax.broadcasted_iota(jnp.int32, kshape, 0), gsh)
              == jnp.right_shift(lax.broadcasted_iota(jnp.int32, kshape, 1), gsh))
    k_ref[...] = jnp.where(same_k, lag, 0.0).astype(BF16)


def s5_tiles(a_re, a_im, log_dt, b_re, b_im, c_re, c_im):
    gh, n, pch = S5_GH, S5_STATE, S5_GROUP
    halves = lambda v: v.reshape((2, gh) + v.shape[1:])
    lane = lambda v: halves(v).reshape(2, 1, S5_HALF)
    rows = lambda v: jnp.tile(jnp.repeat(halves(v), pch, axis=1), (1, 1, 2))
    ldt = jnp.broadcast_to(log_dt[:, None], (S5_GROUPS, n))
    bt = lambda b: halves(jnp.swapaxes(b, 1, 2)).reshape(2, S5_ROW, n)
    btr, bti = bt(b_re), bt(b_im)
    b1 = jnp.concatenate([btr, bti], axis=-1)
    b2 = jnp.concatenate([-bti, btr], axis=-1)
    wide = lambda v: jnp.tile(v, (1, 1, gh))
    cq = lambda c: jnp.tile(jnp.transpose(halves(c), (0, 2, 1, 3)).reshape(2, pch, S5_HALF), (1, gh, 1))
    cmt = lambda c: jnp.transpose(halves(c), (0, 3, 1, 2)).reshape(2, n, S5_ROW)
    cm = jnp.concatenate([cmt(c_re), -cmt(c_im)], axis=1)
    half = lambda *shape: pl.BlockSpec((None,) + shape, lambda h, s: (h,) + (0,) * len(shape))
    pt, qt, kt, a16r, a16i = pl.pallas_call(
        _s5_tiles_kernel,
        grid=(2, S5_CHUNK),
        in_specs=[half(1, S5_HALF)] * 3 + [half(S5_ROW, 2 * n)] * 5 + [half(S5_ROW, S5_HALF)] * 4
                 + [half(2 * n, S5_ROW)],
        out_specs=[pl.BlockSpec((None, S5_ROW, 2 * S5_HALF), lambda h, s: (h, s, 0)),
                   pl.BlockSpec((None, None, S5_ROW, 2 * S5_HALF), lambda h, s: (h, s, 0, 0)),
                   pl.BlockSpec((None, S5_ROW, S5_ROW), lambda h, s: (h, S5_CHUNK - 1 - s, 0)),
                   half(1, S5_HALF), half(1, S5_HALF)],
        out_shape=[jax.ShapeDtypeStruct((2, S5_CHUNK * S5_ROW, 2 * S5_HALF), BF16),
                   jax.ShapeDtypeStruct((2, S5_CHUNK, S5_ROW, 2 * S5_HALF), BF16),
                   jax.ShapeDtypeStruct((2, S5_CHUNK * S5_ROW, S5_ROW), BF16),
                   jax.ShapeDtypeStruct((2, 1, S5_HALF), F32), jax.ShapeDtypeStruct((2, 1, S5_HALF), F32)],
        compiler_params=_cparams(("arbitrary", "arbitrary")),
        name="s5_tiles",
    )(lane(a_re), lane(a_im), lane(ldt), rows(a_re), rows(a_im), rows(ldt), b1, b2,
      wide(btr), wide(bti), cq(c_re), cq(c_im), cm)
    nstate = S5_GROUPS * S5_STATE
    return kt, pt, qt, a16r.reshape(1, nstate), a16i.reshape(1, nstate)


def _s5_state_kernel(us_ref, p_ref, sre_ref, sim_ref):
    steps = jnp.concatenate([us_ref[s] for s in range(S5_CHUNK)], axis=1)
    part = _dot(steps, p_ref[...])
    sre_ref[...] = part[:, :S5_HALF]
    sim_ref[...] = part[:, S5_HALF:]


def _s5_carry_kernel(sre_ref, sim_ref, h0re_ref, h0im_ref, ar_ref, ai_ref,
                     hre_ref, him_ref, lre_ref, lim_ref, *, bn, nj):
    ar, ai = ar_ref[...], ai_ref[...]

    def body(j, carry):
        hr, hi = carry
        rows = pl.ds(j, bn, stride=nj)
        hre_ref[rows, :] = hr
        him_ref[rows, :] = hi
        return ar * hr - ai * hi + sre_ref[rows, :], ar * hi + ai * hr + sim_ref[rows, :]

    hr, hi = lax.fori_loop(0, nj, body, (h0re_ref[...], h0im_ref[...]))
    lre_ref[...] = hr
    lim_ref[...] = hi


def _s5_out_kernel(us_ref, k_ref, hre_ref, him_ref, q_ref, y_ref):
    hin = jnp.concatenate([hre_ref[...].astype(BF16), him_ref[...].astype(BF16)], axis=1)
    for t in range(S5_CHUNK):
        steps = jnp.concatenate([us_ref[s] for s in range(t + 1)], axis=1)
        lags = k_ref[(S5_CHUNK - 1 - t) * S5_ROW:, :]
        y_ref[t] = _dot(steps, lags) + _dot_nt(hin, q_ref[t])


def s5_mix(us, h0re, h0im, tiles, bn, nj):
    kt, pt, qq, ar, ai = tiles
    r = bn * nj
    nstate = S5_GROUPS * S5_STATE
    us4 = us.reshape(S5_CHUNK, 2, r, S5_ROW)
    half_once = lambda *shape: pl.BlockSpec((None,) + shape, lambda h, i: (h,) + (0,) * len(shape),
                                            pipeline_mode=pl.Buffered(1))
    rows = min(r, 512)
    sre, sim = pl.pallas_call(
        _s5_state_kernel,
        grid=(2, r // rows),
        in_specs=[pl.BlockSpec((S5_CHUNK, None, rows, S5_ROW), lambda h, i: (0, h, i, 0)),
                  half_once(S5_CHUNK * S5_ROW, 2 * S5_HALF)],
        out_specs=[pl.BlockSpec((rows, S5_HALF), lambda h, i: (i, h))] * 2,
        out_shape=[jax.ShapeDtypeStruct((r, nstate), F32)] * 2,
        compiler_params=_cparams(("arbitrary", "arbitrary")),
        name="s5_state",
    )(us4, pt)
    lanes = S5_CARRY_LANES
    col = lambda n: pl.BlockSpec((n, lanes), lambda i: (0, i))
    hre, him, lre, lim = pl.pallas_call(
        functools.partial(_s5_carry_kernel, bn=bn, nj=nj),
        grid=(nstate // lanes,),
        in_specs=[col(r), col(r), col(bn), col(bn), col(1), col(1)],
        out_specs=[col(r), col(r), col(bn), col(bn)],
        out_shape=[jax.ShapeDtypeStruct((r, nstate), F32)] * 2 + [jax.ShapeDtypeStruct((bn, nstate), F32)] * 2,
        compiler_params=_cparams(("parallel",)),
        name="s5_carry",
    )(sre, sim, h0re, h0im, ar, ai)
    rows = min(r, 256)
    slabs = pl.BlockSpec((S5_CHUNK, None, rows, S5_ROW), lambda h, i: (0, h, i, 0))
    y = pl.pallas_call(
        _s5_out_kernel,
        grid=(2, r // rows),
        in_specs=[slabs, half_once(S5_CHUNK * S5_ROW, S5_ROW),
                  pl.BlockSpec((rows, S5_HALF), lambda h, i: (i, h)),
                  pl.BlockSpec((rows, S5_HALF), lambda h, i: (i, h)),
                  half_once(S5_CHUNK, S5_ROW, 2 * S5_HALF)],
        out_specs=slabs,
        out_shape=jax.ShapeDtypeStruct((S5_CHUNK, 2, r, S5_ROW), F32),
        compiler_params=_cparams(("arbitrary", "arbitrary")),
        name="s5_out",
    )(us4, kt, hre, him, qq)
    return y.reshape(S5_SLABS, r, S5_ROW), lre, lim


SB_TK = 256
LOG2E = math.log2(math.e)
SB_SKIP_LOG2 = 152.0


def _sb_chains(tasks, ones):
    zs = [_dot_nt(q, k) for q, k, _, _, _ in tasks]
    tails = [jnp.log2(1.0 + jnp.exp2(-jnp.abs(z))) for z in zs]
    sps = [jnp.maximum(z, 0.0) + t for z, t in zip(zs, tails)]
    sps = [sp if task[3] is None else jnp.where(task[3], sp, 0.0) for sp, task in zip(sps, tasks)]
    mass = [jnp.sum(sp, axis=-1, keepdims=True) for sp in sps]
    his = [sp.astype(BF16) for sp in sps]
    los = [(sp - hi.astype(F32)).astype(BF16) for sp, hi in zip(sps, his)]
    sums = [_dot(jnp.concatenate([hi, lo], axis=0), ones) for hi, lo in zip(his, los)]
    outs = []
    for i, (_, _, v, mask, carry) in enumerate(tasks):
        rows = zs[i].shape[0]
        suffix = sums[i][:rows] + sums[i][rows:]
        logw = jnp.minimum(zs[i], 0.0) - tails[i] - suffix
        if carry is not None:
            logw = logw - (mass[carry] if isinstance(carry, int) else carry)
        w = jnp.exp2(logw)
        if mask is not None:
            w = jnp.where(mask, w, 0.0)
        outs.append(_dot(w.astype(BF16), v))
    return outs, mass


def _sb_block(q2, k, v, ones, carry, mask):
    half = q2.shape[0] // 2
    halves = (slice(0, half), slice(half, 2 * half))
    outs, mass = _sb_chains([(q2[h], k, v, None if mask is None else mask[h],
                              None if carry is None else carry[h]) for h in halves], ones)
    return jnp.concatenate(outs, axis=0), jnp.concatenate(mass, axis=0)


def _sb_walk(q2, kp_ref, vp_ref, ones, blocks, acc, carry):
    def cond(s):
        return jnp.logical_and(s[0] < blocks, s[1] < SB_SKIP_LOG2)

    def body(s):
        i, _, acc, carry = s
        rows = pl.ds(pl.multiple_of((blocks - 1 - i) * SB_TK, SB_TK), SB_TK)
        d, m = _sb_block(q2, kp_ref[rows, :].astype(BF16), vp_ref[rows, :].astype(BF16), ones, carry, None)
        carry = carry + m
        return i + 1, jnp.min(carry), acc + d, carry

    return lax.while_loop(cond, body, (jnp.int32(0), jnp.min(carry), acc, carry))[2]


def _sb_prompt_kernel(q_ref, kd_ref, vd_ref, kp_ref, vp_ref, ones_ref, o_ref):
    i = pl.program_id(2)
    t = SB_TK
    lanes = 2 * SB_HEAD_DIM
    q = q_ref[...]
    lane_lo = lax.broadcasted_iota(jnp.int32, (2 * t, lanes), 1) < SB_HEAD_DIM
    zero = jnp.zeros_like(q)
    heads = (jnp.where(lane_lo, q, zero), jnp.where(lane_lo, zero, q))
    row = lax.broadcasted_iota(jnp.int32, (t, t), 0)
    col = lax.broadcasted_iota(jnp.int32, (t, t), 1)
    earlier = col < row
    ones = ones_ref[...]
    kd = [kd_ref[s * t:(s + 1) * t, :] for s in range(2)]
    vd = [vd_ref[s * t:(s + 1) * t, :] for s in range(2)]
    qs = [[h[s * t:(s + 1) * t, :] for h in heads] for s in range(2)]

    def front(with_past):
        tasks = [(qs[s][h], kd[s], vd[s], earlier, None) for s in range(2) for h in range(2)]
        tasks += [(qs[1][h], kd[0], vd[0], None, 2 + h) for h in range(2)]
        if with_past:
            rows = pl.ds(pl.multiple_of((2 * i - 1) * t, t), t)
            kp, vp = kp_ref[rows, :], vp_ref[rows, :]
            tasks += [(qs[0][h], kp, vp, None, h) for h in range(2)]
        outs, mass = _sb_chains(tasks, ones)
        stack = lambda a, b: jnp.concatenate([a, b], axis=0)
        acc1, m1 = stack(outs[2] + outs[4], outs[3] + outs[5]), stack(mass[2] + mass[4], mass[3] + mass[5])
        if with_past:
            acc0, m0 = stack(outs[0] + outs[6], outs[1] + outs[7]), stack(mass[0] + mass[6], mass[1] + mass[7])
        else:
            acc0, m0 = stack(outs[0], outs[1]), stack(mass[0], mass[1])
        return acc0, m0, acc1, m1

    acc0, m0, acc1, m1 = lax.cond(i > 0, lambda: front(True), lambda: front(False))
    q2 = [jnp.concatenate(qs[s], axis=0) for s in range(2)]
    acc0 = _sb_walk(q2[0], kp_ref, vp_ref, ones, jnp.maximum(2 * i - 1, 0), acc0, m0)
    acc1 = _sb_walk(q2[1], kp_ref, vp_ref, ones, 2 * i, acc1, m1)
    first = lax.broadcasted_iota(jnp.int32, (t, lanes), 1) < SB_HEAD_DIM
    o_ref[0:t, :] = jnp.where(first, acc0[:t], acc0[t:])
    o_ref[t:2 * t, :] = jnp.where(first, acc1[:t], acc1[t:])


def _sb_cached_kernel(q_ref, kd_ref, vd_ref, kp_ref, vp_ref, ud_ref, up_ref, o_ref, *, tq, blocks):
    q = q_ref[...]
    lane_lo = lax.broadcasted_iota(jnp.int32, (tq, 2 * SB_HEAD_DIM), 1) < SB_HEAD_DIM
    row = lax.broadcasted_iota(jnp.int32, (2 * tq, tq), 0)
    col = lax.broadcasted_iota(jnp.int32, (2 * tq, tq), 1)
    earlier = col < jnp.where(row >= tq, row - tq, row)
    zero = jnp.zeros_like(q)
    q2 = jnp.concatenate([jnp.where(lane_lo, q, zero), jnp.where(lane_lo, zero, q)], axis=0)
    acc, mass = _sb_block(q2, kd_ref[...], vd_ref[...], ud_ref[...], None, earlier)
    acc = _sb_walk(q2, kp_ref, vp_ref, up_ref[...], blocks, acc, mass)
    o_ref[...] = jnp.where(lane_lo, acc[:tq], acc[tq:])


def _strict_lower_ones(n):
    r = lax.broadcasted_iota(jnp.int32, (n, n), 0)
    c = lax.broadcasted_iota(jnp.int32, (n, n), 1)
    return (r > c).astype(BF16)


def stick_breaking_fresh(q, k, v, bn, length):
    lanes = 2 * SB_HEAD_DIM
    tq = 2 * SB_TK
    assert length % tq == 0
    blk = pl.BlockSpec((None, tq, lanes), lambda b, h, i: (b, i, h))
    seq = pl.BlockSpec((None, length, lanes), lambda b, h, i: (b, 0, h))
    return pl.pallas_call(
        _sb_prompt_kernel,
        grid=(bn, SB_HEADS // 2, length // tq),
        in_specs=[blk, blk, blk, seq, seq, _full((SB_TK, SB_TK))],
        out_specs=blk,
        out_shape=jax.ShapeDtypeStruct((bn, length, SB_WIDTH), F32),
        compiler_params=_cparams(("parallel", "parallel", "arbitrary")),
        name="stick_breaking",
    )(q, k, v, k, v, _strict_lower_ones(SB_TK))


def stick_breaking_cached(q, k_new, v_new, k_past, v_past, bn, lq):
    lp = k_past.shape[1]
    assert lp % SB_TK == 0
    lanes = 2 * SB_HEAD_DIM
    blk = pl.BlockSpec((None, lq, lanes), lambda b, h: (b, 0, h))
    past = pl.BlockSpec((None, lp, lanes), lambda b, h: (b, 0, h))
    return pl.pallas_call(
        functools.partial(_sb_cached_kernel, tq=lq, blocks=lp // SB_TK),
        grid=(bn, SB_HEADS // 2),
        in_specs=[blk, blk, blk, past, past, _full((lq, lq)), _full((SB_TK, SB_TK))],
        out_specs=blk,
        out_shape=jax.ShapeDtypeStruct((bn, lq, SB_WIDTH), F32),
        compiler_params=_cparams(("parallel", "parallel")),
        name="stick_breaking_cached",
    )(q, k_new, v_new, k_past, v_past, _strict_lower_ones(lq), _strict_lower_ones(SB_TK))


def _gelu(x):
    return 0.5 * x * (1.0 + jnp.tanh(math.sqrt(2.0 / math.pi) * (x + 0.044715 * (x * x * x))))


def _mix0_out_kernel(x_ref, ys_ref, u_ref, sb_ref, d_ref, wglu_ref, bglu_ref, wo1_ref, wo2_ref, g_ref, b_ref,
                     o_ref, y_scr):
    x = x_ref[...]
    chunks = y_scr.shape[1] // S5_CHUNK
    per_slab = S5_ROW // LANES
    for s in range(S5_CHUNK):
        for half in range(2):
            for c in range(per_slab):
                y_scr[half * per_slab + c, pl.ds(s, chunks, stride=S5_CHUNK), :] = (
                    ys_ref[2 * s + half, :, c * LANES:(c + 1) * LANES])
    ys = jnp.concatenate([y_scr[c] for c in range(S5_WIDTH // LANES)], axis=1)
    gl = _gelu(ys + d_ref[...] * u_ref[...])
    gate = _sigmoid(_dot(gl.astype(BF16), wglu_ref[...]) + bglu_ref[...])
    m = _dot((gl * gate).astype(BF16), wo1_ref[...]) + _dot(sb_ref[...].astype(BF16), wo2_ref[...])
    o_ref[...] = _layer_norm(ALPHA * x + m, g_ref[...], b_ref[...])


def mix0_out(x, ys, u, sb, d, wglu, bglu, wo1, wo2, g, b):
    t = x.shape[0]
    tm = _row_tile(t)
    rowblk = lambda n: pl.BlockSpec((tm, n), lambda i: (i, 0))
    return pl.pallas_call(
        _mix0_out_kernel,
        grid=(t // tm,),
        in_specs=[rowblk(D_MODEL),
                  pl.BlockSpec((S5_SLABS, tm // S5_CHUNK, S5_ROW), lambda i: (0, i, 0)),
                  rowblk(S5_WIDTH), rowblk(SB_WIDTH),
                  _full((1, S5_WIDTH)), _full((S5_WIDTH, S5_WIDTH)), _full((1, S5_WIDTH)),
                  _full((S5_WIDTH, D_MODEL)), _full((SB_WIDTH, D_MODEL)),
                  _full((1, D_MODEL)), _full((1, D_MODEL))],
        out_specs=rowblk(D_MODEL),
        out_shape=jax.ShapeDtypeStruct((t, D_MODEL), F32),
        scratch_shapes=[pltpu.VMEM((S5_WIDTH // LANES, tm, LANES), F32)],
        compiler_params=_cparams(("parallel",)),
        name="mix0_out",
    )(x, ys, u, sb, d.reshape(1, -1), wglu, bglu.reshape(1, -1), wo1, wo2, g.reshape(1, -1), b.reshape(1, -1))


def _ssd_kernel(xbc_ref, z_ref, dt_ref, dtt_ref, conv0_ref, h0_ref, cw_ref, cb_ref, dtb_ref, dtbt_ref,
                alog_ref, alogt_ref, dsk_ref, ng_ref, tril_ref, triu_ref,
                y_ref, hl_ref, ext_ref, h_ref, *, lc, nc):
    c = pl.program_id(1)

    @pl.when(c == 0)
    def _():
        ext_ref[0:CONV_PAD, :] = conv0_ref[...]
        h_ref[...] = h0_ref[...]

    ext_ref[CONV_PAD:CONV_PAD + lc, :] = xbc_ref[...]
    ext = ext_ref[...]
    conv = cb_ref[...] + ext[CONV_PAD:, :] * cw_ref[SSD_CONV - 1:SSD_CONV, :]
    for back in range(1, SSD_CONV):
        tap = SSD_CONV - 1 - back
        conv = conv + pltpu.roll(ext, back, 0)[CONV_PAD:, :] * cw_ref[tap:tap + 1, :]
    ext_ref[0:CONV_PAD, :] = ext[lc:lc + CONV_PAD, :]
    act = conv * _sigmoid(conv)

    dt = _softplus(dt_ref[...] + dtb_ref[...])
    dtt = _softplus(dtt_ref[...] + dtbt_ref[...])
    cs = _sum_left(tril_ref[...], dt * (-LOG2E * jnp.exp(alog_ref[...])))
    cst = _sum_right(dtt * (-LOG2E * jnp.exp(alogt_ref[...])), triu_ref[...])
    cs_last = cs[lc - 1:lc, :]

    row = lax.broadcasted_iota(jnp.int32, (lc, lc), 0)
    col = lax.broadcasted_iota(jnp.int32, (lc, lc), 1)
    causal = col <= row
    lanes = 2 * SSD_HEAD_DIM
    lane_lo = lax.broadcasted_iota(jnp.int32, (lc, lanes), 1) < SSD_HEAD_DIM
    pairs_per_group = SSD_PAIRS // SSD_GROUPS

    for g in range(SSD_GROUPS):
        bm = act[:, SSD_INNER + g * SSD_STATE:SSD_INNER + (g + 1) * SSD_STATE].astype(BF16)
        cm = act[:, SSD_INNER + SSD_GN + g * SSD_STATE:SSD_INNER + SSD_GN + (g + 1) * SSD_STATE].astype(BF16)
        cb = _dot_nt(cm, bm)
        gated, sq = [], jnp.zeros((lc, lanes), F32)
        for k in range(g * pairs_per_group, (g + 1) * pairs_per_group):
            sl = slice(k * lanes, (k + 1) * lanes)
            xp = act[:, sl]
            heads = (2 * k, 2 * k + 1)
            pair = lambda v: jnp.where(lane_lo[:v[0].shape[0]], v[0], v[1])
            xdt = xp * pair([dt[:, h:h + 1] for h in heads])
            xdtb = xdt.astype(BF16)
            csb = [jnp.broadcast_to(cs[:, h:h + 1], (lc, lanes)) for h in heads]
            ys = []
            for h, csh in zip(heads, csb):
                seg = csh[:, :lc] - cst[h:h + 1, :]
                w = cb * jnp.exp2(jnp.where(causal, seg, -jnp.inf))
                ys.append(_dot(w.astype(BF16), xdtb))
            y = pair(ys)
            cs_pair = pair(csb)
            last_pair = pair([cs_last[:, h:h + 1] for h in heads])
            ht = h_ref[k]
            y = y + _dot(cm, ht.astype(BF16)) * jnp.exp2(cs_pair)
            xw = (xdt * jnp.exp2(last_pair - cs_pair)).astype(BF16)
            h_ref[k] = ht * jnp.exp2(last_pair) + _dot_tn(bm, xw)
            y = y + dsk_ref[:, sl] * xp
            zz = z_ref[:, sl]
            yg = y * (zz * _sigmoid(zz))
            sq = sq + yg * yg
            gated.append(yg)
        mean_sq = jnp.sum(sq, axis=-1, keepdims=True) * (1.0 / (pairs_per_group * lanes))
        scale = lax.rsqrt(mean_sq + RMS_EPS)
        for k, yg in zip(range(g * pairs_per_group, (g + 1) * pairs_per_group), gated):
            sl = slice(k * lanes, (k + 1) * lanes)
            y_ref[:, sl] = yg * scale * ng_ref[:, sl]

    @pl.when(c == nc - 1)
    def _():
        hl_ref[...] = h_ref[...]


def ssd_mix(xbc, z, dt_raw, conv0, h0, conv_w, conv_b, dt_bias, a_log, d_skip, norm_g, bn, length, lc):
    nc = length // lc
    t = bn * length
    lanes = 2 * SSD_HEAD_DIM
    dtt = jnp.swapaxes(dt_raw.reshape(bn * nc, lc, SSD_HEADS), 1, 2)
    conv0p = jnp.pad(conv0, ((0, 0), (CONV_PAD - (SSD_CONV - 1), 0), (0, 0)))
    a_log = a_log.astype(F32)
    r = lax.broadcasted_iota(jnp.int32, (lc, lc), 0)
    cc = lax.broadcasted_iota(jnp.int32, (lc, lc), 1)
    tril = (cc <= r).astype(BF16)
    triu = (r <= cc).astype(BF16)
    tok = lambda n: pl.BlockSpec((lc, n), lambda b, c: (b * nc + c, 0))
    y, hl = pl.pallas_call(
        functools.partial(_ssd_kernel, lc=lc, nc=nc),
        grid=(bn, nc),
        in_specs=[tok(SSD_CONV_DIM), tok(SSD_INNER), tok(SSD_HEADS),
                  pl.BlockSpec((None, SSD_HEADS, lc), lambda b, c: (b * nc + c, 0, 0)),
                  pl.BlockSpec((None, CONV_PAD, SSD_CONV_DIM), lambda b, c: (b, 0, 0)),
                  pl.BlockSpec((None, SSD_PAIRS, lanes, SSD_STATE), lambda b, c: (b, 0, 0, 0)),
                  _full((SSD_CONV, SSD_CONV_DIM)), _full((1, SSD_CONV_DIM)),
                  _full((1, SSD_HEADS)), _full((SSD_HEADS, 1)), _full((1, SSD_HEADS)), _full((SSD_HEADS, 1)),
                  _full((1, SSD_INNER)), _full((1, SSD_INNER)), _full((lc, lc)), _full((lc, lc))],
        out_specs=[tok(SSD_INNER),
                   pl.BlockSpec((None, SSD_PAIRS, lanes, SSD_STATE), lambda b, c: (b, 0, 0, 0))],
        out_shape=[jax.ShapeDtypeStruct((t, SSD_INNER), F32),
                   jax.ShapeDtypeStruct((bn, SSD_PAIRS, lanes, SSD_STATE), F32)],
        scratch_shapes=[pltpu.VMEM((CONV_PAD + lc, SSD_CONV_DIM), F32),
                        pltpu.VMEM((SSD_PAIRS, lanes, SSD_STATE), F32)],
        compiler_params=_cparams(("parallel", "arbitrary")),
        name="ssd_mix",
    )(xbc, z, dt_raw, dtt, conv0p, jnp.swapaxes(h0.reshape(bn, SSD_PAIRS, lanes, SSD_STATE), 2, 3),
      conv_w, conv_b.reshape(1, -1), dt_bias.reshape(1, -1), dt_bias.reshape(-1, 1),
      a_log.reshape(1, -1), a_log.reshape(-1, 1), jnp.repeat(d_skip, SSD_HEAD_DIM).reshape(1, -1),
      norm_g.reshape(1, -1), tril, triu)
    return y, jnp.swapaxes(hl, 2, 3).reshape(bn, SSD_HEADS, SSD_HEAD_DIM, SSD_STATE)


def _out_ln_kernel(x_ref, a_ref, w_ref, g_ref, b_ref, o_ref):
    x = x_ref[...]
    m = _dot(a_ref[...].astype(BF16), w_ref[...])
    o_ref[...] = _layer_norm(ALPHA * x + m, g_ref[...], b_ref[...])


def out_ln(x, a, w, g, b):
    t = x.shape[0]
    tm = _row_tile(t)
    k = a.shape[1]
    return pl.pallas_call(
        _out_ln_kernel,
        grid=(t // tm,),
        in_specs=[pl.BlockSpec((tm, D_MODEL), lambda i: (i, 0)), pl.BlockSpec((tm, k), lambda i: (i, 0)),
                  _full((k, D_MODEL)), _full((1, D_MODEL)), _full((1, D_MODEL))],
        out_specs=pl.BlockSpec((tm, D_MODEL), lambda i: (i, 0)),
        out_shape=jax.ShapeDtypeStruct((t, D_MODEL), F32),
        compiler_params=_cparams(("parallel",)),
        name="out_ln",
    )(x, a, w, g.reshape(1, -1), b.reshape(1, -1))


def _trunk(x, bn, length, states, w):
    t = bn * length
    x = x.reshape(t, D_MODEL)
    x = ffn_ln(x, w["wg"][0][0], w["wu"][0][0], w["wd"][0][0], w["ln_g"][0, 0], w["ln_b"][0, 0])

    u, u_slabs, q_bf, k, k_bf, v, v_bf = mix0_in(x, w["mix0_in"], SB_HEAD_DIM ** -0.5 * LOG2E)
    nj = length // S5_CHUNK
    nstate = S5_GROUPS * S5_STATE
    if states is None:
        h0re = h0im = jnp.zeros((bn, nstate), F32)
    else:
        h0re = states[0][0].astype(F32).reshape(bn, nstate)
        h0im = states[1][0].astype(F32).reshape(bn, nstate)
    y_s5, h_re, h_im = s5_mix(u_slabs, h0re, h0im, w["s5_tiles"], bn, nj)
    s5_re = h_re.reshape(1, bn, S5_GROUPS, S5_STATE)
    s5_im = h_im.reshape(1, bn, S5_GROUPS, S5_STATE)

    seq = lambda a: a.reshape(bn, length, SB_WIDTH)
    if states is None:
        sb = stick_breaking_fresh(seq(q_bf), seq(k_bf), seq(v_bf), bn, length)
    else:
        past = states[2][0].shape[1]
        sb = stick_breaking_cached(seq(q_bf), seq(k_bf), seq(v_bf), states[2][0].reshape(bn, past, SB_WIDTH),
                                   states[3][0].reshape(bn, past, SB_WIDTH), bn, length)
    x = mix0_out(x, y_s5, u, sb.reshape(t, SB_WIDTH), w["s5_d"], w["s5_wglu"], w["s5_bglu"],
                 w["mix0_out"][:S5_WIDTH], w["mix0_out"][S5_WIDTH:], w["ln_g"][0, 1], w["ln_b"][0, 1])
    x = ffn_ln(x, w["wg"][0][1], w["wu"][0][1], w["wd"][0][1], w["ln_g"][0, 2], w["ln_b"][0, 2])

    x = ffn_ln(x, w["wg"][1][0], w["wu"][1][0], w["wd"][1][0], w["ln_g"][1, 0], w["ln_b"][1, 0])
    one = ((F32, 1.0),)
    z, xbc, dt_raw = proj(x, w["ssd_in"], (
        (0, SSD_INNER, one), (SSD_INNER, SSD_INNER + SSD_CONV_DIM, one),
        (SSD_INNER + SSD_CONV_DIM, SSD_INNER + SSD_CONV_DIM + SSD_HEADS, one)))
    if states is None:
        conv0 = jnp.zeros((bn, SSD_CONV - 1, SSD_CONV_DIM), F32)
        hs0 = jnp.zeros((bn, SSD_HEADS, SSD_HEAD_DIM, SSD_STATE), F32)
    else:
        conv0 = states[5][0].astype(F32)
        hs0 = states[4][0].astype(F32)
    lc = min(128, length)
    yg, h_ssd = ssd_mix(xbc, z, dt_raw, conv0, hs0, w["ssd_conv_w"], w["ssd_conv_b"], w["ssd_dt_bias"],
                        w["ssd_a_log"], w["ssd_d"], w["ssd_norm_g"], bn, length, lc)
    ext = jnp.concatenate([conv0, xbc.reshape(bn, length, SSD_CONV_DIM)[:, length - (SSD_CONV - 1):]], axis=1)
    new_conv = ext[:, ext.shape[1] - (SSD_CONV - 1):]
    x = out_ln(x, yg, w["ssd_out"], w["ln_g"][1, 1], w["ln_b"][1, 1])
    x = ffn_ln(x, w["wg"][1][1], w["wu"][1][1], w["wd"][1][1], w["ln_g"][1, 2], w["ln_b"][1, 2])

    return (x.reshape(bn, length, D_MODEL), s5_re, s5_im,
            k.reshape(1, bn, length, SB_HEADS, SB_HEAD_DIM), v.reshape(1, bn, length, SB_HEADS, SB_HEAD_DIM),
            h_ssd[None], new_conv[None])


def kernel(x_prompt, x_sample, state_s5_re, state_s5_im, cache_sb_k, cache_sb_v, state_ssd, state_conv, ln_g, ln_b, ffn_w_gate, ffn_w_up, ffn_w_down, mix0_w_in, s5_a_re, s5_a_im, s5_log_dt, s5_b_re, s5_b_im, s5_c_re, s5_c_im, s5_d, s5_w_glu, s5_b_glu, mix0_w_out, ssd_w_in, ssd_conv_w, ssd_conv_b, ssd_dt_bias, ssd_a_log, ssd_d, ssd_norm_g, ssd_w_out):
    bf = lambda a: a.astype(BF16)
    w = {
        "wg": [[bf(ffn_w_gate[l, s]) for s in range(2)] for l in range(DEPTH)],
        "wu": [[bf(ffn_w_up[l, s]) for s in range(2)] for l in range(DEPTH)],
        "wd": [[bf(ffn_w_down[l, s]) for s in range(2)] for l in range(DEPTH)],
        "ln_g": ln_g, "ln_b": ln_b,
        "mix0_in": bf(mix0_w_in[0]),
        "s5_tiles": s5_tiles(s5_a_re[0], s5_a_im[0], s5_log_dt[0], s5_b_re[0], s5_b_im[0], s5_c_re[0], s5_c_im[0]),
        "s5_d": s5_d[0], "s5_wglu": bf(s5_w_glu[0]), "s5_bglu": s5_b_glu[0], "mix0_out": bf(mix0_w_out[0]),
        "ssd_in": bf(ssd_w_in[0]), "ssd_conv_w": ssd_conv_w[0], "ssd_conv_b": ssd_conv_b[0],
        "ssd_dt_bias": ssd_dt_bias[0], "ssd_a_log": ssd_a_log[0], "ssd_d": ssd_d[0],
        "ssd_norm_g": ssd_norm_g[0], "ssd_out": bf(ssd_w_out[0]),
    }
    bp, lp = x_prompt.shape[0], x_prompt.shape[1]
    bs, ls = x_sample.shape[0], x_sample.shape[1]
    yp, s5rp, s5ip, kp, vp, ssdp, convp = _trunk(x_prompt, bp, lp, None, w)
    ys, s5rs, s5is, ks, vs, ssds, convs = _trunk(
        x_sample, bs, ls, (state_s5_re, state_s5_im, cache_sb_k, cache_sb_v, state_ssd, state_conv), w)
    return (yp, ys, s5rp, s5ip, kp, vp, ssdp, convp, s5rs, s5is, ks, vs, ssds, convs)
```

```python
import functools
import math

import jax
import jax.numpy as jnp
from jax import lax
from jax.experimental import pallas as pl
from jax.experimental.pallas import tpu as pltpu

F32 = jnp.float32
BF16 = jnp.bfloat16

D_MODEL = 1024
DEPTH = 2
ALPHA = (2.0 * DEPTH) ** 0.25
LN_EPS = 1e-5
RMS_EPS = 1e-5
D_FF = 2816

S5_WIDTH = 512
S5_GROUP = 16
S5_GROUPS = 32
S5_STATE = 64
S5_CHUNK = 16
S5_ROW = S5_CHUNK * S5_GROUP
SB_HEADS = 8
SB_HEAD_DIM = 64
SB_WIDTH = 512
MIX0_IN = 2048

SSD_INNER = 2048
SSD_HEAD_DIM = 64
SSD_HEADS = 32
SSD_GROUPS = 4
SSD_STATE = 128
SSD_CONV = 4
SSD_GN = 512
SSD_CONV_DIM = 3072
SSD_PAIRS = SSD_HEADS // 2
CONV_PAD = 8

LANES = 128
VMEM_LIMIT = 56 * 1024 * 1024


def _cparams(sem):
    return pltpu.CompilerParams(dimension_semantics=sem, vmem_limit_bytes=VMEM_LIMIT)


def _sigmoid(x):
    return 1.0 / (1.0 + jnp.exp2(x * -math.log2(math.e)))


def _softplus(x):
    return jnp.maximum(x, 0.0) + jnp.log1p(jnp.exp(-jnp.abs(x)))


def _layer_norm(y, g, b):
    mu = jnp.mean(y, axis=-1, keepdims=True)
    d = y - mu
    var = jnp.mean(d * d, axis=-1, keepdims=True)
    return d * lax.rsqrt(var + LN_EPS) * g + b


def _dot(a, b):
    return jnp.dot(a, b, preferred_element_type=F32)


def _dot_nt(a, b):
    return lax.dot_general(a, b, (((1,), (1,)), ((), ())), preferred_element_type=F32)


def _dot_tn(a, b):
    return lax.dot_general(a, b, (((0,), (0,)), ((), ())), preferred_element_type=F32)


def _split3(x):
    hi = x.astype(BF16)
    r = x - hi.astype(F32)
    mid = r.astype(BF16)
    lo = (r - mid.astype(F32)).astype(BF16)
    return hi, mid, lo


def _sum_right(x, ones_mat):
    hi, mid, lo = _split3(x)
    return _dot(hi, ones_mat) + _dot(mid, ones_mat) + _dot(lo, ones_mat)


def _sum_left(ones_mat, x):
    hi, mid, lo = _split3(x)
    return _dot(ones_mat, hi) + _dot(ones_mat, mid) + _dot(ones_mat, lo)


def _row_tile(t):
    for tm in (512, 256, 128, 64, 32, 16, 8):
        if t % tm == 0:
            return tm
    raise ValueError(f"token count {t} is not a multiple of 8")


def _full(shape):
    return pl.BlockSpec(shape, lambda *_: (0,) * len(shape), pipeline_mode=pl.Buffered(1))


FFN_TF = 1408


def _ffn_sublayer(x, wg_ref, wu_ref, wd_ref, g_ref, b_ref):
    xb = x.astype(BF16)
    acc = None
    for c in range(D_FF // FFN_TF):
        sl = slice(c * FFN_TF, (c + 1) * FFN_TF)
        gate = _dot(xb, wg_ref[:, sl])
        up = _dot(xb, wu_ref[:, sl])
        h = (gate * _sigmoid(gate) * up).astype(BF16)
        part = _dot(h, wd_ref[sl, :])
        acc = part if acc is None else acc + part
    return _layer_norm(ALPHA * x + 0.5 * acc, g_ref[...], b_ref[...])


def _ffn_kernel(x_ref, wg_ref, wu_ref, wd_ref, g_ref, b_ref, o_ref):
    o_ref[...] = _ffn_sublayer(x_ref[...], wg_ref, wu_ref, wd_ref, g_ref, b_ref)


def _ffn_operands(ffn):
    wg, wu, wd, g, b = ffn
    specs = [_full((D_MODEL, D_FF)), _full((D_MODEL, D_FF)), _full((D_FF, D_MODEL)),
             _full((1, D_MODEL)), _full((1, D_MODEL))]
    return [wg, wu, wd, g.reshape(1, D_MODEL), b.reshape(1, D_MODEL)], specs


def ffn_ln(x, ffn):
    t = x.shape[0]
    tm = _row_tile(t)
    operands, specs = _ffn_operands(ffn)
    return pl.pallas_call(
        _ffn_kernel,
        grid=(t // tm,),
        in_specs=[pl.BlockSpec((tm, D_MODEL), lambda i: (i, 0))] + specs,
        out_specs=pl.BlockSpec((tm, D_MODEL), lambda i: (i, 0)),
        out_shape=jax.ShapeDtypeStruct((t, D_MODEL), F32),
        compiler_params=_cparams(("parallel",)),
        name="ffn_ln",
    )(x, *operands)


def _proj_kernel(x_ref, w_ref, *o_refs, plan):
    xb = x_ref[...].astype(BF16)
    refs = iter(o_refs)
    for lo, hi, outs in plan:
        y = _dot(xb, w_ref[:, lo:hi])
        for _, scale in outs:
            o_ref = next(refs)
            o_ref[...] = (y if scale == 1.0 else y * scale).astype(o_ref.dtype)


def proj(x, w, plan):
    t = x.shape[0]
    tm = _row_tile(t)
    n = w.shape[1]
    flat = [(hi - lo, dt) for lo, hi, outs in plan for dt, _ in outs]
    return pl.pallas_call(
        functools.partial(_proj_kernel, plan=plan),
        grid=(t // tm,),
        in_specs=[pl.BlockSpec((tm, D_MODEL), lambda i: (i, 0)), _full((D_MODEL, n))],
        out_specs=[pl.BlockSpec((tm, width), lambda i: (i, 0)) for width, _ in flat],
        out_shape=[jax.ShapeDtypeStruct((t, width), dt) for width, dt in flat],
        compiler_params=_cparams(("parallel",)),
        name="proj",
    )(x, w)


def _mix0_in_kernel(x_ref, w_ref, u_ref, us_ref, q_ref, k_ref, kb_ref, v_ref, vb_ref, u_scr, *, q_scale):
    xb = x_ref[...].astype(BF16)
    sw = S5_WIDTH
    u = _dot(xb, w_ref[:, 0:sw])
    u_ref[...] = u
    for c in range(sw // LANES):
        u_scr[c] = u[:, c * LANES:(c + 1) * LANES]
    chunks = u.shape[0] // S5_CHUNK
    per_slab = S5_ROW // LANES
    for s in range(S5_CHUNK):
        for half in range(2):
            for c in range(per_slab):
                rows = u_scr[half * per_slab + c, pl.ds(s, chunks, stride=S5_CHUNK), :]
                us_ref[2 * s + half, :, c * LANES:(c + 1) * LANES] = rows.astype(BF16)
    q_ref[...] = (_dot(xb, w_ref[:, sw:2 * sw]) * q_scale).astype(BF16)
    k = _dot(xb, w_ref[:, 2 * sw:3 * sw])
    k_ref[...] = k
    kb_ref[...] = k.astype(BF16)
    v = _dot(xb, w_ref[:, 3 * sw:4 * sw])
    v_ref[...] = v
    vb_ref[...] = v.astype(BF16)


def mix0_in(x, w, q_scale):
    t = x.shape[0]
    tm = _row_tile(t)
    sw = S5_WIDTH
    tok = pl.BlockSpec((tm, sw), lambda i: (i, 0))
    return pl.pallas_call(
        functools.partial(_mix0_in_kernel, q_scale=q_scale),
        grid=(t // tm,),
        in_specs=[pl.BlockSpec((tm, D_MODEL), lambda i: (i, 0)), _full((D_MODEL, MIX0_IN))],
        out_specs=[tok, pl.BlockSpec((S5_SLABS, tm // S5_CHUNK, S5_ROW), lambda i: (0, i, 0)),
                   tok, tok, tok, tok, tok],
        out_shape=[jax.ShapeDtypeStruct((t, sw), F32),
                   jax.ShapeDtypeStruct((S5_SLABS, t // S5_CHUNK, S5_ROW), BF16),
                   jax.ShapeDtypeStruct((t, sw), BF16), jax.ShapeDtypeStruct((t, sw), F32),
                   jax.ShapeDtypeStruct((t, sw), BF16), jax.ShapeDtypeStruct((t, sw), F32),
                   jax.ShapeDtypeStruct((t, sw), BF16)],
        scratch_shapes=[pltpu.VMEM((sw // LANES, tm, LANES), F32)],
        compiler_params=_cparams(("parallel",)),
        name="mix0_in",
    )(x, w)


S5_SLABS = 2 * S5_CHUNK
S5_GH = S5_GROUPS // 2
S5_HALF = S5_GH * S5_STATE
S5_CARRY_LANES = LANES


def _s5_tiles_kernel(a_re_ref, a_im_ref, ldt_ref, ar_rows_ref, ai_rows_ref, ldt_rows_ref, b1_ref, b2_ref,
                     bt_re_ref, bt_im_ref, cq_re_ref, cq_im_ref, cm_ref,
                     p_ref, qt_ref, k_ref, a16r_ref, a16i_ref):
    s = pl.program_id(1).astype(F32)

    def cpow(m, ar, ai, dt):
        mag = jnp.exp(m * (ar * dt))
        ang = m * (ai * dt)
        return mag * jnp.cos(ang), mag * jnp.sin(ang)

    def zoh(ar, ai, dt):
        abr, abi = cpow(1.0, ar, ai, dt)
        nr, ni, den = abr - 1.0, abi, ar * ar + ai * ai
        return (nr * ar + ni * ai) / den, (ni * ar - nr * ai) / den

    ar, ai, dt = a_re_ref[...], a_im_ref[...], jnp.exp(ldt_ref[...])
    fr, fi = zoh(ar, ai, dt)
    pwr, pwi = cpow(float(S5_CHUNK - 1) - s, ar, ai, dt)
    cr, ci = pwr * fr - pwi * fi, pwr * fi + pwi * fr
    shape = (S5_ROW, S5_HALF)
    same = (jnp.right_shift(lax.broadcasted_iota(jnp.int32, shape, 0), S5_GROUP.bit_length() - 1)
            == jnp.right_shift(lax.broadcasted_iota(jnp.int32, shape, 1), S5_STATE.bit_length() - 1))
    btr, bti = bt_re_ref[...], bt_im_ref[...]
    p_ref[:, :S5_HALF] = jnp.where(same, cr * btr - ci * bti, 0.0).astype(BF16)
    p_ref[:, S5_HALF:] = jnp.where(same, cr * bti + ci * btr, 0.0).astype(BF16)
    qwr, qwi = cpow(s + 1.0, ar, ai, dt)
    cqr, cqi = cq_re_ref[...], cq_im_ref[...]
    qt_ref[:, :S5_HALF] = jnp.where(same, cqr * qwr - cqi * qwi, 0.0).astype(BF16)
    qt_ref[:, S5_HALF:] = jnp.where(same, -(cqr * qwi + cqi * qwr), 0.0).astype(BF16)
    a16r_ref[...], a16i_ref[...] = cpow(float(S5_CHUNK), ar, ai, dt)

    ar2, ai2, dt2 = ar_rows_ref[...], ai_rows_ref[...], jnp.exp(ldt_rows_ref[...])
    fr2, fi2 = zoh(ar2, ai2, dt2)
    b1, b2 = b1_ref[...], b2_ref[...]
    bb, bbsw = fr2 * b1 + fi2 * b2, fr2 * b2 - fi2 * b1
    lwr, lwi = cpow(s, ar2, ai2, dt2)
    w = (lwr * bb + lwi * bbsw).astype(BF16)
    lag = _dot(w, cm_ref[...].astype(BF16))
    kshape = (S5_ROW, S5_ROW)
    gsh = S5_GROUP.bit_length() - 1
    same_k = (jnp.right_shift(lax.broadcasted_iota(jnp.int32, kshape, 0), gsh)
              == jnp.right_shift(lax.broadcasted_iota(jnp.int32, kshape, 1), gsh))
    k_ref[...] = jnp.where(same_k, lag, 0.0).astype(BF16)


def s5_tiles(a_re, a_im, log_dt, b_re, b_im, c_re, c_im):
    gh, n, pch = S5_GH, S5_STATE, S5_GROUP
    halves = lambda v: v.reshape((2, gh) + v.shape[1:])
    lane = lambda v: halves(v).reshape(2, 1, S5_HALF)
    rows = lambda v: jnp.tile(jnp.repeat(halves(v), pch, axis=1), (1, 1, 2))
    ldt = jnp.broadcast_to(log_dt[:, None], (S5_GROUPS, n))
    bt = lambda b: halves(jnp.swapaxes(b, 1, 2)).reshape(2, S5_ROW, n)
    btr, bti = bt(b_re), bt(b_im)
    b1 = jnp.concatenate([btr, bti], axis=-1)
    b2 = jnp.concatenate([-bti, btr], axis=-1)
    wide = lambda v: jnp.tile(v, (1, 1, gh))
    cq = lambda c: jnp.tile(jnp.transpose(halves(c), (0, 2, 1, 3)).reshape(2, pch, S5_HALF), (1, gh, 1))
    cmt = lambda c: jnp.transpose(halves(c), (0, 3, 1, 2)).reshape(2, n, S5_ROW)
    cm = jnp.concatenate([cmt(c_re), -cmt(c_im)], axis=1)
    half = lambda *shape: pl.BlockSpec((None,) + shape, lambda h, s: (h,) + (0,) * len(shape))
    pt, qt, kt, a16r, a16i = pl.pallas_call(
        _s5_tiles_kernel,
        grid=(2, S5_CHUNK),
        in_specs=[half(1, S5_HALF)] * 3 + [half(S5_ROW, 2 * n)] * 5 + [half(S5_ROW, S5_HALF)] * 4
                 + [half(2 * n, S5_ROW)],
        out_specs=[pl.BlockSpec((None, S5_ROW, 2 * S5_HALF), lambda h, s: (h, s, 0)),
                   pl.BlockSpec((None, None, S5_ROW, 2 * S5_HALF), lambda h, s: (h, s, 0, 0)),
                   pl.BlockSpec((None, S5_ROW, S5_ROW), lambda h, s: (h, S5_CHUNK - 1 - s, 0)),
                   half(1, S5_HALF), half(1, S5_HALF)],
        out_shape=[jax.ShapeDtypeStruct((2, S5_CHUNK * S5_ROW, 2 * S5_HALF), BF16),
                   jax.ShapeDtypeStruct((2, S5_CHUNK, S5_ROW, 2 * S5_HALF), BF16),
                   jax.ShapeDtypeStruct((2, S5_CHUNK * S5_ROW, S5_ROW), BF16),
                   jax.ShapeDtypeStruct((2, 1, S5_HALF), F32), jax.ShapeDtypeStruct((2, 1, S5_HALF), F32)],
        compiler_params=_cparams(("arbitrary", "arbitrary")),
        name="s5_tiles",
    )(lane(a_re), lane(a_im), lane(ldt), rows(a_re), rows(a_im), rows(ldt), b1, b2,
      wide(btr), wide(bti), cq(c_re), cq(c_im), cm)
    nstate = S5_GROUPS * S5_STATE
    return kt, pt, qt, a16r.reshape(1, nstate), a16i.reshape(1, nstate)


def _s5_state_kernel(us_ref, p_ref, sre_ref, sim_ref):
    steps = jnp.concatenate([us_ref[s] for s in range(S5_CHUNK)], axis=1)
    part = _dot(steps, p_ref[...])
    for c in range(S5_HALF // LANES):
        sre_ref[c] = part[:, c * LANES:(c + 1) * LANES]
        sim_ref[c] = part[:, S5_HALF + c * LANES:S5_HALF + (c + 1) * LANES]


def _s5_carry_kernel(sre_ref, sim_ref, h0re_ref, h0im_ref, ar_ref, ai_ref,
                     hre_ref, him_ref, lre_ref, lim_ref, *, bn, nj):
    ar, ai = ar_ref[...], ai_ref[...]

    def body(j, carry):
        hr, hi = carry
        rows = pl.ds(j, bn, stride=nj)
        hre_ref[rows, :] = hr
        him_ref[rows, :] = hi
        return ar * hr - ai * hi + sre_ref[rows, :], ar * hi + ai * hr + sim_ref[rows, :]

    hr, hi = lax.fori_loop(0, nj, body, (h0re_ref[...], h0im_ref[...]))
    lre_ref[...] = hr
    lim_ref[...] = hi


def _s5_out_kernel(us_ref, k_ref, hre_ref, him_ref, q_ref, y_ref):
    blocks = range(S5_HALF // LANES)
    hin = jnp.concatenate([hre_ref[c] for c in blocks] + [him_ref[c] for c in blocks], axis=1).astype(BF16)
    for t in range(S5_CHUNK):
        steps = jnp.concatenate([us_ref[s] for s in range(t + 1)], axis=1)
        lags = k_ref[(S5_CHUNK - 1 - t) * S5_ROW:, :]
        y_ref[t] = _dot(steps, lags) + _dot_nt(hin, q_ref[t])


def s5_mix(us, h0re, h0im, tiles, bn, nj):
    kt, pt, qq, ar, ai = tiles
    r = bn * nj
    nstate = S5_GROUPS * S5_STATE
    us4 = us.reshape(S5_CHUNK, 2, r, S5_ROW)
    half_once = lambda *shape: pl.BlockSpec((None,) + shape, lambda h, i: (h,) + (0,) * len(shape),
                                            pipeline_mode=pl.Buffered(1))
    rows = min(r, 512)
    lanes = S5_CARRY_LANES
    sre, sim = pl.pallas_call(
        _s5_state_kernel,
        grid=(2, r // rows),
        in_specs=[pl.BlockSpec((S5_CHUNK, None, rows, S5_ROW), lambda h, i: (0, h, i, 0)),
                  half_once(S5_CHUNK * S5_ROW, 2 * S5_HALF)],
        out_specs=[pl.BlockSpec((S5_HALF // lanes, rows, lanes), lambda h, i: (h, i, 0))] * 2,
        out_shape=[jax.ShapeDtypeStruct((nstate // lanes, r, lanes), F32)] * 2,
        compiler_params=_cparams(("arbitrary", "arbitrary")),
        name="s5_state",
    )(us4, pt)
    col = lambda n: pl.BlockSpec((n, lanes), lambda i: (0, i))
    blk = pl.BlockSpec((None, r, lanes), lambda i: (i, 0, 0))
    hre, him, lre, lim = pl.pallas_call(
        functools.partial(_s5_carry_kernel, bn=bn, nj=nj),
        grid=(nstate // lanes,),
        in_specs=[blk, blk, col(bn), col(bn), col(1), col(1)],
        out_specs=[blk, blk, col(bn), col(bn)],
        out_shape=[jax.ShapeDtypeStruct((nstate // lanes, r, lanes), F32)] * 2
                  + [jax.ShapeDtypeStruct((bn, nstate), F32)] * 2,
        compiler_params=_cparams(("parallel",)),
        name="s5_carry",
    )(sre, sim, h0re, h0im, ar, ai)
    rows = min(r, 256)
    slabs = pl.BlockSpec((S5_CHUNK, None, rows, S5_ROW), lambda h, i: (0, h, i, 0))
    y = pl.pallas_call(
        _s5_out_kernel,
        grid=(2, r // rows),
        in_specs=[slabs, half_once(S5_CHUNK * S5_ROW, S5_ROW),
                  pl.BlockSpec((S5_HALF // lanes, rows, lanes), lambda h, i: (h, i, 0)),
                  pl.BlockSpec((S5_HALF // lanes, rows, lanes), lambda h, i: (h, i, 0)),
                  half_once(S5_CHUNK, S5_ROW, 2 * S5_HALF)],
        out_specs=slabs,
        out_shape=jax.ShapeDtypeStruct((S5_CHUNK, 2, r, S5_ROW), F32),
        compiler_params=_cparams(("arbitrary", "arbitrary")),
        name="s5_out",
    )(us4, kt, hre, him, qq)
    return y.reshape(S5_SLABS, r, S5_ROW), lre, lim


SB_TK = 256
LOG2E = math.log2(math.e)
SB_SKIP_LOG2 = 152.0


def _sb_chains(tasks, ones):
    zs = [_dot_nt(q, k) for q, k, _, _, _ in tasks]
    tails = [jnp.log2(1.0 + jnp.exp2(-jnp.abs(z))) for z in zs]
    sps = [jnp.maximum(z, 0.0) + t for z, t in zip(zs, tails)]
    sps = [sp if task[3] is None else jnp.where(task[3], sp, 0.0) for sp, task in zip(sps, tasks)]
    mass = [jnp.sum(sp, axis=-1, keepdims=True) for sp in sps]
    his = [sp.astype(BF16) for sp in sps]
    los = [(sp - hi.astype(F32)).astype(BF16) for sp, hi in zip(sps, his)]
    sums = [_dot(jnp.concatenate([hi, lo], axis=0), ones) for hi, lo in zip(his, los)]
    outs = []
    for i, (_, _, v, mask, carry) in enumerate(tasks):
        rows = zs[i].shape[0]
        suffix = sums[i][:rows] + sums[i][rows:]
        logw = jnp.minimum(zs[i], 0.0) - tails[i] - suffix
        if carry is not None:
            logw = logw - (mass[carry] if isinstance(carry, int) else carry)
        w = jnp.exp2(logw)
        if mask is not None:
            w = jnp.where(mask, w, 0.0)
        outs.append(_dot(w.astype(BF16), v))
    return outs, mass


def _sb_block(q2, k, v, ones, carry, mask):
    half = q2.shape[0] // 2
    halves = (slice(0, half), slice(half, 2 * half))
    outs, mass = _sb_chains([(q2[h], k, v, None if mask is None else mask[h],
                              None if carry is None else carry[h]) for h in halves], ones)
    return jnp.concatenate(outs, axis=0), jnp.concatenate(mass, axis=0)


def _sb_walk(q2, kp_ref, vp_ref, ones, blocks, acc, carry):
    def cond(s):
        return jnp.logical_and(s[0] < blocks, s[1] < SB_SKIP_LOG2)

    def body(s):
        i, _, acc, carry = s
        rows = pl.ds(pl.multiple_of((blocks - 1 - i) * SB_TK, SB_TK), SB_TK)
        d, m = _sb_block(q2, kp_ref[rows, :].astype(BF16), vp_ref[rows, :].astype(BF16), ones, carry, None)
        carry = carry + m
        return i + 1, jnp.min(carry), acc + d, carry

    return lax.while_loop(cond, body, (jnp.int32(0), jnp.min(carry), acc, carry))[2]


def _sb_prompt_kernel(q_ref, kd_ref, vd_ref, kp_ref, vp_ref, ones_ref, o_ref):
    i = pl.program_id(2)
    t = SB_TK
    lanes = 2 * SB_HEAD_DIM
    q = q_ref[...]
    lane_lo = lax.broadcasted_iota(jnp.int32, (2 * t, lanes), 1) < SB_HEAD_DIM
    zero = jnp.zeros_like(q)
    heads = (jnp.where(lane_lo, q, zero), jnp.where(lane_lo, zero, q))
    row = lax.broadcasted_iota(jnp.int32, (t, t), 0)
    col = lax.broadcasted_iota(jnp.int32, (t, t), 1)
    earlier = col < row
    ones = ones_ref[...]
    kd = [kd_ref[s * t:(s + 1) * t, :] for s in range(2)]
    vd = [vd_ref[s * t:(s + 1) * t, :] for s in range(2)]
    qs = [[h[s * t:(s + 1) * t, :] for h in heads] for s in range(2)]

    def front(with_past):
        tasks = [(qs[s][h], kd[s], vd[s], earlier, None) for s in range(2) for h in range(2)]
        tasks += [(qs[1][h], kd[0], vd[0], None, 2 + h) for h in range(2)]
        if with_past:
            rows = pl.ds(pl.multiple_of((2 * i - 1) * t, t), t)
            kp, vp = kp_ref[rows, :], vp_ref[rows, :]
            tasks += [(qs[0][h], kp, vp, None, h) for h in range(2)]
        outs, mass = _sb_chains(tasks, ones)
        stack = lambda a, b: jnp.concatenate([a, b], axis=0)
        acc1, m1 = stack(outs[2] + outs[4], outs[3] + outs[5]), stack(mass[2] + mass[4], mass[3] + mass[5])
        if with_past:
            acc0, m0 = stack(outs[0] + outs[6], outs[1] + outs[7]), stack(mass[0] + mass[6], mass[1] + mass[7])
        else:
            acc0, m0 = stack(outs[0], outs[1]), stack(mass[0], mass[1])
        return acc0, m0, acc1, m1

    acc0, m0, acc1, m1 = lax.cond(i > 0, lambda: front(True), lambda: front(False))
    q2 = [jnp.concatenate(qs[s], axis=0) for s in range(2)]
    acc0 = _sb_walk(q2[0], kp_ref, vp_ref, ones, jnp.maximum(2 * i - 1, 0), acc0, m0)
    acc1 = _sb_walk(q2[1], kp_ref, vp_ref, ones, 2 * i, acc1, m1)
    first = lax.broadcasted_iota(jnp.int32, (t, lanes), 1) < SB_HEAD_DIM
    o_ref[0:t, :] = jnp.where(first, acc0[:t], acc0[t:])
    o_ref[t:2 * t, :] = jnp.where(first, acc1[:t], acc1[t:])


def _sb_cached_kernel(q_ref, kd_ref, vd_ref, kp_ref, vp_ref, ud_ref, up_ref, o_ref, *, tq, blocks):
    q = q_ref[...]
    lane_lo = lax.broadcasted_iota(jnp.int32, (tq, 2 * SB_HEAD_DIM), 1) < SB_HEAD_DIM
    row = lax.broadcasted_iota(jnp.int32, (2 * tq, tq), 0)
    col = lax.broadcasted_iota(jnp.int32, (2 * tq, tq), 1)
    earlier = col < jnp.where(row >= tq, row - tq, row)
    zero = jnp.zeros_like(q)
    q2 = jnp.concatenate([jnp.where(lane_lo, q, zero), jnp.where(lane_lo, zero, q)], axis=0)
    acc, mass = _sb_block(q2, kd_ref[...], vd_ref[...], ud_ref[...], None, earlier)
    acc = _sb_walk(q2, kp_ref, vp_ref, up_ref[...], blocks, acc, mass)
    o_ref[...] = jnp.where(lane_lo, acc[:tq], acc[tq:])


def _strict_lower_ones(n):
    r = lax.broadcasted_iota(jnp.int32, (n, n), 0)
    c = lax.broadcasted_iota(jnp.int32, (n, n), 1)
    return (r > c).astype(BF16)


def stick_breaking_fresh(q, k, v, bn, length):
    lanes = 2 * SB_HEAD_DIM
    tq = 2 * SB_TK
    assert length % tq == 0
    blk = pl.BlockSpec((None, tq, lanes), lambda b, h, i: (b, i, h))
    seq = pl.BlockSpec((None, length, lanes), lambda b, h, i: (b, 0, h))
    return pl.pallas_call(
        _sb_prompt_kernel,
        grid=(bn, SB_HEADS // 2, length // tq),
        in_specs=[blk, blk, blk, seq, seq, _full((SB_TK, SB_TK))],
        out_specs=blk,
        out_shape=jax.ShapeDtypeStruct((bn, length, SB_WIDTH), F32),
        compiler_params=_cparams(("parallel", "parallel", "arbitrary")),
        name="stick_breaking",
    )(q, k, v, k, v, _strict_lower_ones(SB_TK))


def stick_breaking_cached(q, k_new, v_new, k_past, v_past, bn, lq):
    lp = k_past.shape[1]
    assert lp % SB_TK == 0
    lanes = 2 * SB_HEAD_DIM
    blk = pl.BlockSpec((None, lq, lanes), lambda b, h: (b, 0, h))
    past = pl.BlockSpec((None, lp, lanes), lambda b, h: (b, 0, h))
    return pl.pallas_call(
        functools.partial(_sb_cached_kernel, tq=lq, blocks=lp // SB_TK),
        grid=(bn, SB_HEADS // 2),
        in_specs=[blk, blk, blk, past, past, _full((lq, lq)), _full((SB_TK, SB_TK))],
        out_specs=blk,
        out_shape=jax.ShapeDtypeStruct((bn, lq, SB_WIDTH), F32),
        compiler_params=_cparams(("parallel", "parallel")),
        name="stick_breaking_cached",
    )(q, k_new, v_new, k_past, v_past, _strict_lower_ones(lq), _strict_lower_ones(SB_TK))


def _gelu(x):
    return 0.5 * x * (1.0 + jnp.tanh(math.sqrt(2.0 / math.pi) * (x + 0.044715 * (x * x * x))))


def _mix0_out_kernel(x_ref, ys_ref, u_ref, sb_ref, d_ref, wglu_ref, bglu_ref, wo1_ref, wo2_ref, g_ref, b_ref,
                     wg_ref, wu_ref, wd_ref, g2_ref, b2_ref, o_ref, y_scr):
    x = x_ref[...]
    chunks = y_scr.shape[1] // S5_CHUNK
    per_slab = S5_ROW // LANES
    for s in range(S5_CHUNK):
        for half in range(2):
            for c in range(per_slab):
                y_scr[half * per_slab + c, pl.ds(s, chunks, stride=S5_CHUNK), :] = (
                    ys_ref[2 * s + half, :, c * LANES:(c + 1) * LANES])
    ys = jnp.concatenate([y_scr[c] for c in range(S5_WIDTH // LANES)], axis=1)
    gl = _gelu(ys + d_ref[...] * u_ref[...])
    gate = _sigmoid(_dot(gl.astype(BF16), wglu_ref[...]) + bglu_ref[...])
    m = _dot((gl * gate).astype(BF16), wo1_ref[...]) + _dot(sb_ref[...].astype(BF16), wo2_ref[...])
    x = _layer_norm(ALPHA * x + m, g_ref[...], b_ref[...])
    o_ref[...] = _ffn_sublayer(x, wg_ref, wu_ref, wd_ref, g2_ref, b2_ref)


def mix0_out_ffn(x, ys, u, sb, d, wglu, bglu, wo1, wo2, g, b, ffn):
    t = x.shape[0]
    tm = _row_tile(t)
    rowblk = lambda n: pl.BlockSpec((tm, n), lambda i: (i, 0))
    ffn_operands, ffn_specs = _ffn_operands(ffn)
    return pl.pallas_call(
        _mix0_out_kernel,
        grid=(t // tm,),
        in_specs=[rowblk(D_MODEL),
                  pl.BlockSpec((S5_SLABS, tm // S5_CHUNK, S5_ROW), lambda i: (0, i, 0)),
                  rowblk(S5_WIDTH), rowblk(SB_WIDTH),
                  _full((1, S5_WIDTH)), _full((S5_WIDTH, S5_WIDTH)), _full((1, S5_WIDTH)),
                  _full((S5_WIDTH, D_MODEL)), _full((SB_WIDTH, D_MODEL)),
                  _full((1, D_MODEL)), _full((1, D_MODEL))] + ffn_specs,
        out_specs=rowblk(D_MODEL),
        out_shape=jax.ShapeDtypeStruct((t, D_MODEL), F32),
        scratch_shapes=[pltpu.VMEM((S5_WIDTH // LANES, tm, LANES), F32)],
        compiler_params=_cparams(("parallel",)),
        name="mix0_out_ffn",
    )(x, ys, u, sb, d.reshape(1, -1), wglu, bglu.reshape(1, -1), wo1, wo2, g.reshape(1, -1), b.reshape(1, -1),
      *ffn_operands)


def _ssd_kernel(xbc_ref, z_ref, dt_ref, dtt_ref, conv0_ref, h0_ref, cw_ref, cb_ref, dtb_ref, dtbt_ref,
                alog_ref, alogt_ref, dsk_ref, ng_ref, tril_ref, triu_ref,
                y_ref, hl_ref, ext_ref, h_ref, *, lc, nc):
    c = pl.program_id(1)

    @pl.when(c == 0)
    def _():
        ext_ref[0:CONV_PAD, :] = conv0_ref[...]
        h_ref[...] = h0_ref[...]

    ext_ref[CONV_PAD:CONV_PAD + lc, :] = xbc_ref[...]
    ext = ext_ref[...]
    conv = cb_ref[...] + ext[CONV_PAD:, :] * cw_ref[SSD_CONV - 1:SSD_CONV, :]
    for back in range(1, SSD_CONV):
        tap = SSD_CONV - 1 - back
        conv = conv + pltpu.roll(ext, back, 0)[CONV_PAD:, :] * cw_ref[tap:tap + 1, :]
    ext_ref[0:CONV_PAD, :] = ext[lc:lc + CONV_PAD, :]
    act = conv * _sigmoid(conv)

    dt = _softplus(dt_ref[...] + dtb_ref[...])
    dtt = _softplus(dtt_ref[...] + dtbt_ref[...])
    cs = _sum_left(tril_ref[...], dt * (-LOG2E * jnp.exp(alog_ref[...])))
    cst = _sum_right(dtt * (-LOG2E * jnp.exp(alogt_ref[...])), triu_ref[...])
    cs_last = cs[lc - 1:lc, :]

    row = lax.broadcasted_iota(jnp.int32, (lc, lc), 0)
    col = lax.broadcasted_iota(jnp.int32, (lc, lc), 1)
    causal = col <= row
    lanes = 2 * SSD_HEAD_DIM
    lane_lo = lax.broadcasted_iota(jnp.int32, (lc, lanes), 1) < SSD_HEAD_DIM
    pairs_per_group = SSD_PAIRS // SSD_GROUPS

    for g in range(SSD_GROUPS):
        bm = act[:, SSD_INNER + g * SSD_STATE:SSD_INNER + (g + 1) * SSD_STATE].astype(BF16)
        cm = act[:, SSD_INNER + SSD_GN + g * SSD_STATE:SSD_INNER + SSD_GN + (g + 1) * SSD_STATE].astype(BF16)
        cb = _dot_nt(cm, bm)
        gated, sq = [], jnp.zeros((lc, lanes), F32)
        for k in range(g * pairs_per_group, (g + 1) * pairs_per_group):
            sl = slice(k * lanes, (k + 1) * lanes)
            xp = act[:, sl]
            heads = (2 * k, 2 * k + 1)
            pair = lambda v: jnp.where(lane_lo[:v[0].shape[0]], v[0], v[1])
            xdt = xp * pair([dt[:, h:h + 1] for h in heads])
            xdtb = xdt.astype(BF16)
            csb = [jnp.broadcast_to(cs[:, h:h + 1], (lc, lanes)) for h in heads]
            ys = []
            for h, csh in zip(heads, csb):
                seg = csh[:, :lc] - cst[h:h + 1, :]
                w = cb * jnp.exp2(jnp.where(causal, seg, -jnp.inf))
                ys.append(_dot(w.astype(BF16), xdtb))
            y = pair(ys)
            cs_pair = pair(csb)
            last_pair = pair([cs_last[:, h:h + 1] for h in heads])
            ht = h_ref[k]
            y = y + _dot(cm, ht.astype(BF16)) * jnp.exp2(cs_pair)
            xw = (xdt * jnp.exp2(last_pair - cs_pair)).astype(BF16)
            h_ref[k] = ht * jnp.exp2(last_pair) + _dot_tn(bm, xw)
            y = y + dsk_ref[:, sl] * xp
            zz = z_ref[:, sl]
            yg = y * (zz * _sigmoid(zz))
            sq = sq + yg * yg
            gated.append(yg)
        mean_sq = jnp.sum(sq, axis=-1, keepdims=True) * (1.0 / (pairs_per_group * lanes))
        scale = lax.rsqrt(mean_sq + RMS_EPS)
        for k, yg in zip(range(g * pairs_per_group, (g + 1) * pairs_per_group), gated):
            sl = slice(k * lanes, (k + 1) * lanes)
            y_ref[:, sl] = yg * scale * ng_ref[:, sl]

    @pl.when(c == nc - 1)
    def _():
        hl_ref[...] = h_ref[...]


def ssd_mix(xbc, z, dt_raw, conv0, h0, conv_w, conv_b, dt_bias, a_log, d_skip, norm_g, bn, length, lc):
    nc = length // lc
    t = bn * length
    lanes = 2 * SSD_HEAD_DIM
    dtt = jnp.swapaxes(dt_raw.reshape(bn * nc, lc, SSD_HEADS), 1, 2)
    conv0p = jnp.pad(conv0, ((0, 0), (CONV_PAD - (SSD_CONV - 1), 0), (0, 0)))
    a_log = a_log.astype(F32)
    r = lax.broadcasted_iota(jnp.int32, (lc, lc), 0)
    cc = lax.broadcasted_iota(jnp.int32, (lc, lc), 1)
    tril = (cc <= r).astype(BF16)
    triu = (r <= cc).astype(BF16)
    tok = lambda n: pl.BlockSpec((lc, n), lambda b, c: (b * nc + c, 0))
    y, hl = pl.pallas_call(
        functools.partial(_ssd_kernel, lc=lc, nc=nc),
        grid=(bn, nc),
        in_specs=[tok(SSD_CONV_DIM), tok(SSD_INNER), tok(SSD_HEADS),
                  pl.BlockSpec((None, SSD_HEADS, lc), lambda b, c: (b * nc + c, 0, 0)),
                  pl.BlockSpec((None, CONV_PAD, SSD_CONV_DIM), lambda b, c: (b, 0, 0)),
                  pl.BlockSpec((None, SSD_PAIRS, lanes, SSD_STATE), lambda b, c: (b, 0, 0, 0)),
                  _full((SSD_CONV, SSD_CONV_DIM)), _full((1, SSD_CONV_DIM)),
                  _full((1, SSD_HEADS)), _full((SSD_HEADS, 1)), _full((1, SSD_HEADS)), _full((SSD_HEADS, 1)),
                  _full((1, SSD_INNER)), _full((1, SSD_INNER)), _full((lc, lc)), _full((lc, lc))],
        out_specs=[tok(SSD_INNER),
                   pl.BlockSpec((None, SSD_PAIRS, lanes, SSD_STATE), lambda b, c: (b, 0, 0, 0))],
        out_shape=[jax.ShapeDtypeStruct((t, SSD_INNER), F32),
                   jax.ShapeDtypeStruct((bn, SSD_PAIRS, lanes, SSD_STATE), F32)],
        scratch_shapes=[pltpu.VMEM((CONV_PAD + lc, SSD_CONV_DIM), F32),
                        pltpu.VMEM((SSD_PAIRS, lanes, SSD_STATE), F32)],
        compiler_params=_cparams(("parallel", "arbitrary")),
        name="ssd_mix",
    )(xbc, z, dt_raw, dtt, conv0p, jnp.swapaxes(h0.reshape(bn, SSD_PAIRS, lanes, SSD_STATE), 2, 3),
      conv_w, conv_b.reshape(1, -1), dt_bias.reshape(1, -1), dt_bias.reshape(-1, 1),
      a_log.reshape(1, -1), a_log.reshape(-1, 1), jnp.repeat(d_skip, SSD_HEAD_DIM).reshape(1, -1),
      norm_g.reshape(1, -1), tril, triu)
    return y, jnp.swapaxes(hl, 2, 3).reshape(bn, SSD_HEADS, SSD_HEAD_DIM, SSD_STATE)


def _out_ffn_kernel(x_ref, a_ref, w_ref, g_ref, b_ref, wg_ref, wu_ref, wd_ref, g2_ref, b2_ref, o_ref):
    m = _dot(a_ref[...].astype(BF16), w_ref[...])
    x = _layer_norm(ALPHA * x_ref[...] + m, g_ref[...], b_ref[...])
    o_ref[...] = _ffn_sublayer(x, wg_ref, wu_ref, wd_ref, g2_ref, b2_ref)


def out_ffn(x, a, w, g, b, ffn):
    t = x.shape[0]
    tm = _row_tile(t)
    k = a.shape[1]
    ffn_operands, ffn_specs = _ffn_operands(ffn)
    return pl.pallas_call(
        _out_ffn_kernel,
        grid=(t // tm,),
        in_specs=[pl.BlockSpec((tm, D_MODEL), lambda i: (i, 0)), pl.BlockSpec((tm, k), lambda i: (i, 0)),
                  _full((k, D_MODEL)), _full((1, D_MODEL)), _full((1, D_MODEL))] + ffn_specs,
        out_specs=pl.BlockSpec((tm, D_MODEL), lambda i: (i, 0)),
        out_shape=jax.ShapeDtypeStruct((t, D_MODEL), F32),
        compiler_params=_cparams(("parallel",)),
        name="out_ffn",
    )(x, a, w, g.reshape(1, -1), b.reshape(1, -1), *ffn_operands)


def _trunk(x, bn, length, states, w):
    t = bn * length
    x = x.reshape(t, D_MODEL)
    ffn = lambda l, s: (w["wg"][l][s], w["wu"][l][s], w["wd"][l][s], w["ln_g"][l, 2 * s], w["ln_b"][l, 2 * s])
    x = ffn_ln(x, ffn(0, 0))

    u, u_slabs, q_bf, k, k_bf, v, v_bf = mix0_in(x, w["mix0_in"], SB_HEAD_DIM ** -0.5 * LOG2E)
    nj = length // S5_CHUNK
    nstate = S5_GROUPS * S5_STATE
    if states is None:
        h0re = h0im = jnp.zeros((bn, nstate), F32)
    else:
        h0re = states[0][0].astype(F32).reshape(bn, nstate)
        h0im = states[1][0].astype(F32).reshape(bn, nstate)
    y_s5, h_re, h_im = s5_mix(u_slabs, h0re, h0im, w["s5_tiles"], bn, nj)
    s5_re = h_re.reshape(1, bn, S5_GROUPS, S5_STATE)
    s5_im = h_im.reshape(1, bn, S5_GROUPS, S5_STATE)

    seq = lambda a: a.reshape(bn, length, SB_WIDTH)
    if states is None:
        sb = stick_breaking_fresh(seq(q_bf), seq(k_bf), seq(v_bf), bn, length)
    else:
        past = states[2][0].shape[1]
        cache = lambda a: a[0].astype(BF16).reshape(bn, past, SB_WIDTH)
        sb = stick_breaking_cached(seq(q_bf), seq(k_bf), seq(v_bf), cache(states[2]), cache(states[3]),
                                   bn, length)
    x = mix0_out_ffn(x, y_s5, u, sb.reshape(t, SB_WIDTH), w["s5_d"], w["s5_wglu"], w["s5_bglu"],
                     w["mix0_out"][:S5_WIDTH], w["mix0_out"][S5_WIDTH:], w["ln_g"][0, 1], w["ln_b"][0, 1],
                     ffn(0, 1))

    x = ffn_ln(x, ffn(1, 0))
    one = ((F32, 1.0),)
    z, xbc, dt_raw = proj(x, w["ssd_in"], (
        (0, SSD_INNER, one), (SSD_INNER, SSD_INNER + SSD_CONV_DIM, one),
        (SSD_INNER + SSD_CONV_DIM, SSD_INNER + SSD_CONV_DIM + SSD_HEADS, one)))
    if states is None:
        conv0 = jnp.zeros((bn, SSD_CONV - 1, SSD_CONV_DIM), F32)
        hs0 = jnp.zeros((bn, SSD_HEADS, SSD_HEAD_DIM, SSD_STATE), F32)
    else:
        conv0 = states[5][0].astype(F32)
        hs0 = states[4][0].astype(F32)
    lc = min(128, length)
    yg, h_ssd = ssd_mix(xbc, z, dt_raw, conv0, hs0, w["ssd_conv_w"], w["ssd_conv_b"], w["ssd_dt_bias"],
                        w["ssd_a_log"], w["ssd_d"], w["ssd_norm_g"], bn, length, lc)
    ext = jnp.concatenate([conv0, xbc.reshape(bn, length, SSD_CONV_DIM)[:, length - (SSD_CONV - 1):]], axis=1)
    new_conv = ext[:, ext.shape[1] - (SSD_CONV - 1):]
    x = out_ffn(x, yg, w["ssd_out"], w["ln_g"][1, 1], w["ln_b"][1, 1], ffn(1, 1))

    return (x.reshape(bn, length, D_MODEL), s5_re, s5_im,
            k.reshape(1, bn, length, SB_HEADS, SB_HEAD_DIM), v.reshape(1, bn, length, SB_HEADS, SB_HEAD_DIM),
            h_ssd[None], new_conv[None])


def kernel(x_prompt, x_sample, state_s5_re, state_s5_im, cache_sb_k, cache_sb_v, state_ssd, state_conv, ln_g, ln_b, ffn_w_gate, ffn_w_up, ffn_w_down, mix0_w_in, s5_a_re, s5_a_im, s5_log_dt, s5_b_re, s5_b_im, s5_c_re, s5_c_im, s5_d, s5_w_glu, s5_b_glu, mix0_w_out, ssd_w_in, ssd_conv_w, ssd_conv_b, ssd_dt_bias, ssd_a_log, ssd_d, ssd_norm_g, ssd_w_out):
    bf = lambda a: a.astype(BF16)
    w = {
        "wg": [[bf(ffn_w_gate[l, s]) for s in range(2)] for l in range(DEPTH)],
        "wu": [[bf(ffn_w_up[l, s]) for s in range(2)] for l in range(DEPTH)],
        "wd": [[bf(ffn_w_down[l, s]) for s in range(2)] for l in range(DEPTH)],
        "ln_g": ln_g, "ln_b": ln_b,
        "mix0_in": bf(mix0_w_in[0]),
        "s5_tiles": s5_tiles(s5_a_re[0], s5_a_im[0], s5_log_dt[0], s5_b_re[0], s5_b_im[0], s5_c_re[0], s5_c_im[0]),
        "s5_d": s5_d[0], "s5_wglu": bf(s5_w_glu[0]), "s5_bglu": s5_b_glu[0], "mix0_out": bf(mix0_w_out[0]),
        "ssd_in": bf(ssd_w_in[0]), "ssd_conv_w": ssd_conv_w[0], "ssd_conv_b": ssd_conv_b[0],
        "ssd_dt_bias": ssd_dt_bias[0], "ssd_a_log": ssd_a_log[0], "ssd_d": ssd_d[0],
        "ssd_norm_g": ssd_norm_g[0], "ssd_out": bf(ssd_w_out[0]),
    }
    bp, lp = x_prompt.shape[0], x_prompt.shape[1]
    bs, ls = x_sample.shape[0], x_sample.shape[1]
    yp, s5rp, s5ip, kp, vp, ssdp, convp = _trunk(x_prompt, bp, lp, None, w)
    ys, s5rs, s5is, ks, vs, ssds, convs = _trunk(
        x_sample, bs, ls, (state_s5_re, state_s5_im, cache_sb_k, cache_sb_v, state_ssd, state_conv), w)
    return (yp, ys, s5rp, s5ip, kp, vp, ssdp, convp, s5rs, s5is, ks, vs, ssds, convs)
```

```python
import functools
import math

import jax
import jax.numpy as jnp
from jax import lax
from jax.experimental import pallas as pl
from jax.experimental.pallas import tpu as pltpu

F32 = jnp.float32
BF16 = jnp.bfloat16

D_MODEL = 1024
DEPTH = 2
ALPHA = (2.0 * DEPTH) ** 0.25
LN_EPS = 1e-5
RMS_EPS = 1e-5
D_FF = 2816

S5_WIDTH = 512
S5_GROUP = 16
S5_GROUPS = 32
S5_STATE = 64
S5_CHUNK = 16
S5_ROW = S5_CHUNK * S5_GROUP
SB_HEADS = 8
SB_HEAD_DIM = 64
SB_WIDTH = 512
MIX0_IN = 2048

SSD_INNER = 2048
SSD_HEAD_DIM = 64
SSD_HEADS = 32
SSD_GROUPS = 4
SSD_STATE = 128
SSD_CONV = 4
SSD_GN = 512
SSD_CONV_DIM = 3072
SSD_PAIRS = SSD_HEADS // 2
CONV_PAD = 8

LANES = 128
VMEM_LIMIT = 56 * 1024 * 1024


def _cparams(sem):
    return pltpu.CompilerParams(dimension_semantics=sem, vmem_limit_bytes=VMEM_LIMIT)


def _sigmoid(x):
    return 1.0 / (1.0 + jnp.exp2(x * -math.log2(math.e)))


def _softplus(x):
    return jnp.maximum(x, 0.0) + jnp.log1p(jnp.exp(-jnp.abs(x)))


def _layer_norm(y, g, b):
    mu = jnp.mean(y, axis=-1, keepdims=True)
    d = y - mu
    var = jnp.mean(d * d, axis=-1, keepdims=True)
    return d * lax.rsqrt(var + LN_EPS) * g + b


def _dot(a, b):
    return jnp.dot(a, b, preferred_element_type=F32)


def _dot_nt(a, b):
    return lax.dot_general(a, b, (((1,), (1,)), ((), ())), preferred_element_type=F32)


def _dot_tn(a, b):
    return lax.dot_general(a, b, (((0,), (0,)), ((), ())), preferred_element_type=F32)


def _split3(x):
    hi = x.astype(BF16)
    r = x - hi.astype(F32)
    mid = r.astype(BF16)
    lo = (r - mid.astype(F32)).astype(BF16)
    return hi, mid, lo


def _sum_right(x, ones_mat):
    hi, mid, lo = _split3(x)
    return _dot(hi, ones_mat) + _dot(mid, ones_mat) + _dot(lo, ones_mat)


def _sum_left(ones_mat, x):
    hi, mid, lo = _split3(x)
    return _dot(ones_mat, hi) + _dot(ones_mat, mid) + _dot(ones_mat, lo)


def _row_tile(t):
    for tm in (512, 256, 128, 64, 32, 16, 8):
        if t % tm == 0:
            return tm
    raise ValueError(f"token count {t} is not a multiple of 8")


def _full(shape):
    return pl.BlockSpec(shape, lambda *_: (0,) * len(shape), pipeline_mode=pl.Buffered(1))


FFN_TF = 1408


def _ffn_sublayer(x, wg_ref, wu_ref, wd_ref, g_ref, b_ref):
    xb = x.astype(BF16)
    acc = None
    for c in range(D_FF // FFN_TF):
        sl = slice(c * FFN_TF, (c + 1) * FFN_TF)
        gate = _dot(xb, wg_ref[:, sl])
        up = _dot(xb, wu_ref[:, sl])
        h = (gate * _sigmoid(gate) * up).astype(BF16)
        part = _dot(h, wd_ref[sl, :])
        acc = part if acc is None else acc + part
    return _layer_norm(ALPHA * x + 0.5 * acc, g_ref[...], b_ref[...])


def _ffn_kernel(x_ref, wg_ref, wu_ref, wd_ref, g_ref, b_ref, o_ref):
    o_ref[...] = _ffn_sublayer(x_ref[...], wg_ref, wu_ref, wd_ref, g_ref, b_ref)


def _ffn_operands(ffn):
    wg, wu, wd, g, b = ffn
    specs = [_full((D_MODEL, D_FF)), _full((D_MODEL, D_FF)), _full((D_FF, D_MODEL)),
             _full((1, D_MODEL)), _full((1, D_MODEL))]
    return [wg, wu, wd, g.reshape(1, D_MODEL), b.reshape(1, D_MODEL)], specs


def ffn_ln(x, ffn):
    t = x.shape[0]
    tm = _row_tile(t)
    operands, specs = _ffn_operands(ffn)
    return pl.pallas_call(
        _ffn_kernel,
        grid=(t // tm,),
        in_specs=[pl.BlockSpec((tm, D_MODEL), lambda i: (i, 0))] + specs,
        out_specs=pl.BlockSpec((tm, D_MODEL), lambda i: (i, 0)),
        out_shape=jax.ShapeDtypeStruct((t, D_MODEL), F32),
        compiler_params=_cparams(("parallel",)),
        name="ffn_ln",
    )(x, *operands)


def _proj_kernel(x_ref, w_ref, *o_refs, plan):
    xb = x_ref[...].astype(BF16)
    refs = iter(o_refs)
    for lo, hi, outs in plan:
        y = _dot(xb, w_ref[:, lo:hi])
        for _, scale in outs:
            o_ref = next(refs)
            o_ref[...] = (y if scale == 1.0 else y * scale).astype(o_ref.dtype)


def proj(x, w, plan):
    t = x.shape[0]
    tm = _row_tile(t)
    n = w.shape[1]
    flat = [(hi - lo, dt) for lo, hi, outs in plan for dt, _ in outs]
    return pl.pallas_call(
        functools.partial(_proj_kernel, plan=plan),
        grid=(t // tm,),
        in_specs=[pl.BlockSpec((tm, D_MODEL), lambda i: (i, 0)), _full((D_MODEL, n))],
        out_specs=[pl.BlockSpec((tm, width), lambda i: (i, 0)) for width, _ in flat],
        out_shape=[jax.ShapeDtypeStruct((t, width), dt) for width, dt in flat],
        compiler_params=_cparams(("parallel",)),
        name="proj",
    )(x, w)


def _mix0_in_kernel(x_ref, w_ref, u_ref, us_ref, q_ref, k_ref, kb_ref, v_ref, vb_ref, u_scr, *, q_scale):
    xb = x_ref[...].astype(BF16)
    sw = S5_WIDTH
    u = _dot(xb, w_ref[:, 0:sw])
    u_ref[...] = u
    for c in range(sw // LANES):
        u_scr[c] = u[:, c * LANES:(c + 1) * LANES]
    chunks = u.shape[0] // S5_CHUNK
    per_slab = S5_ROW // LANES
    for s in range(S5_CHUNK):
        for half in range(2):
            for c in range(per_slab):
                rows = u_scr[half * per_slab + c, pl.ds(s, chunks, stride=S5_CHUNK), :]
                us_ref[2 * s + half, :, c * LANES:(c + 1) * LANES] = rows.astype(BF16)
    q_ref[...] = (_dot(xb, w_ref[:, sw:2 * sw]) * q_scale).astype(BF16)
    k = _dot(xb, w_ref[:, 2 * sw:3 * sw])
    k_ref[...] = k.reshape(k.shape[0], SB_HEADS, SB_HEAD_DIM)
    kb_ref[...] = k.astype(BF16)
    v = _dot(xb, w_ref[:, 3 * sw:4 * sw])
    v_ref[...] = v.reshape(v.shape[0], SB_HEADS, SB_HEAD_DIM)
    vb_ref[...] = v.astype(BF16)


def mix0_in(x, w, q_scale):
    t = x.shape[0]
    tm = _row_tile(t)
    sw = S5_WIDTH
    tok = pl.BlockSpec((tm, sw), lambda i: (i, 0))
    heads = pl.BlockSpec((tm, SB_HEADS, SB_HEAD_DIM), lambda i: (i, 0, 0))
    return pl.pallas_call(
        functools.partial(_mix0_in_kernel, q_scale=q_scale),
        grid=(t // tm,),
        in_specs=[pl.BlockSpec((tm, D_MODEL), lambda i: (i, 0)), _full((D_MODEL, MIX0_IN))],
        out_specs=[tok, pl.BlockSpec((S5_SLABS, tm // S5_CHUNK, S5_ROW), lambda i: (0, i, 0)),
                   tok, heads, tok, heads, tok],
        out_shape=[jax.ShapeDtypeStruct((t, sw), F32),
                   jax.ShapeDtypeStruct((S5_SLABS, t // S5_CHUNK, S5_ROW), BF16),
                   jax.ShapeDtypeStruct((t, sw), BF16), jax.ShapeDtypeStruct((t, SB_HEADS, SB_HEAD_DIM), F32),
                   jax.ShapeDtypeStruct((t, sw), BF16), jax.ShapeDtypeStruct((t, SB_HEADS, SB_HEAD_DIM), F32),
                   jax.ShapeDtypeStruct((t, sw), BF16)],
        scratch_shapes=[pltpu.VMEM((sw // LANES, tm, LANES), F32)],
        compiler_params=_cparams(("parallel",)),
        name="mix0_in",
    )(x, w)


S5_SLABS = 2 * S5_CHUNK
S5_GH = S5_GROUPS // 2
S5_HALF = S5_GH * S5_STATE
S5_CARRY_LANES = LANES


def _s5_tiles_kernel(a_re_ref, a_im_ref, ldt_ref, ar_rows_ref, ai_rows_ref, ldt_rows_ref, b1_ref, b2_ref,
                     bt_re_ref, bt_im_ref, cq_re_ref, cq_im_ref, cm_ref,
                     p_ref, qt_ref, k_ref, a16r_ref, a16i_ref):
    s = pl.program_id(1).astype(F32)

    def cpow(m, ar, ai, dt):
        mag = jnp.exp(m * (ar * dt))
        ang = m * (ai * dt)
        return mag * jnp.cos(ang), mag * jnp.sin(ang)

    def zoh(ar, ai, dt):
        abr, abi = cpow(1.0, ar, ai, dt)
        nr, ni, den = abr - 1.0, abi, ar * ar + ai * ai
        return (nr * ar + ni * ai) / den, (ni * ar - nr * ai) / den

    ar, ai, dt = a_re_ref[...], a_im_ref[...], jnp.exp(ldt_ref[...])
    fr, fi = zoh(ar, ai, dt)
    pwr, pwi = cpow(float(S5_CHUNK - 1) - s, ar, ai, dt)
    cr, ci = pwr * fr - pwi * fi, pwr * fi + pwi * fr
    shape = (S5_ROW, S5_HALF)
    same = (jnp.right_shift(lax.broadcasted_iota(jnp.int32, shape, 0), S5_GROUP.bit_length() - 1)
            == jnp.right_shift(lax.broadcasted_iota(jnp.int32, shape, 1), S5_STATE.bit_length() - 1))
    btr, bti = bt_re_ref[...], bt_im_ref[...]
    p_ref[:, :S5_HALF] = jnp.where(same, cr * btr - ci * bti, 0.0).astype(BF16)
    p_ref[:, S5_HALF:] = jnp.where(same, cr * bti + ci * btr, 0.0).astype(BF16)
    qwr, qwi = cpow(s + 1.0, ar, ai, dt)
    cqr, cqi = cq_re_ref[...], cq_im_ref[...]
    qt_ref[:, :S5_HALF] = jnp.where(same, cqr * qwr - cqi * qwi, 0.0).astype(BF16)
    qt_ref[:, S5_HALF:] = jnp.where(same, -(cqr * qwi + cqi * qwr), 0.0).astype(BF16)
    a16r_ref[...], a16i_ref[...] = cpow(float(S5_CHUNK), ar, ai, dt)

    ar2, ai2, dt2 = ar_rows_ref[...], ai_rows_ref[...], jnp.exp(ldt_rows_ref[...])
    fr2, fi2 = zoh(ar2, ai2, dt2)
    b1, b2 = b1_ref[...], b2_ref[...]
    bb, bbsw = fr2 * b1 + fi2 * b2, fr2 * b2 - fi2 * b1
    lwr, lwi = cpow(s, ar2, ai2, dt2)
    w = (lwr * bb + lwi * bbsw).astype(BF16)
    lag = _dot(w, cm_ref[...].astype(BF16))
    kshape = (S5_ROW, S5_ROW)
    gsh = S5_GROUP.bit_length() - 1
    same_k = (jnp.right_shift(lax.broadcasted_iota(jnp.int32, kshape, 0), gsh)
              == jnp.right_shift(lax.broadcasted_iota(jnp.int32, kshape, 1), gsh))
    k_ref[...] = jnp.where(same_k, lag, 0.0).astype(BF16)


def s5_tiles(a_re, a_im, log_dt, b_re, b_im, c_re, c_im):
    gh, n, pch = S5_GH, S5_STATE, S5_GROUP
    halves = lambda v: v.reshape((2, gh) + v.shape[1:])
    lane = lambda v: halves(v).reshape(2, 1, S5_HALF)
    rows = lambda v: jnp.tile(jnp.repeat(halves(v), pch, axis=1), (1, 1, 2))
    ldt = jnp.broadcast_to(log_dt[:, None], (S5_GROUPS, n))
    bt = lambda b: halves(jnp.swapaxes(b, 1, 2)).reshape(2, S5_ROW, n)
    btr, bti = bt(b_re), bt(b_im)
    b1 = jnp.concatenate([btr, bti], axis=-1)
    b2 = jnp.concatenate([-bti, btr], axis=-1)
    wide = lambda v: jnp.tile(v, (1, 1, gh))
    cq = lambda c: jnp.tile(jnp.transpose(halves(c), (0, 2, 1, 3)).reshape(2, pch, S5_HALF), (1, gh, 1))
    cmt = lambda c: jnp.transpose(halves(c), (0, 3, 1, 2)).reshape(2, n, S5_ROW)
    cm = jnp.concatenate([cmt(c_re), -cmt(c_im)], axis=1)
    half = lambda *shape: pl.BlockSpec((None,) + shape, lambda h, s: (h,) + (0,) * len(shape))
    pt, qt, kt, a16r, a16i = pl.pallas_call(
        _s5_tiles_kernel,
        grid=(2, S5_CHUNK),
        in_specs=[half(1, S5_HALF)] * 3 + [half(S5_ROW, 2 * n)] * 5 + [half(S5_ROW, S5_HALF)] * 4
                 + [half(2 * n, S5_ROW)],
        out_specs=[pl.BlockSpec((None, S5_ROW, 2 * S5_HALF), lambda h, s: (h, s, 0)),
                   pl.BlockSpec((None, None, S5_ROW, 2 * S5_HALF), lambda h, s: (h, s, 0, 0)),
                   pl.BlockSpec((None, S5_ROW, S5_ROW), lambda h, s: (h, S5_CHUNK - 1 - s, 0)),
                   half(1, S5_HALF), half(1, S5_HALF)],
        out_shape=[jax.ShapeDtypeStruct((2, S5_CHUNK * S5_ROW, 2 * S5_HALF), BF16),
                   jax.ShapeDtypeStruct((2, S5_CHUNK, S5_ROW, 2 * S5_HALF), BF16),
                   jax.ShapeDtypeStruct((2, S5_CHUNK * S5_ROW, S5_ROW), BF16),
                   jax.ShapeDtypeStruct((2, 1, S5_HALF), F32), jax.ShapeDtypeStruct((2, 1, S5_HALF), F32)],
        compiler_params=_cparams(("arbitrary", "arbitrary")),
        name="s5_tiles",
    )(lane(a_re), lane(a_im), lane(ldt), rows(a_re), rows(a_im), rows(ldt), b1, b2,
      wide(btr), wide(bti), cq(c_re), cq(c_im), cm)
    nstate = S5_GROUPS * S5_STATE
    return kt, pt, qt, a16r.reshape(1, nstate), a16i.reshape(1, nstate)


def _s5_state_kernel(us_ref, p_ref, sre_ref, sim_ref):
    steps = jnp.concatenate([us_ref[s] for s in range(S5_CHUNK)], axis=1)
    part = _dot(steps, p_ref[...])
    for c in range(S5_HALF // LANES):
        sre_ref[c] = part[:, c * LANES:(c + 1) * LANES]
        sim_ref[c] = part[:, S5_HALF + c * LANES:S5_HALF + (c + 1) * LANES]


def _s5_carry_kernel(sre_ref, sim_ref, h0re_ref, h0im_ref, ar_ref, ai_ref,
                     hre_ref, him_ref, lre_ref, lim_ref, *, bn, nj):
    ar, ai = ar_ref[...], ai_ref[...]

    def body(j, carry):
        hr, hi = carry
        rows = pl.ds(j, bn, stride=nj)
        hre_ref[rows, :] = hr
        him_ref[rows, :] = hi
        return ar * hr - ai * hi + sre_ref[rows, :], ar * hi + ai * hr + sim_ref[rows, :]

    hr, hi = lax.fori_loop(0, nj, body, (h0re_ref[...], h0im_ref[...]))
    lre_ref[...] = hr
    lim_ref[...] = hi


def _s5_out_kernel(us_ref, k_ref, hre_ref, him_ref, q_ref, y_ref):
    blocks = range(S5_HALF // LANES)
    hin = jnp.concatenate([hre_ref[c] for c in blocks] + [him_ref[c] for c in blocks], axis=1).astype(BF16)
    for t in range(S5_CHUNK):
        steps = jnp.concatenate([us_ref[s] for s in range(t + 1)], axis=1)
        lags = k_ref[(S5_CHUNK - 1 - t) * S5_ROW:, :]
        y_ref[t] = _dot(steps, lags) + _dot_nt(hin, q_ref[t])


def s5_mix(us, h0re, h0im, tiles, bn, nj):
    kt, pt, qq, ar, ai = tiles
    r = bn * nj
    nstate = S5_GROUPS * S5_STATE
    us4 = us.reshape(S5_CHUNK, 2, r, S5_ROW)
    half_once = lambda *shape: pl.BlockSpec((None,) + shape, lambda h, i: (h,) + (0,) * len(shape),
                                            pipeline_mode=pl.Buffered(1))
    rows = min(r, 512)
    lanes = S5_CARRY_LANES
    sre, sim = pl.pallas_call(
        _s5_state_kernel,
        grid=(2, r // rows),
        in_specs=[pl.BlockSpec((S5_CHUNK, None, rows, S5_ROW), lambda h, i: (0, h, i, 0)),
                  half_once(S5_CHUNK * S5_ROW, 2 * S5_HALF)],
        out_specs=[pl.BlockSpec((S5_HALF // lanes, rows, lanes), lambda h, i: (h, i, 0))] * 2,
        out_shape=[jax.ShapeDtypeStruct((nstate // lanes, r, lanes), F32)] * 2,
        compiler_params=_cparams(("arbitrary", "arbitrary")),
        name="s5_state",
    )(us4, pt)
    col = lambda n: pl.BlockSpec((n, lanes), lambda i: (0, i))
    blk = pl.BlockSpec((None, r, lanes), lambda i: (i, 0, 0))
    hre, him, lre, lim = pl.pallas_call(
        functools.partial(_s5_carry_kernel, bn=bn, nj=nj),
        grid=(nstate // lanes,),
        in_specs=[blk, blk, col(bn), col(bn), col(1), col(1)],
        out_specs=[blk, blk, col(bn), col(bn)],
        out_shape=[jax.ShapeDtypeStruct((nstate // lanes, r, lanes), F32)] * 2
                  + [jax.ShapeDtypeStruct((bn, nstate), F32)] * 2,
        compiler_params=_cparams(("parallel",)),
        name="s5_carry",
    )(sre, sim, h0re, h0im, ar, ai)
    rows = min(r, 256)
    slabs = pl.BlockSpec((S5_CHUNK, None, rows, S5_ROW), lambda h, i: (0, h, i, 0))
    y = pl.pallas_call(
        _s5_out_kernel,
        grid=(2, r // rows),
        in_specs=[slabs, half_once(S5_CHUNK * S5_ROW, S5_ROW),
                  pl.BlockSpec((S5_HALF // lanes, rows, lanes), lambda h, i: (h, i, 0)),
                  pl.BlockSpec((S5_HALF // lanes, rows, lanes), lambda h, i: (h, i, 0)),
                  half_once(S5_CHUNK, S5_ROW, 2 * S5_HALF)],
        out_specs=slabs,
        out_shape=jax.ShapeDtypeStruct((S5_CHUNK, 2, r, S5_ROW), F32),
        compiler_params=_cparams(("arbitrary", "arbitrary")),
        name="s5_out",
    )(us4, kt, hre, him, qq)
    return y.reshape(S5_SLABS, r, S5_ROW), lre, lim


SB_TK = 256
LOG2E = math.log2(math.e)
SB_SKIP_LOG2 = 152.0


def _sb_chains(tasks, ones):
    zs = [_dot_nt(q, k) for q, k, _, _, _ in tasks]
    tails = [jnp.log2(1.0 + jnp.exp2(-jnp.abs(z))) for z in zs]
    sps = [jnp.maximum(z, 0.0) + t for z, t in zip(zs, tails)]
    sps = [sp if task[3] is None else jnp.where(task[3], sp, 0.0) for sp, task in zip(sps, tasks)]
    mass = [jnp.sum(sp, axis=-1, keepdims=True) for sp in sps]
    his = [sp.astype(BF16) for sp in sps]
    los = [(sp - hi.astype(F32)).astype(BF16) for sp, hi in zip(sps, his)]
    sums = [_dot(jnp.concatenate([hi, lo], axis=0), ones) for hi, lo in zip(his, los)]
    outs = []
    for i, (_, _, v, mask, carry) in enumerate(tasks):
        rows = zs[i].shape[0]
        suffix = sums[i][:rows] + sums[i][rows:]
        logw = jnp.minimum(zs[i], 0.0) - tails[i] - suffix
        if carry is not None:
            logw = logw - (mass[carry] if isinstance(carry, int) else carry)
        w = jnp.exp2(logw)
        if mask is not None:
            w = jnp.where(mask, w, 0.0)
        outs.append(_dot(w.astype(BF16), v))
    return outs, mass


def _sb_block(q2, k, v, ones, carry, mask):
    half = q2.shape[0] // 2
    halves = (slice(0, half), slice(half, 2 * half))
    outs, mass = _sb_chains([(q2[h], k, v, None if mask is None else mask[h],
                              None if carry is None else carry[h]) for h in halves], ones)
    return jnp.concatenate(outs, axis=0), jnp.concatenate(mass, axis=0)


def _sb_walk(q2, load_kv, ones, blocks, acc, carry):
    def cond(s):
        return jnp.logical_and(s[0] < blocks, s[1] < SB_SKIP_LOG2)

    def body(s):
        i, _, acc, carry = s
        rows = pl.ds(pl.multiple_of((blocks - 1 - i) * SB_TK, SB_TK), SB_TK)
        k, v = load_kv(rows)
        d, m = _sb_block(q2, k, v, ones, carry, None)
        carry = carry + m
        return i + 1, jnp.min(carry), acc + d, carry

    return lax.while_loop(cond, body, (jnp.int32(0), jnp.min(carry), acc, carry))[2]


def _sb_prompt_kernel(q_ref, kd_ref, vd_ref, kp_ref, vp_ref, ones_ref, o_ref):
    i = pl.program_id(2)
    t = SB_TK
    lanes = 2 * SB_HEAD_DIM
    q = q_ref[...]
    lane_lo = lax.broadcasted_iota(jnp.int32, (2 * t, lanes), 1) < SB_HEAD_DIM
    zero = jnp.zeros_like(q)
    heads = (jnp.where(lane_lo, q, zero), jnp.where(lane_lo, zero, q))
    row = lax.broadcasted_iota(jnp.int32, (t, t), 0)
    col = lax.broadcasted_iota(jnp.int32, (t, t), 1)
    earlier = col < row
    ones = ones_ref[...]
    kd = [kd_ref[s * t:(s + 1) * t, :] for s in range(2)]
    vd = [vd_ref[s * t:(s + 1) * t, :] for s in range(2)]
    qs = [[h[s * t:(s + 1) * t, :] for h in heads] for s in range(2)]

    def front(with_past):
        tasks = [(qs[s][h], kd[s], vd[s], earlier, None) for s in range(2) for h in range(2)]
        tasks += [(qs[1][h], kd[0], vd[0], None, 2 + h) for h in range(2)]
        if with_past:
            rows = pl.ds(pl.multiple_of((2 * i - 1) * t, t), t)
            kp, vp = kp_ref[rows, :], vp_ref[rows, :]
            tasks += [(qs[0][h], kp, vp, None, h) for h in range(2)]
        outs, mass = _sb_chains(tasks, ones)
        stack = lambda a, b: jnp.concatenate([a, b], axis=0)
        acc1, m1 = stack(outs[2] + outs[4], outs[3] + outs[5]), stack(mass[2] + mass[4], mass[3] + mass[5])
        if with_past:
            acc0, m0 = stack(outs[0] + outs[6], outs[1] + outs[7]), stack(mass[0] + mass[6], mass[1] + mass[7])
        else:
            acc0, m0 = stack(outs[0], outs[1]), stack(mass[0], mass[1])
        return acc0, m0, acc1, m1

    acc0, m0, acc1, m1 = lax.cond(i > 0, lambda: front(True), lambda: front(False))
    q2 = [jnp.concatenate(qs[s], axis=0) for s in range(2)]
    load_kv = lambda rows: (kp_ref[rows, :], vp_ref[rows, :])
    acc0 = _sb_walk(q2[0], load_kv, ones, jnp.maximum(2 * i - 1, 0), acc0, m0)
    acc1 = _sb_walk(q2[1], load_kv, ones, 2 * i, acc1, m1)
    first = lax.broadcasted_iota(jnp.int32, (t, lanes), 1) < SB_HEAD_DIM
    o_ref[0:t, :] = jnp.where(first, acc0[:t], acc0[t:])
    o_ref[t:2 * t, :] = jnp.where(first, acc1[:t], acc1[t:])


def _sb_cached_kernel(q_ref, kd_ref, vd_ref, kp_ref, vp_ref, ud_ref, up_ref, o_ref, *, tq, blocks):
    lanes = 2 * SB_HEAD_DIM
    lane_lo = lax.broadcasted_iota(jnp.int32, (tq, lanes), 1) < SB_HEAD_DIM
    row = lax.broadcasted_iota(jnp.int32, (2 * tq, tq), 0)
    col = lax.broadcasted_iota(jnp.int32, (2 * tq, tq), 1)
    earlier = col < jnp.where(row >= tq, row - tq, row)
    for pair in range(SB_HEADS // 2):
        cols = slice(pair * lanes, (pair + 1) * lanes)
        q = q_ref[:, cols]
        zero = jnp.zeros_like(q)
        q2 = jnp.concatenate([jnp.where(lane_lo, q, zero), jnp.where(lane_lo, zero, q)], axis=0)
        acc, mass = _sb_block(q2, kd_ref[:, cols], vd_ref[:, cols], ud_ref[...], None, earlier)

        def load_kv(rows, cols=cols):
            flat = lambda ref: ref[rows, :, :].reshape(SB_TK, SB_WIDTH)[:, cols].astype(BF16)
            return flat(kp_ref), flat(vp_ref)

        acc = _sb_walk(q2, load_kv, up_ref[...], blocks, acc, mass)
        o_ref[:, cols] = jnp.where(lane_lo, acc[:tq], acc[tq:])


def _strict_lower_ones(n):
    r = lax.broadcasted_iota(jnp.int32, (n, n), 0)
    c = lax.broadcasted_iota(jnp.int32, (n, n), 1)
    return (r > c).astype(BF16)


def stick_breaking_fresh(q, k, v, bn, length):
    lanes = 2 * SB_HEAD_DIM
    tq = 2 * SB_TK
    assert length % tq == 0
    blk = pl.BlockSpec((None, tq, lanes), lambda b, h, i: (b, i, h))
    seq = pl.BlockSpec((None, length, lanes), lambda b, h, i: (b, 0, h))
    return pl.pallas_call(
        _sb_prompt_kernel,
        grid=(bn, SB_HEADS // 2, length // tq),
        in_specs=[blk, blk, blk, seq, seq, _full((SB_TK, SB_TK))],
        out_specs=blk,
        out_shape=jax.ShapeDtypeStruct((bn, length, SB_WIDTH), F32),
        compiler_params=_cparams(("parallel", "parallel", "arbitrary")),
        name="stick_breaking",
    )(q, k, v, k, v, _strict_lower_ones(SB_TK))


def stick_breaking_cached(q, k_new, v_new, k_past, v_past, bn, lq):
    lp = k_past.shape[1]
    assert lp % SB_TK == 0
    blk = pl.BlockSpec((None, lq, SB_WIDTH), lambda b: (b, 0, 0))
    past = pl.BlockSpec((None, lp, SB_HEADS, SB_HEAD_DIM), lambda b: (b, 0, 0, 0))
    return pl.pallas_call(
        functools.partial(_sb_cached_kernel, tq=lq, blocks=lp // SB_TK),
        grid=(bn,),
        in_specs=[blk, blk, blk, past, past, _full((lq, lq)), _full((SB_TK, SB_TK))],
        out_specs=blk,
        out_shape=jax.ShapeDtypeStruct((bn, lq, SB_WIDTH), F32),
        compiler_params=_cparams(("parallel",)),
        name="stick_breaking_cached",
    )(q, k_new, v_new, k_past, v_past, _strict_lower_ones(lq), _strict_lower_ones(SB_TK))


def _gelu(x):
    return 0.5 * x * (1.0 + jnp.tanh(math.sqrt(2.0 / math.pi) * (x + 0.044715 * (x * x * x))))


def _mix0_out_kernel(x_ref, ys_ref, u_ref, sb_ref, d_ref, wglu_ref, bglu_ref, wo1_ref, wo2_ref, g_ref, b_ref,
                     wg_ref, wu_ref, wd_ref, g2_ref, b2_ref, o_ref, y_scr):
    x = x_ref[...]
    chunks = y_scr.shape[1] // S5_CHUNK
    per_slab = S5_ROW // LANES
    for s in range(S5_CHUNK):
        for half in range(2):
            for c in range(per_slab):
                y_scr[half * per_slab + c, pl.ds(s, chunks, stride=S5_CHUNK), :] = (
                    ys_ref[2 * s + half, :, c * LANES:(c + 1) * LANES])
    ys = jnp.concatenate([y_scr[c] for c in range(S5_WIDTH // LANES)], axis=1)
    gl = _gelu(ys + d_ref[...] * u_ref[...])
    gate = _sigmoid(_dot(gl.astype(BF16), wglu_ref[...]) + bglu_ref[...])
    m = _dot((gl * gate).astype(BF16), wo1_ref[...]) + _dot(sb_ref[...].astype(BF16), wo2_ref[...])
    x = _layer_norm(ALPHA * x + m, g_ref[...], b_ref[...])
    o_ref[...] = _ffn_sublayer(x, wg_ref, wu_ref, wd_ref, g2_ref, b2_ref)


def mix0_out_ffn(x, ys, u, sb, d, wglu, bglu, wo1, wo2, g, b, ffn):
    t = x.shape[0]
    tm = _row_tile(t)
    rowblk = lambda n: pl.BlockSpec((tm, n), lambda i: (i, 0))
    ffn_operands, ffn_specs = _ffn_operands(ffn)
    return pl.pallas_call(
        _mix0_out_kernel,
        grid=(t // tm,),
        in_specs=[rowblk(D_MODEL),
                  pl.BlockSpec((S5_SLABS, tm // S5_CHUNK, S5_ROW), lambda i: (0, i, 0)),
                  rowblk(S5_WIDTH), rowblk(SB_WIDTH),
                  _full((1, S5_WIDTH)), _full((S5_WIDTH, S5_WIDTH)), _full((1, S5_WIDTH)),
                  _full((S5_WIDTH, D_MODEL)), _full((SB_WIDTH, D_MODEL)),
                  _full((1, D_MODEL)), _full((1, D_MODEL))] + ffn_specs,
        out_specs=rowblk(D_MODEL),
        out_shape=jax.ShapeDtypeStruct((t, D_MODEL), F32),
        scratch_shapes=[pltpu.VMEM((S5_WIDTH // LANES, tm, LANES), F32)],
        compiler_params=_cparams(("parallel",)),
        name="mix0_out_ffn",
    )(x, ys, u, sb, d.reshape(1, -1), wglu, bglu.reshape(1, -1), wo1, wo2, g.reshape(1, -1), b.reshape(1, -1),
      *ffn_operands)


def _ssd_kernel(xbc_ref, z_ref, dt_ref, dtt_ref, conv0_ref, h0_ref, cw_ref, cb_ref, dtb_ref, dtbt_ref,
                alog_ref, alogt_ref, dsk_ref, ng_ref, tril_ref, triu_ref,
                y_ref, hl_ref, ext_ref, h_ref, *, lc, nc):
    c = pl.program_id(1)

    @pl.when(c == 0)
    def _():
        ext_ref[0:CONV_PAD, :] = conv0_ref[...]
        h_ref[...] = h0_ref[...]

    ext_ref[CONV_PAD:CONV_PAD + lc, :] = xbc_ref[...]
    ext = ext_ref[...]
    conv = cb_ref[...] + ext[CONV_PAD:, :] * cw_ref[SSD_CONV - 1:SSD_CONV, :]
    for back in range(1, SSD_CONV):
        tap = SSD_CONV - 1 - back
        conv = conv + pltpu.roll(ext, back, 0)[CONV_PAD:, :] * cw_ref[tap:tap + 1, :]
    ext_ref[0:CONV_PAD, :] = ext[lc:lc + CONV_PAD, :]
    act = conv * _sigmoid(conv)

    dt = _softplus(dt_ref[...] + dtb_ref[...])
    dtt = _softplus(dtt_ref[...] + dtbt_ref[...])
    cs = _sum_left(tril_ref[...], dt * (-LOG2E * jnp.exp(alog_ref[...])))
    cst = _sum_right(dtt * (-LOG2E * jnp.exp(alogt_ref[...])), triu_ref[...])
    cs_last = cs[lc - 1:lc, :]

    row = lax.broadcasted_iota(jnp.int32, (lc, lc), 0)
    col = lax.broadcasted_iota(jnp.int32, (lc, lc), 1)
    causal = col <= row
    lanes = 2 * SSD_HEAD_DIM
    lane_lo = lax.broadcasted_iota(jnp.int32, (lc, lanes), 1) < SSD_HEAD_DIM
    pairs_per_group = SSD_PAIRS // SSD_GROUPS

    for g in range(SSD_GROUPS):
        bm = act[:, SSD_INNER + g * SSD_STATE:SSD_INNER + (g + 1) * SSD_STATE].astype(BF16)
        cm = act[:, SSD_INNER + SSD_GN + g * SSD_STATE:SSD_INNER + SSD_GN + (g + 1) * SSD_STATE].astype(BF16)
        cb = _dot_nt(cm, bm)
        gated, sq = [], jnp.zeros((lc, lanes), F32)
        for k in range(g * pairs_per_group, (g + 1) * pairs_per_group):
            sl = slice(k * lanes, (k + 1) * lanes)
            xp = act[:, sl]
            heads = (2 * k, 2 * k + 1)
            pair = lambda v: jnp.where(lane_lo[:v[0].shape[0]], v[0], v[1])
            xdt = xp * pair([dt[:, h:h + 1] for h in heads])
            xdtb = xdt.astype(BF16)
            csb = [jnp.broadcast_to(cs[:, h:h + 1], (lc, lanes)) for h in heads]
            ys = []
            for h, csh in zip(heads, csb):
                seg = csh[:, :lc] - cst[h:h + 1, :]
                w = cb * jnp.exp2(jnp.where(causal, seg, -jnp.inf))
                ys.append(_dot(w.astype(BF16), xdtb))
            y = pair(ys)
            cs_pair = pair(csb)
            last_pair = pair([cs_last[:, h:h + 1] for h in heads])
            ht = h_ref[k]
            y = y + _dot(cm, ht.astype(BF16)) * jnp.exp2(cs_pair)
            xw = (xdt * jnp.exp2(last_pair - cs_pair)).astype(BF16)
            h_ref[k] = ht * jnp.exp2(last_pair) + _dot_tn(bm, xw)
            y = y + dsk_ref[:, sl] * xp
            zz = z_ref[:, sl]
            yg = y * (zz * _sigmoid(zz))
            sq = sq + yg * yg
            gated.append(yg)
        mean_sq = jnp.sum(sq, axis=-1, keepdims=True) * (1.0 / (pairs_per_group * lanes))
        scale = lax.rsqrt(mean_sq + RMS_EPS)
        for k, yg in zip(range(g * pairs_per_group, (g + 1) * pairs_per_group), gated):
            sl = slice(k * lanes, (k + 1) * lanes)
            y_ref[:, sl] = yg * scale * ng_ref[:, sl]

    @pl.when(c == nc - 1)
    def _():
        hl_ref[...] = h_ref[...]


def ssd_mix(xbc, z, dt_raw, conv0, h0, conv_w, conv_b, dt_bias, a_log, d_skip, norm_g, bn, length, lc):
    nc = length // lc
    t = bn * length
    lanes = 2 * SSD_HEAD_DIM
    dtt = jnp.swapaxes(dt_raw.reshape(bn * nc, lc, SSD_HEADS), 1, 2)
    conv0p = jnp.pad(conv0, ((0, 0), (CONV_PAD - (SSD_CONV - 1), 0), (0, 0)))
    a_log = a_log.astype(F32)
    r = lax.broadcasted_iota(jnp.int32, (lc, lc), 0)
    cc = lax.broadcasted_iota(jnp.int32, (lc, lc), 1)
    tril = (cc <= r).astype(BF16)
    triu = (r <= cc).astype(BF16)
    tok = lambda n: pl.BlockSpec((lc, n), lambda b, c: (b * nc + c, 0))
    y, hl = pl.pallas_call(
        functools.partial(_ssd_kernel, lc=lc, nc=nc),
        grid=(bn, nc),
        in_specs=[tok(SSD_CONV_DIM), tok(SSD_INNER), tok(SSD_HEADS),
                  pl.BlockSpec((None, SSD_HEADS, lc), lambda b, c: (b * nc + c, 0, 0)),
                  pl.BlockSpec((None, CONV_PAD, SSD_CONV_DIM), lambda b, c: (b, 0, 0)),
                  pl.BlockSpec((None, SSD_PAIRS, lanes, SSD_STATE), lambda b, c: (b, 0, 0, 0)),
                  _full((SSD_CONV, SSD_CONV_DIM)), _full((1, SSD_CONV_DIM)),
                  _full((1, SSD_HEADS)), _full((SSD_HEADS, 1)), _full((1, SSD_HEADS)), _full((SSD_HEADS, 1)),
                  _full((1, SSD_INNER)), _full((1, SSD_INNER)), _full((lc, lc)), _full((lc, lc))],
        out_specs=[tok(SSD_INNER),
                   pl.BlockSpec((None, SSD_PAIRS, lanes, SSD_STATE), lambda b, c: (b, 0, 0, 0))],
        out_shape=[jax.ShapeDtypeStruct((t, SSD_INNER), F32),
                   jax.ShapeDtypeStruct((bn, SSD_PAIRS, lanes, SSD_STATE), F32)],
        scratch_shapes=[pltpu.VMEM((CONV_PAD + lc, SSD_CONV_DIM), F32),
                        pltpu.VMEM((SSD_PAIRS, lanes, SSD_STATE), F32)],
        compiler_params=_cparams(("parallel", "arbitrary")),
        name="ssd_mix",
    )(xbc, z, dt_raw, dtt, conv0p, jnp.swapaxes(h0.reshape(bn, SSD_PAIRS, lanes, SSD_STATE), 2, 3),
      conv_w, conv_b.reshape(1, -1), dt_bias.reshape(1, -1), dt_bias.reshape(-1, 1),
      a_log.reshape(1, -1), a_log.reshape(-1, 1), jnp.repeat(d_skip, SSD_HEAD_DIM).reshape(1, -1),
      norm_g.reshape(1, -1), tril, triu)
    return y, jnp.swapaxes(hl, 2, 3).reshape(bn, SSD_HEADS, SSD_HEAD_DIM, SSD_STATE)


def _out_ffn_kernel(x_ref, a_ref, w_ref, g_ref, b_ref, wg_ref, wu_ref, wd_ref, g2_ref, b2_ref, o_ref):
    m = _dot(a_ref[...].astype(BF16), w_ref[...])
    x = _layer_norm(ALPHA * x_ref[...] + m, g_ref[...], b_ref[...])
    o_ref[...] = _ffn_sublayer(x, wg_ref, wu_ref, wd_ref, g2_ref, b2_ref)


def out_ffn(x, a, w, g, b, ffn):
    t = x.shape[0]
    tm = _row_tile(t)
    k = a.shape[1]
    ffn_operands, ffn_specs = _ffn_operands(ffn)
    return pl.pallas_call(
        _out_ffn_kernel,
        grid=(t // tm,),
        in_specs=[pl.BlockSpec((tm, D_MODEL), lambda i: (i, 0)), pl.BlockSpec((tm, k), lambda i: (i, 0)),
                  _full((k, D_MODEL)), _full((1, D_MODEL)), _full((1, D_MODEL))] + ffn_specs,
        out_specs=pl.BlockSpec((tm, D_MODEL), lambda i: (i, 0)),
        out_shape=jax.ShapeDtypeStruct((t, D_MODEL), F32),
        compiler_params=_cparams(("parallel",)),
        name="out_ffn",
    )(x, a, w, g.reshape(1, -1), b.reshape(1, -1), *ffn_operands)


def _trunk(x, bn, length, states, w):
    t = bn * length
    x = x.reshape(t, D_MODEL)
    ffn = lambda l, s: (w["wg"][l][s], w["wu"][l][s], w["wd"][l][s], w["ln_g"][l, 2 * s], w["ln_b"][l, 2 * s])
    x = ffn_ln(x, ffn(0, 0))

    u, u_slabs, q_bf, k, k_bf, v, v_bf = mix0_in(x, w["mix0_in"], SB_HEAD_DIM ** -0.5 * LOG2E)
    nj = length // S5_CHUNK
    nstate = S5_GROUPS * S5_STATE
    if states is None:
        h0re = h0im = jnp.zeros((bn, nstate), F32)
    else:
        h0re = states[0][0].astype(F32).reshape(bn, nstate)
        h0im = states[1][0].astype(F32).reshape(bn, nstate)
    y_s5, h_re, h_im = s5_mix(u_slabs, h0re, h0im, w["s5_tiles"], bn, nj)
    s5_re = h_re.reshape(1, bn, S5_GROUPS, S5_STATE)
    s5_im = h_im.reshape(1, bn, S5_GROUPS, S5_STATE)

    seq = lambda a: a.reshape(bn, length, SB_WIDTH)
    if states is None:
        sb = stick_breaking_fresh(seq(q_bf), seq(k_bf), seq(v_bf), bn, length)
    else:
        sb = stick_breaking_cached(seq(q_bf), seq(k_bf), seq(v_bf), states[2][0], states[3][0], bn, length)
    x = mix0_out_ffn(x, y_s5, u, sb.reshape(t, SB_WIDTH), w["s5_d"], w["s5_wglu"], w["s5_bglu"],
                     w["mix0_out"][:S5_WIDTH], w["mix0_out"][S5_WIDTH:], w["ln_g"][0, 1], w["ln_b"][0, 1],
                     ffn(0, 1))

    x = ffn_ln(x, ffn(1, 0))
    one = ((F32, 1.0),)
    z, xbc, dt_raw = proj(x, w["ssd_in"], (
        (0, SSD_INNER, one), (SSD_INNER, SSD_INNER + SSD_CONV_DIM, one),
        (SSD_INNER + SSD_CONV_DIM, SSD_INNER + SSD_CONV_DIM + SSD_HEADS, one)))
    if states is None:
        conv0 = jnp.zeros((bn, SSD_CONV - 1, SSD_CONV_DIM), F32)
        hs0 = jnp.zeros((bn, SSD_HEADS, SSD_HEAD_DIM, SSD_STATE), F32)
    else:
        conv0 = states[5][0].astype(F32)
        hs0 = states[4][0].astype(F32)
    lc = min(128, length)
    yg, h_ssd = ssd_mix(xbc, z, dt_raw, conv0, hs0, w["ssd_conv_w"], w["ssd_conv_b"], w["ssd_dt_bias"],
                        w["ssd_a_log"], w["ssd_d"], w["ssd_norm_g"], bn, length, lc)
    ext = jnp.concatenate([conv0, xbc.reshape(bn, length, SSD_CONV_DIM)[:, length - (SSD_CONV - 1):]], axis=1)
    new_conv = ext[:, ext.shape[1] - (SSD_CONV - 1):]
    x = out_ffn(x, yg, w["ssd_out"], w["ln_g"][1, 1], w["ln_b"][1, 1], ffn(1, 1))

    return (x.reshape(bn, length, D_MODEL), s5_re, s5_im,
            k.reshape(1, bn, length, SB_HEADS, SB_HEAD_DIM), v.reshape(1, bn, length, SB_HEADS, SB_HEAD_DIM),
            h_ssd[None], new_conv[None])


def kernel(x_prompt, x_sample, state_s5_re, state_s5_im, cache_sb_k, cache_sb_v, state_ssd, state_conv, ln_g, ln_b, ffn_w_gate, ffn_w_up, ffn_w_down, mix0_w_in, s5_a_re, s5_a_im, s5_log_dt, s5_b_re, s5_b_im, s5_c_re, s5_c_im, s5_d, s5_w_glu, s5_b_glu, mix0_w_out, ssd_w_in, ssd_conv_w, ssd_conv_b, ssd_dt_bias, ssd_a_log, ssd_d, ssd_norm_g, ssd_w_out):
    bf = lambda a: a.astype(BF16)
    w = {
        "wg": [[bf(ffn_w_gate[l, s]) for s in range(2)] for l in range(DEPTH)],
        "wu": [[bf(ffn_w_up[l, s]) for s in range(2)] for l in range(DEPTH)],
        "wd": [[bf(ffn_w_down[l, s]) for s in range(2)] for l in range(DEPTH)],
        "ln_g": ln_g, "ln_b": ln_b,
        "mix0_in": bf(mix0_w_in[0]),
        "s5_tiles": s5_tiles(s5_a_re[0], s5_a_im[0], s5_log_dt[0], s5_b_re[0], s5_b_im[0], s5_c_re[0], s5_c_im[0]),
        "s5_d": s5_d[0], "s5_wglu": bf(s5_w_glu[0]), "s5_bglu": s5_b_glu[0], "mix0_out": bf(mix0_w_out[0]),
        "ssd_in": bf(ssd_w_in[0]), "ssd_conv_w": ssd_conv_w[0], "ssd_conv_b": ssd_conv_b[0],
        "ssd_dt_bias": ssd_dt_bias[0], "ssd_a_log": ssd_a_log[0], "ssd_d": ssd_d[0],
        "ssd_norm_g": ssd_norm_g[0], "ssd_out": bf(ssd_w_out[0]),
    }
    bp, lp = x_prompt.shape[0], x_prompt.shape[1]
    bs, ls = x_sample.shape[0], x_sample.shape[1]
    yp, s5rp, s5ip, kp, vp, ssdp, convp = _trunk(x_prompt, bp, lp, None, w)
    ys, s5rs, s5is, ks, vs, ssds, convs = _trunk(
        x_sample, bs, ls, (state_s5_re, state_s5_im, cache_sb_k, cache_sb_v, state_ssd, state_conv), w)
    return (yp, ys, s5rp, s5ip, kp, vp, ssdp, convp, s5rs, s5is, ks, vs, ssds, convs)
```

```python
import functools
import math

import jax
import jax.numpy as jnp
from jax import lax
from jax.experimental import pallas as pl
from jax.experimental.pallas import tpu as pltpu

F32 = jnp.float32
BF16 = jnp.bfloat16

D_MODEL = 1024
DEPTH = 2
ALPHA = (2.0 * DEPTH) ** 0.25
LN_EPS = 1e-5
RMS_EPS = 1e-5
D_FF = 2816

S5_WIDTH = 512
S5_GROUP = 16
S5_GROUPS = 32
S5_STATE = 64
S5_CHUNK = 16
S5_ROW = S5_CHUNK * S5_GROUP
SB_HEADS = 8
SB_HEAD_DIM = 64
SB_WIDTH = 512
MIX0_IN = 2048

SSD_INNER = 2048
SSD_HEAD_DIM = 64
SSD_HEADS = 32
SSD_GROUPS = 4
SSD_STATE = 128
SSD_CONV = 4
SSD_GN = 512
SSD_CONV_DIM = 3072
SSD_PAIRS = SSD_HEADS // 2
CONV_PAD = 8

LANES = 128
VMEM_LIMIT = 56 * 1024 * 1024


def _cparams(sem):
    return pltpu.CompilerParams(dimension_semantics=sem, vmem_limit_bytes=VMEM_LIMIT)


def _sigmoid(x):
    return 1.0 / (1.0 + jnp.exp2(x * -math.log2(math.e)))


def _softplus(x):
    return jnp.maximum(x, 0.0) + jnp.log1p(jnp.exp(-jnp.abs(x)))


def _layer_norm(y, g, b):
    mu = jnp.mean(y, axis=-1, keepdims=True)
    d = y - mu
    var = jnp.mean(d * d, axis=-1, keepdims=True)
    return d * lax.rsqrt(var + LN_EPS) * g + b


def _dot(a, b):
    return jnp.dot(a, b, preferred_element_type=F32)


def _dot_nt(a, b):
    return lax.dot_general(a, b, (((1,), (1,)), ((), ())), preferred_element_type=F32)


def _dot_tn(a, b):
    return lax.dot_general(a, b, (((0,), (0,)), ((), ())), preferred_element_type=F32)


def _split3(x):
    hi = x.astype(BF16)
    r = x - hi.astype(F32)
    mid = r.astype(BF16)
    lo = (r - mid.astype(F32)).astype(BF16)
    return hi, mid, lo


def _sum_right(x, ones_mat):
    hi, mid, lo = _split3(x)
    return _dot(hi, ones_mat) + _dot(mid, ones_mat) + _dot(lo, ones_mat)


def _sum_left(ones_mat, x):
    hi, mid, lo = _split3(x)
    return _dot(ones_mat, hi) + _dot(ones_mat, mid) + _dot(ones_mat, lo)


def _row_tile(t):
    for tm in (512, 256, 128, 64, 32, 16, 8):
        if t % tm == 0:
            return tm
    raise ValueError(f"token count {t} is not a multiple of 8")


def _full(shape):
    return pl.BlockSpec(shape, lambda *_: (0,) * len(shape), pipeline_mode=pl.Buffered(1))


FFN_TF = 1408


def _ffn_sublayer(x, wg_ref, wu_ref, wd_ref, g_ref, b_ref):
    xb = x.astype(BF16)
    acc = None
    for c in range(D_FF // FFN_TF):
        sl = slice(c * FFN_TF, (c + 1) * FFN_TF)
        gate = _dot(xb, wg_ref[:, sl])
        up = _dot(xb, wu_ref[:, sl])
        h = (gate * _sigmoid(gate) * up).astype(BF16)
        part = _dot(h, wd_ref[sl, :])
        acc = part if acc is None else acc + part
    return _layer_norm(ALPHA * x + 0.5 * acc, g_ref[...], b_ref[...])


def _ffn_kernel(x_ref, wg_ref, wu_ref, wd_ref, g_ref, b_ref, o_ref):
    o_ref[...] = _ffn_sublayer(x_ref[...], wg_ref, wu_ref, wd_ref, g_ref, b_ref)


def _ffn_operands(ffn):
    wg, wu, wd, g, b = ffn
    specs = [_full((D_MODEL, D_FF)), _full((D_MODEL, D_FF)), _full((D_FF, D_MODEL)),
             _full((1, D_MODEL)), _full((1, D_MODEL))]
    return [wg, wu, wd, g.reshape(1, D_MODEL), b.reshape(1, D_MODEL)], specs


def ffn_ln(x, ffn):
    t = x.shape[0]
    tm = _row_tile(t)
    operands, specs = _ffn_operands(ffn)
    return pl.pallas_call(
        _ffn_kernel,
        grid=(t // tm,),
        in_specs=[pl.BlockSpec((tm, D_MODEL), lambda i: (i, 0))] + specs,
        out_specs=pl.BlockSpec((tm, D_MODEL), lambda i: (i, 0)),
        out_shape=jax.ShapeDtypeStruct((t, D_MODEL), F32),
        compiler_params=_cparams(("parallel",)),
        name="ffn_ln",
    )(x, *operands)


def _proj_kernel(x_ref, w_ref, *o_refs, plan):
    xb = x_ref[...].astype(BF16)
    refs = iter(o_refs)
    for lo, hi, outs in plan:
        y = _dot(xb, w_ref[:, lo:hi])
        for _, scale in outs:
            o_ref = next(refs)
            o_ref[...] = (y if scale == 1.0 else y * scale).astype(o_ref.dtype)


def proj(x, w, plan):
    t = x.shape[0]
    tm = _row_tile(t)
    n = w.shape[1]
    flat = [(hi - lo, dt) for lo, hi, outs in plan for dt, _ in outs]
    return pl.pallas_call(
        functools.partial(_proj_kernel, plan=plan),
        grid=(t // tm,),
        in_specs=[pl.BlockSpec((tm, D_MODEL), lambda i: (i, 0)), _full((D_MODEL, n))],
        out_specs=[pl.BlockSpec((tm, width), lambda i: (i, 0)) for width, _ in flat],
        out_shape=[jax.ShapeDtypeStruct((t, width), dt) for width, dt in flat],
        compiler_params=_cparams(("parallel",)),
        name="proj",
    )(x, w)


def _mix0_in_kernel(x_ref, w_ref, u_ref, us_ref, q_ref, k_ref, kb_ref, v_ref, vb_ref, u_scr, *, q_scale):
    xb = x_ref[...].astype(BF16)
    sw = S5_WIDTH
    u = _dot(xb, w_ref[:, 0:sw])
    u_ref[...] = u
    for c in range(sw // LANES):
        u_scr[c] = u[:, c * LANES:(c + 1) * LANES]
    chunks = u.shape[0] // S5_CHUNK
    per_slab = S5_ROW // LANES
    for s in range(S5_CHUNK):
        for half in range(2):
            for c in range(per_slab):
                rows = u_scr[half * per_slab + c, pl.ds(s, chunks, stride=S5_CHUNK), :]
                us_ref[2 * s + half, :, c * LANES:(c + 1) * LANES] = rows.astype(BF16)
    q_ref[...] = (_dot(xb, w_ref[:, sw:2 * sw]) * q_scale).astype(BF16)
    k = _dot(xb, w_ref[:, 2 * sw:3 * sw])
    k_ref[...] = k.reshape(k.shape[0], SB_HEADS, SB_HEAD_DIM)
    kb_ref[...] = k.astype(BF16)
    v = _dot(xb, w_ref[:, 3 * sw:4 * sw])
    v_ref[...] = v.reshape(v.shape[0], SB_HEADS, SB_HEAD_DIM)
    vb_ref[...] = v.astype(BF16)


def mix0_in(x, w, q_scale):
    t = x.shape[0]
    tm = _row_tile(t)
    sw = S5_WIDTH
    tok = pl.BlockSpec((tm, sw), lambda i: (i, 0))
    heads = pl.BlockSpec((tm, SB_HEADS, SB_HEAD_DIM), lambda i: (i, 0, 0))
    return pl.pallas_call(
        functools.partial(_mix0_in_kernel, q_scale=q_scale),
        grid=(t // tm,),
        in_specs=[pl.BlockSpec((tm, D_MODEL), lambda i: (i, 0)), _full((D_MODEL, MIX0_IN))],
        out_specs=[tok, pl.BlockSpec((S5_SLABS, tm // S5_CHUNK, S5_ROW), lambda i: (0, i, 0)),
                   tok, heads, tok, heads, tok],
        out_shape=[jax.ShapeDtypeStruct((t, sw), F32),
                   jax.ShapeDtypeStruct((S5_SLABS, t // S5_CHUNK, S5_ROW), BF16),
                   jax.ShapeDtypeStruct((t, sw), BF16), jax.ShapeDtypeStruct((t, SB_HEADS, SB_HEAD_DIM), F32),
                   jax.ShapeDtypeStruct((t, sw), BF16), jax.ShapeDtypeStruct((t, SB_HEADS, SB_HEAD_DIM), F32),
                   jax.ShapeDtypeStruct((t, sw), BF16)],
        scratch_shapes=[pltpu.VMEM((sw // LANES, tm, LANES), F32)],
        compiler_params=_cparams(("parallel",)),
        name="mix0_in",
    )(x, w)


S5_SLABS = 2 * S5_CHUNK
S5_GH = S5_GROUPS // 2
S5_HALF = S5_GH * S5_STATE
S5_CARRY_LANES = LANES


def _s5_tiles_kernel(a_re_ref, a_im_ref, ldt_ref, ar_rows_ref, ai_rows_ref, ldt_rows_ref, b1_ref, b2_ref,
                     bt_re_ref, bt_im_ref, cq_re_ref, cq_im_ref, cm_ref,
                     p_ref, qt_ref, k_ref, a16r_ref, a16i_ref):
    s = pl.program_id(1).astype(F32)

    def cpow(m, ar, ai, dt):
        mag = jnp.exp(m * (ar * dt))
        ang = m * (ai * dt)
        return mag * jnp.cos(ang), mag * jnp.sin(ang)

    def zoh(ar, ai, dt):
        abr, abi = cpow(1.0, ar, ai, dt)
        nr, ni, den = abr - 1.0, abi, ar * ar + ai * ai
        return (nr * ar + ni * ai) / den, (ni * ar - nr * ai) / den

    ar, ai, dt = a_re_ref[...], a_im_ref[...], jnp.exp(ldt_ref[...])
    fr, fi = zoh(ar, ai, dt)
    pwr, pwi = cpow(float(S5_CHUNK - 1) - s, ar, ai, dt)
    cr, ci = pwr * fr - pwi * fi, pwr * fi + pwi * fr
    shape = (S5_ROW, S5_HALF)
    same = (jnp.right_shift(lax.broadcasted_iota(jnp.int32, shape, 0), S5_GROUP.bit_length() - 1)
            == jnp.right_shift(lax.broadcasted_iota(jnp.int32, shape, 1), S5_STATE.bit_length() - 1))
    btr, bti = bt_re_ref[...], bt_im_ref[...]
    p_ref[:, :S5_HALF] = jnp.where(same, cr * btr - ci * bti, 0.0).astype(BF16)
    p_ref[:, S5_HALF:] = jnp.where(same, cr * bti + ci * btr, 0.0).astype(BF16)
    qwr, qwi = cpow(s + 1.0, ar, ai, dt)
    cqr, cqi = cq_re_ref[...], cq_im_ref[...]
    qt_ref[:, :S5_HALF] = jnp.where(same, cqr * qwr - cqi * qwi, 0.0).astype(BF16)
    qt_ref[:, S5_HALF:] = jnp.where(same, -(cqr * qwi + cqi * qwr), 0.0).astype(BF16)
    a16r_ref[...], a16i_ref[...] = cpow(float(S5_CHUNK), ar, ai, dt)

    ar2, ai2, dt2 = ar_rows_ref[...], ai_rows_ref[...], jnp.exp(ldt_rows_ref[...])
    fr2, fi2 = zoh(ar2, ai2, dt2)
    b1, b2 = b1_ref[...], b2_ref[...]
    bb, bbsw = fr2 * b1 + fi2 * b2, fr2 * b2 - fi2 * b1
    lwr, lwi = cpow(s, ar2, ai2, dt2)
    w = (lwr * bb + lwi * bbsw).astype(BF16)
    lag = _dot(w, cm_ref[...].astype(BF16))
    kshape = (S5_ROW, S5_ROW)
    gsh = S5_GROUP.bit_length() - 1
    same_k = (jnp.right_shift(lax.broadcasted_iota(jnp.int32, kshape, 0), gsh)
              == jnp.right_shift(lax.broadcasted_iota(jnp.int32, kshape, 1), gsh))
    k_ref[...] = jnp.where(same_k, lag, 0.0).astype(BF16)


def s5_tiles(a_re, a_im, log_dt, b_re, b_im, c_re, c_im):
    gh, n, pch = S5_GH, S5_STATE, S5_GROUP
    halves = lambda v: v.reshape((2, gh) + v.shape[1:])
    lane = lambda v: halves(v).reshape(2, 1, S5_HALF)
    rows = lambda v: jnp.tile(jnp.repeat(halves(v), pch, axis=1), (1, 1, 2))
    ldt = jnp.broadcast_to(log_dt[:, None], (S5_GROUPS, n))
    bt = lambda b: halves(jnp.swapaxes(b, 1, 2)).reshape(2, S5_ROW, n)
    btr, bti = bt(b_re), bt(b_im)
    b1 = jnp.concatenate([btr, bti], axis=-1)
    b2 = jnp.concatenate([-bti, btr], axis=-1)
    wide = lambda v: jnp.tile(v, (1, 1, gh))
    cq = lambda c: jnp.tile(jnp.transpose(halves(c), (0, 2, 1, 3)).reshape(2, pch, S5_HALF), (1, gh, 1))
    cmt = lambda c: jnp.transpose(halves(c), (0, 3, 1, 2)).reshape(2, n, S5_ROW)
    cm = jnp.concatenate([cmt(c_re), -cmt(c_im)], axis=1)
    half = lambda *shape: pl.BlockSpec((None,) + shape, lambda h, s: (h,) + (0,) * len(shape))
    pt, qt, kt, a16r, a16i = pl.pallas_call(
        _s5_tiles_kernel,
        grid=(2, S5_CHUNK),
        in_specs=[half(1, S5_HALF)] * 3 + [half(S5_ROW, 2 * n)] * 5 + [half(S5_ROW, S5_HALF)] * 4
                 + [half(2 * n, S5_ROW)],
        out_specs=[pl.BlockSpec((None, S5_ROW, 2 * S5_HALF), lambda h, s: (h, s, 0)),
                   pl.BlockSpec((None, None, S5_ROW, 2 * S5_HALF), lambda h, s: (h, s, 0, 0)),
                   pl.BlockSpec((None, S5_ROW, S5_ROW), lambda h, s: (h, S5_CHUNK - 1 - s, 0)),
                   half(1, S5_HALF), half(1, S5_HALF)],
        out_shape=[jax.ShapeDtypeStruct((2, S5_CHUNK * S5_ROW, 2 * S5_HALF), BF16),
                   jax.ShapeDtypeStruct((2, S5_CHUNK, S5_ROW, 2 * S5_HALF), BF16),
                   jax.ShapeDtypeStruct((2, S5_CHUNK * S5_ROW, S5_ROW), BF16),
                   jax.ShapeDtypeStruct((2, 1, S5_HALF), F32), jax.ShapeDtypeStruct((2, 1, S5_HALF), F32)],
        compiler_params=_cparams(("arbitrary", "arbitrary")),
        name="s5_tiles",
    )(lane(a_re), lane(a_im), lane(ldt), rows(a_re), rows(a_im), rows(ldt), b1, b2,
      wide(btr), wide(bti), cq(c_re), cq(c_im), cm)
    nstate = S5_GROUPS * S5_STATE
    return kt, pt, qt, a16r.reshape(1, nstate), a16i.reshape(1, nstate)


def _s5_state_kernel(us_ref, p_ref, sre_ref, sim_ref):
    steps = jnp.concatenate([us_ref[s] for s in range(S5_CHUNK)], axis=1)
    part = _dot(steps, p_ref[...])
    for c in range(S5_HALF // LANES):
        sre_ref[c] = part[:, c * LANES:(c + 1) * LANES]
        sim_ref[c] = part[:, S5_HALF + c * LANES:S5_HALF + (c + 1) * LANES]


def _s5_carry_kernel(sre_ref, sim_ref, h0re_ref, h0im_ref, ar_ref, ai_ref,
                     hre_ref, him_ref, lre_ref, lim_ref, *, bn, nj):
    ar, ai = ar_ref[...], ai_ref[...]

    def body(j, carry):
        hr, hi = carry
        rows = pl.ds(j, bn, stride=nj)
        hre_ref[rows, :] = hr
        him_ref[rows, :] = hi
        return ar * hr - ai * hi + sre_ref[rows, :], ar * hi + ai * hr + sim_ref[rows, :]

    hr, hi = lax.fori_loop(0, nj, body, (h0re_ref[...], h0im_ref[...]))
    lre_ref[...] = hr
    lim_ref[...] = hi


def _s5_out_kernel(us_ref, k_ref, hre_ref, him_ref, q_ref, y_ref):
    blocks = range(S5_HALF // LANES)
    hin = jnp.concatenate([hre_ref[c] for c in blocks] + [him_ref[c] for c in blocks], axis=1).astype(BF16)
    for t in range(S5_CHUNK):
        steps = jnp.concatenate([us_ref[s] for s in range(t + 1)], axis=1)
        lags = k_ref[(S5_CHUNK - 1 - t) * S5_ROW:, :]
        y_ref[t] = _dot(steps, lags) + _dot_nt(hin, q_ref[t])


def s5_mix(us, h0re, h0im, tiles, bn, nj):
    kt, pt, qq, ar, ai = tiles
    r = bn * nj
    nstate = S5_GROUPS * S5_STATE
    us4 = us.reshape(S5_CHUNK, 2, r, S5_ROW)
    half_once = lambda *shape: pl.BlockSpec((None,) + shape, lambda h, i: (h,) + (0,) * len(shape),
                                            pipeline_mode=pl.Buffered(1))
    rows = min(r, 512)
    lanes = S5_CARRY_LANES
    sre, sim = pl.pallas_call(
        _s5_state_kernel,
        grid=(2, r // rows),
        in_specs=[pl.BlockSpec((S5_CHUNK, None, rows, S5_ROW), lambda h, i: (0, h, i, 0)),
                  half_once(S5_CHUNK * S5_ROW, 2 * S5_HALF)],
        out_specs=[pl.BlockSpec((S5_HALF // lanes, rows, lanes), lambda h, i: (h, i, 0))] * 2,
        out_shape=[jax.ShapeDtypeStruct((nstate // lanes, r, lanes), F32)] * 2,
        compiler_params=_cparams(("arbitrary", "arbitrary")),
        name="s5_state",
    )(us4, pt)
    col = lambda n: pl.BlockSpec((n, lanes), lambda i: (0, i))
    blk = pl.BlockSpec((None, r, lanes), lambda i: (i, 0, 0))
    hre, him, lre, lim = pl.pallas_call(
        functools.partial(_s5_carry_kernel, bn=bn, nj=nj),
        grid=(nstate // lanes,),
        in_specs=[blk, blk, col(bn), col(bn), col(1), col(1)],
        out_specs=[blk, blk, col(bn), col(bn)],
        out_shape=[jax.ShapeDtypeStruct((nstate // lanes, r, lanes), F32)] * 2
                  + [jax.ShapeDtypeStruct((bn, nstate), F32)] * 2,
        compiler_params=_cparams(("parallel",)),
        name="s5_carry",
    )(sre, sim, h0re, h0im, ar, ai)
    rows = min(r, 256)
    slabs = pl.BlockSpec((S5_CHUNK, None, rows, S5_ROW), lambda h, i: (0, h, i, 0))
    y = pl.pallas_call(
        _s5_out_kernel,
        grid=(2, r // rows),
        in_specs=[slabs, half_once(S5_CHUNK * S5_ROW, S5_ROW),
                  pl.BlockSpec((S5_HALF // lanes, rows, lanes), lambda h, i: (h, i, 0)),
                  pl.BlockSpec((S5_HALF // lanes, rows, lanes), lambda h, i: (h, i, 0)),
                  half_once(S5_CHUNK, S5_ROW, 2 * S5_HALF)],
        out_specs=slabs,
        out_shape=jax.ShapeDtypeStruct((S5_CHUNK, 2, r, S5_ROW), F32),
        compiler_params=_cparams(("arbitrary", "arbitrary")),
        name="s5_out",
    )(us4, kt, hre, him, qq)
    return y.reshape(S5_SLABS, r, S5_ROW), lre, lim


SB_TK = 256
LOG2E = math.log2(math.e)
SB_SKIP_LOG2 = 152.0


def _sb_chains(tasks, ones, keys_on_lanes=False):
    zs = [_dot(q, k) if keys_on_lanes else _dot_nt(q, k) for q, k, _, _, _ in tasks]
    tails = [jnp.log2(1.0 + jnp.exp2(-jnp.abs(z))) for z in zs]
    sps = [jnp.maximum(z, 0.0) + t for z, t in zip(zs, tails)]
    sps = [sp if task[3] is None else jnp.where(task[3], sp, 0.0) for sp, task in zip(sps, tasks)]
    mass = [jnp.sum(sp, axis=-1, keepdims=True) for sp in sps]
    his = [sp.astype(BF16) for sp in sps]
    los = [(sp - hi.astype(F32)).astype(BF16) for sp, hi in zip(sps, his)]
    sums = [_dot(jnp.concatenate([hi, lo], axis=0), ones) for hi, lo in zip(his, los)]
    outs = []
    for i, (_, _, v, mask, carry) in enumerate(tasks):
        rows = zs[i].shape[0]
        suffix = sums[i][:rows] + sums[i][rows:]
        logw = jnp.minimum(zs[i], 0.0) - tails[i] - suffix
        if carry is not None:
            logw = logw - (mass[carry] if isinstance(carry, int) else carry)
        w = jnp.exp2(logw)
        if mask is not None:
            w = jnp.where(mask, w, 0.0)
        wb = w.astype(BF16)
        outs.append(_dot_nt(wb, v) if keys_on_lanes else _dot(wb, v))
    return outs, mass


def _sb_block(q2, k, v, ones, carry, mask, keys_on_lanes=False):
    half = q2.shape[0] // 2
    halves = (slice(0, half), slice(half, 2 * half))
    outs, mass = _sb_chains([(q2[h], k, v, None if mask is None else mask[h],
                              None if carry is None else carry[h]) for h in halves], ones, keys_on_lanes)
    return jnp.concatenate(outs, axis=0), jnp.concatenate(mass, axis=0)


def _sb_walk(q2, load_kv, ones, blocks, acc, carry, keys_on_lanes=False):
    def cond(s):
        return jnp.logical_and(s[0] < blocks, s[1] < SB_SKIP_LOG2)

    def body(s):
        i, _, acc, carry = s
        rows = pl.ds(pl.multiple_of((blocks - 1 - i) * SB_TK, SB_TK), SB_TK)
        k, v = load_kv(rows)
        d, m = _sb_block(q2, k, v, ones, carry, None, keys_on_lanes)
        carry = carry + m
        return i + 1, jnp.min(carry), acc + d, carry

    return lax.while_loop(cond, body, (jnp.int32(0), jnp.min(carry), acc, carry))[2]


def _sb_prompt_kernel(q_ref, kd_ref, vd_ref, kp_ref, vp_ref, ones_ref, o_ref):
    i = pl.program_id(2)
    t = SB_TK
    lanes = 2 * SB_HEAD_DIM
    q = q_ref[...]
    lane_lo = lax.broadcasted_iota(jnp.int32, (2 * t, lanes), 1) < SB_HEAD_DIM
    zero = jnp.zeros_like(q)
    heads = (jnp.where(lane_lo, q, zero), jnp.where(lane_lo, zero, q))
    row = lax.broadcasted_iota(jnp.int32, (t, t), 0)
    col = lax.broadcasted_iota(jnp.int32, (t, t), 1)
    earlier = col < row
    ones = ones_ref[...]
    kd = [kd_ref[s * t:(s + 1) * t, :] for s in range(2)]
    vd = [vd_ref[s * t:(s + 1) * t, :] for s in range(2)]
    qs = [[h[s * t:(s + 1) * t, :] for h in heads] for s in range(2)]

    def front(with_past):
        tasks = [(qs[s][h], kd[s], vd[s], earlier, None) for s in range(2) for h in range(2)]
        tasks += [(qs[1][h], kd[0], vd[0], None, 2 + h) for h in range(2)]
        if with_past:
            rows = pl.ds(pl.multiple_of((2 * i - 1) * t, t), t)
            kp, vp = kp_ref[rows, :], vp_ref[rows, :]
            tasks += [(qs[0][h], kp, vp, None, h) for h in range(2)]
        outs, mass = _sb_chains(tasks, ones)
        stack = lambda a, b: jnp.concatenate([a, b], axis=0)
        acc1, m1 = stack(outs[2] + outs[4], outs[3] + outs[5]), stack(mass[2] + mass[4], mass[3] + mass[5])
        if with_past:
            acc0, m0 = stack(outs[0] + outs[6], outs[1] + outs[7]), stack(mass[0] + mass[6], mass[1] + mass[7])
        else:
            acc0, m0 = stack(outs[0], outs[1]), stack(mass[0], mass[1])
        return acc0, m0, acc1, m1

    acc0, m0, acc1, m1 = lax.cond(i > 0, lambda: front(True), lambda: front(False))
    q2 = [jnp.concatenate(qs[s], axis=0) for s in range(2)]
    load_kv = lambda rows: (kp_ref[rows, :], vp_ref[rows, :])
    acc0 = _sb_walk(q2[0], load_kv, ones, jnp.maximum(2 * i - 1, 0), acc0, m0)
    acc1 = _sb_walk(q2[1], load_kv, ones, 2 * i, acc1, m1)
    first = lax.broadcasted_iota(jnp.int32, (t, lanes), 1) < SB_HEAD_DIM
    o_ref[0:t, :] = jnp.where(first, acc0[:t], acc0[t:])
    o_ref[t:2 * t, :] = jnp.where(first, acc1[:t], acc1[t:])


def _sb_cached_kernel(q_ref, kd_ref, vd_ref, kp_ref, vp_ref, ud_ref, up_ref, o_ref, *, tq, blocks):
    lanes = 2 * SB_HEAD_DIM
    lane_lo = lax.broadcasted_iota(jnp.int32, (tq, lanes), 1) < SB_HEAD_DIM
    row = lax.broadcasted_iota(jnp.int32, (2 * tq, tq), 0)
    col = lax.broadcasted_iota(jnp.int32, (2 * tq, tq), 1)
    earlier = col < jnp.where(row >= tq, row - tq, row)
    for pair in range(SB_HEADS // 2):
        cols = slice(pair * lanes, (pair + 1) * lanes)
        q = q_ref[:, cols]
        zero = jnp.zeros_like(q)
        q2 = jnp.concatenate([jnp.where(lane_lo, q, zero), jnp.where(lane_lo, zero, q)], axis=0)
        acc, mass = _sb_block(q2, kd_ref[:, cols], vd_ref[:, cols], ud_ref[...], None, earlier)

        def load_kv(keys, pair=pair):
            flat = lambda ref: ref[2 * pair:2 * pair + 2, :, keys].reshape(lanes, SB_TK).astype(BF16)
            return flat(kp_ref), flat(vp_ref)

        acc = _sb_walk(q2, load_kv, up_ref[...], blocks, acc, mass, keys_on_lanes=True)
        o_ref[:, cols] = jnp.where(lane_lo, acc[:tq], acc[tq:])


def _strict_lower_ones(n):
    r = lax.broadcasted_iota(jnp.int32, (n, n), 0)
    c = lax.broadcasted_iota(jnp.int32, (n, n), 1)
    return (r > c).astype(BF16)


def stick_breaking_fresh(q, k, v, bn, length):
    lanes = 2 * SB_HEAD_DIM
    tq = 2 * SB_TK
    assert length % tq == 0
    blk = pl.BlockSpec((None, tq, lanes), lambda b, h, i: (b, i, h))
    seq = pl.BlockSpec((None, length, lanes), lambda b, h, i: (b, 0, h))
    return pl.pallas_call(
        _sb_prompt_kernel,
        grid=(bn, SB_HEADS // 2, length // tq),
        in_specs=[blk, blk, blk, seq, seq, _full((SB_TK, SB_TK))],
        out_specs=blk,
        out_shape=jax.ShapeDtypeStruct((bn, length, SB_WIDTH), F32),
        compiler_params=_cparams(("parallel", "parallel", "arbitrary")),
        name="stick_breaking",
    )(q, k, v, k, v, _strict_lower_ones(SB_TK))


def stick_breaking_cached(q, k_new, v_new, k_past, v_past, bn, lq):
    lp = k_past.shape[3]
    assert lp % SB_TK == 0
    blk = pl.BlockSpec((None, lq, SB_WIDTH), lambda b: (b, 0, 0))
    past = pl.BlockSpec((None, SB_HEADS, SB_HEAD_DIM, lp), lambda b: (b, 0, 0, 0))
    return pl.pallas_call(
        functools.partial(_sb_cached_kernel, tq=lq, blocks=lp // SB_TK),
        grid=(bn,),
        in_specs=[blk, blk, blk, past, past, _full((lq, lq)), _full((SB_TK, SB_TK))],
        out_specs=blk,
        out_shape=jax.ShapeDtypeStruct((bn, lq, SB_WIDTH), F32),
        compiler_params=_cparams(("parallel",)),
        name="stick_breaking_cached",
    )(q, k_new, v_new, k_past, v_past, _strict_lower_ones(lq), _strict_lower_ones(SB_TK))


def _gelu(x):
    return 0.5 * x * (1.0 + jnp.tanh(math.sqrt(2.0 / math.pi) * (x + 0.044715 * (x * x * x))))


def _mix0_out_kernel(x_ref, ys_ref, u_ref, sb_ref, d_ref, wglu_ref, bglu_ref, wo1_ref, wo2_ref, g_ref, b_ref,
                     wg_ref, wu_ref, wd_ref, g2_ref, b2_ref, o_ref, y_scr):
    x = x_ref[...]
    chunks = y_scr.shape[1] // S5_CHUNK
    per_slab = S5_ROW // LANES
    for s in range(S5_CHUNK):
        for half in range(2):
            for c in range(per_slab):
                y_scr[half * per_slab + c, pl.ds(s, chunks, stride=S5_CHUNK), :] = (
                    ys_ref[2 * s + half, :, c * LANES:(c + 1) * LANES])
    ys = jnp.concatenate([y_scr[c] for c in range(S5_WIDTH // LANES)], axis=1)
    gl = _gelu(ys + d_ref[...] * u_ref[...])
    gate = _sigmoid(_dot(gl.astype(BF16), wglu_ref[...]) + bglu_ref[...])
    m = _dot((gl * gate).astype(BF16), wo1_ref[...]) + _dot(sb_ref[...].astype(BF16), wo2_ref[...])
    x = _layer_norm(ALPHA * x + m, g_ref[...], b_ref[...])
    o_ref[...] = _ffn_sublayer(x, wg_ref, wu_ref, wd_ref, g2_ref, b2_ref)


def mix0_out_ffn(x, ys, u, sb, d, wglu, bglu, wo1, wo2, g, b, ffn):
    t = x.shape[0]
    tm = _row_tile(t)
    rowblk = lambda n: pl.BlockSpec((tm, n), lambda i: (i, 0))
    ffn_operands, ffn_specs = _ffn_operands(ffn)
    return pl.pallas_call(
        _mix0_out_kernel,
        grid=(t // tm,),
        in_specs=[rowblk(D_MODEL),
                  pl.BlockSpec((S5_SLABS, tm // S5_CHUNK, S5_ROW), lambda i: (0, i, 0)),
                  rowblk(S5_WIDTH), rowblk(SB_WIDTH),
                  _full((1, S5_WIDTH)), _full((S5_WIDTH, S5_WIDTH)), _full((1, S5_WIDTH)),
                  _full((S5_WIDTH, D_MODEL)), _full((SB_WIDTH, D_MODEL)),
                  _full((1, D_MODEL)), _full((1, D_MODEL))] + ffn_specs,
        out_specs=rowblk(D_MODEL),
        out_shape=jax.ShapeDtypeStruct((t, D_MODEL), F32),
        scratch_shapes=[pltpu.VMEM((S5_WIDTH // LANES, tm, LANES), F32)],
        compiler_params=_cparams(("parallel",)),
        name="mix0_out_ffn",
    )(x, ys, u, sb, d.reshape(1, -1), wglu, bglu.reshape(1, -1), wo1, wo2, g.reshape(1, -1), b.reshape(1, -1),
      *ffn_operands)


def _ssd_kernel(xbc_ref, z_ref, dt_ref, dtt_ref, conv0_ref, h0_ref, cw_ref, cb_ref, dtb_ref, dtbt_ref,
                alog_ref, alogt_ref, dsk_ref, ng_ref, tril_ref, triu_ref,
                y_ref, hl_ref, ext_ref, h_ref, *, lc, nc):
    c = pl.program_id(1)

    @pl.when(c == 0)
    def _():
        ext_ref[0:CONV_PAD, :] = conv0_ref[...]
        h_ref[...] = h0_ref[...]

    ext_ref[CONV_PAD:CONV_PAD + lc, :] = xbc_ref[...]
    ext = ext_ref[...]
    conv = cb_ref[...] + ext[CONV_PAD:, :] * cw_ref[SSD_CONV - 1:SSD_CONV, :]
    for back in range(1, SSD_CONV):
        tap = SSD_CONV - 1 - back
        conv = conv + pltpu.roll(ext, back, 0)[CONV_PAD:, :] * cw_ref[tap:tap + 1, :]
    ext_ref[0:CONV_PAD, :] = ext[lc:lc + CONV_PAD, :]
    act = conv * _sigmoid(conv)

    dt = _softplus(dt_ref[...] + dtb_ref[...])
    dtt = _softplus(dtt_ref[...] + dtbt_ref[...])
    cs = _sum_left(tril_ref[...], dt * (-LOG2E * jnp.exp(alog_ref[...])))
    cst = _sum_right(dtt * (-LOG2E * jnp.exp(alogt_ref[...])), triu_ref[...])
    cs_last = cs[lc - 1:lc, :]

    row = lax.broadcasted_iota(jnp.int32, (lc, lc), 0)
    col = lax.broadcasted_iota(jnp.int32, (lc, lc), 1)
    causal = col <= row
    lanes = 2 * SSD_HEAD_DIM
    lane_lo = lax.broadcasted_iota(jnp.int32, (lc, lanes), 1) < SSD_HEAD_DIM
    pairs_per_group = SSD_PAIRS // SSD_GROUPS

    for g in range(SSD_GROUPS):
        bm = act[:, SSD_INNER + g * SSD_STATE:SSD_INNER + (g + 1) * SSD_STATE].astype(BF16)
        cm = act[:, SSD_INNER + SSD_GN + g * SSD_STATE:SSD_INNER + SSD_GN + (g + 1) * SSD_STATE].astype(BF16)
        cb = _dot_nt(cm, bm)
        gated, sq = [], jnp.zeros((lc, lanes), F32)
        for k in range(g * pairs_per_group, (g + 1) * pairs_per_group):
            sl = slice(k * lanes, (k + 1) * lanes)
            xp = act[:, sl]
            heads = (2 * k, 2 * k + 1)
            pair = lambda v: jnp.where(lane_lo[:v[0].shape[0]], v[0], v[1])
            xdt = xp * pair([dt[:, h:h + 1] for h in heads])
            xdtb = xdt.astype(BF16)
            csb = [jnp.broadcast_to(cs[:, h:h + 1], (lc, lanes)) for h in heads]
            ys = []
            for h, csh in zip(heads, csb):
                seg = csh[:, :lc] - cst[h:h + 1, :]
                w = cb * jnp.exp2(jnp.where(causal, seg, -jnp.inf))
                ys.append(_dot(w.astype(BF16), xdtb))
            y = pair(ys)
            cs_pair = pair(csb)
            last_pair = pair([cs_last[:, h:h + 1] for h in heads])
            ht = h_ref[k]
            y = y + _dot(cm, ht.astype(BF16)) * jnp.exp2(cs_pair)
            xw = (xdt * jnp.exp2(last_pair - cs_pair)).astype(BF16)
            h_ref[k] = ht * jnp.exp2(last_pair) + _dot_tn(bm, xw)
            y = y + dsk_ref[:, sl] * xp
            zz = z_ref[:, sl]
            yg = y * (zz * _sigmoid(zz))
            sq = sq + yg * yg
            gated.append(yg)
        mean_sq = jnp.sum(sq, axis=-1, keepdims=True) * (1.0 / (pairs_per_group * lanes))
        scale = lax.rsqrt(mean_sq + RMS_EPS)
        for k, yg in zip(range(g * pairs_per_group, (g + 1) * pairs_per_group), gated):
            sl = slice(k * lanes, (k + 1) * lanes)
            y_ref[:, sl] = yg * scale * ng_ref[:, sl]

    @pl.when(c == nc - 1)
    def _():
        hl_ref[...] = h_ref[...]


def ssd_mix(xbc, z, dt_raw, conv0, h0, conv_w, conv_b, dt_bias, a_log, d_skip, norm_g, bn, length, lc):
    nc = length // lc
    t = bn * length
    lanes = 2 * SSD_HEAD_DIM
    dtt = jnp.swapaxes(dt_raw.reshape(bn * nc, lc, SSD_HEADS), 1, 2)
    conv0p = jnp.pad(conv0, ((0, 0), (CONV_PAD - (SSD_CONV - 1), 0), (0, 0)))
    a_log = a_log.astype(F32)
    r = lax.broadcasted_iota(jnp.int32, (lc, lc), 0)
    cc = lax.broadcasted_iota(jnp.int32, (lc, lc), 1)
    tril = (cc <= r).astype(BF16)
    triu = (r <= cc).astype(BF16)
    tok = lambda n: pl.BlockSpec((lc, n), lambda b, c: (b * nc + c, 0))
    y, hl = pl.pallas_call(
        functools.partial(_ssd_kernel, lc=lc, nc=nc),
        grid=(bn, nc),
        in_specs=[tok(SSD_CONV_DIM), tok(SSD_INNER), tok(SSD_HEADS),
                  pl.BlockSpec((None, SSD_HEADS, lc), lambda b, c: (b * nc + c, 0, 0)),
                  pl.BlockSpec((None, CONV_PAD, SSD_CONV_DIM), lambda b, c: (b, 0, 0)),
                  pl.BlockSpec((None, SSD_PAIRS, lanes, SSD_STATE), lambda b, c: (b, 0, 0, 0)),
                  _full((SSD_CONV, SSD_CONV_DIM)), _full((1, SSD_CONV_DIM)),
                  _full((1, SSD_HEADS)), _full((SSD_HEADS, 1)), _full((1, SSD_HEADS)), _full((SSD_HEADS, 1)),
                  _full((1, SSD_INNER)), _full((1, SSD_INNER)), _full((lc, lc)), _full((lc, lc))],
        out_specs=[tok(SSD_INNER),
                   pl.BlockSpec((None, SSD_PAIRS, lanes, SSD_STATE), lambda b, c: (b, 0, 0, 0))],
        out_shape=[jax.ShapeDtypeStruct((t, SSD_INNER), F32),
                   jax.ShapeDtypeStruct((bn, SSD_PAIRS, lanes, SSD_STATE), F32)],
        scratch_shapes=[pltpu.VMEM((CONV_PAD + lc, SSD_CONV_DIM), F32),
                        pltpu.VMEM((SSD_PAIRS, lanes, SSD_STATE), F32)],
        compiler_params=_cparams(("parallel", "arbitrary")),
        name="ssd_mix",
    )(xbc, z, dt_raw, dtt, conv0p, jnp.swapaxes(h0.reshape(bn, SSD_PAIRS, lanes, SSD_STATE), 2, 3),
      conv_w, conv_b.reshape(1, -1), dt_bias.reshape(1, -1), dt_bias.reshape(-1, 1),
      a_log.reshape(1, -1), a_log.reshape(-1, 1), jnp.repeat(d_skip, SSD_HEAD_DIM).reshape(1, -1),
      norm_g.reshape(1, -1), tril, triu)
    return y, jnp.swapaxes(hl, 2, 3).reshape(bn, SSD_HEADS, SSD_HEAD_DIM, SSD_STATE)


def _out_ffn_kernel(x_ref, a_ref, w_ref, g_ref, b_ref, wg_ref, wu_ref, wd_ref, g2_ref, b2_ref, o_ref):
    m = _dot(a_ref[...].astype(BF16), w_ref[...])
    x = _layer_norm(ALPHA * x_ref[...] + m, g_ref[...], b_ref[...])
    o_ref[...] = _ffn_sublayer(x, wg_ref, wu_ref, wd_ref, g2_ref, b2_ref)


def out_ffn(x, a, w, g, b, ffn):
    t = x.shape[0]
    tm = _row_tile(t)
    k = a.shape[1]
    ffn_operands, ffn_specs = _ffn_operands(ffn)
    return pl.pallas_call(
        _out_ffn_kernel,
        grid=(t // tm,),
        in_specs=[pl.BlockSpec((tm, D_MODEL), lambda i: (i, 0)), pl.BlockSpec((tm, k), lambda i: (i, 0)),
                  _full((k, D_MODEL)), _full((1, D_MODEL)), _full((1, D_MODEL))] + ffn_specs,
        out_specs=pl.BlockSpec((tm, D_MODEL), lambda i: (i, 0)),
        out_shape=jax.ShapeDtypeStruct((t, D_MODEL), F32),
        compiler_params=_cparams(("parallel",)),
        name="out_ffn",
    )(x, a, w, g.reshape(1, -1), b.reshape(1, -1), *ffn_operands)


def _trunk(x, bn, length, states, w):
    t = bn * length
    x = x.reshape(t, D_MODEL)
    ffn = lambda l, s: (w["wg"][l][s], w["wu"][l][s], w["wd"][l][s], w["ln_g"][l, 2 * s], w["ln_b"][l, 2 * s])
    x = ffn_ln(x, ffn(0, 0))

    u, u_slabs, q_bf, k, k_bf, v, v_bf = mix0_in(x, w["mix0_in"], SB_HEAD_DIM ** -0.5 * LOG2E)
    nj = length // S5_CHUNK
    nstate = S5_GROUPS * S5_STATE
    if states is None:
        h0re = h0im = jnp.zeros((bn, nstate), F32)
    else:
        h0re = states[0][0].astype(F32).reshape(bn, nstate)
        h0im = states[1][0].astype(F32).reshape(bn, nstate)
    y_s5, h_re, h_im = s5_mix(u_slabs, h0re, h0im, w["s5_tiles"], bn, nj)
    s5_re = h_re.reshape(1, bn, S5_GROUPS, S5_STATE)
    s5_im = h_im.reshape(1, bn, S5_GROUPS, S5_STATE)

    seq = lambda a: a.reshape(bn, length, SB_WIDTH)
    if states is None:
        sb = stick_breaking_fresh(seq(q_bf), seq(k_bf), seq(v_bf), bn, length)
    else:
        cache = lambda a: jnp.transpose(a[0], (0, 2, 3, 1))
        sb = stick_breaking_cached(seq(q_bf), seq(k_bf), seq(v_bf), cache(states[2]), cache(states[3]), bn, length)
    x = mix0_out_ffn(x, y_s5, u, sb.reshape(t, SB_WIDTH), w["s5_d"], w["s5_wglu"], w["s5_bglu"],
                     w["mix0_out"][:S5_WIDTH], w["mix0_out"][S5_WIDTH:], w["ln_g"][0, 1], w["ln_b"][0, 1],
                     ffn(0, 1))

    x = ffn_ln(x, ffn(1, 0))
    one = ((F32, 1.0),)
    z, xbc, dt_raw = proj(x, w["ssd_in"], (
        (0, SSD_INNER, one), (SSD_INNER, SSD_INNER + SSD_CONV_DIM, one),
        (SSD_INNER + SSD_CONV_DIM, SSD_INNER + SSD_CONV_DIM + SSD_HEADS, one)))
    if states is None:
        conv0 = jnp.zeros((bn, SSD_CONV - 1, SSD_CONV_DIM), F32)
        hs0 = jnp.zeros((bn, SSD_HEADS, SSD_HEAD_DIM, SSD_STATE), F32)
    else:
        conv0 = states[5][0].astype(F32)
        hs0 = states[4][0].astype(F32)
    lc = min(128, length)
    yg, h_ssd = ssd_mix(xbc, z, dt_raw, conv0, hs0, w["ssd_conv_w"], w["ssd_conv_b"], w["ssd_dt_bias"],
                        w["ssd_a_log"], w["ssd_d"], w["ssd_norm_g"], bn, length, lc)
    ext = jnp.concatenate([conv0, xbc.reshape(bn, length, SSD_CONV_DIM)[:, length - (SSD_CONV - 1):]], axis=1)
    new_conv = ext[:, ext.shape[1] - (SSD_CONV - 1):]
    x = out_ffn(x, yg, w["ssd_out"], w["ln_g"][1, 1], w["ln_b"][1, 1], ffn(1, 1))

    return (x.reshape(bn, length, D_MODEL), s5_re, s5_im,
            k.reshape(1, bn, length, SB_HEADS, SB_HEAD_DIM), v.reshape(1, bn, length, SB_HEADS, SB_HEAD_DIM),
            h_ssd[None], new_conv[None])


def kernel(x_prompt, x_sample, state_s5_re, state_s5_im, cache_sb_k, cache_sb_v, state_ssd, state_conv, ln_g, ln_b, ffn_w_gate, ffn_w_up, ffn_w_down, mix0_w_in, s5_a_re, s5_a_im, s5_log_dt, s5_b_re, s5_b_im, s5_c_re, s5_c_im, s5_d, s5_w_glu, s5_b_glu, mix0_w_out, ssd_w_in, ssd_conv_w, ssd_conv_b, ssd_dt_bias, ssd_a_log, ssd_d, ssd_norm_g, ssd_w_out):
    bf = lambda a: a.astype(BF16)
    w = {
        "wg": [[bf(ffn_w_gate[l, s]) for s in range(2)] for l in range(DEPTH)],
        "wu": [[bf(ffn_w_up[l, s]) for s in range(2)] for l in range(DEPTH)],
        "wd": [[bf(ffn_w_down[l, s]) for s in range(2)] for l in range(DEPTH)],
        "ln_g": ln_g, "ln_b": ln_b,
        "mix0_in": bf(mix0_w_in[0]),
        "s5_tiles": s5_tiles(s5_a_re[0], s5_a_im[0], s5_log_dt[0], s5_b_re[0], s5_b_im[0], s5_c_re[0], s5_c_im[0]),
        "s5_d": s5_d[0], "s5_wglu": bf(s5_w_glu[0]), "s5_bglu": s5_b_glu[0], "mix0_out": bf(mix0_w_out[0]),
        "ssd_in": bf(ssd_w_in[0]), "ssd_conv_w": ssd_conv_w[0], "ssd_conv_b": ssd_conv_b[0],
        "ssd_dt_bias": ssd_dt_bias[0], "ssd_a_log": ssd_a_log[0], "ssd_d": ssd_d[0],
        "ssd_norm_g": ssd_norm_g[0], "ssd_out": bf(ssd_w_out[0]),
    }
    bp, lp = x_prompt.shape[0], x_prompt.shape[1]
    bs, ls = x_sample.shape[0], x_sample.shape[1]
    yp, s5rp, s5ip, kp, vp, ssdp, convp = _trunk(x_prompt, bp, lp, None, w)
    ys, s5rs, s5is, ks, vs, ssds, convs = _trunk(
        x_sample, bs, ls, (state_s5_re, state_s5_im, cache_sb_k, cache_sb_v, state_ssd, state_conv), w)
    return (yp, ys, s5rp, s5ip, kp, vp, ssdp, convp, s5rs, s5is, ks, vs, ssds, convs)
```

```python
import functools
import math

import jax
import jax.numpy as jnp
from jax import lax
from jax.experimental import pallas as pl
from jax.experimental.pallas import tpu as pltpu

F32 = jnp.float32
BF16 = jnp.bfloat16

D_MODEL = 1024
DEPTH = 2
ALPHA = (2.0 * DEPTH) ** 0.25
LN_EPS = 1e-5
RMS_EPS = 1e-5
D_FF = 2816

S5_WIDTH = 512
S5_GROUP = 16
S5_GROUPS = 32
S5_STATE = 64
S5_CHUNK = 16
S5_ROW = S5_CHUNK * S5_GROUP
SB_HEADS = 8
SB_HEAD_DIM = 64
SB_WIDTH = 512
MIX0_IN = 2048

SSD_INNER = 2048
SSD_HEAD_DIM = 64
SSD_HEADS = 32
SSD_GROUPS = 4
SSD_STATE = 128
SSD_CONV = 4
SSD_GN = 512
SSD_CONV_DIM = 3072
SSD_PAIRS = SSD_HEADS // 2
CONV_PAD = 8

LANES = 128
VMEM_LIMIT = 56 * 1024 * 1024


def _cparams(sem):
    return pltpu.CompilerParams(dimension_semantics=sem, vmem_limit_bytes=VMEM_LIMIT)


def _sigmoid(x):
    return 1.0 / (1.0 + jnp.exp2(x * -math.log2(math.e)))


def _softplus(x):
    return jnp.maximum(x, 0.0) + jnp.log1p(jnp.exp(-jnp.abs(x)))


def _layer_norm(y, g, b):
    mu = jnp.mean(y, axis=-1, keepdims=True)
    d = y - mu
    var = jnp.mean(d * d, axis=-1, keepdims=True)
    return d * lax.rsqrt(var + LN_EPS) * g + b


def _dot(a, b):
    return jnp.dot(a, b, preferred_element_type=F32)


def _dot_nt(a, b):
    return lax.dot_general(a, b, (((1,), (1,)), ((), ())), preferred_element_type=F32)


def _dot_tn(a, b):
    return lax.dot_general(a, b, (((0,), (0,)), ((), ())), preferred_element_type=F32)


def _split3(x):
    hi = x.astype(BF16)
    r = x - hi.astype(F32)
    mid = r.astype(BF16)
    lo = (r - mid.astype(F32)).astype(BF16)
    return hi, mid, lo


def _sum_right(x, ones_mat):
    hi, mid, lo = _split3(x)
    return _dot(hi, ones_mat) + _dot(mid, ones_mat) + _dot(lo, ones_mat)


def _sum_left(ones_mat, x):
    hi, mid, lo = _split3(x)
    return _dot(ones_mat, hi) + _dot(ones_mat, mid) + _dot(ones_mat, lo)


def _row_tile(t):
    for tm in (512, 256, 128, 64, 32, 16, 8):
        if t % tm == 0:
            return tm
    raise ValueError(f"token count {t} is not a multiple of 8")


def _full(shape):
    return pl.BlockSpec(shape, lambda *_: (0,) * len(shape), pipeline_mode=pl.Buffered(1))


MXU_TILE = 256
FFN_SPLIT = (D_FF // MXU_TILE // 2) * MXU_TILE


def _ffn_sublayer(x, wg_ref, wu_ref, wd_ref, g_ref, b_ref):
    xb = x.astype(BF16)
    acc = None
    for sl in (slice(0, FFN_SPLIT), slice(FFN_SPLIT, D_FF)):
        gate = _dot(xb, wg_ref[:, sl])
        up = _dot(xb, wu_ref[:, sl])
        h = (gate * _sigmoid(gate) * up).astype(BF16)
        part = _dot(h, wd_ref[sl, :])
        acc = part if acc is None else acc + part
    return _layer_norm(ALPHA * x + 0.5 * acc, g_ref[...], b_ref[...])


def _ffn_kernel(x_ref, wg_ref, wu_ref, wd_ref, g_ref, b_ref, o_ref):
    o_ref[...] = _ffn_sublayer(x_ref[...], wg_ref, wu_ref, wd_ref, g_ref, b_ref)


def _ffn_operands(ffn):
    wg, wu, wd, g, b = ffn
    specs = [_full((D_MODEL, D_FF)), _full((D_MODEL, D_FF)), _full((D_FF, D_MODEL)),
             _full((1, D_MODEL)), _full((1, D_MODEL))]
    return [wg, wu, wd, g.reshape(1, D_MODEL), b.reshape(1, D_MODEL)], specs


def ffn_ln(x, ffn):
    t = x.shape[0]
    tm = _row_tile(t)
    operands, specs = _ffn_operands(ffn)
    return pl.pallas_call(
        _ffn_kernel,
        grid=(t // tm,),
        in_specs=[pl.BlockSpec((tm, D_MODEL), lambda i: (i, 0))] + specs,
        out_specs=pl.BlockSpec((tm, D_MODEL), lambda i: (i, 0)),
        out_shape=jax.ShapeDtypeStruct((t, D_MODEL), F32),
        compiler_params=_cparams(("parallel",)),
        name="ffn_ln",
    )(x, *operands)


def _proj_kernel(x_ref, w_ref, *o_refs, plan):
    xb = x_ref[...].astype(BF16)
    refs = iter(o_refs)
    for lo, hi, outs in plan:
        y = _dot(xb, w_ref[:, lo:hi])
        for _, scale in outs:
            o_ref = next(refs)
            o_ref[...] = (y if scale == 1.0 else y * scale).astype(o_ref.dtype)


def proj(x, w, plan):
    t = x.shape[0]
    tm = _row_tile(t)
    n = w.shape[1]
    flat = [(hi - lo, dt) for lo, hi, outs in plan for dt, _ in outs]
    return pl.pallas_call(
        functools.partial(_proj_kernel, plan=plan),
        grid=(t // tm,),
        in_specs=[pl.BlockSpec((tm, D_MODEL), lambda i: (i, 0)), _full((D_MODEL, n))],
        out_specs=[pl.BlockSpec((tm, width), lambda i: (i, 0)) for width, _ in flat],
        out_shape=[jax.ShapeDtypeStruct((t, width), dt) for width, dt in flat],
        compiler_params=_cparams(("parallel",)),
        name="proj",
    )(x, w)


def _mix0_in_kernel(x_ref, w_ref, u_ref, us_ref, q_ref, k_ref, kb_ref, v_ref, vb_ref, u_scr, *, q_scale):
    xb = x_ref[...].astype(BF16)
    sw = S5_WIDTH
    u = _dot(xb, w_ref[:, 0:sw])
    u_ref[...] = u
    for c in range(sw // LANES):
        u_scr[c] = u[:, c * LANES:(c + 1) * LANES]
    chunks = u.shape[0] // S5_CHUNK
    per_slab = S5_ROW // LANES
    for s in range(S5_CHUNK):
        for half in range(2):
            for c in range(per_slab):
                rows = u_scr[half * per_slab + c, pl.ds(s, chunks, stride=S5_CHUNK), :]
                us_ref[2 * s + half, :, c * LANES:(c + 1) * LANES] = rows.astype(BF16)
    q_ref[...] = (_dot(xb, w_ref[:, sw:2 * sw]) * q_scale).astype(BF16)
    k = _dot(xb, w_ref[:, 2 * sw:3 * sw])
    k_ref[...] = k.reshape(k.shape[0], SB_HEADS, SB_HEAD_DIM)
    kb_ref[...] = k.astype(BF16)
    v = _dot(xb, w_ref[:, 3 * sw:4 * sw])
    v_ref[...] = v.reshape(v.shape[0], SB_HEADS, SB_HEAD_DIM)
    vb_ref[...] = v.astype(BF16)


def mix0_in(x, w, q_scale):
    t = x.shape[0]
    tm = _row_tile(t)
    sw = S5_WIDTH
    tok = pl.BlockSpec((tm, sw), lambda i: (i, 0))
    heads = pl.BlockSpec((tm, SB_HEADS, SB_HEAD_DIM), lambda i: (i, 0, 0))
    return pl.pallas_call(
        functools.partial(_mix0_in_kernel, q_scale=q_scale),
        grid=(t // tm,),
        in_specs=[pl.BlockSpec((tm, D_MODEL), lambda i: (i, 0)), _full((D_MODEL, MIX0_IN))],
        out_specs=[tok, pl.BlockSpec((S5_SLABS, tm // S5_CHUNK, S5_ROW), lambda i: (0, i, 0)),
                   tok, heads, tok, heads, tok],
        out_shape=[jax.ShapeDtypeStruct((t, sw), F32),
                   jax.ShapeDtypeStruct((S5_SLABS, t // S5_CHUNK, S5_ROW), BF16),
                   jax.ShapeDtypeStruct((t, sw), BF16), jax.ShapeDtypeStruct((t, SB_HEADS, SB_HEAD_DIM), F32),
                   jax.ShapeDtypeStruct((t, sw), BF16), jax.ShapeDtypeStruct((t, SB_HEADS, SB_HEAD_DIM), F32),
                   jax.ShapeDtypeStruct((t, sw), BF16)],
        scratch_shapes=[pltpu.VMEM((sw // LANES, tm, LANES), F32)],
        compiler_params=_cparams(("parallel",)),
        name="mix0_in",
    )(x, w)


S5_SLABS = 2 * S5_CHUNK
S5_GH = S5_GROUPS // 2
S5_HALF = S5_GH * S5_STATE
S5_CARRY_LANES = LANES


def _s5_tiles_kernel(a_re_ref, a_im_ref, ldt_ref, ar_rows_ref, ai_rows_ref, ldt_rows_ref, b1_ref, b2_ref,
                     bt_re_ref, bt_im_ref, cq_re_ref, cq_im_ref, cm_ref,
                     p_ref, qt_ref, k_ref, a16r_ref, a16i_ref):
    s = pl.program_id(1).astype(F32)

    def cpow(m, ar, ai, dt):
        mag = jnp.exp(m * (ar * dt))
        ang = m * (ai * dt)
        return mag * jnp.cos(ang), mag * jnp.sin(ang)

    def zoh(ar, ai, dt):
        abr, abi = cpow(1.0, ar, ai, dt)
        nr, ni, den = abr - 1.0, abi, ar * ar + ai * ai
        return (nr * ar + ni * ai) / den, (ni * ar - nr * ai) / den

    ar, ai, dt = a_re_ref[...], a_im_ref[...], jnp.exp(ldt_ref[...])
    fr, fi = zoh(ar, ai, dt)
    pwr, pwi = cpow(float(S5_CHUNK - 1) - s, ar, ai, dt)
    cr, ci = pwr * fr - pwi * fi, pwr * fi + pwi * fr
    shape = (S5_ROW, S5_HALF)
    same = (jnp.right_shift(lax.broadcasted_iota(jnp.int32, shape, 0), S5_GROUP.bit_length() - 1)
            == jnp.right_shift(lax.broadcasted_iota(jnp.int32, shape, 1), S5_STATE.bit_length() - 1))
    btr, bti = bt_re_ref[...], bt_im_ref[...]
    p_ref[:, :S5_HALF] = jnp.where(same, cr * btr - ci * bti, 0.0).astype(BF16)
    p_ref[:, S5_HALF:] = jnp.where(same, cr * bti + ci * btr, 0.0).astype(BF16)
    qwr, qwi = cpow(s + 1.0, ar, ai, dt)
    cqr, cqi = cq_re_ref[...], cq_im_ref[...]
    qt_ref[:, :S5_HALF] = jnp.where(same, cqr * qwr - cqi * qwi, 0.0).astype(BF16)
    qt_ref[:, S5_HALF:] = jnp.where(same, -(cqr * qwi + cqi * qwr), 0.0).astype(BF16)
    a16r_ref[...], a16i_ref[...] = cpow(float(S5_CHUNK), ar, ai, dt)

    ar2, ai2, dt2 = ar_rows_ref[...], ai_rows_ref[...], jnp.exp(ldt_rows_ref[...])
    fr2, fi2 = zoh(ar2, ai2, dt2)
    b1, b2 = b1_ref[...], b2_ref[...]
    bb, bbsw = fr2 * b1 + fi2 * b2, fr2 * b2 - fi2 * b1
    lwr, lwi = cpow(s, ar2, ai2, dt2)
    w = (lwr * bb + lwi * bbsw).astype(BF16)
    lag = _dot(w, cm_ref[...].astype(BF16))
    kshape = (S5_ROW, S5_ROW)
    gsh = S5_GROUP.bit_length() - 1
    same_k = (jnp.right_shift(lax.broadcasted_iota(jnp.int32, kshape, 0), gsh)
              == jnp.right_shift(lax.broadcasted_iota(jnp.int32, kshape, 1), gsh))
    k_ref[...] = jnp.where(same_k, lag, 0.0).astype(BF16)


def s5_tiles(a_re, a_im, log_dt, b_re, b_im, c_re, c_im):
    gh, n, pch = S5_GH, S5_STATE, S5_GROUP
    halves = lambda v: v.reshape((2, gh) + v.shape[1:])
    lane = lambda v: halves(v).reshape(2, 1, S5_HALF)
    rows = lambda v: jnp.tile(jnp.repeat(halves(v), pch, axis=1), (1, 1, 2))
    ldt = jnp.broadcast_to(log_dt[:, None], (S5_GROUPS, n))
    bt = lambda b: halves(jnp.swapaxes(b, 1, 2)).reshape(2, S5_ROW, n)
    btr, bti = bt(b_re), bt(b_im)
    b1 = jnp.concatenate([btr, bti], axis=-1)
    b2 = jnp.concatenate([-bti, btr], axis=-1)
    wide = lambda v: jnp.tile(v, (1, 1, gh))
    cq = lambda c: jnp.tile(jnp.transpose(halves(c), (0, 2, 1, 3)).reshape(2, pch, S5_HALF), (1, gh, 1))
    cmt = lambda c: jnp.transpose(halves(c), (0, 3, 1, 2)).reshape(2, n, S5_ROW)
    cm = jnp.concatenate([cmt(c_re), -cmt(c_im)], axis=1)
    half = lambda *shape: pl.BlockSpec((None,) + shape, lambda h, s: (h,) + (0,) * len(shape))
    pt, qt, kt, a16r, a16i = pl.pallas_call(
        _s5_tiles_kernel,
        grid=(2, S5_CHUNK),
        in_specs=[half(1, S5_HALF)] * 3 + [half(S5_ROW, 2 * n)] * 5 + [half(S5_ROW, S5_HALF)] * 4
                 + [half(2 * n, S5_ROW)],
        out_specs=[pl.BlockSpec((None, S5_ROW, 2 * S5_HALF), lambda h, s: (h, s, 0)),
                   pl.BlockSpec((None, None, S5_ROW, 2 * S5_HALF), lambda h, s: (h, s, 0, 0)),
                   pl.BlockSpec((None, S5_ROW, S5_ROW), lambda h, s: (h, S5_CHUNK - 1 - s, 0)),
                   half(1, S5_HALF), half(1, S5_HALF)],
        out_shape=[jax.ShapeDtypeStruct((2, S5_CHUNK * S5_ROW, 2 * S5_HALF), BF16),
                   jax.ShapeDtypeStruct((2, S5_CHUNK, S5_ROW, 2 * S5_HALF), BF16),
                   jax.ShapeDtypeStruct((2, S5_CHUNK * S5_ROW, S5_ROW), BF16),
                   jax.ShapeDtypeStruct((2, 1, S5_HALF), F32), jax.ShapeDtypeStruct((2, 1, S5_HALF), F32)],
        compiler_params=_cparams(("arbitrary", "arbitrary")),
        name="s5_tiles",
    )(lane(a_re), lane(a_im), lane(ldt), rows(a_re), rows(a_im), rows(ldt), b1, b2,
      wide(btr), wide(bti), cq(c_re), cq(c_im), cm)
    nstate = S5_GROUPS * S5_STATE
    return kt, pt, qt, a16r.reshape(1, nstate), a16i.reshape(1, nstate)


def _s5_state_kernel(us_ref, p_ref, sre_ref, sim_ref):
    steps = jnp.concatenate([us_ref[s] for s in range(S5_CHUNK)], axis=1)
    part = _dot(steps, p_ref[...])
    for c in range(S5_HALF // LANES):
        sre_ref[c] = part[:, c * LANES:(c + 1) * LANES]
        sim_ref[c] = part[:, S5_HALF + c * LANES:S5_HALF + (c + 1) * LANES]


def _s5_carry_kernel(sre_ref, sim_ref, h0re_ref, h0im_ref, ar_ref, ai_ref,
                     hre_ref, him_ref, lre_ref, lim_ref, *, bn, nj):
    ar, ai = ar_ref[...], ai_ref[...]

    def body(j, carry):
        hr, hi = carry
        rows = pl.ds(j, bn, stride=nj)
        hre_ref[rows, :] = hr
        him_ref[rows, :] = hi
        return ar * hr - ai * hi + sre_ref[rows, :], ar * hi + ai * hr + sim_ref[rows, :]

    hr, hi = lax.fori_loop(0, nj, body, (h0re_ref[...], h0im_ref[...]))
    lre_ref[...] = hr
    lim_ref[...] = hi


def _s5_out_kernel(us_ref, k_ref, hre_ref, him_ref, q_ref, y_ref):
    blocks = range(S5_HALF // LANES)
    hin = jnp.concatenate([hre_ref[c] for c in blocks] + [him_ref[c] for c in blocks], axis=1).astype(BF16)
    for t in range(S5_CHUNK):
        steps = jnp.concatenate([us_ref[s] for s in range(t + 1)], axis=1)
        lags = k_ref[(S5_CHUNK - 1 - t) * S5_ROW:, :]
        y_ref[t] = _dot(steps, lags) + _dot_nt(hin, q_ref[t])


def s5_mix(us, h0re, h0im, tiles, bn, nj):
    kt, pt, qq, ar, ai = tiles
    r = bn * nj
    nstate = S5_GROUPS * S5_STATE
    us4 = us.reshape(S5_CHUNK, 2, r, S5_ROW)
    half_once = lambda *shape: pl.BlockSpec((None,) + shape, lambda h, i: (h,) + (0,) * len(shape),
                                            pipeline_mode=pl.Buffered(1))
    rows = min(r, 512)
    lanes = S5_CARRY_LANES
    sre, sim = pl.pallas_call(
        _s5_state_kernel,
        grid=(2, r // rows),
        in_specs=[pl.BlockSpec((S5_CHUNK, None, rows, S5_ROW), lambda h, i: (0, h, i, 0)),
                  half_once(S5_CHUNK * S5_ROW, 2 * S5_HALF)],
        out_specs=[pl.BlockSpec((S5_HALF // lanes, rows, lanes), lambda h, i: (h, i, 0))] * 2,
        out_shape=[jax.ShapeDtypeStruct((nstate // lanes, r, lanes), F32)] * 2,
        compiler_params=_cparams(("arbitrary", "arbitrary")),
        name="s5_state",
    )(us4, pt)
    col = lambda n: pl.BlockSpec((n, lanes), lambda i: (0, i))
    blk = pl.BlockSpec((None, r, lanes), lambda i: (i, 0, 0))
    hre, him, lre, lim = pl.pallas_call(
        functools.partial(_s5_carry_kernel, bn=bn, nj=nj),
        grid=(nstate // lanes,),
        in_specs=[blk, blk, col(bn), col(bn), col(1), col(1)],
        out_specs=[blk, blk, col(bn), col(bn)],
        out_shape=[jax.ShapeDtypeStruct((nstate // lanes, r, lanes), F32)] * 2
                  + [jax.ShapeDtypeStruct((bn, nstate), F32)] * 2,
        compiler_params=_cparams(("parallel",)),
        name="s5_carry",
    )(sre, sim, h0re, h0im, ar, ai)
    rows = min(r, 256)
    slabs = pl.BlockSpec((S5_CHUNK, None, rows, S5_ROW), lambda h, i: (0, h, i, 0))
    y = pl.pallas_call(
        _s5_out_kernel,
        grid=(2, r // rows),
        in_specs=[slabs, half_once(S5_CHUNK * S5_ROW, S5_ROW),
                  pl.BlockSpec((S5_HALF // lanes, rows, lanes), lambda h, i: (h, i, 0)),
                  pl.BlockSpec((S5_HALF // lanes, rows, lanes), lambda h, i: (h, i, 0)),
                  half_once(S5_CHUNK, S5_ROW, 2 * S5_HALF)],
        out_specs=slabs,
        out_shape=jax.ShapeDtypeStruct((S5_CHUNK, 2, r, S5_ROW), F32),
        compiler_params=_cparams(("arbitrary", "arbitrary")),
        name="s5_out",
    )(us4, kt, hre, him, qq)
    return y.reshape(S5_SLABS, r, S5_ROW), lre, lim


SB_TK = 256
LOG2E = math.log2(math.e)
SB_SKIP_LOG2 = 152.0


def _sb_chains(tasks, ones, keys_on_lanes=False):
    zs = [_dot(q, k) if keys_on_lanes else _dot_nt(q, k) for q, k, _, _, _ in tasks]
    tails = [jnp.log2(1.0 + jnp.exp2(-jnp.abs(z))) for z in zs]
    sps = [jnp.maximum(z, 0.0) + t for z, t in zip(zs, tails)]
    sps = [sp if task[3] is None else jnp.where(task[3], sp, 0.0) for sp, task in zip(sps, tasks)]
    mass = [jnp.sum(sp, axis=-1, keepdims=True) for sp in sps]
    his = [sp.astype(BF16) for sp in sps]
    los = [(sp - hi.astype(F32)).astype(BF16) for sp, hi in zip(sps, his)]
    sums = [_dot(jnp.concatenate([hi, lo], axis=0), ones) for hi, lo in zip(his, los)]
    outs = []
    for i, (_, _, v, mask, carry) in enumerate(tasks):
        rows = zs[i].shape[0]
        suffix = sums[i][:rows] + sums[i][rows:]
        logw = jnp.minimum(zs[i], 0.0) - tails[i] - suffix
        if carry is not None:
            logw = logw - (mass[carry] if isinstance(carry, int) else carry)
        w = jnp.exp2(logw)
        if mask is not None:
            w = jnp.where(mask, w, 0.0)
        wb = w.astype(BF16)
        outs.append(_dot_nt(wb, v) if keys_on_lanes else _dot(wb, v))
    return outs, mass


def _sb_block(q2, k, v, ones, carry, mask, keys_on_lanes=False):
    half = q2.shape[0] // 2
    halves = (slice(0, half), slice(half, 2 * half))
    outs, mass = _sb_chains([(q2[h], k, v, None if mask is None else mask[h],
                              None if carry is None else carry[h]) for h in halves], ones, keys_on_lanes)
    return jnp.concatenate(outs, axis=0), jnp.concatenate(mass, axis=0)


def _sb_walk(q2, load_kv, ones, blocks, acc, carry, keys_on_lanes=False):
    def cond(s):
        return jnp.logical_and(s[0] < blocks, s[1] < SB_SKIP_LOG2)

    def body(s):
        i, _, acc, carry = s
        rows = pl.ds(pl.multiple_of((blocks - 1 - i) * SB_TK, SB_TK), SB_TK)
        k, v = load_kv(rows)
        d, m = _sb_block(q2, k, v, ones, carry, None, keys_on_lanes)
        carry = carry + m
        return i + 1, jnp.min(carry), acc + d, carry

    return lax.while_loop(cond, body, (jnp.int32(0), jnp.min(carry), acc, carry))[2]


def _sb_prompt_kernel(q_ref, kd_ref, vd_ref, kp_ref, vp_ref, ones_ref, o_ref):
    i = pl.program_id(2)
    t = SB_TK
    lanes = 2 * SB_HEAD_DIM
    q = q_ref[...]
    lane_lo = lax.broadcasted_iota(jnp.int32, (2 * t, lanes), 1) < SB_HEAD_DIM
    zero = jnp.zeros_like(q)
    heads = (jnp.where(lane_lo, q, zero), jnp.where(lane_lo, zero, q))
    row = lax.broadcasted_iota(jnp.int32, (t, t), 0)
    col = lax.broadcasted_iota(jnp.int32, (t, t), 1)
    earlier = col < row
    ones = ones_ref[...]
    kd = [kd_ref[s * t:(s + 1) * t, :] for s in range(2)]
    vd = [vd_ref[s * t:(s + 1) * t, :] for s in range(2)]
    qs = [[h[s * t:(s + 1) * t, :] for h in heads] for s in range(2)]

    def front(with_past):
        tasks = [(qs[s][h], kd[s], vd[s], earlier, None) for s in range(2) for h in range(2)]
        tasks += [(qs[1][h], kd[0], vd[0], None, 2 + h) for h in range(2)]
        if with_past:
            rows = pl.ds(pl.multiple_of((2 * i - 1) * t, t), t)
            kp, vp = kp_ref[rows, :], vp_ref[rows, :]
            tasks += [(qs[0][h], kp, vp, None, h) for h in range(2)]
        outs, mass = _sb_chains(tasks, ones)
        stack = lambda a, b: jnp.concatenate([a, b], axis=0)
        acc1, m1 = stack(outs[2] + outs[4], outs[3] + outs[5]), stack(mass[2] + mass[4], mass[3] + mass[5])
        if with_past:
            acc0, m0 = stack(outs[0] + outs[6], outs[1] + outs[7]), stack(mass[0] + mass[6], mass[1] + mass[7])
        else:
            acc0, m0 = stack(outs[0], outs[1]), stack(mass[0], mass[1])
        return acc0, m0, acc1, m1

    acc0, m0, acc1, m1 = lax.cond(i > 0, lambda: front(True), lambda: front(False))
    q2 = [jnp.concatenate(qs[s], axis=0) for s in range(2)]
    load_kv = lambda rows: (kp_ref[rows, :], vp_ref[rows, :])
    acc0 = _sb_walk(q2[0], load_kv, ones, jnp.maximum(2 * i - 1, 0), acc0, m0)
    acc1 = _sb_walk(q2[1], load_kv, ones, 2 * i, acc1, m1)
    first = lax.broadcasted_iota(jnp.int32, (t, lanes), 1) < SB_HEAD_DIM
    o_ref[0:t, :] = jnp.where(first, acc0[:t], acc0[t:])
    o_ref[t:2 * t, :] = jnp.where(first, acc1[:t], acc1[t:])


def _sb_cached_kernel(q_ref, kd_ref, vd_ref, kp_ref, vp_ref, ud_ref, up_ref, o_ref, *, tq, blocks):
    lanes = 2 * SB_HEAD_DIM
    lane_lo = lax.broadcasted_iota(jnp.int32, (tq, lanes), 1) < SB_HEAD_DIM
    row = lax.broadcasted_iota(jnp.int32, (2 * tq, tq), 0)
    col = lax.broadcasted_iota(jnp.int32, (2 * tq, tq), 1)
    earlier = col < jnp.where(row >= tq, row - tq, row)
    for pair in range(SB_HEADS // 2):
        cols = slice(pair * lanes, (pair + 1) * lanes)
        q = q_ref[:, cols]
        zero = jnp.zeros_like(q)
        q2 = jnp.concatenate([jnp.where(lane_lo, q, zero), jnp.where(lane_lo, zero, q)], axis=0)
        acc, mass = _sb_block(q2, kd_ref[:, cols], vd_ref[:, cols], ud_ref[...], None, earlier)

        def load_kv(keys, pair=pair):
            flat = lambda ref: ref[2 * pair:2 * pair + 2, :, keys].reshape(lanes, SB_TK).astype(BF16)
            return flat(kp_ref), flat(vp_ref)

        acc = _sb_walk(q2, load_kv, up_ref[...], blocks, acc, mass, keys_on_lanes=True)
        o_ref[:, cols] = jnp.where(lane_lo, acc[:tq], acc[tq:])


def _strict_lower_ones(n):
    r = lax.broadcasted_iota(jnp.int32, (n, n), 0)
    c = lax.broadcasted_iota(jnp.int32, (n, n), 1)
    return (r > c).astype(BF16)


def stick_breaking_fresh(q, k, v, bn, length):
    lanes = 2 * SB_HEAD_DIM
    tq = 2 * SB_TK
    assert length % tq == 0
    blk = pl.BlockSpec((None, tq, lanes), lambda b, h, i: (b, i, h))
    seq = pl.BlockSpec((None, length, lanes), lambda b, h, i: (b, 0, h))
    return pl.pallas_call(
        _sb_prompt_kernel,
        grid=(bn, SB_HEADS // 2, length // tq),
        in_specs=[blk, blk, blk, seq, seq, _full((SB_TK, SB_TK))],
        out_specs=blk,
        out_shape=jax.ShapeDtypeStruct((bn, length, SB_WIDTH), F32),
        compiler_params=_cparams(("parallel", "parallel", "arbitrary")),
        name="stick_breaking",
    )(q, k, v, k, v, _strict_lower_ones(SB_TK))


def stick_breaking_cached(q, k_new, v_new, k_past, v_past, bn, lq):
    lp = k_past.shape[3]
    assert lp % SB_TK == 0
    blk = pl.BlockSpec((None, lq, SB_WIDTH), lambda b: (b, 0, 0))
    past = pl.BlockSpec((None, SB_HEADS, SB_HEAD_DIM, lp), lambda b: (b, 0, 0, 0))
    return pl.pallas_call(
        functools.partial(_sb_cached_kernel, tq=lq, blocks=lp // SB_TK),
        grid=(bn,),
        in_specs=[blk, blk, blk, past, past, _full((lq, lq)), _full((SB_TK, SB_TK))],
        out_specs=blk,
        out_shape=jax.ShapeDtypeStruct((bn, lq, SB_WIDTH), F32),
        compiler_params=_cparams(("parallel",)),
        name="stick_breaking_cached",
    )(q, k_new, v_new, k_past, v_past, _strict_lower_ones(lq), _strict_lower_ones(SB_TK))


def _gelu(x):
    return 0.5 * x * (1.0 + jnp.tanh(math.sqrt(2.0 / math.pi) * (x + 0.044715 * (x * x * x))))


def _mix0_out_kernel(x_ref, ys_ref, u_ref, sb_ref, d_ref, wglu_ref, bglu_ref, wo1_ref, wo2_ref, g_ref, b_ref,
                     wg_ref, wu_ref, wd_ref, g2_ref, b2_ref, o_ref, y_scr):
    x = x_ref[...]
    chunks = y_scr.shape[1] // S5_CHUNK
    per_slab = S5_ROW // LANES
    for s in range(S5_CHUNK):
        for half in range(2):
            for c in range(per_slab):
                y_scr[half * per_slab + c, pl.ds(s, chunks, stride=S5_CHUNK), :] = (
                    ys_ref[2 * s + half, :, c * LANES:(c + 1) * LANES])
    ys = jnp.concatenate([y_scr[c] for c in range(S5_WIDTH // LANES)], axis=1)
    gl = _gelu(ys + d_ref[...] * u_ref[...])
    gate = _sigmoid(_dot(gl.astype(BF16), wglu_ref[...]) + bglu_ref[...])
    m = _dot((gl * gate).astype(BF16), wo1_ref[...]) + _dot(sb_ref[...].astype(BF16), wo2_ref[...])
    x = _layer_norm(ALPHA * x + m, g_ref[...], b_ref[...])
    o_ref[...] = _ffn_sublayer(x, wg_ref, wu_ref, wd_ref, g2_ref, b2_ref)


def mix0_out_ffn(x, ys, u, sb, d, wglu, bglu, wo1, wo2, g, b, ffn):
    t = x.shape[0]
    tm = _row_tile(t)
    rowblk = lambda n: pl.BlockSpec((tm, n), lambda i: (i, 0))
    ffn_operands, ffn_specs = _ffn_operands(ffn)
    return pl.pallas_call(
        _mix0_out_kernel,
        grid=(t // tm,),
        in_specs=[rowblk(D_MODEL),
                  pl.BlockSpec((S5_SLABS, tm // S5_CHUNK, S5_ROW), lambda i: (0, i, 0)),
                  rowblk(S5_WIDTH), rowblk(SB_WIDTH),
                  _full((1, S5_WIDTH)), _full((S5_WIDTH, S5_WIDTH)), _full((1, S5_WIDTH)),
                  _full((S5_WIDTH, D_MODEL)), _full((SB_WIDTH, D_MODEL)),
                  _full((1, D_MODEL)), _full((1, D_MODEL))] + ffn_specs,
        out_specs=rowblk(D_MODEL),
        out_shape=jax.ShapeDtypeStruct((t, D_MODEL), F32),
        scratch_shapes=[pltpu.VMEM((S5_WIDTH // LANES, tm, LANES), F32)],
        compiler_params=_cparams(("parallel",)),
        name="mix0_out_ffn",
    )(x, ys, u, sb, d.reshape(1, -1), wglu, bglu.reshape(1, -1), wo1, wo2, g.reshape(1, -1), b.reshape(1, -1),
      *ffn_operands)


def _ssd_kernel(xbc_ref, z_ref, dt_ref, dtt_ref, conv0_ref, h0_ref, cw_ref, cb_ref, dtb_ref, dtbt_ref,
                alog_ref, alogt_ref, dsk_ref, ng_ref, tril_ref, triu_ref,
                y_ref, hl_ref, ext_ref, h_ref, *, lc, nc):
    c = pl.program_id(1)

    @pl.when(c == 0)
    def _():
        ext_ref[0:CONV_PAD, :] = conv0_ref[...]
        h_ref[...] = h0_ref[...]

    ext_ref[CONV_PAD:CONV_PAD + lc, :] = xbc_ref[...]
    ext = ext_ref[...]
    conv = cb_ref[...] + ext[CONV_PAD:, :] * cw_ref[SSD_CONV - 1:SSD_CONV, :]
    for back in range(1, SSD_CONV):
        tap = SSD_CONV - 1 - back
        conv = conv + pltpu.roll(ext, back, 0)[CONV_PAD:, :] * cw_ref[tap:tap + 1, :]
    ext_ref[0:CONV_PAD, :] = ext[lc:lc + CONV_PAD, :]
    act = conv * _sigmoid(conv)

    dt = _softplus(dt_ref[...] + dtb_ref[...])
    dtt = _softplus(dtt_ref[...] + dtbt_ref[...])
    cs = _sum_left(tril_ref[...], dt * (-LOG2E * jnp.exp(alog_ref[...])))
    cst = _sum_right(dtt * (-LOG2E * jnp.exp(alogt_ref[...])), triu_ref[...])
    cs_last = cs[lc - 1:lc, :]

    row = lax.broadcasted_iota(jnp.int32, (lc, lc), 0)
    col = lax.broadcasted_iota(jnp.int32, (lc, lc), 1)
    causal = col <= row
    lanes = 2 * SSD_HEAD_DIM
    lane_lo = lax.broadcasted_iota(jnp.int32, (lc, lanes), 1) < SSD_HEAD_DIM
    pairs_per_group = SSD_PAIRS // SSD_GROUPS

    for g in range(SSD_GROUPS):
        bm = act[:, SSD_INNER + g * SSD_STATE:SSD_INNER + (g + 1) * SSD_STATE].astype(BF16)
        cm = act[:, SSD_INNER + SSD_GN + g * SSD_STATE:SSD_INNER + SSD_GN + (g + 1) * SSD_STATE].astype(BF16)
        cb = _dot_nt(cm, bm)
        gated, sq = [], jnp.zeros((lc, lanes), F32)
        for k in range(g * pairs_per_group, (g + 1) * pairs_per_group):
            sl = slice(k * lanes, (k + 1) * lanes)
            xp = act[:, sl]
            heads = (2 * k, 2 * k + 1)
            pair = lambda v: jnp.where(lane_lo[:v[0].shape[0]], v[0], v[1])
            xdt = xp * pair([dt[:, h:h + 1] for h in heads])
            xdtb = xdt.astype(BF16)
            csb = [jnp.broadcast_to(cs[:, h:h + 1], (lc, lanes)) for h in heads]
            ys = []
            for h, csh in zip(heads, csb):
                seg = csh[:, :lc] - cst[h:h + 1, :]
                w = cb * jnp.exp2(jnp.where(causal, seg, -jnp.inf))
                ys.append(_dot(w.astype(BF16), xdtb))
            y = pair(ys)
            cs_pair = pair(csb)
            last_pair = pair([cs_last[:, h:h + 1] for h in heads])
            ht = h_ref[k]
            y = y + _dot(cm, ht.astype(BF16)) * jnp.exp2(cs_pair)
            xw = (xdt * jnp.exp2(last_pair - cs_pair)).astype(BF16)
            h_ref[k] = ht * jnp.exp2(last_pair) + _dot_tn(bm, xw)
            y = y + dsk_ref[:, sl] * xp
            zz = z_ref[:, sl]
            yg = y * (zz * _sigmoid(zz))
            sq = sq + yg * yg
            gated.append(yg)
        mean_sq = jnp.sum(sq, axis=-1, keepdims=True) * (1.0 / (pairs_per_group * lanes))
        scale = lax.rsqrt(mean_sq + RMS_EPS)
        for k, yg in zip(range(g * pairs_per_group, (g + 1) * pairs_per_group), gated):
            sl = slice(k * lanes, (k + 1) * lanes)
            y_ref[:, sl] = yg * scale * ng_ref[:, sl]

    @pl.when(c == nc - 1)
    def _():
        hl_ref[...] = h_ref[...]


def ssd_mix(xbc, z, dt_raw, conv0, h0, conv_w, conv_b, dt_bias, a_log, d_skip, norm_g, bn, length, lc):
    nc = length // lc
    t = bn * length
    lanes = 2 * SSD_HEAD_DIM
    dtt = jnp.swapaxes(dt_raw.reshape(bn * nc, lc, SSD_HEADS), 1, 2)
    conv0p = jnp.pad(conv0, ((0, 0), (CONV_PAD - (SSD_CONV - 1), 0), (0, 0)))
    a_log = a_log.astype(F32)
    r = lax.broadcasted_iota(jnp.int32, (lc, lc), 0)
    cc = lax.broadcasted_iota(jnp.int32, (lc, lc), 1)
    tril = (cc <= r).astype(BF16)
    triu = (r <= cc).astype(BF16)
    tok = lambda n: pl.BlockSpec((lc, n), lambda b, c: (b * nc + c, 0))
    y, hl = pl.pallas_call(
        functools.partial(_ssd_kernel, lc=lc, nc=nc),
        grid=(bn, nc),
        in_specs=[tok(SSD_CONV_DIM), tok(SSD_INNER), tok(SSD_HEADS),
                  pl.BlockSpec((None, SSD_HEADS, lc), lambda b, c: (b * nc + c, 0, 0)),
                  pl.BlockSpec((None, CONV_PAD, SSD_CONV_DIM), lambda b, c: (b, 0, 0)),
                  pl.BlockSpec((None, SSD_PAIRS, lanes, SSD_STATE), lambda b, c: (b, 0, 0, 0)),
                  _full((SSD_CONV, SSD_CONV_DIM)), _full((1, SSD_CONV_DIM)),
                  _full((1, SSD_HEADS)), _full((SSD_HEADS, 1)), _full((1, SSD_HEADS)), _full((SSD_HEADS, 1)),
                  _full((1, SSD_INNER)), _full((1, SSD_INNER)), _full((lc, lc)), _full((lc, lc))],
        out_specs=[tok(SSD_INNER),
                   pl.BlockSpec((None, SSD_PAIRS, lanes, SSD_STATE), lambda b, c: (b, 0, 0, 0))],
        out_shape=[jax.ShapeDtypeStruct((t, SSD_INNER), F32),
                   jax.ShapeDtypeStruct((bn, SSD_PAIRS, lanes, SSD_STATE), F32)],
        scratch_shapes=[pltpu.VMEM((CONV_PAD + lc, SSD_CONV_DIM), F32),
                        pltpu.VMEM((SSD_PAIRS, lanes, SSD_STATE), F32)],
        compiler_params=_cparams(("parallel", "arbitrary")),
        name="ssd_mix",
    )(xbc, z, dt_raw, dtt, conv0p, jnp.swapaxes(h0.reshape(bn, SSD_PAIRS, lanes, SSD_STATE), 2, 3),
      conv_w, conv_b.reshape(1, -1), dt_bias.reshape(1, -1), dt_bias.reshape(-1, 1),
      a_log.reshape(1, -1), a_log.reshape(-1, 1), jnp.repeat(d_skip, SSD_HEAD_DIM).reshape(1, -1),
      norm_g.reshape(1, -1), tril, triu)
    return y, jnp.swapaxes(hl, 2, 3).reshape(bn, SSD_HEADS, SSD_HEAD_DIM, SSD_STATE)


def _out_ffn_kernel(x_ref, a_ref, w_ref, g_ref, b_ref, wg_ref, wu_ref, wd_ref, g2_ref, b2_ref, o_ref):
    m = _dot(a_ref[...].astype(BF16), w_ref[...])
    x = _layer_norm(ALPHA * x_ref[...] + m, g_ref[...], b_ref[...])
    o_ref[...] = _ffn_sublayer(x, wg_ref, wu_ref, wd_ref, g2_ref, b2_ref)


def out_ffn(x, a, w, g, b, ffn):
    t = x.shape[0]
    tm = _row_tile(t)
    k = a.shape[1]
    ffn_operands, ffn_specs = _ffn_operands(ffn)
    return pl.pallas_call(
        _out_ffn_kernel,
        grid=(t // tm,),
        in_specs=[pl.BlockSpec((tm, D_MODEL), lambda i: (i, 0)), pl.BlockSpec((tm, k), lambda i: (i, 0)),
                  _full((k, D_MODEL)), _full((1, D_MODEL)), _full((1, D_MODEL))] + ffn_specs,
        out_specs=pl.BlockSpec((tm, D_MODEL), lambda i: (i, 0)),
        out_shape=jax.ShapeDtypeStruct((t, D_MODEL), F32),
        compiler_params=_cparams(("parallel",)),
        name="out_ffn",
    )(x, a, w, g.reshape(1, -1), b.reshape(1, -1), *ffn_operands)


def _trunk(x, bn, length, states, w):
    t = bn * length
    x = x.reshape(t, D_MODEL)
    ffn = lambda l, s: (w["wg"][l][s], w["wu"][l][s], w["wd"][l][s], w["ln_g"][l, 2 * s], w["ln_b"][l, 2 * s])
    x = ffn_ln(x, ffn(0, 0))

    u, u_slabs, q_bf, k, k_bf, v, v_bf = mix0_in(x, w["mix0_in"], SB_HEAD_DIM ** -0.5 * LOG2E)
    nj = length // S5_CHUNK
    nstate = S5_GROUPS * S5_STATE
    if states is None:
        h0re = h0im = jnp.zeros((bn, nstate), F32)
    else:
        h0re = states[0][0].astype(F32).reshape(bn, nstate)
        h0im = states[1][0].astype(F32).reshape(bn, nstate)
    y_s5, h_re, h_im = s5_mix(u_slabs, h0re, h0im, w["s5_tiles"], bn, nj)
    s5_re = h_re.reshape(1, bn, S5_GROUPS, S5_STATE)
    s5_im = h_im.reshape(1, bn, S5_GROUPS, S5_STATE)

    seq = lambda a: a.reshape(bn, length, SB_WIDTH)
    if states is None:
        sb = stick_breaking_fresh(seq(q_bf), seq(k_bf), seq(v_bf), bn, length)
    else:
        cache = lambda a: jnp.transpose(a[0], (0, 2, 3, 1))
        sb = stick_breaking_cached(seq(q_bf), seq(k_bf), seq(v_bf), cache(states[2]), cache(states[3]), bn, length)
    x = mix0_out_ffn(x, y_s5, u, sb.reshape(t, SB_WIDTH), w["s5_d"], w["s5_wglu"], w["s5_bglu"],
                     w["mix0_out"][:S5_WIDTH], w["mix0_out"][S5_WIDTH:], w["ln_g"][0, 1], w["ln_b"][0, 1],
                     ffn(0, 1))

    x = ffn_ln(x, ffn(1, 0))
    one = ((F32, 1.0),)
    z, xbc, dt_raw = proj(x, w["ssd_in"], (
        (0, SSD_INNER, one), (SSD_INNER, SSD_INNER + SSD_CONV_DIM, one),
        (SSD_INNER + SSD_CONV_DIM, SSD_INNER + SSD_CONV_DIM + SSD_HEADS, one)))
    if states is None:
        conv0 = jnp.zeros((bn, SSD_CONV - 1, SSD_CONV_DIM), F32)
        hs0 = jnp.zeros((bn, SSD_HEADS, SSD_HEAD_DIM, SSD_STATE), F32)
    else:
        conv0 = states[5][0].astype(F32)
        hs0 = states[4][0].astype(F32)
    lc = min(128, length)
    yg, h_ssd = ssd_mix(xbc, z, dt_raw, conv0, hs0, w["ssd_conv_w"], w["ssd_conv_b"], w["ssd_dt_bias"],
                        w["ssd_a_log"], w["ssd_d"], w["ssd_norm_g"], bn, length, lc)
    ext = jnp.concatenate([conv0, xbc.reshape(bn, length, SSD_CONV_DIM)[:, length - (SSD_CONV - 1):]], axis=1)
    new_conv = ext[:, ext.shape[1] - (SSD_CONV - 1):]
    x = out_ffn(x, yg, w["ssd_out"], w["ln_g"][1, 1], w["ln_b"][1, 1], ffn(1, 1))

    return (x.reshape(bn, length, D_MODEL), s5_re, s5_im,
            k.reshape(1, bn, length, SB_HEADS, SB_HEAD_DIM), v.reshape(1, bn, length, SB_HEADS, SB_HEAD_DIM),
            h_ssd[None], new_conv[None])


def kernel(x_prompt, x_sample, state_s5_re, state_s5_im, cache_sb_k, cache_sb_v, state_ssd, state_conv, ln_g, ln_b, ffn_w_gate, ffn_w_up, ffn_w_down, mix0_w_in, s5_a_re, s5_a_im, s5_log_dt, s5_b_re, s5_b_im, s5_c_re, s5_c_im, s5_d, s5_w_glu, s5_b_glu, mix0_w_out, ssd_w_in, ssd_conv_w, ssd_conv_b, ssd_dt_bias, ssd_a_log, ssd_d, ssd_norm_g, ssd_w_out):
    bf = lambda a: a.astype(BF16)
    w = {
        "wg": [[bf(ffn_w_gate[l, s]) for s in range(2)] for l in range(DEPTH)],
        "wu": [[bf(ffn_w_up[l, s]) for s in range(2)] for l in range(DEPTH)],
        "wd": [[bf(ffn_w_down[l, s]) for s in range(2)] for l in range(DEPTH)],
        "ln_g": ln_g, "ln_b": ln_b,
        "mix0_in": bf(mix0_w_in[0]),
        "s5_tiles": s5_tiles(s5_a_re[0], s5_a_im[0], s5_log_dt[0], s5_b_re[0], s5_b_im[0], s5_c_re[0], s5_c_im[0]),
        "s5_d": s5_d[0], "s5_wglu": bf(s5_w_glu[0]), "s5_bglu": s5_b_glu[0], "mix0_out": bf(mix0_w_out[0]),
        "ssd_in": bf(ssd_w_in[0]), "ssd_conv_w": ssd_conv_w[0], "ssd_conv_b": ssd_conv_b[0],
        "ssd_dt_bias": ssd_dt_bias[0], "ssd_a_log": ssd_a_log[0], "ssd_d": ssd_d[0],
        "ssd_norm_g": ssd_norm_g[0], "ssd_out": bf(ssd_w_out[0]),
    }
    bp, lp = x_prompt.shape[0], x_prompt.shape[1]
    bs, ls = x_sample.shape[0], x_sample.shape[1]
    yp, s5rp, s5ip, kp, vp, ssdp, convp = _trunk(x_prompt, bp, lp, None, w)
    ys, s5rs, s5is, ks, vs, ssds, convs = _trunk(
        x_sample, bs, ls, (state_s5_re, state_s5_im, cache_sb_k, cache_sb_v, state_ssd, state_conv), w)
    return (yp, ys, s5rp, s5ip, kp, vp, ssdp, convp, s5rs, s5is, ks, vs, ssds, convs)
```

```python
import functools
import math

import jax
import jax.numpy as jnp
from jax import lax
from jax.experimental import pallas as pl
from jax.experimental.pallas import tpu as pltpu

F32 = jnp.float32
BF16 = jnp.bfloat16

D_MODEL = 1024
DEPTH = 2
ALPHA = (2.0 * DEPTH) ** 0.25
LN_EPS = 1e-5
RMS_EPS = 1e-5
D_FF = 2816

S5_WIDTH = 512
S5_GROUP = 16
S5_GROUPS = 32
S5_STATE = 64
S5_CHUNK = 16
S5_ROW = S5_CHUNK * S5_GROUP
SB_HEADS = 8
SB_HEAD_DIM = 64
SB_WIDTH = 512
MIX0_IN = 2048

SSD_INNER = 2048
SSD_HEAD_DIM = 64
SSD_HEADS = 32
SSD_GROUPS = 4
SSD_STATE = 128
SSD_CONV = 4
SSD_GN = 512
SSD_CONV_DIM = 3072
SSD_PAIRS = SSD_HEADS // 2
CONV_PAD = 8

LANES = 128
VMEM_LIMIT = 56 * 1024 * 1024


def _cparams(sem):
    return pltpu.CompilerParams(dimension_semantics=sem, vmem_limit_bytes=VMEM_LIMIT)


def _sigmoid(x):
    return 1.0 / (1.0 + jnp.exp2(x * -math.log2(math.e)))


def _softplus(x):
    return jnp.maximum(x, 0.0) + jnp.log1p(jnp.exp(-jnp.abs(x)))


def _layer_norm(y, g, b):
    mu = jnp.mean(y, axis=-1, keepdims=True)
    d = y - mu
    var = jnp.mean(d * d, axis=-1, keepdims=True)
    return d * lax.rsqrt(var + LN_EPS) * g + b


def _dot(a, b):
    return jnp.dot(a, b, preferred_element_type=F32)


def _dot_nt(a, b):
    return lax.dot_general(a, b, (((1,), (1,)), ((), ())), preferred_element_type=F32)


def _dot_tn(a, b):
    return lax.dot_general(a, b, (((0,), (0,)), ((), ())), preferred_element_type=F32)


def _split3(x):
    hi = x.astype(BF16)
    r = x - hi.astype(F32)
    mid = r.astype(BF16)
    lo = (r - mid.astype(F32)).astype(BF16)
    return hi, mid, lo


def _sum_right(x, ones_mat):
    hi, mid, lo = _split3(x)
    return _dot(hi, ones_mat) + _dot(mid, ones_mat) + _dot(lo, ones_mat)


def _sum_left(ones_mat, x):
    hi, mid, lo = _split3(x)
    return _dot(ones_mat, hi) + _dot(ones_mat, mid) + _dot(ones_mat, lo)


def _row_tile(t):
    for tm in (512, 256, 128, 64, 32, 16, 8):
        if t % tm == 0:
            return tm
    raise ValueError(f"token count {t} is not a multiple of 8")


def _full(shape):
    return pl.BlockSpec(shape, lambda *_: (0,) * len(shape), pipeline_mode=pl.Buffered(1))


MXU_TILE = 256
FFN_SPLIT = (D_FF // MXU_TILE // 2) * MXU_TILE


def _ffn_sublayer(x, wg_ref, wu_ref, wd_ref, g_ref, b_ref):
    xb = x.astype(BF16)
    acc = None
    for sl in (slice(0, FFN_SPLIT), slice(FFN_SPLIT, D_FF)):
        gate = _dot(xb, wg_ref[:, sl])
        up = _dot(xb, wu_ref[:, sl])
        h = (gate * _sigmoid(gate) * up).astype(BF16)
        part = _dot(h, wd_ref[sl, :])
        acc = part if acc is None else acc + part
    return _layer_norm(ALPHA * x + 0.5 * acc, g_ref[...], b_ref[...])


def _ffn_kernel(x_ref, wg_ref, wu_ref, wd_ref, g_ref, b_ref, o_ref):
    o_ref[...] = _ffn_sublayer(x_ref[...], wg_ref, wu_ref, wd_ref, g_ref, b_ref)


def _ffn_operands(ffn):
    wg, wu, wd, g, b = ffn
    specs = [_full((D_MODEL, D_FF)), _full((D_MODEL, D_FF)), _full((D_FF, D_MODEL)),
             _full((1, D_MODEL)), _full((1, D_MODEL))]
    return [wg, wu, wd, g.reshape(1, D_MODEL), b.reshape(1, D_MODEL)], specs


def ffn_ln(x, ffn):
    t = x.shape[0]
    tm = _row_tile(t)
    operands, specs = _ffn_operands(ffn)
    return pl.pallas_call(
        _ffn_kernel,
        grid=(t // tm,),
        in_specs=[pl.BlockSpec((tm, D_MODEL), lambda i: (i, 0))] + specs,
        out_specs=pl.BlockSpec((tm, D_MODEL), lambda i: (i, 0)),
        out_shape=jax.ShapeDtypeStruct((t, D_MODEL), F32),
        compiler_params=_cparams(("parallel",)),
        name="ffn_ln",
    )(x, *operands)


def _proj_kernel(x_ref, w_ref, *o_refs, plan):
    xb = x_ref[...].astype(BF16)
    refs = iter(o_refs)
    for lo, hi, outs in plan:
        y = _dot(xb, w_ref[:, lo:hi])
        for _, scale in outs:
            o_ref = next(refs)
            o_ref[...] = (y if scale == 1.0 else y * scale).astype(o_ref.dtype)


def proj(x, w, plan):
    t = x.shape[0]
    tm = _row_tile(t)
    n = w.shape[1]
    flat = [(hi - lo, dt) for lo, hi, outs in plan for dt, _ in outs]
    return pl.pallas_call(
        functools.partial(_proj_kernel, plan=plan),
        grid=(t // tm,),
        in_specs=[pl.BlockSpec((tm, D_MODEL), lambda i: (i, 0)), _full((D_MODEL, n))],
        out_specs=[pl.BlockSpec((tm, width), lambda i: (i, 0)) for width, _ in flat],
        out_shape=[jax.ShapeDtypeStruct((t, width), dt) for width, dt in flat],
        compiler_params=_cparams(("parallel",)),
        name="proj",
    )(x, w)


def _mix0_in_kernel(x_ref, w_ref, u_ref, us_ref, q_ref, k_ref, kb_ref, v_ref, vb_ref, u_scr, *, q_scale):
    xb = x_ref[...].astype(BF16)
    sw = S5_WIDTH
    u = _dot(xb, w_ref[:, 0:sw])
    u_ref[...] = u
    for c in range(sw // LANES):
        u_scr[c] = u[:, c * LANES:(c + 1) * LANES]
    chunks = u.shape[0] // S5_CHUNK
    per_slab = S5_ROW // LANES
    for s in range(S5_CHUNK):
        for half in range(2):
            for c in range(per_slab):
                rows = u_scr[half * per_slab + c, pl.ds(s, chunks, stride=S5_CHUNK), :]
                us_ref[2 * s + half, :, c * LANES:(c + 1) * LANES] = rows.astype(BF16)
    q_ref[...] = (_dot(xb, w_ref[:, sw:2 * sw]) * q_scale).astype(BF16)
    k = _dot(xb, w_ref[:, 2 * sw:3 * sw])
    k_ref[...] = k.reshape(k.shape[0], SB_HEADS, SB_HEAD_DIM)
    kb_ref[...] = k.astype(BF16)
    v = _dot(xb, w_ref[:, 3 * sw:4 * sw])
    v_ref[...] = v.reshape(v.shape[0], SB_HEADS, SB_HEAD_DIM)
    vb_ref[...] = v.astype(BF16)


def mix0_in(x, w, q_scale):
    t = x.shape[0]
    tm = _row_tile(t)
    sw = S5_WIDTH
    tok = pl.BlockSpec((tm, sw), lambda i: (i, 0))
    heads = pl.BlockSpec((tm, SB_HEADS, SB_HEAD_DIM), lambda i: (i, 0, 0))
    return pl.pallas_call(
        functools.partial(_mix0_in_kernel, q_scale=q_scale),
        grid=(t // tm,),
        in_specs=[pl.BlockSpec((tm, D_MODEL), lambda i: (i, 0)), _full((D_MODEL, MIX0_IN))],
        out_specs=[tok, pl.BlockSpec((S5_SLABS, tm // S5_CHUNK, S5_ROW), lambda i: (0, i, 0)),
                   tok, heads, tok, heads, tok],
        out_shape=[jax.ShapeDtypeStruct((t, sw), F32),
                   jax.ShapeDtypeStruct((S5_SLABS, t // S5_CHUNK, S5_ROW), BF16),
                   jax.ShapeDtypeStruct((t, sw), BF16), jax.ShapeDtypeStruct((t, SB_HEADS, SB_HEAD_DIM), F32),
                   jax.ShapeDtypeStruct((t, sw), BF16), jax.ShapeDtypeStruct((t, SB_HEADS, SB_HEAD_DIM), F32),
                   jax.ShapeDtypeStruct((t, sw), BF16)],
        scratch_shapes=[pltpu.VMEM((sw // LANES, tm, LANES), F32)],
        compiler_params=_cparams(("parallel",)),
        name="mix0_in",
    )(x, w)


S5_SLABS = 2 * S5_CHUNK
S5_GH = S5_GROUPS // 2
S5_HALF = S5_GH * S5_STATE
S5_CARRY_LANES = LANES


def _s5_tiles_kernel(a_re_ref, a_im_ref, ldt_ref, bt_re_ref, bt_im_ref, cq_re_ref, cq_im_ref,
                     p_ref, qt_ref, k_ref, a16r_ref, a16i_ref):
    s = pl.program_id(1).astype(F32)

    def cpow(m, ar, ai, dt):
        mag = jnp.exp(m * (ar * dt))
        ang = m * (ai * dt)
        return mag * jnp.cos(ang), mag * jnp.sin(ang)

    def zoh(ar, ai, dt):
        abr, abi = cpow(1.0, ar, ai, dt)
        nr, ni, den = abr - 1.0, abi, ar * ar + ai * ai
        return (nr * ar + ni * ai) / den, (ni * ar - nr * ai) / den

    ar, ai, dt = a_re_ref[...], a_im_ref[...], jnp.exp(ldt_ref[...])
    fr, fi = zoh(ar, ai, dt)
    pwr, pwi = cpow(float(S5_CHUNK - 1) - s, ar, ai, dt)
    cr, ci = pwr * fr - pwi * fi, pwr * fi + pwi * fr
    shape = (S5_ROW, S5_HALF)
    same = (jnp.right_shift(lax.broadcasted_iota(jnp.int32, shape, 0), S5_GROUP.bit_length() - 1)
            == jnp.right_shift(lax.broadcasted_iota(jnp.int32, shape, 1), S5_STATE.bit_length() - 1))
    btr, bti = bt_re_ref[...], bt_im_ref[...]
    p_re = jnp.where(same, cr * btr - ci * bti, 0.0).astype(BF16)
    p_im = jnp.where(same, cr * bti + ci * btr, 0.0).astype(BF16)
    p_ref[:, :S5_HALF] = p_re
    p_ref[:, S5_HALF:] = p_im
    qwr, qwi = cpow(s + 1.0, ar, ai, dt)
    cqr, cqi = cq_re_ref[...], cq_im_ref[...]
    qt_ref[:, :S5_HALF] = jnp.where(same, cqr * qwr - cqi * qwi, 0.0).astype(BF16)
    qt_ref[:, S5_HALF:] = jnp.where(same, -(cqr * qwi + cqi * qwr), 0.0).astype(BF16)
    a16r_ref[...], a16i_ref[...] = cpow(float(S5_CHUNK), ar, ai, dt)

    c0 = jnp.concatenate([jnp.where(same, cqr, 0.0), jnp.where(same, -cqi, 0.0)], axis=1).astype(BF16)
    k_ref[...] = _dot_nt(jnp.concatenate([p_re, p_im], axis=1), c0).astype(BF16)


def s5_tiles(a_re, a_im, log_dt, b_re, b_im, c_re, c_im):
    gh, n, pch = S5_GH, S5_STATE, S5_GROUP
    halves = lambda v: v.reshape((2, gh) + v.shape[1:])
    lane = lambda v: halves(v).reshape(2, 1, S5_HALF)
    ldt = jnp.broadcast_to(log_dt[:, None], (S5_GROUPS, n))
    bt = lambda b: halves(jnp.swapaxes(b, 1, 2)).reshape(2, S5_ROW, n)
    wide = lambda v: jnp.tile(v, (1, 1, gh))
    cq = lambda c: jnp.tile(jnp.transpose(halves(c), (0, 2, 1, 3)).reshape(2, pch, S5_HALF), (1, gh, 1))
    half = lambda *shape: pl.BlockSpec((None,) + shape, lambda h, s: (h,) + (0,) * len(shape))
    pt, qt, kt, a16r, a16i = pl.pallas_call(
        _s5_tiles_kernel,
        grid=(2, S5_CHUNK),
        in_specs=[half(1, S5_HALF)] * 3 + [half(S5_ROW, S5_HALF)] * 4,
        out_specs=[pl.BlockSpec((None, S5_ROW, 2 * S5_HALF), lambda h, s: (h, s, 0)),
                   pl.BlockSpec((None, None, S5_ROW, 2 * S5_HALF), lambda h, s: (h, s, 0, 0)),
                   pl.BlockSpec((None, S5_ROW, S5_ROW), lambda h, s: (h, s, 0)),
                   half(1, S5_HALF), half(1, S5_HALF)],
        out_shape=[jax.ShapeDtypeStruct((2, S5_CHUNK * S5_ROW, 2 * S5_HALF), BF16),
                   jax.ShapeDtypeStruct((2, S5_CHUNK, S5_ROW, 2 * S5_HALF), BF16),
                   jax.ShapeDtypeStruct((2, S5_CHUNK * S5_ROW, S5_ROW), BF16),
                   jax.ShapeDtypeStruct((2, 1, S5_HALF), F32), jax.ShapeDtypeStruct((2, 1, S5_HALF), F32)],
        compiler_params=_cparams(("arbitrary", "arbitrary")),
        name="s5_tiles",
    )(lane(a_re), lane(a_im), lane(ldt), wide(bt(b_re)), wide(bt(b_im)), cq(c_re), cq(c_im))
    nstate = S5_GROUPS * S5_STATE
    return kt, pt, qt, a16r.reshape(1, nstate), a16i.reshape(1, nstate)


def _s5_state_kernel(us_ref, p_ref, sre_ref, sim_ref):
    steps = jnp.concatenate([us_ref[s] for s in range(S5_CHUNK)], axis=1)
    part = _dot(steps, p_ref[...])
    for c in range(S5_HALF // LANES):
        sre_ref[c] = part[:, c * LANES:(c + 1) * LANES]
        sim_ref[c] = part[:, S5_HALF + c * LANES:S5_HALF + (c + 1) * LANES]


def _s5_carry_kernel(sre_ref, sim_ref, h0re_ref, h0im_ref, ar_ref, ai_ref,
                     hre_ref, him_ref, lre_ref, lim_ref, *, bn, nj):
    ar, ai = ar_ref[...], ai_ref[...]

    def body(j, carry):
        hr, hi = carry
        rows = pl.ds(j, bn, stride=nj)
        hre_ref[rows, :] = hr
        him_ref[rows, :] = hi
        return ar * hr - ai * hi + sre_ref[rows, :], ar * hi + ai * hr + sim_ref[rows, :]

    hr, hi = lax.fori_loop(0, nj, body, (h0re_ref[...], h0im_ref[...]))
    lre_ref[...] = hr
    lim_ref[...] = hi


def _s5_out_kernel(us_ref, k_ref, hre_ref, him_ref, q_ref, y_ref):
    blocks = range(S5_HALF // LANES)
    hin = jnp.concatenate([hre_ref[c] for c in blocks] + [him_ref[c] for c in blocks], axis=1).astype(BF16)
    for t in range(S5_CHUNK):
        steps = jnp.concatenate([us_ref[s] for s in range(t + 1)], axis=1)
        lags = k_ref[(S5_CHUNK - 1 - t) * S5_ROW:, :]
        y_ref[t] = _dot(steps, lags) + _dot_nt(hin, q_ref[t])


def s5_mix(us, h0re, h0im, tiles, bn, nj):
    kt, pt, qq, ar, ai = tiles
    r = bn * nj
    nstate = S5_GROUPS * S5_STATE
    us4 = us.reshape(S5_CHUNK, 2, r, S5_ROW)
    half_once = lambda *shape: pl.BlockSpec((None,) + shape, lambda h, i: (h,) + (0,) * len(shape),
                                            pipeline_mode=pl.Buffered(1))
    rows = min(r, 512)
    lanes = S5_CARRY_LANES
    sre, sim = pl.pallas_call(
        _s5_state_kernel,
        grid=(2, r // rows),
        in_specs=[pl.BlockSpec((S5_CHUNK, None, rows, S5_ROW), lambda h, i: (0, h, i, 0)),
                  half_once(S5_CHUNK * S5_ROW, 2 * S5_HALF)],
        out_specs=[pl.BlockSpec((S5_HALF // lanes, rows, lanes), lambda h, i: (h, i, 0))] * 2,
        out_shape=[jax.ShapeDtypeStruct((nstate // lanes, r, lanes), F32)] * 2,
        compiler_params=_cparams(("arbitrary", "arbitrary")),
        name="s5_state",
    )(us4, pt)
    col = lambda n: pl.BlockSpec((n, lanes), lambda i: (0, i))
    blk = pl.BlockSpec((None, r, lanes), lambda i: (i, 0, 0))
    hre, him, lre, lim = pl.pallas_call(
        functools.partial(_s5_carry_kernel, bn=bn, nj=nj),
        grid=(nstate // lanes,),
        in_specs=[blk, blk, col(bn), col(bn), col(1), col(1)],
        out_specs=[blk, blk, col(bn), col(bn)],
        out_shape=[jax.ShapeDtypeStruct((nstate // lanes, r, lanes), F32)] * 2
                  + [jax.ShapeDtypeStruct((bn, nstate), F32)] * 2,
        compiler_params=_cparams(("parallel",)),
        name="s5_carry",
    )(sre, sim, h0re, h0im, ar, ai)
    rows = min(r, 256)
    slabs = pl.BlockSpec((S5_CHUNK, None, rows, S5_ROW), lambda h, i: (0, h, i, 0))
    y = pl.pallas_call(
        _s5_out_kernel,
        grid=(2, r // rows),
        in_specs=[slabs, half_once(S5_CHUNK * S5_ROW, S5_ROW),
                  pl.BlockSpec((S5_HALF // lanes, rows, lanes), lambda h, i: (h, i, 0)),
                  pl.BlockSpec((S5_HALF // lanes, rows, lanes), lambda h, i: (h, i, 0)),
                  half_once(S5_CHUNK, S5_ROW, 2 * S5_HALF)],
        out_specs=slabs,
        out_shape=jax.ShapeDtypeStruct((S5_CHUNK, 2, r, S5_ROW), F32),
        compiler_params=_cparams(("arbitrary", "arbitrary")),
        name="s5_out",
    )(us4, kt, hre, him, qq)
    return y.reshape(S5_SLABS, r, S5_ROW), lre, lim


SB_TK = 256
LOG2E = math.log2(math.e)
SB_SKIP_LOG2 = 152.0
SB_SUBS = 4


def _sb_chains(tasks, ones, keys_on_lanes=False):
    zs = [_dot(q, k) if keys_on_lanes else _dot_nt(q, k) for q, k, _, _, _ in tasks]
    tails = [jnp.log2(1.0 + jnp.exp2(-jnp.abs(z))) for z in zs]
    sps = [jnp.maximum(z, 0.0) + t for z, t in zip(zs, tails)]
    sps = [sp if task[3] is None else jnp.where(task[3], sp, 0.0) for sp, task in zip(sps, tasks)]
    mass = [jnp.sum(sp, axis=-1, keepdims=True) for sp in sps]
    his = [sp.astype(BF16) for sp in sps]
    los = [(sp - hi.astype(F32)).astype(BF16) for sp, hi in zip(sps, his)]
    sums = [_dot(jnp.concatenate([hi, lo], axis=0), ones) for hi, lo in zip(his, los)]
    outs = []
    for i, (_, _, v, mask, carry) in enumerate(tasks):
        rows = zs[i].shape[0]
        suffix = sums[i][:rows] + sums[i][rows:]
        logw = jnp.minimum(zs[i], 0.0) - tails[i] - suffix
        if carry is not None:
            logw = logw - (mass[carry] if isinstance(carry, int) else carry)
        w = jnp.exp2(logw)
        if mask is not None:
            w = jnp.where(mask, w, 0.0)
        wb = w.astype(BF16)
        outs.append(_dot_nt(wb, v) if keys_on_lanes else _dot(wb, v))
    return outs, mass


def _sb_block(q2, k, v, ones, carry, mask, keys_on_lanes=False):
    half = q2.shape[0] // 2
    halves = (slice(0, half), slice(half, 2 * half))
    outs, mass = _sb_chains([(q2[h], k, v, None if mask is None else mask[h],
                              None if carry is None else carry[h]) for h in halves], ones, keys_on_lanes)
    return jnp.concatenate(outs, axis=0), jnp.concatenate(mass, axis=0)


def _sb_walk(q2, load_kv, ones, blocks, acc, carry, keys_on_lanes=False):
    def cond(s):
        return jnp.logical_and(s[0] < blocks, s[1] < SB_SKIP_LOG2)

    def body(s):
        i, _, acc, carry = s
        rows = pl.ds(pl.multiple_of((blocks - 1 - i) * SB_TK, SB_TK), SB_TK)
        k, v = load_kv(rows)
        d, m = _sb_block(q2, k, v, ones, carry, None, keys_on_lanes)
        carry = carry + m
        return i + 1, jnp.min(carry), acc + d, carry

    return lax.while_loop(cond, body, (jnp.int32(0), jnp.min(carry), acc, carry))[2]


def _sb_prompt_kernel(q_ref, kd_ref, vd_ref, kp_ref, vp_ref, ones_ref, o_ref):
    i = pl.program_id(2)
    t, n = SB_TK, SB_SUBS
    lanes = 2 * SB_HEAD_DIM
    q = q_ref[...]
    lane_lo = lax.broadcasted_iota(jnp.int32, (n * t, lanes), 1) < SB_HEAD_DIM
    zero = jnp.zeros_like(q)
    heads = (jnp.where(lane_lo, q, zero), jnp.where(lane_lo, zero, q))
    row = lax.broadcasted_iota(jnp.int32, (t, t), 0)
    col = lax.broadcasted_iota(jnp.int32, (t, t), 1)
    earlier = col < row
    ones = ones_ref[...]
    kd = [kd_ref[s * t:(s + 1) * t, :] for s in range(n)]
    vd = [vd_ref[s * t:(s + 1) * t, :] for s in range(n)]
    qs = [[h[s * t:(s + 1) * t, :] for h in heads] for s in range(n)]

    def front(with_past):
        tasks = [(qs[s][h], kd[s], vd[s], earlier, None) for s in range(n) for h in range(2)]
        past_of = {}
        for s in range(1, n):
            past_of[s] = len(tasks)
            tasks += [(qs[s][h], kd[s - 1], vd[s - 1], None, 2 * s + h) for h in range(2)]
        if with_past:
            rows = pl.ds(pl.multiple_of((n * i - 1) * t, t), t)
            past_of[0] = len(tasks)
            tasks += [(qs[0][h], kp_ref[rows, :], vp_ref[rows, :], None, h) for h in range(2)]
        outs, mass = _sb_chains(tasks, ones)
        result = []
        for s in range(n):
            parts = [(outs[2 * s + h], mass[2 * s + h]) for h in range(2)]
            if s in past_of:
                parts = [(o + outs[past_of[s] + h], m + mass[past_of[s] + h]) for h, (o, m) in enumerate(parts)]
            result += [jnp.concatenate([parts[0][0], parts[1][0]], axis=0),
                       jnp.concatenate([parts[0][1], parts[1][1]], axis=0)]
        return tuple(result)

    state = lax.cond(i > 0, lambda: front(True), lambda: front(False))
    load_kv = lambda rows: (kp_ref[rows, :], vp_ref[rows, :])
    first = lax.broadcasted_iota(jnp.int32, (t, lanes), 1) < SB_HEAD_DIM
    for s in range(n):
        acc, mass = state[2 * s], state[2 * s + 1]
        q2 = jnp.concatenate(qs[s], axis=0)
        acc = _sb_walk(q2, load_kv, ones, jnp.maximum(n * i + s - 1, 0), acc, mass)
        o_ref[s * t:(s + 1) * t, :] = jnp.where(first, acc[:t], acc[t:])


def _sb_cached_kernel(q_ref, kd_ref, vd_ref, kp_ref, vp_ref, ud_ref, up_ref, o_ref, *, tq, blocks):
    lanes = 2 * SB_HEAD_DIM
    pairs = SB_HEADS // 2
    lane_lo = lax.broadcasted_iota(jnp.int32, (tq, lanes), 1) < SB_HEAD_DIM
    row = lax.broadcasted_iota(jnp.int32, (tq, tq), 0)
    col = lax.broadcasted_iota(jnp.int32, (tq, tq), 1)
    earlier = col < row
    cols = [slice(p * lanes, (p + 1) * lanes) for p in range(pairs)]

    def load_kv(keys, pair):
        flat = lambda ref: ref[2 * pair:2 * pair + 2, :, keys].reshape(lanes, SB_TK).astype(BF16)
        return flat(kp_ref), flat(vp_ref)

    qs = []
    for p in range(pairs):
        q = q_ref[:, cols[p]]
        zero = jnp.zeros_like(q)
        qs += [jnp.where(lane_lo, q, zero), jnp.where(lane_lo, zero, q)]
    new = [(qs[2 * p + h], kd_ref[:, cols[p]], vd_ref[:, cols[p]], earlier, None)
           for p in range(pairs) for h in range(2)]
    outs_new, mass_new = _sb_chains(new, ud_ref[...])
    newest = slice((blocks - 1) * SB_TK, blocks * SB_TK)
    cached = []
    for p in range(pairs):
        k, v = load_kv(newest, p)
        cached += [(qs[2 * p + h], k, v, None, mass_new[2 * p + h]) for h in range(2)]
    outs_old, mass_old = _sb_chains(cached, up_ref[...], keys_on_lanes=True)
    for p in range(pairs):
        a, b = 2 * p, 2 * p + 1
        acc = jnp.concatenate([outs_new[a] + outs_old[a], outs_new[b] + outs_old[b]], axis=0)
        mass = jnp.concatenate([mass_new[a] + mass_old[a], mass_new[b] + mass_old[b]], axis=0)
        q2 = jnp.concatenate([qs[a], qs[b]], axis=0)
        acc = _sb_walk(q2, functools.partial(load_kv, pair=p), up_ref[...], blocks - 1, acc, mass,
                       keys_on_lanes=True)
        o_ref[:, cols[p]] = jnp.where(lane_lo, acc[:tq], acc[tq:])


def _strict_lower_ones(n):
    r = lax.broadcasted_iota(jnp.int32, (n, n), 0)
    c = lax.broadcasted_iota(jnp.int32, (n, n), 1)
    return (r > c).astype(BF16)


def stick_breaking_fresh(q, k, v, bn, length):
    lanes = 2 * SB_HEAD_DIM
    tq = SB_SUBS * SB_TK
    assert length % tq == 0
    blk = pl.BlockSpec((None, tq, lanes), lambda b, h, i: (b, i, h))
    seq = pl.BlockSpec((None, length, lanes), lambda b, h, i: (b, 0, h))
    return pl.pallas_call(
        _sb_prompt_kernel,
        grid=(bn, SB_HEADS // 2, length // tq),
        in_specs=[blk, blk, blk, seq, seq, _full((SB_TK, SB_TK))],
        out_specs=blk,
        out_shape=jax.ShapeDtypeStruct((bn, length, SB_WIDTH), F32),
        compiler_params=_cparams(("parallel", "parallel", "arbitrary")),
        name="stick_breaking",
    )(q, k, v, k, v, _strict_lower_ones(SB_TK))


def stick_breaking_cached(q, k_new, v_new, k_past, v_past, bn, lq):
    lp = k_past.shape[3]
    assert lp % SB_TK == 0 and lp >= SB_TK
    blk = pl.BlockSpec((None, lq, SB_WIDTH), lambda b: (b, 0, 0))
    past = pl.BlockSpec((None, SB_HEADS, SB_HEAD_DIM, lp), lambda b: (b, 0, 0, 0))
    return pl.pallas_call(
        functools.partial(_sb_cached_kernel, tq=lq, blocks=lp // SB_TK),
        grid=(bn,),
        in_specs=[blk, blk, blk, past, past, _full((lq, lq)), _full((SB_TK, SB_TK))],
        out_specs=blk,
        out_shape=jax.ShapeDtypeStruct((bn, lq, SB_WIDTH), F32),
        compiler_params=_cparams(("parallel",)),
        name="stick_breaking_cached",
    )(q, k_new, v_new, k_past, v_past, _strict_lower_ones(lq), _strict_lower_ones(SB_TK))


def _gelu(x):
    return 0.5 * x * (1.0 + jnp.tanh(math.sqrt(2.0 / math.pi) * (x + 0.044715 * (x * x * x))))


def _mix0_out_kernel(x_ref, ys_ref, u_ref, sb_ref, d_ref, wglu_ref, bglu_ref, wo1_ref, wo2_ref, g_ref, b_ref,
                     wg_ref, wu_ref, wd_ref, g2_ref, b2_ref, o_ref, y_scr):
    x = x_ref[...]
    chunks = y_scr.shape[1] // S5_CHUNK
    per_slab = S5_ROW // LANES
    for s in range(S5_CHUNK):
        for half in range(2):
            for c in range(per_slab):
                y_scr[half * per_slab + c, pl.ds(s, chunks, stride=S5_CHUNK), :] = (
                    ys_ref[2 * s + half, :, c * LANES:(c + 1) * LANES])
    ys = jnp.concatenate([y_scr[c] for c in range(S5_WIDTH // LANES)], axis=1)
    gl = _gelu(ys + d_ref[...] * u_ref[...])
    gate = _sigmoid(_dot(gl.astype(BF16), wglu_ref[...]) + bglu_ref[...])
    m = _dot((gl * gate).astype(BF16), wo1_ref[...]) + _dot(sb_ref[...].astype(BF16), wo2_ref[...])
    x = _layer_norm(ALPHA * x + m, g_ref[...], b_ref[...])
    o_ref[...] = _ffn_sublayer(x, wg_ref, wu_ref, wd_ref, g2_ref, b2_ref)


def mix0_out_ffn(x, ys, u, sb, d, wglu, bglu, wo1, wo2, g, b, ffn):
    t = x.shape[0]
    tm = _row_tile(t)
    rowblk = lambda n: pl.BlockSpec((tm, n), lambda i: (i, 0))
    ffn_operands, ffn_specs = _ffn_operands(ffn)
    return pl.pallas_call(
        _mix0_out_kernel,
        grid=(t // tm,),
        in_specs=[rowblk(D_MODEL),
                  pl.BlockSpec((S5_SLABS, tm // S5_CHUNK, S5_ROW), lambda i: (0, i, 0)),
                  rowblk(S5_WIDTH), rowblk(SB_WIDTH),
                  _full((1, S5_WIDTH)), _full((S5_WIDTH, S5_WIDTH)), _full((1, S5_WIDTH)),
                  _full((S5_WIDTH, D_MODEL)), _full((SB_WIDTH, D_MODEL)),
                  _full((1, D_MODEL)), _full((1, D_MODEL))] + ffn_specs,
        out_specs=rowblk(D_MODEL),
        out_shape=jax.ShapeDtypeStruct((t, D_MODEL), F32),
        scratch_shapes=[pltpu.VMEM((S5_WIDTH // LANES, tm, LANES), F32)],
        compiler_params=_cparams(("parallel",)),
        name="mix0_out_ffn",
    )(x, ys, u, sb, d.reshape(1, -1), wglu, bglu.reshape(1, -1), wo1, wo2, g.reshape(1, -1), b.reshape(1, -1),
      *ffn_operands)


def _ssd_kernel(xbc_ref, z_ref, dt_ref, dtt_ref, conv0_ref, h0_ref, cw_ref, cb_ref, dtb_ref, dtbt_ref,
                alog_ref, alogt_ref, dsk_ref, ng_ref, tril_ref, triu_ref,
                y_ref, hl_ref, ext_ref, h_ref, *, lc, nc):
    c = pl.program_id(1)

    @pl.when(c == 0)
    def _():
        ext_ref[0:CONV_PAD, :] = conv0_ref[...]
        h_ref[...] = h0_ref[...]

    ext_ref[CONV_PAD:CONV_PAD + lc, :] = xbc_ref[...]
    ext = ext_ref[...]
    conv = cb_ref[...] + ext[CONV_PAD:, :] * cw_ref[SSD_CONV - 1:SSD_CONV, :]
    for back in range(1, SSD_CONV):
        tap = SSD_CONV - 1 - back
        conv = conv + pltpu.roll(ext, back, 0)[CONV_PAD:, :] * cw_ref[tap:tap + 1, :]
    ext_ref[0:CONV_PAD, :] = ext[lc:lc + CONV_PAD, :]
    act = conv * _sigmoid(conv)

    dt = _softplus(dt_ref[...] + dtb_ref[...])
    dtt = _softplus(dtt_ref[...] + dtbt_ref[...])
    cs = _sum_left(tril_ref[...], dt * (-LOG2E * jnp.exp(alog_ref[...])))
    cst = _sum_right(dtt * (-LOG2E * jnp.exp(alogt_ref[...])), triu_ref[...])
    cs_last = cs[lc - 1:lc, :]

    row = lax.broadcasted_iota(jnp.int32, (lc, lc), 0)
    col = lax.broadcasted_iota(jnp.int32, (lc, lc), 1)
    causal = col <= row
    lanes = 2 * SSD_HEAD_DIM
    lane_lo = lax.broadcasted_iota(jnp.int32, (lc, lanes), 1) < SSD_HEAD_DIM
    pairs_per_group = SSD_PAIRS // SSD_GROUPS

    for g in range(SSD_GROUPS):
        bm = act[:, SSD_INNER + g * SSD_STATE:SSD_INNER + (g + 1) * SSD_STATE].astype(BF16)
        cm = act[:, SSD_INNER + SSD_GN + g * SSD_STATE:SSD_INNER + SSD_GN + (g + 1) * SSD_STATE].astype(BF16)
        cb = _dot_nt(cm, bm)
        gated, sq = [], jnp.zeros((lc, lanes), F32)
        for k in range(g * pairs_per_group, (g + 1) * pairs_per_group):
            sl = slice(k * lanes, (k + 1) * lanes)
            xp = act[:, sl]
            heads = (2 * k, 2 * k + 1)
            pair = lambda v: jnp.where(lane_lo[:v[0].shape[0]], v[0], v[1])
            xdt = xp * pair([dt[:, h:h + 1] for h in heads])
            xdtb = xdt.astype(BF16)
            csb = [jnp.broadcast_to(cs[:, h:h + 1], (lc, lanes)) for h in heads]
            ys = []
            for h, csh in zip(heads, csb):
                seg = csh[:, :lc] - cst[h:h + 1, :]
                w = cb * jnp.exp2(jnp.where(causal, seg, -jnp.inf))
                ys.append(_dot(w.astype(BF16), xdtb))
            y = pair(ys)
            cs_pair = pair(csb)
            last_pair = pair([cs_last[:, h:h + 1] for h in heads])
            ht = h_ref[k]
            y = y + _dot(cm, ht.astype(BF16)) * jnp.exp2(cs_pair)
            xw = (xdt * jnp.exp2(last_pair - cs_pair)).astype(BF16)
            h_ref[k] = ht * jnp.exp2(last_pair) + _dot_tn(bm, xw)
            y = y + dsk_ref[:, sl] * xp
            zz = z_ref[:, sl]
            yg = y * (zz * _sigmoid(zz))
            sq = sq + yg * yg
            gated.append(yg)
        mean_sq = jnp.sum(sq, axis=-1, keepdims=True) * (1.0 / (pairs_per_group * lanes))
        scale = lax.rsqrt(mean_sq + RMS_EPS)
        for k, yg in zip(range(g * pairs_per_group, (g + 1) * pairs_per_group), gated):
            sl = slice(k * lanes, (k + 1) * lanes)
            y_ref[:, sl] = yg * scale * ng_ref[:, sl]

    @pl.when(c == nc - 1)
    def _():
        hl_ref[...] = h_ref[...]


def ssd_mix(xbc, z, dt_raw, conv0, h0, conv_w, conv_b, dt_bias, a_log, d_skip, norm_g, bn, length, lc):
    nc = length // lc
    t = bn * length
    lanes = 2 * SSD_HEAD_DIM
    dtt = jnp.swapaxes(dt_raw.reshape(bn * nc, lc, SSD_HEADS), 1, 2)
    conv0p = jnp.pad(conv0, ((0, 0), (CONV_PAD - (SSD_CONV - 1), 0), (0, 0)))
    a_log = a_log.astype(F32)
    r = lax.broadcasted_iota(jnp.int32, (lc, lc), 0)
    cc = lax.broadcasted_iota(jnp.int32, (lc, lc), 1)
    tril = (cc <= r).astype(BF16)
    triu = (r <= cc).astype(BF16)
    tok = lambda n: pl.BlockSpec((lc, n), lambda b, c: (b * nc + c, 0))
    y, hl = pl.pallas_call(
        functools.partial(_ssd_kernel, lc=lc, nc=nc),
        grid=(bn, nc),
        in_specs=[tok(SSD_CONV_DIM), tok(SSD_INNER), tok(SSD_HEADS),
                  pl.BlockSpec((None, SSD_HEADS, lc), lambda b, c: (b * nc + c, 0, 0)),
                  pl.BlockSpec((None, CONV_PAD, SSD_CONV_DIM), lambda b, c: (b, 0, 0)),
                  pl.BlockSpec((None, SSD_PAIRS, lanes, SSD_STATE), lambda b, c: (b, 0, 0, 0)),
                  _full((SSD_CONV, SSD_CONV_DIM)), _full((1, SSD_CONV_DIM)),
                  _full((1, SSD_HEADS)), _full((SSD_HEADS, 1)), _full((1, SSD_HEADS)), _full((SSD_HEADS, 1)),
                  _full((1, SSD_INNER)), _full((1, SSD_INNER)), _full((lc, lc)), _full((lc, lc))],
        out_specs=[tok(SSD_INNER),
                   pl.BlockSpec((None, SSD_PAIRS, lanes, SSD_STATE), lambda b, c: (b, 0, 0, 0))],
        out_shape=[jax.ShapeDtypeStruct((t, SSD_INNER), F32),
                   jax.ShapeDtypeStruct((bn, SSD_PAIRS, lanes, SSD_STATE), F32)],
        scratch_shapes=[pltpu.VMEM((CONV_PAD + lc, SSD_CONV_DIM), F32),
                        pltpu.VMEM((SSD_PAIRS, lanes, SSD_STATE), F32)],
        compiler_params=_cparams(("parallel", "arbitrary")),
        name="ssd_mix",
    )(xbc, z, dt_raw, dtt, conv0p, jnp.swapaxes(h0.reshape(bn, SSD_PAIRS, lanes, SSD_STATE), 2, 3),
      conv_w, conv_b.reshape(1, -1), dt_bias.reshape(1, -1), dt_bias.reshape(-1, 1),
      a_log.reshape(1, -1), a_log.reshape(-1, 1), jnp.repeat(d_skip, SSD_HEAD_DIM).reshape(1, -1),
      norm_g.reshape(1, -1), tril, triu)
    return y, jnp.swapaxes(hl, 2, 3).reshape(bn, SSD_HEADS, SSD_HEAD_DIM, SSD_STATE)


def _out_ffn_kernel(x_ref, a_ref, w_ref, g_ref, b_ref, wg_ref, wu_ref, wd_ref, g2_ref, b2_ref, o_ref):
    m = _dot(a_ref[...].astype(BF16), w_ref[...])
    x = _layer_norm(ALPHA * x_ref[...] + m, g_ref[...], b_ref[...])
    o_ref[...] = _ffn_sublayer(x, wg_ref, wu_ref, wd_ref, g2_ref, b2_ref)


def out_ffn(x, a, w, g, b, ffn):
    t = x.shape[0]
    tm = _row_tile(t)
    k = a.shape[1]
    ffn_operands, ffn_specs = _ffn_operands(ffn)
    return pl.pallas_call(
        _out_ffn_kernel,
        grid=(t // tm,),
        in_specs=[pl.BlockSpec((tm, D_MODEL), lambda i: (i, 0)), pl.BlockSpec((tm, k), lambda i: (i, 0)),
                  _full((k, D_MODEL)), _full((1, D_MODEL)), _full((1, D_MODEL))] + ffn_specs,
        out_specs=pl.BlockSpec((tm, D_MODEL), lambda i: (i, 0)),
        out_shape=jax.ShapeDtypeStruct((t, D_MODEL), F32),
        compiler_params=_cparams(("parallel",)),
        name="out_ffn",
    )(x, a, w, g.reshape(1, -1), b.reshape(1, -1), *ffn_operands)


def _trunk(x, bn, length, states, w):
    t = bn * length
    x = x.reshape(t, D_MODEL)
    ffn = lambda l, s: (w["wg"][l][s], w["wu"][l][s], w["wd"][l][s], w["ln_g"][l, 2 * s], w["ln_b"][l, 2 * s])
    x = ffn_ln(x, ffn(0, 0))

    u, u_slabs, q_bf, k, k_bf, v, v_bf = mix0_in(x, w["mix0_in"], SB_HEAD_DIM ** -0.5 * LOG2E)
    nj = length // S5_CHUNK
    nstate = S5_GROUPS * S5_STATE
    if states is None:
        h0re = h0im = jnp.zeros((bn, nstate), F32)
    else:
        h0re = states[0][0].astype(F32).reshape(bn, nstate)
        h0im = states[1][0].astype(F32).reshape(bn, nstate)
    y_s5, h_re, h_im = s5_mix(u_slabs, h0re, h0im, w["s5_tiles"], bn, nj)
    s5_re = h_re.reshape(1, bn, S5_GROUPS, S5_STATE)
    s5_im = h_im.reshape(1, bn, S5_GROUPS, S5_STATE)

    seq = lambda a: a.reshape(bn, length, SB_WIDTH)
    if states is None:
        sb = stick_breaking_fresh(seq(q_bf), seq(k_bf), seq(v_bf), bn, length)
    else:
        cache = lambda a: jnp.transpose(a[0], (0, 2, 3, 1))
        sb = stick_breaking_cached(seq(q_bf), seq(k_bf), seq(v_bf), cache(states[2]), cache(states[3]), bn, length)
    x = mix0_out_ffn(x, y_s5, u, sb.reshape(t, SB_WIDTH), w["s5_d"], w["s5_wglu"], w["s5_bglu"],
                     w["mix0_out"][:S5_WIDTH], w["mix0_out"][S5_WIDTH:], w["ln_g"][0, 1], w["ln_b"][0, 1],
                     ffn(0, 1))

    x = ffn_ln(x, ffn(1, 0))
    one = ((F32, 1.0),)
    z, xbc, dt_raw = proj(x, w["ssd_in"], (
        (0, SSD_INNER, one), (SSD_INNER, SSD_INNER + SSD_CONV_DIM, one),
        (SSD_INNER + SSD_CONV_DIM, SSD_INNER + SSD_CONV_DIM + SSD_HEADS, one)))
    if states is None:
        conv0 = jnp.zeros((bn, SSD_CONV - 1, SSD_CONV_DIM), F32)
        hs0 = jnp.zeros((bn, SSD_HEADS, SSD_HEAD_DIM, SSD_STATE), F32)
    else:
        conv0 = states[5][0].astype(F32)
        hs0 = states[4][0].astype(F32)
    lc = min(128, length)
    yg, h_ssd = ssd_mix(xbc, z, dt_raw, conv0, hs0, w["ssd_conv_w"], w["ssd_conv_b"], w["ssd_dt_bias"],
                        w["ssd_a_log"], w["ssd_d"], w["ssd_norm_g"], bn, length, lc)
    ext = jnp.concatenate([conv0, xbc.reshape(bn, length, SSD_CONV_DIM)[:, length - (SSD_CONV - 1):]], axis=1)
    new_conv = ext[:, ext.shape[1] - (SSD_CONV - 1):]
    x = out_ffn(x, yg, w["ssd_out"], w["ln_g"][1, 1], w["ln_b"][1, 1], ffn(1, 1))

    return (x.reshape(bn, length, D_MODEL), s5_re, s5_im,
            k.reshape(1, bn, length, SB_HEADS, SB_HEAD_DIM), v.reshape(1, bn, length, SB_HEADS, SB_HEAD_DIM),
            h_ssd[None], new_conv[None])


def kernel(x_prompt, x_sample, state_s5_re, state_s5_im, cache_sb_k, cache_sb_v, state_ssd, state_conv, ln_g, ln_b, ffn_w_gate, ffn_w_up, ffn_w_down, mix0_w_in, s5_a_re, s5_a_im, s5_log_dt, s5_b_re, s5_b_im, s5_c_re, s5_c_im, s5_d, s5_w_glu, s5_b_glu, mix0_w_out, ssd_w_in, ssd_conv_w, ssd_conv_b, ssd_dt_bias, ssd_a_log, ssd_d, ssd_norm_g, ssd_w_out):
    bf = lambda a: a.astype(BF16)
    w = {
        "wg": [[bf(ffn_w_gate[l, s]) for s in range(2)] for l in range(DEPTH)],
        "wu": [[bf(ffn_w_up[l, s]) for s in range(2)] for l in range(DEPTH)],
        "wd": [[bf(ffn_w_down[l, s]) for s in range(2)] for l in range(DEPTH)],
        "ln_g": ln_g, "ln_b": ln_b,
        "mix0_in": bf(mix0_w_in[0]),
        "s5_tiles": s5_tiles(s5_a_re[0], s5_a_im[0], s5_log_dt[0], s5_b_re[0], s5_b_im[0], s5_c_re[0], s5_c_im[0]),
        "s5_d": s5_d[0], "s5_wglu": bf(s5_w_glu[0]), "s5_bglu": s5_b_glu[0], "mix0_out": bf(mix0_w_out[0]),
        "ssd_in": bf(ssd_w_in[0]), "ssd_conv_w": ssd_conv_w[0], "ssd_conv_b": ssd_conv_b[0],
        "ssd_dt_bias": ssd_dt_bias[0], "ssd_a_log": ssd_a_log[0], "ssd_d": ssd_d[0],
        "ssd_norm_g": ssd_norm_g[0], "ssd_out": bf(ssd_w_out[0]),
    }
    bp, lp = x_prompt.shape[0], x_prompt.shape[1]
    bs, ls = x_sample.shape[0], x_sample.shape[1]
    yp, s5rp, s5ip, kp, vp, ssdp, convp = _trunk(x_prompt, bp, lp, None, w)
    ys, s5rs, s5is, ks, vs, ssds, convs = _trunk(
        x_sample, bs, ls, (state_s5_re, state_s5_im, cache_sb_k, cache_sb_v, state_ssd, state_conv), w)
    return (yp, ys, s5rp, s5ip, kp, vp, ssdp, convp, s5rs, s5is, ks, vs, ssds, convs)
```

```python
import functools
import math

import jax
import jax.numpy as jnp
from jax import lax
from jax.experimental import pallas as pl
from jax.experimental.pallas import tpu as pltpu

F32 = jnp.float32
BF16 = jnp.bfloat16

D_MODEL = 1024
DEPTH = 2
ALPHA = (2.0 * DEPTH) ** 0.25
LN_EPS = 1e-5
RMS_EPS = 1e-5
D_FF = 2816

S5_WIDTH = 512
S5_GROUP = 16
S5_GROUPS = 32
S5_STATE = 64
S5_CHUNK = 16
S5_ROW = S5_CHUNK * S5_GROUP
SB_HEADS = 8
SB_HEAD_DIM = 64
SB_WIDTH = 512
MIX0_IN = 2048

SSD_INNER = 2048
SSD_HEAD_DIM = 64
SSD_HEADS = 32
SSD_GROUPS = 4
SSD_STATE = 128
SSD_CONV = 4
SSD_GN = 512
SSD_CONV_DIM = 3072
SSD_PAIRS = SSD_HEADS // 2
CONV_PAD = 8

LANES = 128
VMEM_LIMIT = 56 * 1024 * 1024


def _cparams(sem):
    return pltpu.CompilerParams(dimension_semantics=sem, vmem_limit_bytes=VMEM_LIMIT)


def _sigmoid(x):
    return 1.0 / (1.0 + jnp.exp2(x * -math.log2(math.e)))


def _softplus(x):
    return jnp.maximum(x, 0.0) + jnp.log1p(jnp.exp(-jnp.abs(x)))


def _layer_norm(y, g, b):
    mu = jnp.mean(y, axis=-1, keepdims=True)
    d = y - mu
    var = jnp.mean(d * d, axis=-1, keepdims=True)
    return d * lax.rsqrt(var + LN_EPS) * g + b


def _dot(a, b):
    return jnp.dot(a, b, preferred_element_type=F32)


def _dot_nt(a, b):
    return lax.dot_general(a, b, (((1,), (1,)), ((), ())), preferred_element_type=F32)


def _dot_tn(a, b):
    return lax.dot_general(a, b, (((0,), (0,)), ((), ())), preferred_element_type=F32)


def _split3(x):
    hi = x.astype(BF16)
    r = x - hi.astype(F32)
    mid = r.astype(BF16)
    lo = (r - mid.astype(F32)).astype(BF16)
    return hi, mid, lo


def _sum_right(x, ones_mat):
    hi, mid, lo = _split3(x)
    return _dot(hi, ones_mat) + _dot(mid, ones_mat) + _dot(lo, ones_mat)


def _sum_left(ones_mat, x):
    hi, mid, lo = _split3(x)
    return _dot(ones_mat, hi) + _dot(ones_mat, mid) + _dot(ones_mat, lo)


def _row_tile(t):
    for tm in (512, 256, 128, 64, 32, 16, 8):
        if t % tm == 0:
            return tm
    raise ValueError(f"token count {t} is not a multiple of 8")


def _full(shape):
    return pl.BlockSpec(shape, lambda *_: (0,) * len(shape), pipeline_mode=pl.Buffered(1))


MXU_TILE = 256
FFN_SPLIT = (D_FF // MXU_TILE // 2) * MXU_TILE


def _ffn_sublayer(x, wg_ref, wu_ref, wd_ref, g_ref, b_ref):
    xb = x.astype(BF16)
    acc = None
    for sl in (slice(0, FFN_SPLIT), slice(FFN_SPLIT, D_FF)):
        gate = _dot(xb, wg_ref[:, sl])
        up = _dot(xb, wu_ref[:, sl])
        h = (gate * _sigmoid(gate) * up).astype(BF16)
        part = _dot(h, wd_ref[sl, :])
        acc = part if acc is None else acc + part
    return _layer_norm(ALPHA * x + 0.5 * acc, g_ref[...], b_ref[...])


def _ffn_kernel(x_ref, wg_ref, wu_ref, wd_ref, g_ref, b_ref, o_ref):
    o_ref[...] = _ffn_sublayer(x_ref[...], wg_ref, wu_ref, wd_ref, g_ref, b_ref)


def _ffn_operands(ffn):
    wg, wu, wd, g, b = ffn
    specs = [_full((D_MODEL, D_FF)), _full((D_MODEL, D_FF)), _full((D_FF, D_MODEL)),
             _full((1, D_MODEL)), _full((1, D_MODEL))]
    return [wg, wu, wd, g.reshape(1, D_MODEL), b.reshape(1, D_MODEL)], specs


def ffn_ln(x, ffn):
    t = x.shape[0]
    tm = _row_tile(t)
    operands, specs = _ffn_operands(ffn)
    return pl.pallas_call(
        _ffn_kernel,
        grid=(t // tm,),
        in_specs=[pl.BlockSpec((tm, D_MODEL), lambda i: (i, 0))] + specs,
        out_specs=pl.BlockSpec((tm, D_MODEL), lambda i: (i, 0)),
        out_shape=jax.ShapeDtypeStruct((t, D_MODEL), F32),
        compiler_params=_cparams(("parallel",)),
        name="ffn_ln",
    )(x, *operands)


def _proj_kernel(x_ref, w_ref, *o_refs, plan):
    xb = x_ref[...].astype(BF16)
    refs = iter(o_refs)
    for lo, hi, outs in plan:
        y = _dot(xb, w_ref[:, lo:hi])
        for _, scale in outs:
            o_ref = next(refs)
            o_ref[...] = (y if scale == 1.0 else y * scale).astype(o_ref.dtype)


def proj(x, w, plan):
    t = x.shape[0]
    tm = _row_tile(t)
    n = w.shape[1]
    flat = [(hi - lo, dt) for lo, hi, outs in plan for dt, _ in outs]
    return pl.pallas_call(
        functools.partial(_proj_kernel, plan=plan),
        grid=(t // tm,),
        in_specs=[pl.BlockSpec((tm, D_MODEL), lambda i: (i, 0)), _full((D_MODEL, n))],
        out_specs=[pl.BlockSpec((tm, width), lambda i: (i, 0)) for width, _ in flat],
        out_shape=[jax.ShapeDtypeStruct((t, width), dt) for width, dt in flat],
        compiler_params=_cparams(("parallel",)),
        name="proj",
    )(x, w)


def _mix0_in_kernel(x_ref, w_ref, u_ref, us_ref, q_ref, k_ref, kb_ref, v_ref, vb_ref, u_scr, *, q_scale):
    xb = x_ref[...].astype(BF16)
    sw = S5_WIDTH
    u = _dot(xb, w_ref[:, 0:sw])
    u_ref[...] = u
    for c in range(sw // LANES):
        u_scr[c] = u[:, c * LANES:(c + 1) * LANES]
    chunks = u.shape[0] // S5_CHUNK
    per_slab = S5_ROW // LANES
    for s in range(S5_CHUNK):
        for half in range(2):
            for c in range(per_slab):
                rows = u_scr[half * per_slab + c, pl.ds(s, chunks, stride=S5_CHUNK), :]
                us_ref[2 * s + half, :, c * LANES:(c + 1) * LANES] = rows.astype(BF16)
    q_ref[...] = (_dot(xb, w_ref[:, sw:2 * sw]) * q_scale).astype(BF16)
    k = _dot(xb, w_ref[:, 2 * sw:3 * sw])
    k_ref[...] = k.reshape(k.shape[0], SB_HEADS, SB_HEAD_DIM)
    kb_ref[...] = k.astype(BF16)
    v = _dot(xb, w_ref[:, 3 * sw:4 * sw])
    v_ref[...] = v.reshape(v.shape[0], SB_HEADS, SB_HEAD_DIM)
    vb_ref[...] = v.astype(BF16)


def mix0_in(x, w, q_scale):
    t = x.shape[0]
    tm = _row_tile(t)
    sw = S5_WIDTH
    tok = pl.BlockSpec((tm, sw), lambda i: (i, 0))
    heads = pl.BlockSpec((tm, SB_HEADS, SB_HEAD_DIM), lambda i: (i, 0, 0))
    return pl.pallas_call(
        functools.partial(_mix0_in_kernel, q_scale=q_scale),
        grid=(t // tm,),
        in_specs=[pl.BlockSpec((tm, D_MODEL), lambda i: (i, 0)), _full((D_MODEL, MIX0_IN))],
        out_specs=[tok, pl.BlockSpec((S5_SLABS, tm // S5_CHUNK, S5_ROW), lambda i: (0, i, 0)),
                   tok, heads, tok, heads, tok],
        out_shape=[jax.ShapeDtypeStruct((t, sw), F32),
                   jax.ShapeDtypeStruct((S5_SLABS, t // S5_CHUNK, S5_ROW), BF16),
                   jax.ShapeDtypeStruct((t, sw), BF16), jax.ShapeDtypeStruct((t, SB_HEADS, SB_HEAD_DIM), F32),
                   jax.ShapeDtypeStruct((t, sw), BF16), jax.ShapeDtypeStruct((t, SB_HEADS, SB_HEAD_DIM), F32),
                   jax.ShapeDtypeStruct((t, sw), BF16)],
        scratch_shapes=[pltpu.VMEM((sw // LANES, tm, LANES), F32)],
        compiler_params=_cparams(("parallel",)),
        name="mix0_in",
    )(x, w)


S5_SLABS = 2 * S5_CHUNK
S5_GH = S5_GROUPS // 2
S5_HALF = S5_GH * S5_STATE
S5_CARRY_LANES = LANES


def _s5_tiles_kernel(a_re_ref, a_im_ref, ldt_ref, bt_re_ref, bt_im_ref, cq_re_ref, cq_im_ref,
                     p_ref, qt_ref, k_ref, a16r_ref, a16i_ref):
    s = pl.program_id(1).astype(F32)

    def cpow(m, ar, ai, dt):
        mag = jnp.exp(m * (ar * dt))
        ang = m * (ai * dt)
        return mag * jnp.cos(ang), mag * jnp.sin(ang)

    def zoh(ar, ai, dt):
        abr, abi = cpow(1.0, ar, ai, dt)
        nr, ni, den = abr - 1.0, abi, ar * ar + ai * ai
        return (nr * ar + ni * ai) / den, (ni * ar - nr * ai) / den

    ar, ai, dt = a_re_ref[...], a_im_ref[...], jnp.exp(ldt_ref[...])
    fr, fi = zoh(ar, ai, dt)
    pwr, pwi = cpow(float(S5_CHUNK - 1) - s, ar, ai, dt)
    cr, ci = pwr * fr - pwi * fi, pwr * fi + pwi * fr
    shape = (S5_ROW, S5_HALF)
    same = (jnp.right_shift(lax.broadcasted_iota(jnp.int32, shape, 0), S5_GROUP.bit_length() - 1)
            == jnp.right_shift(lax.broadcasted_iota(jnp.int32, shape, 1), S5_STATE.bit_length() - 1))
    btr, bti = bt_re_ref[...], bt_im_ref[...]
    p_re = jnp.where(same, cr * btr - ci * bti, 0.0).astype(BF16)
    p_im = jnp.where(same, cr * bti + ci * btr, 0.0).astype(BF16)
    p_ref[:, :S5_HALF] = p_re
    p_ref[:, S5_HALF:] = p_im
    qwr, qwi = cpow(s + 1.0, ar, ai, dt)
    cqr, cqi = cq_re_ref[...], cq_im_ref[...]
    qt_ref[:, :S5_HALF] = jnp.where(same, cqr * qwr - cqi * qwi, 0.0).astype(BF16)
    qt_ref[:, S5_HALF:] = jnp.where(same, -(cqr * qwi + cqi * qwr), 0.0).astype(BF16)
    a16r_ref[...], a16i_ref[...] = cpow(float(S5_CHUNK), ar, ai, dt)

    c0 = jnp.concatenate([jnp.where(same, cqr, 0.0), jnp.where(same, -cqi, 0.0)], axis=1).astype(BF16)
    k_ref[...] = _dot_nt(jnp.concatenate([p_re, p_im], axis=1), c0).astype(BF16)


def s5_tiles(a_re, a_im, log_dt, b_re, b_im, c_re, c_im):
    gh, n, pch = S5_GH, S5_STATE, S5_GROUP
    halves = lambda v: v.reshape((2, gh) + v.shape[1:])
    lane = lambda v: halves(v).reshape(2, 1, S5_HALF)
    ldt = jnp.broadcast_to(log_dt[:, None], (S5_GROUPS, n))
    bt = lambda b: halves(jnp.swapaxes(b, 1, 2)).reshape(2, S5_ROW, n)
    wide = lambda v: jnp.tile(v, (1, 1, gh))
    cq = lambda c: jnp.tile(jnp.transpose(halves(c), (0, 2, 1, 3)).reshape(2, pch, S5_HALF), (1, gh, 1))
    half = lambda *shape: pl.BlockSpec((None,) + shape, lambda h, s: (h,) + (0,) * len(shape))
    pt, qt, kt, a16r, a16i = pl.pallas_call(
        _s5_tiles_kernel,
        grid=(2, S5_CHUNK),
        in_specs=[half(1, S5_HALF)] * 3 + [half(S5_ROW, S5_HALF)] * 4,
        out_specs=[pl.BlockSpec((None, S5_ROW, 2 * S5_HALF), lambda h, s: (h, s, 0)),
                   pl.BlockSpec((None, None, S5_ROW, 2 * S5_HALF), lambda h, s: (h, s, 0, 0)),
                   pl.BlockSpec((None, S5_ROW, S5_ROW), lambda h, s: (h, s, 0)),
                   half(1, S5_HALF), half(1, S5_HALF)],
        out_shape=[jax.ShapeDtypeStruct((2, S5_CHUNK * S5_ROW, 2 * S5_HALF), BF16),
                   jax.ShapeDtypeStruct((2, S5_CHUNK, S5_ROW, 2 * S5_HALF), BF16),
                   jax.ShapeDtypeStruct((2, S5_CHUNK * S5_ROW, S5_ROW), BF16),
                   jax.ShapeDtypeStruct((2, 1, S5_HALF), F32), jax.ShapeDtypeStruct((2, 1, S5_HALF), F32)],
        compiler_params=_cparams(("arbitrary", "arbitrary")),
        name="s5_tiles",
    )(lane(a_re), lane(a_im), lane(ldt), wide(bt(b_re)), wide(bt(b_im)), cq(c_re), cq(c_im))
    nstate = S5_GROUPS * S5_STATE
    return kt, pt, qt, a16r.reshape(1, nstate), a16i.reshape(1, nstate)


def _s5_state_kernel(us_ref, p_ref, sre_ref, sim_ref):
    steps = jnp.concatenate([us_ref[s] for s in range(S5_CHUNK)], axis=1)
    part = _dot(steps, p_ref[...])
    for c in range(S5_HALF // LANES):
        sre_ref[c] = part[:, c * LANES:(c + 1) * LANES]
        sim_ref[c] = part[:, S5_HALF + c * LANES:S5_HALF + (c + 1) * LANES]


def _s5_carry_kernel(sre_ref, sim_ref, h0re_ref, h0im_ref, ar_ref, ai_ref,
                     hre_ref, him_ref, lre_ref, lim_ref, *, bn, nj):
    ar, ai = ar_ref[...], ai_ref[...]

    def body(j, carry):
        hr, hi = carry
        rows = pl.ds(j, bn, stride=nj)
        hre_ref[rows, :] = hr
        him_ref[rows, :] = hi
        return ar * hr - ai * hi + sre_ref[rows, :], ar * hi + ai * hr + sim_ref[rows, :]

    hr, hi = lax.fori_loop(0, nj, body, (h0re_ref[...], h0im_ref[...]))
    lre_ref[...] = hr
    lim_ref[...] = hi


def _s5_out_kernel(us_ref, k_ref, hre_ref, him_ref, q_ref, y_ref):
    blocks = range(S5_HALF // LANES)
    hin = jnp.concatenate([hre_ref[c] for c in blocks] + [him_ref[c] for c in blocks], axis=1).astype(BF16)
    for t in range(S5_CHUNK):
        steps = jnp.concatenate([us_ref[s] for s in range(t + 1)], axis=1)
        lags = k_ref[(S5_CHUNK - 1 - t) * S5_ROW:, :]
        y_ref[t] = _dot(steps, lags) + _dot_nt(hin, q_ref[t])


def s5_mix(us, h0re, h0im, tiles, bn, nj):
    kt, pt, qq, ar, ai = tiles
    r = bn * nj
    nstate = S5_GROUPS * S5_STATE
    us4 = us.reshape(S5_CHUNK, 2, r, S5_ROW)
    half_once = lambda *shape: pl.BlockSpec((None,) + shape, lambda h, i: (h,) + (0,) * len(shape),
                                            pipeline_mode=pl.Buffered(1))
    rows = min(r, 512)
    lanes = S5_CARRY_LANES
    sre, sim = pl.pallas_call(
        _s5_state_kernel,
        grid=(2, r // rows),
        in_specs=[pl.BlockSpec((S5_CHUNK, None, rows, S5_ROW), lambda h, i: (0, h, i, 0)),
                  half_once(S5_CHUNK * S5_ROW, 2 * S5_HALF)],
        out_specs=[pl.BlockSpec((S5_HALF // lanes, rows, lanes), lambda h, i: (h, i, 0))] * 2,
        out_shape=[jax.ShapeDtypeStruct((nstate // lanes, r, lanes), F32)] * 2,
        compiler_params=_cparams(("arbitrary", "arbitrary")),
        name="s5_state",
    )(us4, pt)
    col = lambda n: pl.BlockSpec((n, lanes), lambda i: (0, i))
    blk = pl.BlockSpec((None, r, lanes), lambda i: (i, 0, 0))
    hre, him, lre, lim = pl.pallas_call(
        functools.partial(_s5_carry_kernel, bn=bn, nj=nj),
        grid=(nstate // lanes,),
        in_specs=[blk, blk, col(bn), col(bn), col(1), col(1)],
        out_specs=[blk, blk, col(bn), col(bn)],
        out_shape=[jax.ShapeDtypeStruct((nstate // lanes, r, lanes), F32)] * 2
                  + [jax.ShapeDtypeStruct((bn, nstate), F32)] * 2,
        compiler_params=_cparams(("parallel",)),
        name="s5_carry",
    )(sre, sim, h0re, h0im, ar, ai)
    rows = min(r, 256)
    slabs = pl.BlockSpec((S5_CHUNK, None, rows, S5_ROW), lambda h, i: (0, h, i, 0))
    y = pl.pallas_call(
        _s5_out_kernel,
        grid=(2, r // rows),
        in_specs=[slabs, half_once(S5_CHUNK * S5_ROW, S5_ROW),
                  pl.BlockSpec((S5_HALF // lanes, rows, lanes), lambda h, i: (h, i, 0)),
                  pl.BlockSpec((S5_HALF // lanes, rows, lanes), lambda h, i: (h, i, 0)),
                  half_once(S5_CHUNK, S5_ROW, 2 * S5_HALF)],
        out_specs=slabs,
        out_shape=jax.ShapeDtypeStruct((S5_CHUNK, 2, r, S5_ROW), F32),
        compiler_params=_cparams(("arbitrary", "arbitrary")),
        name="s5_out",
    )(us4, kt, hre, him, qq)
    return y.reshape(S5_SLABS, r, S5_ROW), lre, lim


SB_TK = 256
LOG2E = math.log2(math.e)
SB_SKIP_LOG2 = 152.0
SB_SUBS = 4


def _sb_chains(tasks, ones, keys_on_lanes=False):
    zs = [_dot(q, k) if keys_on_lanes else _dot_nt(q, k) for q, k, _, _, _ in tasks]
    tails = [jnp.log2(1.0 + jnp.exp2(-jnp.abs(z))) for z in zs]
    sps = [jnp.maximum(z, 0.0) + t for z, t in zip(zs, tails)]
    sps = [sp if task[3] is None else jnp.where(task[3], sp, 0.0) for sp, task in zip(sps, tasks)]
    mass = [jnp.sum(sp, axis=-1, keepdims=True) for sp in sps]
    suffixes = [_dot(sp.astype(BF16), ones) for sp in sps]
    outs = []
    for i, (_, _, v, mask, carry) in enumerate(tasks):
        logw = jnp.minimum(zs[i], 0.0) - tails[i] - suffixes[i]
        if carry is not None:
            logw = logw - (mass[carry] if isinstance(carry, int) else carry)
        w = jnp.exp2(logw)
        if mask is not None:
            w = jnp.where(mask, w, 0.0)
        wb = w.astype(BF16)
        outs.append(_dot_nt(wb, v) if keys_on_lanes else _dot(wb, v))
    return outs, mass


def _sb_block(q2, k, v, ones, carry, mask, keys_on_lanes=False):
    half = q2.shape[0] // 2
    halves = (slice(0, half), slice(half, 2 * half))
    outs, mass = _sb_chains([(q2[h], k, v, None if mask is None else mask[h],
                              None if carry is None else carry[h]) for h in halves], ones, keys_on_lanes)
    return jnp.concatenate(outs, axis=0), jnp.concatenate(mass, axis=0)


def _sb_walk(q2, load_kv, ones, blocks, acc, carry, keys_on_lanes=False):
    def cond(s):
        return jnp.logical_and(s[0] < blocks, s[1] < SB_SKIP_LOG2)

    def body(s):
        i, _, acc, carry = s
        rows = pl.ds(pl.multiple_of((blocks - 1 - i) * SB_TK, SB_TK), SB_TK)
        k, v = load_kv(rows)
        d, m = _sb_block(q2, k, v, ones, carry, None, keys_on_lanes)
        carry = carry + m
        return i + 1, jnp.min(carry), acc + d, carry

    return lax.while_loop(cond, body, (jnp.int32(0), jnp.min(carry), acc, carry))[2]


def _sb_prompt_kernel(q_ref, kd_ref, vd_ref, kp_ref, vp_ref, ones_ref, o_ref):
    i = pl.program_id(2)
    t, n = SB_TK, SB_SUBS
    lanes = 2 * SB_HEAD_DIM
    q = q_ref[...]
    lane_lo = lax.broadcasted_iota(jnp.int32, (n * t, lanes), 1) < SB_HEAD_DIM
    zero = jnp.zeros_like(q)
    heads = (jnp.where(lane_lo, q, zero), jnp.where(lane_lo, zero, q))
    row = lax.broadcasted_iota(jnp.int32, (t, t), 0)
    col = lax.broadcasted_iota(jnp.int32, (t, t), 1)
    earlier = col < row
    ones = ones_ref[...]
    kd = [kd_ref[s * t:(s + 1) * t, :] for s in range(n)]
    vd = [vd_ref[s * t:(s + 1) * t, :] for s in range(n)]
    qs = [[h[s * t:(s + 1) * t, :] for h in heads] for s in range(n)]

    def front(with_past):
        tasks = [(qs[s][h], kd[s], vd[s], earlier, None) for s in range(n) for h in range(2)]
        past_of = {}
        for s in range(1, n):
            past_of[s] = len(tasks)
            tasks += [(qs[s][h], kd[s - 1], vd[s - 1], None, 2 * s + h) for h in range(2)]
        if with_past:
            rows = pl.ds(pl.multiple_of((n * i - 1) * t, t), t)
            past_of[0] = len(tasks)
            tasks += [(qs[0][h], kp_ref[rows, :], vp_ref[rows, :], None, h) for h in range(2)]
        outs, mass = _sb_chains(tasks, ones)
        result = []
        for s in range(n):
            parts = [(outs[2 * s + h], mass[2 * s + h]) for h in range(2)]
            if s in past_of:
                parts = [(o + outs[past_of[s] + h], m + mass[past_of[s] + h]) for h, (o, m) in enumerate(parts)]
            result += [jnp.concatenate([parts[0][0], parts[1][0]], axis=0),
                       jnp.concatenate([parts[0][1], parts[1][1]], axis=0)]
        return tuple(result)

    state = lax.cond(i > 0, lambda: front(True), lambda: front(False))
    load_kv = lambda rows: (kp_ref[rows, :], vp_ref[rows, :])
    first = lax.broadcasted_iota(jnp.int32, (t, lanes), 1) < SB_HEAD_DIM
    for s in range(n):
        acc, mass = state[2 * s], state[2 * s + 1]
        q2 = jnp.concatenate(qs[s], axis=0)
        acc = _sb_walk(q2, load_kv, ones, jnp.maximum(n * i + s - 1, 0), acc, mass)
        o_ref[s * t:(s + 1) * t, :] = jnp.where(first, acc[:t], acc[t:])


def _sb_cached_kernel(q_ref, kd_ref, vd_ref, kp_ref, vp_ref, ud_ref, up_ref, o_ref, *, tq, blocks):
    lanes = 2 * SB_HEAD_DIM
    pairs = SB_HEADS // 2
    lane_lo = lax.broadcasted_iota(jnp.int32, (tq, lanes), 1) < SB_HEAD_DIM
    row = lax.broadcasted_iota(jnp.int32, (tq, tq), 0)
    col = lax.broadcasted_iota(jnp.int32, (tq, tq), 1)
    earlier = col < row
    cols = [slice(p * lanes, (p + 1) * lanes) for p in range(pairs)]

    def load_kv(keys, pair):
        flat = lambda ref: ref[2 * pair:2 * pair + 2, :, keys].reshape(lanes, SB_TK).astype(BF16)
        return flat(kp_ref), flat(vp_ref)

    qs = []
    for p in range(pairs):
        q = q_ref[:, cols[p]]
        zero = jnp.zeros_like(q)
        qs += [jnp.where(lane_lo, q, zero), jnp.where(lane_lo, zero, q)]
    new = [(qs[2 * p + h], kd_ref[:, cols[p]], vd_ref[:, cols[p]], earlier, None)
           for p in range(pairs) for h in range(2)]
    outs_new, mass_new = _sb_chains(new, ud_ref[...])
    newest = slice((blocks - 1) * SB_TK, blocks * SB_TK)
    cached = []
    for p in range(pairs):
        k, v = load_kv(newest, p)
        cached += [(qs[2 * p + h], k, v, None, mass_new[2 * p + h]) for h in range(2)]
    outs_old, mass_old = _sb_chains(cached, up_ref[...], keys_on_lanes=True)
    for p in range(pairs):
        a, b = 2 * p, 2 * p + 1
        acc = jnp.concatenate([outs_new[a] + outs_old[a], outs_new[b] + outs_old[b]], axis=0)
        mass = jnp.concatenate([mass_new[a] + mass_old[a], mass_new[b] + mass_old[b]], axis=0)
        q2 = jnp.concatenate([qs[a], qs[b]], axis=0)
        acc = _sb_walk(q2, functools.partial(load_kv, pair=p), up_ref[...], blocks - 1, acc, mass,
                       keys_on_lanes=True)
        o_ref[:, cols[p]] = jnp.where(lane_lo, acc[:tq], acc[tq:])


def _strict_lower_ones(n):
    r = lax.broadcasted_iota(jnp.int32, (n, n), 0)
    c = lax.broadcasted_iota(jnp.int32, (n, n), 1)
    return (r > c).astype(BF16)


def stick_breaking_fresh(q, k, v, bn, length):
    lanes = 2 * SB_HEAD_DIM
    tq = SB_SUBS * SB_TK
    assert length % tq == 0
    blk = pl.BlockSpec((None, tq, lanes), lambda b, h, i: (b, i, h))
    seq = pl.BlockSpec((None, length, lanes), lambda b, h, i: (b, 0, h))
    return pl.pallas_call(
        _sb_prompt_kernel,
        grid=(bn, SB_HEADS // 2, length // tq),
        in_specs=[blk, blk, blk, seq, seq, _full((SB_TK, SB_TK))],
        out_specs=blk,
        out_shape=jax.ShapeDtypeStruct((bn, length, SB_WIDTH), F32),
        compiler_params=_cparams(("parallel", "parallel", "arbitrary")),
        name="stick_breaking",
    )(q, k, v, k, v, _strict_lower_ones(SB_TK))


def stick_breaking_cached(q, k_new, v_new, k_past, v_past, bn, lq):
    lp = k_past.shape[3]
    assert lp % SB_TK == 0 and lp >= SB_TK
    blk = pl.BlockSpec((None, lq, SB_WIDTH), lambda b: (b, 0, 0))
    past = pl.BlockSpec((None, SB_HEADS, SB_HEAD_DIM, lp), lambda b: (b, 0, 0, 0))
    return pl.pallas_call(
        functools.partial(_sb_cached_kernel, tq=lq, blocks=lp // SB_TK),
        grid=(bn,),
        in_specs=[blk, blk, blk, past, past, _full((lq, lq)), _full((SB_TK, SB_TK))],
        out_specs=blk,
        out_shape=jax.ShapeDtypeStruct((bn, lq, SB_WIDTH), F32),
        compiler_params=_cparams(("parallel",)),
        name="stick_breaking_cached",
    )(q, k_new, v_new, k_past, v_past, _strict_lower_ones(lq), _strict_lower_ones(SB_TK))


def _gelu(x):
    return 0.5 * x * (1.0 + jnp.tanh(math.sqrt(2.0 / math.pi) * (x + 0.044715 * (x * x * x))))


def _mix0_out_kernel(x_ref, ys_ref, u_ref, sb_ref, d_ref, wglu_ref, bglu_ref, wo1_ref, wo2_ref, g_ref, b_ref,
                     wg_ref, wu_ref, wd_ref, g2_ref, b2_ref, o_ref, y_scr):
    x = x_ref[...]
    chunks = y_scr.shape[1] // S5_CHUNK
    per_slab = S5_ROW // LANES
    for s in range(S5_CHUNK):
        for half in range(2):
            for c in range(per_slab):
                y_scr[half * per_slab + c, pl.ds(s, chunks, stride=S5_CHUNK), :] = (
                    ys_ref[2 * s + half, :, c * LANES:(c + 1) * LANES])
    ys = jnp.concatenate([y_scr[c] for c in range(S5_WIDTH // LANES)], axis=1)
    gl = _gelu(ys + d_ref[...] * u_ref[...])
    gate = _sigmoid(_dot(gl.astype(BF16), wglu_ref[...]) + bglu_ref[...])
    m = _dot((gl * gate).astype(BF16), wo1_ref[...]) + _dot(sb_ref[...].astype(BF16), wo2_ref[...])
    x = _layer_norm(ALPHA * x + m, g_ref[...], b_ref[...])
    o_ref[...] = _ffn_sublayer(x, wg_ref, wu_ref, wd_ref, g2_ref, b2_ref)


def mix0_out_ffn(x, ys, u, sb, d, wglu, bglu, wo1, wo2, g, b, ffn):
    t = x.shape[0]
    tm = _row_tile(t)
    rowblk = lambda n: pl.BlockSpec((tm, n), lambda i: (i, 0))
    ffn_operands, ffn_specs = _ffn_operands(ffn)
    return pl.pallas_call(
        _mix0_out_kernel,
        grid=(t // tm,),
        in_specs=[rowblk(D_MODEL),
                  pl.BlockSpec((S5_SLABS, tm // S5_CHUNK, S5_ROW), lambda i: (0, i, 0)),
                  rowblk(S5_WIDTH), rowblk(SB_WIDTH),
                  _full((1, S5_WIDTH)), _full((S5_WIDTH, S5_WIDTH)), _full((1, S5_WIDTH)),
                  _full((S5_WIDTH, D_MODEL)), _full((SB_WIDTH, D_MODEL)),
                  _full((1, D_MODEL)), _full((1, D_MODEL))] + ffn_specs,
        out_specs=rowblk(D_MODEL),
        out_shape=jax.ShapeDtypeStruct((t, D_MODEL), F32),
        scratch_shapes=[pltpu.VMEM((S5_WIDTH // LANES, tm, LANES), F32)],
        compiler_params=_cparams(("parallel",)),
        name="mix0_out_ffn",
    )(x, ys, u, sb, d.reshape(1, -1), wglu, bglu.reshape(1, -1), wo1, wo2, g.reshape(1, -1), b.reshape(1, -1),
      *ffn_operands)


def _ssd_kernel(xbc_ref, z_ref, dt_ref, dtt_ref, conv0_ref, h0_ref, cw_ref, cb_ref, dtb_ref, dtbt_ref,
                alog_ref, alogt_ref, dsk_ref, ng_ref, tril_ref, triu_ref,
                y_ref, hl_ref, ext_ref, h_ref, *, lc, nc):
    c = pl.program_id(1)

    @pl.when(c == 0)
    def _():
        ext_ref[0:CONV_PAD, :] = conv0_ref[...]
        h_ref[...] = h0_ref[...]

    ext_ref[CONV_PAD:CONV_PAD + lc, :] = xbc_ref[...]
    ext = ext_ref[...]
    conv = cb_ref[...] + ext[CONV_PAD:, :] * cw_ref[SSD_CONV - 1:SSD_CONV, :]
    for back in range(1, SSD_CONV):
        tap = SSD_CONV - 1 - back
        conv = conv + pltpu.roll(ext, back, 0)[CONV_PAD:, :] * cw_ref[tap:tap + 1, :]
    ext_ref[0:CONV_PAD, :] = ext[lc:lc + CONV_PAD, :]
    act = conv * _sigmoid(conv)

    dt = _softplus(dt_ref[...] + dtb_ref[...])
    dtt = _softplus(dtt_ref[...] + dtbt_ref[...])
    cs = _sum_left(tril_ref[...], dt * (-LOG2E * jnp.exp(alog_ref[...])))
    cst = _sum_right(dtt * (-LOG2E * jnp.exp(alogt_ref[...])), triu_ref[...])
    cs_last = cs[lc - 1:lc, :]

    row = lax.broadcasted_iota(jnp.int32, (lc, lc), 0)
    col = lax.broadcasted_iota(jnp.int32, (lc, lc), 1)
    causal = col <= row
    lanes = 2 * SSD_HEAD_DIM
    lane_lo = lax.broadcasted_iota(jnp.int32, (lc, lanes), 1) < SSD_HEAD_DIM
    pairs_per_group = SSD_PAIRS // SSD_GROUPS

    for g in range(SSD_GROUPS):
        bm = act[:, SSD_INNER + g * SSD_STATE:SSD_INNER + (g + 1) * SSD_STATE].astype(BF16)
        cm = act[:, SSD_INNER + SSD_GN + g * SSD_STATE:SSD_INNER + SSD_GN + (g + 1) * SSD_STATE].astype(BF16)
        cb = _dot_nt(cm, bm)
        gated, sq = [], jnp.zeros((lc, lanes), F32)
        for k in range(g * pairs_per_group, (g + 1) * pairs_per_group):
            sl = slice(k * lanes, (k + 1) * lanes)
            xp = act[:, sl]
            heads = (2 * k, 2 * k + 1)
            pair = lambda v: jnp.where(lane_lo[:v[0].shape[0]], v[0], v[1])
            xdt = xp * pair([dt[:, h:h + 1] for h in heads])
            xdtb = xdt.astype(BF16)
            csb = [jnp.broadcast_to(cs[:, h:h + 1], (lc, lanes)) for h in heads]
            ys = []
            for h, csh in zip(heads, csb):
                seg = csh[:, :lc] - cst[h:h + 1, :]
                w = cb * jnp.exp2(jnp.where(causal, seg, -jnp.inf))
                ys.append(_dot(w.astype(BF16), xdtb))
            y = pair(ys)
            cs_pair = pair(csb)
            last_pair = pair([cs_last[:, h:h + 1] for h in heads])
            ht = h_ref[k]
            y = y + _dot(cm, ht.astype(BF16)) * jnp.exp2(cs_pair)
            xw = (xdt * jnp.exp2(last_pair - cs_pair)).astype(BF16)
            h_ref[k] = ht * jnp.exp2(last_pair) + _dot_tn(bm, xw)
            y = y + dsk_ref[:, sl] * xp
            zz = z_ref[:, sl]
            yg = y * (zz * _sigmoid(zz))
            sq = sq + yg * yg
            gated.append(yg)
        mean_sq = jnp.sum(sq, axis=-1, keepdims=True) * (1.0 / (pairs_per_group * lanes))
        scale = lax.rsqrt(mean_sq + RMS_EPS)
        for k, yg in zip(range(g * pairs_per_group, (g + 1) * pairs_per_group), gated):
            sl = slice(k * lanes, (k + 1) * lanes)
            y_ref[:, sl] = yg * scale * ng_ref[:, sl]

    @pl.when(c == nc - 1)
    def _():
        hl_ref[...] = h_ref[...]


def ssd_mix(xbc, z, dt_raw, conv0, h0, conv_w, conv_b, dt_bias, a_log, d_skip, norm_g, bn, length, lc):
    nc = length // lc
    t = bn * length
    lanes = 2 * SSD_HEAD_DIM
    dtt = jnp.swapaxes(dt_raw.reshape(bn * nc, lc, SSD_HEADS), 1, 2)
    conv0p = jnp.pad(conv0, ((0, 0), (CONV_PAD - (SSD_CONV - 1), 0), (0, 0)))
    a_log = a_log.astype(F32)
    r = lax.broadcasted_iota(jnp.int32, (lc, lc), 0)
    cc = lax.broadcasted_iota(jnp.int32, (lc, lc), 1)
    tril = (cc <= r).astype(BF16)
    triu = (r <= cc).astype(BF16)
    tok = lambda n: pl.BlockSpec((lc, n), lambda b, c: (b * nc + c, 0))
    y, hl = pl.pallas_call(
        functools.partial(_ssd_kernel, lc=lc, nc=nc),
        grid=(bn, nc),
        in_specs=[tok(SSD_CONV_DIM), tok(SSD_INNER), tok(SSD_HEADS),
                  pl.BlockSpec((None, SSD_HEADS, lc), lambda b, c: (b * nc + c, 0, 0)),
                  pl.BlockSpec((None, CONV_PAD, SSD_CONV_DIM), lambda b, c: (b, 0, 0)),
                  pl.BlockSpec((None, SSD_PAIRS, lanes, SSD_STATE), lambda b, c: (b, 0, 0, 0)),
                  _full((SSD_CONV, SSD_CONV_DIM)), _full((1, SSD_CONV_DIM)),
                  _full((1, SSD_HEADS)), _full((SSD_HEADS, 1)), _full((1, SSD_HEADS)), _full((SSD_HEADS, 1)),
                  _full((1, SSD_INNER)), _full((1, SSD_INNER)), _full((lc, lc)), _full((lc, lc))],
        out_specs=[tok(SSD_INNER),
                   pl.BlockSpec((None, SSD_PAIRS, lanes, SSD_STATE), lambda b, c: (b, 0, 0, 0))],
        out_shape=[jax.ShapeDtypeStruct((t, SSD_INNER), F32),
                   jax.ShapeDtypeStruct((bn, SSD_PAIRS, lanes, SSD_STATE), F32)],
        scratch_shapes=[pltpu.VMEM((CONV_PAD + lc, SSD_CONV_DIM), F32),
                        pltpu.VMEM((SSD_PAIRS, lanes, SSD_STATE), F32)],
        compiler_params=_cparams(("parallel", "arbitrary")),
        name="ssd_mix",
    )(xbc, z, dt_raw, dtt, conv0p, jnp.swapaxes(h0.reshape(bn, SSD_PAIRS, lanes, SSD_STATE), 2, 3),
      conv_w, conv_b.reshape(1, -1), dt_bias.reshape(1, -1), dt_bias.reshape(-1, 1),
      a_log.reshape(1, -1), a_log.reshape(-1, 1), jnp.repeat(d_skip, SSD_HEAD_DIM).reshape(1, -1),
      norm_g.reshape(1, -1), tril, triu)
    return y, jnp.swapaxes(hl, 2, 3).reshape(bn, SSD_HEADS, SSD_HEAD_DIM, SSD_STATE)


def _out_ffn_kernel(x_ref, a_ref, w_ref, g_ref, b_ref, wg_ref, wu_ref, wd_ref, g2_ref, b2_ref, o_ref):
    m = _dot(a_ref[...].astype(BF16), w_ref[...])
    x = _layer_norm(ALPHA * x_ref[...] + m, g_ref[...], b_ref[...])
    o_ref[...] = _ffn_sublayer(x, wg_ref, wu_ref, wd_ref, g2_ref, b2_ref)


def out_ffn(x, a, w, g, b, ffn):
    t = x.shape[0]
    tm = _row_tile(t)
    k = a.shape[1]
    ffn_operands, ffn_specs = _ffn_operands(ffn)
    return pl.pallas_call(
        _out_ffn_kernel,
        grid=(t // tm,),
        in_specs=[pl.BlockSpec((tm, D_MODEL), lambda i: (i, 0)), pl.BlockSpec((tm, k), lambda i: (i, 0)),
                  _full((k, D_MODEL)), _full((1, D_MODEL)), _full((1, D_MODEL))] + ffn_specs,
        out_specs=pl.BlockSpec((tm, D_MODEL), lambda i: (i, 0)),
        out_shape=jax.ShapeDtypeStruct((t, D_MODEL), F32),
        compiler_params=_cparams(("parallel",)),
        name="out_ffn",
    )(x, a, w, g.reshape(1, -1), b.reshape(1, -1), *ffn_operands)


def _trunk(x, bn, length, states, w):
    t = bn * length
    x = x.reshape(t, D_MODEL)
    ffn = lambda l, s: (w["wg"][l][s], w["wu"][l][s], w["wd"][l][s], w["ln_g"][l, 2 * s], w["ln_b"][l, 2 * s])
    x = ffn_ln(x, ffn(0, 0))

    u, u_slabs, q_bf, k, k_bf, v, v_bf = mix0_in(x, w["mix0_in"], SB_HEAD_DIM ** -0.5 * LOG2E)
    nj = length // S5_CHUNK
    nstate = S5_GROUPS * S5_STATE
    if states is None:
        h0re = h0im = jnp.zeros((bn, nstate), F32)
    else:
        h0re = states[0][0].astype(F32).reshape(bn, nstate)
        h0im = states[1][0].astype(F32).reshape(bn, nstate)
    y_s5, h_re, h_im = s5_mix(u_slabs, h0re, h0im, w["s5_tiles"], bn, nj)
    s5_re = h_re.reshape(1, bn, S5_GROUPS, S5_STATE)
    s5_im = h_im.reshape(1, bn, S5_GROUPS, S5_STATE)

    seq = lambda a: a.reshape(bn, length, SB_WIDTH)
    if states is None:
        sb = stick_breaking_fresh(seq(q_bf), seq(k_bf), seq(v_bf), bn, length)
    else:
        cache = lambda a: jnp.transpose(a[0], (0, 2, 3, 1))
        sb = stick_breaking_cached(seq(q_bf), seq(k_bf), seq(v_bf), cache(states[2]), cache(states[3]), bn, length)
    x = mix0_out_ffn(x, y_s5, u, sb.reshape(t, SB_WIDTH), w["s5_d"], w["s5_wglu"], w["s5_bglu"],
                     w["mix0_out"][:S5_WIDTH], w["mix0_out"][S5_WIDTH:], w["ln_g"][0, 1], w["ln_b"][0, 1],
                     ffn(0, 1))

    x = ffn_ln(x, ffn(1, 0))
    one = ((F32, 1.0),)
    z, xbc, dt_raw = proj(x, w["ssd_in"], (
        (0, SSD_INNER, one), (SSD_INNER, SSD_INNER + SSD_CONV_DIM, one),
        (SSD_INNER + SSD_CONV_DIM, SSD_INNER + SSD_CONV_DIM + SSD_HEADS, one)))
    if states is None:
        conv0 = jnp.zeros((bn, SSD_CONV - 1, SSD_CONV_DIM), F32)
        hs0 = jnp.zeros((bn, SSD_HEADS, SSD_HEAD_DIM, SSD_STATE), F32)
    else:
        conv0 = states[5][0].astype(F32)
        hs0 = states[4][0].astype(F32)
    lc = min(128, length)
    yg, h_ssd = ssd_mix(xbc, z, dt_raw, conv0, hs0, w["ssd_conv_w"], w["ssd_conv_b"], w["ssd_dt_bias"],
                        w["ssd_a_log"], w["ssd_d"], w["ssd_norm_g"], bn, length, lc)
    ext = jnp.concatenate([conv0, xbc.reshape(bn, length, SSD_CONV_DIM)[:, length - (SSD_CONV - 1):]], axis=1)
    new_conv = ext[:, ext.shape[1] - (SSD_CONV - 1):]
    x = out_ffn(x, yg, w["ssd_out"], w["ln_g"][1, 1], w["ln_b"][1, 1], ffn(1, 1))

    return (x.reshape(bn, length, D_MODEL), s5_re, s5_im,
            k.reshape(1, bn, length, SB_HEADS, SB_HEAD_DIM), v.reshape(1, bn, length, SB_HEADS, SB_HEAD_DIM),
            h_ssd[None], new_conv[None])


def kernel(x_prompt, x_sample, state_s5_re, state_s5_im, cache_sb_k, cache_sb_v, state_ssd, state_conv, ln_g, ln_b, ffn_w_gate, ffn_w_up, ffn_w_down, mix0_w_in, s5_a_re, s5_a_im, s5_log_dt, s5_b_re, s5_b_im, s5_c_re, s5_c_im, s5_d, s5_w_glu, s5_b_glu, mix0_w_out, ssd_w_in, ssd_conv_w, ssd_conv_b, ssd_dt_bias, ssd_a_log, ssd_d, ssd_norm_g, ssd_w_out):
    bf = lambda a: a.astype(BF16)
    w = {
        "wg": [[bf(ffn_w_gate[l, s]) for s in range(2)] for l in range(DEPTH)],
        "wu": [[bf(ffn_w_up[l, s]) for s in range(2)] for l in range(DEPTH)],
        "wd": [[bf(ffn_w_down[l, s]) for s in range(2)] for l in range(DEPTH)],
        "ln_g": ln_g, "ln_b": ln_b,
        "mix0_in": bf(mix0_w_in[0]),
        "s5_tiles": s5_tiles(s5_a_re[0], s5_a_im[0], s5_log_dt[0], s5_b_re[0], s5_b_im[0], s5_c_re[0], s5_c_im[0]),
        "s5_d": s5_d[0], "s5_wglu": bf(s5_w_glu[0]), "s5_bglu": s5_b_glu[0], "mix0_out": bf(mix0_w_out[0]),
        "ssd_in": bf(ssd_w_in[0]), "ssd_conv_w": ssd_conv_w[0], "ssd_conv_b": ssd_conv_b[0],
        "ssd_dt_bias": ssd_dt_bias[0], "ssd_a_log": ssd_a_log[0], "ssd_d": ssd_d[0],
        "ssd_norm_g": ssd_norm_g[0], "ssd_out": bf(ssd_w_out[0]),
    }
    bp, lp = x_prompt.shape[0], x_prompt.shape[1]
    bs, ls = x_sample.shape[0], x_sample.shape[1]
    yp, s5rp, s5ip, kp, vp, ssdp, convp = _trunk(x_prompt, bp, lp, None, w)
    ys, s5rs, s5is, ks, vs, ssds, convs = _trunk(
        x_sample, bs, ls, (state_s5_re, state_s5_im, cache_sb_k, cache_sb_v, state_ssd, state_conv), w)
    return (yp, ys, s5rp, s5ip, kp, vp, ssdp, convp, s5rs, s5is, ks, vs, ssds, convs)
```

```python
import functools
import math

import jax
import jax.numpy as jnp
from jax import lax
from jax.experimental import pallas as pl
from jax.experimental.pallas import tpu as pltpu

F32 = jnp.float32
BF16 = jnp.bfloat16

D_MODEL = 1024
DEPTH = 2
ALPHA = (2.0 * DEPTH) ** 0.25
LN_EPS = 1e-5
RMS_EPS = 1e-5
D_FF = 2816

S5_WIDTH = 512
S5_GROUP = 16
S5_GROUPS = 32
S5_STATE = 64
S5_CHUNK = 16
S5_ROW = S5_CHUNK * S5_GROUP
SB_HEADS = 8
SB_HEAD_DIM = 64
SB_WIDTH = 512
MIX0_IN = 2048

SSD_INNER = 2048
SSD_HEAD_DIM = 64
SSD_HEADS = 32
SSD_GROUPS = 4
SSD_STATE = 128
SSD_CONV = 4
SSD_GN = 512
SSD_CONV_DIM = 3072
SSD_PAIRS = SSD_HEADS // 2
CONV_PAD = 8

LANES = 128
V7X_VMEM_BYTES = 64 * 1024 * 1024
VMEM_LIMIT = V7X_VMEM_BYTES * 7 // 8


def _cparams(sem):
    return pltpu.CompilerParams(dimension_semantics=sem, vmem_limit_bytes=VMEM_LIMIT)


def _sigmoid(x):
    return 1.0 / (1.0 + jnp.exp2(x * -math.log2(math.e)))


def _softplus(x):
    return jnp.maximum(x, 0.0) + jnp.log1p(jnp.exp(-jnp.abs(x)))


def _layer_norm(y, g, b):
    mu = jnp.mean(y, axis=-1, keepdims=True)
    d = y - mu
    var = jnp.mean(d * d, axis=-1, keepdims=True)
    return d * lax.rsqrt(var + LN_EPS) * g + b


def _dot(a, b):
    return jnp.dot(a, b, preferred_element_type=F32)


def _dot_nt(a, b):
    return lax.dot_general(a, b, (((1,), (1,)), ((), ())), preferred_element_type=F32)


def _dot_tn(a, b):
    return lax.dot_general(a, b, (((0,), (0,)), ((), ())), preferred_element_type=F32)


def _split3(x):
    hi = x.astype(BF16)
    r = x - hi.astype(F32)
    mid = r.astype(BF16)
    lo = (r - mid.astype(F32)).astype(BF16)
    return hi, mid, lo


def _sum_right(x, ones_mat):
    hi, mid, lo = _split3(x)
    return _dot(hi, ones_mat) + _dot(mid, ones_mat) + _dot(lo, ones_mat)


def _sum_left(ones_mat, x):
    hi, mid, lo = _split3(x)
    return _dot(ones_mat, hi) + _dot(ones_mat, mid) + _dot(ones_mat, lo)


def _row_tile(t):
    for tm in (512, 256, 128, 64, 32, 16, 8):
        if t % tm == 0:
            return tm
    raise ValueError(f"token count {t} is not a multiple of 8")


def _full(shape):
    return pl.BlockSpec(shape, lambda *_: (0,) * len(shape), pipeline_mode=pl.Buffered(1))


MXU_TILE = 256
FFN_SPLIT = (D_FF // MXU_TILE // 2) * MXU_TILE


def _ffn_sublayer(x, wg_ref, wu_ref, wd_ref, g_ref, b_ref):
    xb = x.astype(BF16)
    acc = None
    for sl in (slice(0, FFN_SPLIT), slice(FFN_SPLIT, D_FF)):
        gate = _dot(xb, wg_ref[:, sl])
        up = _dot(xb, wu_ref[:, sl])
        h = (gate * _sigmoid(gate) * up).astype(BF16)
        part = _dot(h, wd_ref[sl, :])
        acc = part if acc is None else acc + part
    return _layer_norm(ALPHA * x + 0.5 * acc, g_ref[...], b_ref[...])


def _ffn_kernel(x_ref, wg_ref, wu_ref, wd_ref, g_ref, b_ref, o_ref):
    o_ref[...] = _ffn_sublayer(x_ref[...], wg_ref, wu_ref, wd_ref, g_ref, b_ref)


def _ffn_operands(ffn):
    wg, wu, wd, g, b = ffn
    specs = [_full((D_MODEL, D_FF)), _full((D_MODEL, D_FF)), _full((D_FF, D_MODEL)),
             _full((1, D_MODEL)), _full((1, D_MODEL))]
    return [wg, wu, wd, g.reshape(1, D_MODEL), b.reshape(1, D_MODEL)], specs


def ffn_ln(x, ffn):
    t = x.shape[0]
    tm = _row_tile(t)
    operands, specs = _ffn_operands(ffn)
    return pl.pallas_call(
        _ffn_kernel,
        grid=(t // tm,),
        in_specs=[pl.BlockSpec((tm, D_MODEL), lambda i: (i, 0))] + specs,
        out_specs=pl.BlockSpec((tm, D_MODEL), lambda i: (i, 0)),
        out_shape=jax.ShapeDtypeStruct((t, D_MODEL), F32),
        compiler_params=_cparams(("parallel",)),
        name="ffn_ln",
    )(x, *operands)


def _proj_kernel(x_ref, w_ref, *o_refs, plan):
    xb = x_ref[...].astype(BF16)
    refs = iter(o_refs)
    for lo, hi, outs in plan:
        y = _dot(xb, w_ref[:, lo:hi])
        for _, scale in outs:
            o_ref = next(refs)
            o_ref[...] = (y if scale == 1.0 else y * scale).astype(o_ref.dtype)


def proj(x, w, plan):
    t = x.shape[0]
    tm = _row_tile(t)
    n = w.shape[1]
    flat = [(hi - lo, dt) for lo, hi, outs in plan for dt, _ in outs]
    return pl.pallas_call(
        functools.partial(_proj_kernel, plan=plan),
        grid=(t // tm,),
        in_specs=[pl.BlockSpec((tm, D_MODEL), lambda i: (i, 0)), _full((D_MODEL, n))],
        out_specs=[pl.BlockSpec((tm, width), lambda i: (i, 0)) for width, _ in flat],
        out_shape=[jax.ShapeDtypeStruct((t, width), dt) for width, dt in flat],
        compiler_params=_cparams(("parallel",)),
        name="proj",
    )(x, w)


def _mix0_in_kernel(x_ref, w_ref, u_ref, us_ref, q_ref, k_ref, kb_ref, v_ref, vb_ref, u_scr, *, q_scale):
    xb = x_ref[...].astype(BF16)
    sw = S5_WIDTH
    u = _dot(xb, w_ref[:, 0:sw])
    u_ref[...] = u
    for c in range(sw // LANES):
        u_scr[c] = u[:, c * LANES:(c + 1) * LANES]
    chunks = u.shape[0] // S5_CHUNK
    per_slab = S5_ROW // LANES
    for s in range(S5_CHUNK):
        for half in range(2):
            for c in range(per_slab):
                rows = u_scr[half * per_slab + c, pl.ds(s, chunks, stride=S5_CHUNK), :]
                us_ref[2 * s + half, :, c * LANES:(c + 1) * LANES] = rows.astype(BF16)
    q_ref[...] = (_dot(xb, w_ref[:, sw:2 * sw]) * q_scale).astype(BF16)
    k = _dot(xb, w_ref[:, 2 * sw:3 * sw])
    k_ref[...] = k.reshape(k.shape[0], SB_HEADS, SB_HEAD_DIM)
    kb_ref[...] = k.astype(BF16)
    v = _dot(xb, w_ref[:, 3 * sw:4 * sw])
    v_ref[...] = v.reshape(v.shape[0], SB_HEADS, SB_HEAD_DIM)
    vb_ref[...] = v.astype(BF16)


def mix0_in(x, w, q_scale):
    t = x.shape[0]
    tm = _row_tile(t)
    sw = S5_WIDTH
    tok = pl.BlockSpec((tm, sw), lambda i: (i, 0))
    heads = pl.BlockSpec((tm, SB_HEADS, SB_HEAD_DIM), lambda i: (i, 0, 0))
    return pl.pallas_call(
        functools.partial(_mix0_in_kernel, q_scale=q_scale),
        grid=(t // tm,),
        in_specs=[pl.BlockSpec((tm, D_MODEL), lambda i: (i, 0)), _full((D_MODEL, MIX0_IN))],
        out_specs=[tok, pl.BlockSpec((S5_SLABS, tm // S5_CHUNK, S5_ROW), lambda i: (0, i, 0)),
                   tok, heads, tok, heads, tok],
        out_shape=[jax.ShapeDtypeStruct((t, sw), F32),
                   jax.ShapeDtypeStruct((S5_SLABS, t // S5_CHUNK, S5_ROW), BF16),
                   jax.ShapeDtypeStruct((t, sw), BF16), jax.ShapeDtypeStruct((t, SB_HEADS, SB_HEAD_DIM), F32),
                   jax.ShapeDtypeStruct((t, sw), BF16), jax.ShapeDtypeStruct((t, SB_HEADS, SB_HEAD_DIM), F32),
                   jax.ShapeDtypeStruct((t, sw), BF16)],
        scratch_shapes=[pltpu.VMEM((sw // LANES, tm, LANES), F32)],
        compiler_params=_cparams(("parallel",)),
        name="mix0_in",
    )(x, w)


S5_SLABS = 2 * S5_CHUNK
S5_GH = S5_GROUPS // 2
S5_HALF = S5_GH * S5_STATE
S5_CARRY_LANES = LANES


def _s5_tiles_kernel(a_re_ref, a_im_ref, ldt_ref, bt_re_ref, bt_im_ref, cq_re_ref, cq_im_ref,
                     p_ref, qt_ref, k_ref, a16r_ref, a16i_ref):
    s = pl.program_id(1).astype(F32)

    def cpow(m, ar, ai, dt):
        mag = jnp.exp(m * (ar * dt))
        ang = m * (ai * dt)
        return mag * jnp.cos(ang), mag * jnp.sin(ang)

    def zoh(ar, ai, dt):
        abr, abi = cpow(1.0, ar, ai, dt)
        nr, ni, den = abr - 1.0, abi, ar * ar + ai * ai
        return (nr * ar + ni * ai) / den, (ni * ar - nr * ai) / den

    ar, ai, dt = a_re_ref[...], a_im_ref[...], jnp.exp(ldt_ref[...])
    fr, fi = zoh(ar, ai, dt)
    pwr, pwi = cpow(float(S5_CHUNK - 1) - s, ar, ai, dt)
    cr, ci = pwr * fr - pwi * fi, pwr * fi + pwi * fr
    shape = (S5_ROW, S5_HALF)
    same = (jnp.right_shift(lax.broadcasted_iota(jnp.int32, shape, 0), S5_GROUP.bit_length() - 1)
            == jnp.right_shift(lax.broadcasted_iota(jnp.int32, shape, 1), S5_STATE.bit_length() - 1))
    btr, bti = bt_re_ref[...], bt_im_ref[...]
    p_re = jnp.where(same, cr * btr - ci * bti, 0.0).astype(BF16)
    p_im = jnp.where(same, cr * bti + ci * btr, 0.0).astype(BF16)
    p_ref[:, :S5_HALF] = p_re
    p_ref[:, S5_HALF:] = p_im
    qwr, qwi = cpow(s + 1.0, ar, ai, dt)
    cqr, cqi = cq_re_ref[...], cq_im_ref[...]
    qt_ref[:, :S5_HALF] = jnp.where(same, cqr * qwr - cqi * qwi, 0.0).astype(BF16)
    qt_ref[:, S5_HALF:] = jnp.where(same, -(cqr * qwi + cqi * qwr), 0.0).astype(BF16)
    a16r_ref[...], a16i_ref[...] = cpow(float(S5_CHUNK), ar, ai, dt)

    c0 = jnp.concatenate([jnp.where(same, cqr, 0.0), jnp.where(same, -cqi, 0.0)], axis=1).astype(BF16)
    k_ref[...] = _dot_nt(jnp.concatenate([p_re, p_im], axis=1), c0).astype(BF16)


def s5_tiles(a_re, a_im, log_dt, b_re, b_im, c_re, c_im):
    gh, n, pch = S5_GH, S5_STATE, S5_GROUP
    halves = lambda v: v.reshape((2, gh) + v.shape[1:])
    lane = lambda v: halves(v).reshape(2, 1, S5_HALF)
    ldt = jnp.broadcast_to(log_dt[:, None], (S5_GROUPS, n))
    bt = lambda b: halves(jnp.swapaxes(b, 1, 2)).reshape(2, S5_ROW, n)
    wide = lambda v: jnp.tile(v, (1, 1, gh))
    cq = lambda c: jnp.tile(jnp.transpose(halves(c), (0, 2, 1, 3)).reshape(2, pch, S5_HALF), (1, gh, 1))
    half = lambda *shape: pl.BlockSpec((None,) + shape, lambda h, s: (h,) + (0,) * len(shape))
    pt, qt, kt, a16r, a16i = pl.pallas_call(
        _s5_tiles_kernel,
        grid=(2, S5_CHUNK),
        in_specs=[half(1, S5_HALF)] * 3 + [half(S5_ROW, S5_HALF)] * 4,
        out_specs=[pl.BlockSpec((None, S5_ROW, 2 * S5_HALF), lambda h, s: (h, s, 0)),
                   pl.BlockSpec((None, None, S5_ROW, 2 * S5_HALF), lambda h, s: (h, s, 0, 0)),
                   pl.BlockSpec((None, S5_ROW, S5_ROW), lambda h, s: (h, s, 0)),
                   half(1, S5_HALF), half(1, S5_HALF)],
        out_shape=[jax.ShapeDtypeStruct((2, S5_CHUNK * S5_ROW, 2 * S5_HALF), BF16),
                   jax.ShapeDtypeStruct((2, S5_CHUNK, S5_ROW, 2 * S5_HALF), BF16),
                   jax.ShapeDtypeStruct((2, S5_CHUNK * S5_ROW, S5_ROW), BF16),
                   jax.ShapeDtypeStruct((2, 1, S5_HALF), F32), jax.ShapeDtypeStruct((2, 1, S5_HALF), F32)],
        compiler_params=_cparams(("arbitrary", "arbitrary")),
        name="s5_tiles",
    )(lane(a_re), lane(a_im), lane(ldt), wide(bt(b_re)), wide(bt(b_im)), cq(c_re), cq(c_im))
    nstate = S5_GROUPS * S5_STATE
    return kt, pt, qt, a16r.reshape(1, nstate), a16i.reshape(1, nstate)


def _s5_state_kernel(us_ref, p_ref, sre_ref, sim_ref):
    steps = jnp.concatenate([us_ref[s] for s in range(S5_CHUNK)], axis=1)
    part = _dot(steps, p_ref[...])
    for c in range(S5_HALF // LANES):
        sre_ref[c] = part[:, c * LANES:(c + 1) * LANES]
        sim_ref[c] = part[:, S5_HALF + c * LANES:S5_HALF + (c + 1) * LANES]


def _s5_carry_kernel(sre_ref, sim_ref, h0re_ref, h0im_ref, ar_ref, ai_ref,
                     hre_ref, him_ref, lre_ref, lim_ref, *, bn, nj):
    ar, ai = ar_ref[...], ai_ref[...]

    def body(j, carry):
        hr, hi = carry
        rows = pl.ds(j, bn, stride=nj)
        hre_ref[rows, :] = hr
        him_ref[rows, :] = hi
        return ar * hr - ai * hi + sre_ref[rows, :], ar * hi + ai * hr + sim_ref[rows, :]

    hr, hi = lax.fori_loop(0, nj, body, (h0re_ref[...], h0im_ref[...]))
    lre_ref[...] = hr
    lim_ref[...] = hi


def _s5_out_kernel(us_ref, k_ref, hre_ref, him_ref, q_ref, y_ref):
    blocks = range(S5_HALF // LANES)
    hin = jnp.concatenate([hre_ref[c] for c in blocks] + [him_ref[c] for c in blocks], axis=1).astype(BF16)
    for t in range(S5_CHUNK):
        steps = jnp.concatenate([us_ref[s] for s in range(t + 1)], axis=1)
        lags = k_ref[(S5_CHUNK - 1 - t) * S5_ROW:, :]
        y_ref[t] = _dot(steps, lags) + _dot_nt(hin, q_ref[t])


def s5_mix(us, h0re, h0im, tiles, bn, nj):
    kt, pt, qq, ar, ai = tiles
    r = bn * nj
    nstate = S5_GROUPS * S5_STATE
    us4 = us.reshape(S5_CHUNK, 2, r, S5_ROW)
    half_once = lambda *shape: pl.BlockSpec((None,) + shape, lambda h, i: (h,) + (0,) * len(shape),
                                            pipeline_mode=pl.Buffered(1))
    rows = min(r, 512)
    lanes = S5_CARRY_LANES
    sre, sim = pl.pallas_call(
        _s5_state_kernel,
        grid=(2, r // rows),
        in_specs=[pl.BlockSpec((S5_CHUNK, None, rows, S5_ROW), lambda h, i: (0, h, i, 0)),
                  half_once(S5_CHUNK * S5_ROW, 2 * S5_HALF)],
        out_specs=[pl.BlockSpec((S5_HALF // lanes, rows, lanes), lambda h, i: (h, i, 0))] * 2,
        out_shape=[jax.ShapeDtypeStruct((nstate // lanes, r, lanes), F32)] * 2,
        compiler_params=_cparams(("arbitrary", "arbitrary")),
        name="s5_state",
    )(us4, pt)
    col = lambda n: pl.BlockSpec((n, lanes), lambda i: (0, i))
    blk = pl.BlockSpec((None, r, lanes), lambda i: (i, 0, 0))
    hre, him, lre, lim = pl.pallas_call(
        functools.partial(_s5_carry_kernel, bn=bn, nj=nj),
        grid=(nstate // lanes,),
        in_specs=[blk, blk, col(bn), col(bn), col(1), col(1)],
        out_specs=[blk, blk, col(bn), col(bn)],
        out_shape=[jax.ShapeDtypeStruct((nstate // lanes, r, lanes), F32)] * 2
                  + [jax.ShapeDtypeStruct((bn, nstate), F32)] * 2,
        compiler_params=_cparams(("parallel",)),
        name="s5_carry",
    )(sre, sim, h0re, h0im, ar, ai)
    rows = min(r, 256)
    slabs = pl.BlockSpec((S5_CHUNK, None, rows, S5_ROW), lambda h, i: (0, h, i, 0))
    y = pl.pallas_call(
        _s5_out_kernel,
        grid=(2, r // rows),
        in_specs=[slabs, half_once(S5_CHUNK * S5_ROW, S5_ROW),
                  pl.BlockSpec((S5_HALF // lanes, rows, lanes), lambda h, i: (h, i, 0)),
                  pl.BlockSpec((S5_HALF // lanes, rows, lanes), lambda h, i: (h, i, 0)),
                  half_once(S5_CHUNK, S5_ROW, 2 * S5_HALF)],
        out_specs=slabs,
        out_shape=jax.ShapeDtypeStruct((S5_CHUNK, 2, r, S5_ROW), F32),
        compiler_params=_cparams(("arbitrary", "arbitrary")),
        name="s5_out",
    )(us4, kt, hre, him, qq)
    return y.reshape(S5_SLABS, r, S5_ROW), lre, lim


SB_TK = 256
LOG2E = math.log2(math.e)
SB_SKIP_LOG2 = 152.0
SB_SUBS = 4


def _sb_chains(tasks, ones, keys_on_lanes=False):
    zs = [_dot(q, k) if keys_on_lanes else _dot_nt(q, k) for q, k, _, _, _ in tasks]
    tails = [jnp.log2(1.0 + jnp.exp2(-jnp.abs(z))) for z in zs]
    sps = [jnp.maximum(z, 0.0) + t for z, t in zip(zs, tails)]
    sps = [sp if task[3] is None else jnp.where(task[3], sp, 0.0) for sp, task in zip(sps, tasks)]
    mass = [jnp.sum(sp, axis=-1, keepdims=True) for sp in sps]
    suffixes = [_dot(sp.astype(BF16), ones) for sp in sps]
    outs = []
    for i, (_, _, v, mask, carry) in enumerate(tasks):
        logw = jnp.minimum(zs[i], 0.0) - tails[i] - suffixes[i]
        if carry is not None:
            logw = logw - (mass[carry] if isinstance(carry, int) else carry)
        w = jnp.exp2(logw)
        if mask is not None:
            w = jnp.where(mask, w, 0.0)
        wb = w.astype(BF16)
        outs.append(_dot_nt(wb, v) if keys_on_lanes else _dot(wb, v))
    return outs, mass


def _sb_block(q2, k, v, ones, carry, mask, keys_on_lanes=False):
    half = q2.shape[0] // 2
    halves = (slice(0, half), slice(half, 2 * half))
    outs, mass = _sb_chains([(q2[h], k, v, None if mask is None else mask[h],
                              None if carry is None else carry[h]) for h in halves], ones, keys_on_lanes)
    return jnp.concatenate(outs, axis=0), jnp.concatenate(mass, axis=0)


def _sb_walk(q2, load_kv, ones, blocks, acc, carry, keys_on_lanes=False):
    def cond(s):
        return jnp.logical_and(s[0] < blocks, s[1] < SB_SKIP_LOG2)

    def body(s):
        i, _, acc, carry = s
        rows = pl.ds(pl.multiple_of((blocks - 1 - i) * SB_TK, SB_TK), SB_TK)
        k, v = load_kv(rows)
        d, m = _sb_block(q2, k, v, ones, carry, None, keys_on_lanes)
        carry = carry + m
        return i + 1, jnp.min(carry), acc + d, carry

    return lax.while_loop(cond, body, (jnp.int32(0), jnp.min(carry), acc, carry))[2]


def _sb_prompt_kernel(q_ref, kd_ref, vd_ref, kp_ref, vp_ref, ones_ref, o_ref):
    i = pl.program_id(2)
    t, n = SB_TK, SB_SUBS
    lanes = 2 * SB_HEAD_DIM
    q = q_ref[...]
    lane_lo = lax.broadcasted_iota(jnp.int32, (n * t, lanes), 1) < SB_HEAD_DIM
    zero = jnp.zeros_like(q)
    heads = (jnp.where(lane_lo, q, zero), jnp.where(lane_lo, zero, q))
    row = lax.broadcasted_iota(jnp.int32, (t, t), 0)
    col = lax.broadcasted_iota(jnp.int32, (t, t), 1)
    earlier = col < row
    ones = ones_ref[...]
    kd = [kd_ref[s * t:(s + 1) * t, :] for s in range(n)]
    vd = [vd_ref[s * t:(s + 1) * t, :] for s in range(n)]
    qs = [[h[s * t:(s + 1) * t, :] for h in heads] for s in range(n)]

    def front(with_past):
        tasks = [(qs[s][h], kd[s], vd[s], earlier, None) for s in range(n) for h in range(2)]
        past_of = {}
        for s in range(1, n):
            past_of[s] = len(tasks)
            tasks += [(qs[s][h], kd[s - 1], vd[s - 1], None, 2 * s + h) for h in range(2)]
        if with_past:
            rows = pl.ds(pl.multiple_of((n * i - 1) * t, t), t)
            past_of[0] = len(tasks)
            tasks += [(qs[0][h], kp_ref[rows, :], vp_ref[rows, :], None, h) for h in range(2)]
        outs, mass = _sb_chains(tasks, ones)
        result = []
        for s in range(n):
            parts = [(outs[2 * s + h], mass[2 * s + h]) for h in range(2)]
            if s in past_of:
                parts = [(o + outs[past_of[s] + h], m + mass[past_of[s] + h]) for h, (o, m) in enumerate(parts)]
            result += [jnp.concatenate([parts[0][0], parts[1][0]], axis=0),
                       jnp.concatenate([parts[0][1], parts[1][1]], axis=0)]
        lightest = [jnp.minimum(jnp.min(result[2 * s + 1][:t]), jnp.min(result[2 * s + 1][t:])) for s in range(n)]
        return tuple(result) + tuple(lightest)

    state = lax.cond(i > 0, lambda: front(True), lambda: front(False))
    accs, masses, lightest = state[0:2 * n:2], state[1:2 * n:2], state[2 * n:]
    older = [jnp.maximum(n * i + s - 1, 0) for s in range(n)]
    load_kv = lambda rows: (kp_ref[rows, :], vp_ref[rows, :])
    first = lax.broadcasted_iota(jnp.int32, (t, lanes), 1) < SB_HEAD_DIM

    def emit(accs):
        for s in range(n):
            o_ref[s * t:(s + 1) * t, :] = jnp.where(first, accs[s][:t], accs[s][t:])

    def walk_all():
        emit([_sb_walk(jnp.concatenate(qs[s], axis=0), load_kv, ones, older[s], accs[s], masses[s])
              for s in range(n)])

    unfinished = functools.reduce(jnp.logical_or, [jnp.logical_and(lightest[s] < SB_SKIP_LOG2, older[s] > 0)
                                                   for s in range(n)])
    lax.cond(unfinished, walk_all, lambda: emit(accs))


def _sb_cached_kernel(q_ref, kd_ref, vd_ref, kp_ref, vp_ref, ud_ref, up_ref, o_ref, *, tq, blocks):
    lanes = 2 * SB_HEAD_DIM
    pairs = SB_HEADS // 2
    lane_lo = lax.broadcasted_iota(jnp.int32, (tq, lanes), 1) < SB_HEAD_DIM
    row = lax.broadcasted_iota(jnp.int32, (tq, tq), 0)
    col = lax.broadcasted_iota(jnp.int32, (tq, tq), 1)
    earlier = col < row
    cols = [slice(p * lanes, (p + 1) * lanes) for p in range(pairs)]

    def load_kv(keys, pair):
        flat = lambda ref: ref[2 * pair:2 * pair + 2, :, keys].reshape(lanes, SB_TK).astype(BF16)
        return flat(kp_ref), flat(vp_ref)

    qs = []
    for p in range(pairs):
        q = q_ref[:, cols[p]]
        zero = jnp.zeros_like(q)
        qs += [jnp.where(lane_lo, q, zero), jnp.where(lane_lo, zero, q)]
    new = [(qs[2 * p + h], kd_ref[:, cols[p]], vd_ref[:, cols[p]], earlier, None)
           for p in range(pairs) for h in range(2)]
    outs_new, mass_new = _sb_chains(new, ud_ref[...])
    newest = slice((blocks - 1) * SB_TK, blocks * SB_TK)
    cached = []
    for p in range(pairs):
        k, v = load_kv(newest, p)
        cached += [(qs[2 * p + h], k, v, None, mass_new[2 * p + h]) for h in range(2)]
    outs_old, mass_old = _sb_chains(cached, up_ref[...], keys_on_lanes=True)
    for p in range(pairs):
        a, b = 2 * p, 2 * p + 1
        acc = jnp.concatenate([outs_new[a] + outs_old[a], outs_new[b] + outs_old[b]], axis=0)
        mass = jnp.concatenate([mass_new[a] + mass_old[a], mass_new[b] + mass_old[b]], axis=0)
        q2 = jnp.concatenate([qs[a], qs[b]], axis=0)
        acc = _sb_walk(q2, functools.partial(load_kv, pair=p), up_ref[...], blocks - 1, acc, mass,
                       keys_on_lanes=True)
        o_ref[:, cols[p]] = jnp.where(lane_lo, acc[:tq], acc[tq:])


def _strict_lower_ones(n):
    r = lax.broadcasted_iota(jnp.int32, (n, n), 0)
    c = lax.broadcasted_iota(jnp.int32, (n, n), 1)
    return (r > c).astype(BF16)


def stick_breaking_fresh(q, k, v, bn, length):
    lanes = 2 * SB_HEAD_DIM
    tq = SB_SUBS * SB_TK
    assert length % tq == 0
    blk = pl.BlockSpec((None, tq, lanes), lambda b, h, i: (b, i, h))
    seq = pl.BlockSpec((None, length, lanes), lambda b, h, i: (b, 0, h))
    return pl.pallas_call(
        _sb_prompt_kernel,
        grid=(bn, SB_HEADS // 2, length // tq),
        in_specs=[blk, blk, blk, seq, seq, _full((SB_TK, SB_TK))],
        out_specs=blk,
        out_shape=jax.ShapeDtypeStruct((bn, length, SB_WIDTH), F32),
        compiler_params=_cparams(("parallel", "parallel", "arbitrary")),
        name="stick_breaking",
    )(q, k, v, k, v, _strict_lower_ones(SB_TK))


def stick_breaking_cached(q, k_new, v_new, k_past, v_past, bn, lq):
    lp = k_past.shape[3]
    assert lp % SB_TK == 0 and lp >= SB_TK
    blk = pl.BlockSpec((None, lq, SB_WIDTH), lambda b: (b, 0, 0))
    past = pl.BlockSpec((None, SB_HEADS, SB_HEAD_DIM, lp), lambda b: (b, 0, 0, 0))
    return pl.pallas_call(
        functools.partial(_sb_cached_kernel, tq=lq, blocks=lp // SB_TK),
        grid=(bn,),
        in_specs=[blk, blk, blk, past, past, _full((lq, lq)), _full((SB_TK, SB_TK))],
        out_specs=blk,
        out_shape=jax.ShapeDtypeStruct((bn, lq, SB_WIDTH), F32),
        compiler_params=_cparams(("parallel",)),
        name="stick_breaking_cached",
    )(q, k_new, v_new, k_past, v_past, _strict_lower_ones(lq), _strict_lower_ones(SB_TK))


def _gelu(x):
    return 0.5 * x * (1.0 + jnp.tanh(math.sqrt(2.0 / math.pi) * (x + 0.044715 * (x * x * x))))


def _mix0_out_kernel(x_ref, ys_ref, u_ref, sb_ref, d_ref, wglu_ref, bglu_ref, wo1_ref, wo2_ref, g_ref, b_ref,
                     wg_ref, wu_ref, wd_ref, g2_ref, b2_ref, o_ref, y_scr):
    x = x_ref[...]
    chunks = y_scr.shape[1] // S5_CHUNK
    per_slab = S5_ROW // LANES
    for s in range(S5_CHUNK):
        for half in range(2):
            for c in range(per_slab):
                y_scr[half * per_slab + c, pl.ds(s, chunks, stride=S5_CHUNK), :] = (
                    ys_ref[2 * s + half, :, c * LANES:(c + 1) * LANES])
    ys = jnp.concatenate([y_scr[c] for c in range(S5_WIDTH // LANES)], axis=1)
    gl = _gelu(ys + d_ref[...] * u_ref[...])
    gate = _sigmoid(_dot(gl.astype(BF16), wglu_ref[...]) + bglu_ref[...])
    m = _dot((gl * gate).astype(BF16), wo1_ref[...]) + _dot(sb_ref[...].astype(BF16), wo2_ref[...])
    x = _layer_norm(ALPHA * x + m, g_ref[...], b_ref[...])
    o_ref[...] = _ffn_sublayer(x, wg_ref, wu_ref, wd_ref, g2_ref, b2_ref)


def mix0_out_ffn(x, ys, u, sb, d, wglu, bglu, wo1, wo2, g, b, ffn):
    t = x.shape[0]
    tm = _row_tile(t)
    rowblk = lambda n: pl.BlockSpec((tm, n), lambda i: (i, 0))
    ffn_operands, ffn_specs = _ffn_operands(ffn)
    return pl.pallas_call(
        _mix0_out_kernel,
        grid=(t // tm,),
        in_specs=[rowblk(D_MODEL),
                  pl.BlockSpec((S5_SLABS, tm // S5_CHUNK, S5_ROW), lambda i: (0, i, 0)),
                  rowblk(S5_WIDTH), rowblk(SB_WIDTH),
                  _full((1, S5_WIDTH)), _full((S5_WIDTH, S5_WIDTH)), _full((1, S5_WIDTH)),
                  _full((S5_WIDTH, D_MODEL)), _full((SB_WIDTH, D_MODEL)),
                  _full((1, D_MODEL)), _full((1, D_MODEL))] + ffn_specs,
        out_specs=rowblk(D_MODEL),
        out_shape=jax.ShapeDtypeStruct((t, D_MODEL), F32),
        scratch_shapes=[pltpu.VMEM((S5_WIDTH // LANES, tm, LANES), F32)],
        compiler_params=_cparams(("parallel",)),
        name="mix0_out_ffn",
    )(x, ys, u, sb, d.reshape(1, -1), wglu, bglu.reshape(1, -1), wo1, wo2, g.reshape(1, -1), b.reshape(1, -1),
      *ffn_operands)


def _ssd_kernel(xbc_ref, z_ref, dt_ref, dtt_ref, conv0_ref, h0_ref, cw_ref, cb_ref, dtb_ref, dtbt_ref,
                alog_ref, alogt_ref, dsk_ref, ng_ref, tril_ref, triu_ref,
                y_ref, hl_ref, ext_ref, h_ref, *, lc, nc):
    c = pl.program_id(1)

    @pl.when(c == 0)
    def _():
        ext_ref[0:CONV_PAD, :] = conv0_ref[...]
        h_ref[...] = h0_ref[...]

    ext_ref[CONV_PAD:CONV_PAD + lc, :] = xbc_ref[...]
    ext = ext_ref[...]
    conv = cb_ref[...] + ext[CONV_PAD:, :] * cw_ref[SSD_CONV - 1:SSD_CONV, :]
    for back in range(1, SSD_CONV):
        tap = SSD_CONV - 1 - back
        conv = conv + pltpu.roll(ext, back, 0)[CONV_PAD:, :] * cw_ref[tap:tap + 1, :]
    ext_ref[0:CONV_PAD, :] = ext[lc:lc + CONV_PAD, :]
    act = conv * _sigmoid(conv)

    dt = _softplus(dt_ref[...] + dtb_ref[...])
    dtt = _softplus(dtt_ref[...] + dtbt_ref[...])
    cs = _sum_left(tril_ref[...], dt * (-LOG2E * jnp.exp(alog_ref[...])))
    cst = _sum_right(dtt * (-LOG2E * jnp.exp(alogt_ref[...])), triu_ref[...])
    cs_last = cs[lc - 1:lc, :]

    row = lax.broadcasted_iota(jnp.int32, (lc, lc), 0)
    col = lax.broadcasted_iota(jnp.int32, (lc, lc), 1)
    causal = col <= row
    lanes = 2 * SSD_HEAD_DIM
    lane_lo = lax.broadcasted_iota(jnp.int32, (lc, lanes), 1) < SSD_HEAD_DIM
    pairs_per_group = SSD_PAIRS // SSD_GROUPS

    for g in range(SSD_GROUPS):
        bm = act[:, SSD_INNER + g * SSD_STATE:SSD_INNER + (g + 1) * SSD_STATE].astype(BF16)
        cm = act[:, SSD_INNER + SSD_GN + g * SSD_STATE:SSD_INNER + SSD_GN + (g + 1) * SSD_STATE].astype(BF16)
        cb = _dot_nt(cm, bm)
        gated, sq = [], jnp.zeros((lc, lanes), F32)
        for k in range(g * pairs_per_group, (g + 1) * pairs_per_group):
            sl = slice(k * lanes, (k + 1) * lanes)
            xp = act[:, sl]
            heads = (2 * k, 2 * k + 1)
            pair = lambda v: jnp.where(lane_lo[:v[0].shape[0]], v[0], v[1])
            xdt = xp * pair([dt[:, h:h + 1] for h in heads])
            xdtb = xdt.astype(BF16)
            csb = [jnp.broadcast_to(cs[:, h:h + 1], (lc, lanes)) for h in heads]
            ys = []
            for h, csh in zip(heads, csb):
                seg = csh[:, :lc] - cst[h:h + 1, :]
                w = cb * jnp.exp2(jnp.where(causal, seg, -jnp.inf))
                ys.append(_dot(w.astype(BF16), xdtb))
            y = pair(ys)
            cs_pair = pair(csb)
            last_pair = pair([cs_last[:, h:h + 1] for h in heads])
            ht = h_ref[k]
            y = y + _dot(cm, ht.astype(BF16)) * jnp.exp2(cs_pair)
            xw = (xdt * jnp.exp2(last_pair - cs_pair)).astype(BF16)
            h_ref[k] = ht * jnp.exp2(last_pair) + _dot_tn(bm, xw)
            y = y + dsk_ref[:, sl] * xp
            zz = z_ref[:, sl]
            yg = y * (zz * _sigmoid(zz))
            sq = sq + yg * yg
            gated.append(yg)
        mean_sq = jnp.sum(sq, axis=-1, keepdims=True) * (1.0 / (pairs_per_group * lanes))
        scale = lax.rsqrt(mean_sq + RMS_EPS)
        for k, yg in zip(range(g * pairs_per_group, (g + 1) * pairs_per_group), gated):
            sl = slice(k * lanes, (k + 1) * lanes)
            y_ref[:, sl] = yg * scale * ng_ref[:, sl]

    @pl.when(c == nc - 1)
    def _():
        hl_ref[...] = h_ref[...]


def ssd_mix(xbc, z, dt_raw, conv0, h0, conv_w, conv_b, dt_bias, a_log, d_skip, norm_g, bn, length, lc):
    nc = length // lc
    t = bn * length
    lanes = 2 * SSD_HEAD_DIM
    dtt = jnp.swapaxes(dt_raw.reshape(bn * nc, lc, SSD_HEADS), 1, 2)
    conv0p = jnp.pad(conv0, ((0, 0), (CONV_PAD - (SSD_CONV - 1), 0), (0, 0)))
    a_log = a_log.astype(F32)
    r = lax.broadcasted_iota(jnp.int32, (lc, lc), 0)
    cc = lax.broadcasted_iota(jnp.int32, (lc, lc), 1)
    tril = (cc <= r).astype(BF16)
    triu = (r <= cc).astype(BF16)
    tok = lambda n: pl.BlockSpec((lc, n), lambda b, c: (b * nc + c, 0))
    y, hl = pl.pallas_call(
        functools.partial(_ssd_kernel, lc=lc, nc=nc),
        grid=(bn, nc),
        in_specs=[tok(SSD_CONV_DIM), tok(SSD_INNER), tok(SSD_HEADS),
                  pl.BlockSpec((None, SSD_HEADS, lc), lambda b, c: (b * nc + c, 0, 0)),
                  pl.BlockSpec((None, CONV_PAD, SSD_CONV_DIM), lambda b, c: (b, 0, 0)),
                  pl.BlockSpec((None, SSD_PAIRS, lanes, SSD_STATE), lambda b, c: (b, 0, 0, 0)),
                  _full((SSD_CONV, SSD_CONV_DIM)), _full((1, SSD_CONV_DIM)),
                  _full((1, SSD_HEADS)), _full((SSD_HEADS, 1)), _full((1, SSD_HEADS)), _full((SSD_HEADS, 1)),
                  _full((1, SSD_INNER)), _full((1, SSD_INNER)), _full((lc, lc)), _full((lc, lc))],
        out_specs=[tok(SSD_INNER),
                   pl.BlockSpec((None, SSD_PAIRS, lanes, SSD_STATE), lambda b, c: (b, 0, 0, 0))],
        out_shape=[jax.ShapeDtypeStruct((t, SSD_INNER), F32),
                   jax.ShapeDtypeStruct((bn, SSD_PAIRS, lanes, SSD_STATE), F32)],
        scratch_shapes=[pltpu.VMEM((CONV_PAD + lc, SSD_CONV_DIM), F32),
                        pltpu.VMEM((SSD_PAIRS, lanes, SSD_STATE), F32)],
        compiler_params=_cparams(("parallel", "arbitrary")),
        name="ssd_mix",
    )(xbc, z, dt_raw, dtt, conv0p, jnp.swapaxes(h0.reshape(bn, SSD_PAIRS, lanes, SSD_STATE), 2, 3),
      conv_w, conv_b.reshape(1, -1), dt_bias.reshape(1, -1), dt_bias.reshape(-1, 1),
      a_log.reshape(1, -1), a_log.reshape(-1, 1), jnp.repeat(d_skip, SSD_HEAD_DIM).reshape(1, -1),
      norm_g.reshape(1, -1), tril, triu)
    return y, jnp.swapaxes(hl, 2, 3).reshape(bn, SSD_HEADS, SSD_HEAD_DIM, SSD_STATE)


def _out_ffn_kernel(x_ref, a_ref, w_ref, g_ref, b_ref, wg_ref, wu_ref, wd_ref, g2_ref, b2_ref, o_ref):
    m = _dot(a_ref[...].astype(BF16), w_ref[...])
    x = _layer_norm(ALPHA * x_ref[...] + m, g_ref[...], b_ref[...])
    o_ref[...] = _ffn_sublayer(x, wg_ref, wu_ref, wd_ref, g2_ref, b2_ref)


def out_ffn(x, a, w, g, b, ffn):
    t = x.shape[0]
    tm = _row_tile(t)
    k = a.shape[1]
    ffn_operands, ffn_specs = _ffn_operands(ffn)
    return pl.pallas_call(
        _out_ffn_kernel,
        grid=(t // tm,),
        in_specs=[pl.BlockSpec((tm, D_MODEL), lambda i: (i, 0)), pl.BlockSpec((tm, k), lambda i: (i, 0)),
                  _full((k, D_MODEL)), _full((1, D_MODEL)), _full((1, D_MODEL))] + ffn_specs,
        out_specs=pl.BlockSpec((tm, D_MODEL), lambda i: (i, 0)),
        out_shape=jax.ShapeDtypeStruct((t, D_MODEL), F32),
        compiler_params=_cparams(("parallel",)),
        name="out_ffn",
    )(x, a, w, g.reshape(1, -1), b.reshape(1, -1), *ffn_operands)


def _trunk(x, bn, length, states, w):
    t = bn * length
    x = x.reshape(t, D_MODEL)
    ffn = lambda l, s: (w["wg"][l][s], w["wu"][l][s], w["wd"][l][s], w["ln_g"][l, 2 * s], w["ln_b"][l, 2 * s])
    x = ffn_ln(x, ffn(0, 0))

    u, u_slabs, q_bf, k, k_bf, v, v_bf = mix0_in(x, w["mix0_in"], SB_HEAD_DIM ** -0.5 * LOG2E)
    nj = length // S5_CHUNK
    nstate = S5_GROUPS * S5_STATE
    if states is None:
        h0re = h0im = jnp.zeros((bn, nstate), F32)
    else:
        h0re = states[0][0].astype(F32).reshape(bn, nstate)
        h0im = states[1][0].astype(F32).reshape(bn, nstate)
    y_s5, h_re, h_im = s5_mix(u_slabs, h0re, h0im, w["s5_tiles"], bn, nj)
    s5_re = h_re.reshape(1, bn, S5_GROUPS, S5_STATE)
    s5_im = h_im.reshape(1, bn, S5_GROUPS, S5_STATE)

    seq = lambda a: a.reshape(bn, length, SB_WIDTH)
    if states is None:
        sb = stick_breaking_fresh(seq(q_bf), seq(k_bf), seq(v_bf), bn, length)
    else:
        cache = lambda a: jnp.transpose(a[0], (0, 2, 3, 1))
        sb = stick_breaking_cached(seq(q_bf), seq(k_bf), seq(v_bf), cache(states[2]), cache(states[3]), bn, length)
    x = mix0_out_ffn(x, y_s5, u, sb.reshape(t, SB_WIDTH), w["s5_d"], w["s5_wglu"], w["s5_bglu"],
                     w["mix0_out"][:S5_WIDTH], w["mix0_out"][S5_WIDTH:], w["ln_g"][0, 1], w["ln_b"][0, 1],
                     ffn(0, 1))

    x = ffn_ln(x, ffn(1, 0))
    one = ((F32, 1.0),)
    z, xbc, dt_raw = proj(x, w["ssd_in"], (
        (0, SSD_INNER, one), (SSD_INNER, SSD_INNER + SSD_CONV_DIM, one),
        (SSD_INNER + SSD_CONV_DIM, SSD_INNER + SSD_CONV_DIM + SSD_HEADS, one)))
    if states is None:
        conv0 = jnp.zeros((bn, SSD_CONV - 1, SSD_CONV_DIM), F32)
        hs0 = jnp.zeros((bn, SSD_HEADS, SSD_HEAD_DIM, SSD_STATE), F32)
    else:
        conv0 = states[5][0].astype(F32)
        hs0 = states[4][0].astype(F32)
    lc = min(128, length)
    yg, h_ssd = ssd_mix(xbc, z, dt_raw, conv0, hs0, w["ssd_conv_w"], w["ssd_conv_b"], w["ssd_dt_bias"],
                        w["ssd_a_log"], w["ssd_d"], w["ssd_norm_g"], bn, length, lc)
    ext = jnp.concatenate([conv0, xbc.reshape(bn, length, SSD_CONV_DIM)[:, length - (SSD_CONV - 1):]], axis=1)
    new_conv = ext[:, ext.shape[1] - (SSD_CONV - 1):]
    x = out_ffn(x, yg, w["ssd_out"], w["ln_g"][1, 1], w["ln_b"][1, 1], ffn(1, 1))

    return (x.reshape(bn, length, D_MODEL), s5_re, s5_im,
            k.reshape(1, bn, length, SB_HEADS, SB_HEAD_DIM), v.reshape(1, bn, length, SB_HEADS, SB_HEAD_DIM),
            h_ssd[None], new_conv[None])


def kernel(x_prompt, x_sample, state_s5_re, state_s5_im, cache_sb_k, cache_sb_v, state_ssd, state_conv, ln_g, ln_b, ffn_w_gate, ffn_w_up, ffn_w_down, mix0_w_in, s5_a_re, s5_a_im, s5_log_dt, s5_b_re, s5_b_im, s5_c_re, s5_c_im, s5_d, s5_w_glu, s5_b_glu, mix0_w_out, ssd_w_in, ssd_conv_w, ssd_conv_b, ssd_dt_bias, ssd_a_log, ssd_d, ssd_norm_g, ssd_w_out):
    bf = lambda a: a.astype(BF16)
    w = {
        "wg": [[bf(ffn_w_gate[l, s]) for s in range(2)] for l in range(DEPTH)],
        "wu": [[bf(ffn_w_up[l, s]) for s in range(2)] for l in range(DEPTH)],
        "wd": [[bf(ffn_w_down[l, s]) for s in range(2)] for l in range(DEPTH)],
        "ln_g": ln_g, "ln_b": ln_b,
        "mix0_in": bf(mix0_w_in[0]),
        "s5_tiles": s5_tiles(s5_a_re[0], s5_a_im[0], s5_log_dt[0], s5_b_re[0], s5_b_im[0], s5_c_re[0], s5_c_im[0]),
        "s5_d": s5_d[0], "s5_wglu": bf(s5_w_glu[0]), "s5_bglu": s5_b_glu[0], "mix0_out": bf(mix0_w_out[0]),
        "ssd_in": bf(ssd_w_in[0]), "ssd_conv_w": ssd_conv_w[0], "ssd_conv_b": ssd_conv_b[0],
        "ssd_dt_bias": ssd_dt_bias[0], "ssd_a_log": ssd_a_log[0], "ssd_d": ssd_d[0],
        "ssd_norm_g": ssd_norm_g[0], "ssd_out": bf(ssd_w_out[0]),
    }
    bp, lp = x_prompt.shape[0], x_prompt.shape[1]
    bs, ls = x_sample.shape[0], x_sample.shape[1]
    yp, s5rp, s5ip, kp, vp, ssdp, convp = _trunk(x_prompt, bp, lp, None, w)
    ys, s5rs, s5is, ks, vs, ssds, convs = _trunk(
        x_sample, bs, ls, (state_s5_re, state_s5_im, cache_sb_k, cache_sb_v, state_ssd, state_conv), w)
    return (yp, ys, s5rp, s5ip, kp, vp, ssdp, convp, s5rs, s5is, ks, vs, ssds, convs)
```

```python
import functools
import math

import jax
import jax.numpy as jnp
from jax import lax
from jax.experimental import pallas as pl
from jax.experimental.pallas import tpu as pltpu

F32 = jnp.float32
BF16 = jnp.bfloat16

D_MODEL = 1024
DEPTH = 2
ALPHA = (2.0 * DEPTH) ** 0.25
LN_EPS = 1e-5
RMS_EPS = 1e-5
D_FF = 2816

S5_WIDTH = 512
S5_GROUP = 16
S5_GROUPS = 32
S5_STATE = 64
S5_CHUNK = 16
S5_ROW = S5_CHUNK * S5_GROUP
SB_HEADS = 8
SB_HEAD_DIM = 64
SB_WIDTH = 512
MIX0_IN = 2048

SSD_INNER = 2048
SSD_HEAD_DIM = 64
SSD_HEADS = 32
SSD_GROUPS = 4
SSD_STATE = 128
SSD_CONV = 4
SSD_GN = 512
SSD_CONV_DIM = 3072
SSD_PAIRS = SSD_HEADS // 2
CONV_PAD = 8

LANES = 128
V7X_VMEM_BYTES = 64 * 1024 * 1024
VMEM_LIMIT = V7X_VMEM_BYTES * 7 // 8


def _cparams(sem):
    return pltpu.CompilerParams(dimension_semantics=sem, vmem_limit_bytes=VMEM_LIMIT)


def _sigmoid(x):
    return 1.0 / (1.0 + jnp.exp2(x * -math.log2(math.e)))


def _softplus(x):
    return jnp.maximum(x, 0.0) + jnp.log1p(jnp.exp(-jnp.abs(x)))


def _layer_norm(y, g, b):
    mu = jnp.mean(y, axis=-1, keepdims=True)
    d = y - mu
    var = jnp.mean(d * d, axis=-1, keepdims=True)
    return d * lax.rsqrt(var + LN_EPS) * g + b


def _dot(a, b):
    return jnp.dot(a, b, preferred_element_type=F32)


def _dot_nt(a, b):
    return lax.dot_general(a, b, (((1,), (1,)), ((), ())), preferred_element_type=F32)


def _dot_tn(a, b):
    return lax.dot_general(a, b, (((0,), (0,)), ((), ())), preferred_element_type=F32)


def _split3(x):
    hi = x.astype(BF16)
    r = x - hi.astype(F32)
    mid = r.astype(BF16)
    lo = (r - mid.astype(F32)).astype(BF16)
    return hi, mid, lo


def _sum_right(x, ones_mat):
    hi, mid, lo = _split3(x)
    return _dot(hi, ones_mat) + _dot(mid, ones_mat) + _dot(lo, ones_mat)


def _sum_left(ones_mat, x):
    hi, mid, lo = _split3(x)
    return _dot(ones_mat, hi) + _dot(ones_mat, mid) + _dot(ones_mat, lo)


def _row_tile(t):
    for tm in (512, 256, 128, 64, 32, 16, 8):
        if t % tm == 0:
            return tm
    raise ValueError(f"token count {t} is not a multiple of 8")


def _full(shape):
    return pl.BlockSpec(shape, lambda *_: (0,) * len(shape), pipeline_mode=pl.Buffered(1))


MXU_TILE = 256
FFN_SPLIT = (D_FF // MXU_TILE // 2) * MXU_TILE


def _ffn_sublayer(x, wg_ref, wu_ref, wd_ref, g_ref, b_ref):
    xb = x.astype(BF16)
    acc = None
    for sl in (slice(0, FFN_SPLIT), slice(FFN_SPLIT, D_FF)):
        gate = _dot(xb, wg_ref[:, sl])
        up = _dot(xb, wu_ref[:, sl])
        h = (gate * _sigmoid(gate) * up).astype(BF16)
        part = _dot(h, wd_ref[sl, :])
        acc = part if acc is None else acc + part
    return _layer_norm(ALPHA * x + 0.5 * acc, g_ref[...], b_ref[...])


def _ffn_kernel(x_ref, wg_ref, wu_ref, wd_ref, g_ref, b_ref, o_ref):
    o_ref[...] = _ffn_sublayer(x_ref[...], wg_ref, wu_ref, wd_ref, g_ref, b_ref)


def _ffn_operands(ffn):
    wg, wu, wd, g, b = ffn
    specs = [_full((D_MODEL, D_FF)), _full((D_MODEL, D_FF)), _full((D_FF, D_MODEL)),
             _full((1, D_MODEL)), _full((1, D_MODEL))]
    return [wg, wu, wd, g.reshape(1, D_MODEL), b.reshape(1, D_MODEL)], specs


def ffn_ln(x, ffn):
    t = x.shape[0]
    tm = _row_tile(t)
    operands, specs = _ffn_operands(ffn)
    return pl.pallas_call(
        _ffn_kernel,
        grid=(t // tm,),
        in_specs=[pl.BlockSpec((tm, D_MODEL), lambda i: (i, 0))] + specs,
        out_specs=pl.BlockSpec((tm, D_MODEL), lambda i: (i, 0)),
        out_shape=jax.ShapeDtypeStruct((t, D_MODEL), F32),
        compiler_params=_cparams(("parallel",)),
        name="ffn_ln",
    )(x, *operands)


def _proj_kernel(x_ref, w_ref, *o_refs, plan):
    xb = x_ref[...].astype(BF16)
    refs = iter(o_refs)
    for lo, hi, outs in plan:
        y = _dot(xb, w_ref[:, lo:hi])
        for _, scale in outs:
            o_ref = next(refs)
            o_ref[...] = (y if scale == 1.0 else y * scale).astype(o_ref.dtype)


def proj(x, w, plan):
    t = x.shape[0]
    tm = _row_tile(t)
    n = w.shape[1]
    flat = [(hi - lo, dt) for lo, hi, outs in plan for dt, _ in outs]
    return pl.pallas_call(
        functools.partial(_proj_kernel, plan=plan),
        grid=(t // tm,),
        in_specs=[pl.BlockSpec((tm, D_MODEL), lambda i: (i, 0)), _full((D_MODEL, n))],
        out_specs=[pl.BlockSpec((tm, width), lambda i: (i, 0)) for width, _ in flat],
        out_shape=[jax.ShapeDtypeStruct((t, width), dt) for width, dt in flat],
        compiler_params=_cparams(("parallel",)),
        name="proj",
    )(x, w)


def _mix0_in_kernel(x_ref, w_ref, u_ref, us_ref, q_ref, k_ref, kb_ref, v_ref, vb_ref, u_scr, *, q_scale):
    xb = x_ref[...].astype(BF16)
    sw = S5_WIDTH
    u = _dot(xb, w_ref[:, 0:sw])
    u_ref[...] = u
    for c in range(sw // LANES):
        u_scr[c] = u[:, c * LANES:(c + 1) * LANES]
    chunks = u.shape[0] // S5_CHUNK
    per_slab = S5_ROW // LANES
    for s in range(S5_CHUNK):
        for half in range(2):
            for c in range(per_slab):
                rows = u_scr[half * per_slab + c, pl.ds(s, chunks, stride=S5_CHUNK), :]
                us_ref[2 * s + half, :, c * LANES:(c + 1) * LANES] = rows.astype(BF16)
    q_ref[...] = (_dot(xb, w_ref[:, sw:2 * sw]) * q_scale).astype(BF16)
    k = _dot(xb, w_ref[:, 2 * sw:3 * sw])
    k_ref[...] = k.reshape(k.shape[0], SB_HEADS, SB_HEAD_DIM)
    kb_ref[...] = k.astype(BF16)
    v = _dot(xb, w_ref[:, 3 * sw:4 * sw])
    v_ref[...] = v.reshape(v.shape[0], SB_HEADS, SB_HEAD_DIM)
    vb_ref[...] = v.astype(BF16)


def mix0_in(x, w, q_scale):
    t = x.shape[0]
    tm = _row_tile(t)
    sw = S5_WIDTH
    tok = pl.BlockSpec((tm, sw), lambda i: (i, 0))
    heads = pl.BlockSpec((tm, SB_HEADS, SB_HEAD_DIM), lambda i: (i, 0, 0))
    return pl.pallas_call(
        functools.partial(_mix0_in_kernel, q_scale=q_scale),
        grid=(t // tm,),
        in_specs=[pl.BlockSpec((tm, D_MODEL), lambda i: (i, 0)), _full((D_MODEL, MIX0_IN))],
        out_specs=[tok, pl.BlockSpec((S5_SLABS, tm // S5_CHUNK, S5_ROW), lambda i: (0, i, 0)),
                   tok, heads, tok, heads, tok],
        out_shape=[jax.ShapeDtypeStruct((t, sw), F32),
                   jax.ShapeDtypeStruct((S5_SLABS, t // S5_CHUNK, S5_ROW), BF16),
                   jax.ShapeDtypeStruct((t, sw), BF16), jax.ShapeDtypeStruct((t, SB_HEADS, SB_HEAD_DIM), F32),
                   jax.ShapeDtypeStruct((t, sw), BF16), jax.ShapeDtypeStruct((t, SB_HEADS, SB_HEAD_DIM), F32),
                   jax.ShapeDtypeStruct((t, sw), BF16)],
        scratch_shapes=[pltpu.VMEM((sw // LANES, tm, LANES), F32)],
        compiler_params=_cparams(("parallel",)),
        name="mix0_in",
    )(x, w)


S5_SLABS = 2 * S5_CHUNK
S5_GH = S5_GROUPS // 2
S5_HALF = S5_GH * S5_STATE
S5_CARRY_LANES = LANES
S5_CARRY_UNROLL = 8


def _s5_tiles_kernel(a_re_ref, a_im_ref, ldt_ref, bt_re_ref, bt_im_ref, cq_re_ref, cq_im_ref,
                     p_ref, qt_ref, k_ref, a16r_ref, a16i_ref):
    s = pl.program_id(1).astype(F32)

    def cpow(m, ar, ai, dt):
        mag = jnp.exp(m * (ar * dt))
        ang = m * (ai * dt)
        return mag * jnp.cos(ang), mag * jnp.sin(ang)

    def zoh(ar, ai, dt):
        abr, abi = cpow(1.0, ar, ai, dt)
        nr, ni, den = abr - 1.0, abi, ar * ar + ai * ai
        return (nr * ar + ni * ai) / den, (ni * ar - nr * ai) / den

    ar, ai, dt = a_re_ref[...], a_im_ref[...], jnp.exp(ldt_ref[...])
    fr, fi = zoh(ar, ai, dt)
    pwr, pwi = cpow(float(S5_CHUNK - 1) - s, ar, ai, dt)
    cr, ci = pwr * fr - pwi * fi, pwr * fi + pwi * fr
    shape = (S5_ROW, S5_HALF)
    same = (jnp.right_shift(lax.broadcasted_iota(jnp.int32, shape, 0), S5_GROUP.bit_length() - 1)
            == jnp.right_shift(lax.broadcasted_iota(jnp.int32, shape, 1), S5_STATE.bit_length() - 1))
    btr, bti = bt_re_ref[...], bt_im_ref[...]
    p_re = jnp.where(same, cr * btr - ci * bti, 0.0).astype(BF16)
    p_im = jnp.where(same, cr * bti + ci * btr, 0.0).astype(BF16)
    p_ref[:, :S5_HALF] = p_re
    p_ref[:, S5_HALF:] = p_im
    qwr, qwi = cpow(s + 1.0, ar, ai, dt)
    cqr, cqi = cq_re_ref[...], cq_im_ref[...]
    qt_ref[:, :S5_HALF] = jnp.where(same, cqr * qwr - cqi * qwi, 0.0).astype(BF16)
    qt_ref[:, S5_HALF:] = jnp.where(same, -(cqr * qwi + cqi * qwr), 0.0).astype(BF16)
    a16r_ref[...], a16i_ref[...] = cpow(float(S5_CHUNK), ar, ai, dt)

    c0 = jnp.concatenate([jnp.where(same, cqr, 0.0), jnp.where(same, -cqi, 0.0)], axis=1).astype(BF16)
    k_ref[...] = _dot_nt(jnp.concatenate([p_re, p_im], axis=1), c0).astype(BF16)


def s5_tiles(a_re, a_im, log_dt, b_re, b_im, c_re, c_im):
    gh, n, pch = S5_GH, S5_STATE, S5_GROUP
    halves = lambda v: v.reshape((2, gh) + v.shape[1:])
    lane = lambda v: halves(v).reshape(2, 1, S5_HALF)
    ldt = jnp.broadcast_to(log_dt[:, None], (S5_GROUPS, n))
    bt = lambda b: halves(jnp.swapaxes(b, 1, 2)).reshape(2, S5_ROW, n)
    wide = lambda v: jnp.tile(v, (1, 1, gh))
    cq = lambda c: jnp.tile(jnp.transpose(halves(c), (0, 2, 1, 3)).reshape(2, pch, S5_HALF), (1, gh, 1))
    half = lambda *shape: pl.BlockSpec((None,) + shape, lambda h, s: (h,) + (0,) * len(shape))
    pt, qt, kt, a16r, a16i = pl.pallas_call(
        _s5_tiles_kernel,
        grid=(2, S5_CHUNK),
        in_specs=[half(1, S5_HALF)] * 3 + [half(S5_ROW, S5_HALF)] * 4,
        out_specs=[pl.BlockSpec((None, S5_ROW, 2 * S5_HALF), lambda h, s: (h, s, 0)),
                   pl.BlockSpec((None, None, S5_ROW, 2 * S5_HALF), lambda h, s: (h, s, 0, 0)),
                   pl.BlockSpec((None, S5_ROW, S5_ROW), lambda h, s: (h, s, 0)),
                   half(1, S5_HALF), half(1, S5_HALF)],
        out_shape=[jax.ShapeDtypeStruct((2, S5_CHUNK * S5_ROW, 2 * S5_HALF), BF16),
                   jax.ShapeDtypeStruct((2, S5_CHUNK, S5_ROW, 2 * S5_HALF), BF16),
                   jax.ShapeDtypeStruct((2, S5_CHUNK * S5_ROW, S5_ROW), BF16),
                   jax.ShapeDtypeStruct((2, 1, S5_HALF), F32), jax.ShapeDtypeStruct((2, 1, S5_HALF), F32)],
        compiler_params=_cparams(("arbitrary", "arbitrary")),
        name="s5_tiles",
    )(lane(a_re), lane(a_im), lane(ldt), wide(bt(b_re)), wide(bt(b_im)), cq(c_re), cq(c_im))
    nstate = S5_GROUPS * S5_STATE
    return kt, pt, qt, a16r.reshape(1, nstate), a16i.reshape(1, nstate)


def _s5_state_kernel(us_ref, p_ref, sre_ref, sim_ref):
    steps = jnp.concatenate([us_ref[s] for s in range(S5_CHUNK)], axis=1)
    part = _dot(steps, p_ref[...])
    for c in range(S5_HALF // LANES):
        sre_ref[c] = part[:, c * LANES:(c + 1) * LANES]
        sim_ref[c] = part[:, S5_HALF + c * LANES:S5_HALF + (c + 1) * LANES]


def _s5_carry_kernel(sre_ref, sim_ref, h0re_ref, h0im_ref, ar_ref, ai_ref,
                     hre_ref, him_ref, lre_ref, lim_ref, *, bn, nj):
    ar, ai = ar_ref[...], ai_ref[...]

    def body(j, carry):
        hr, hi = carry
        rows = pl.ds(j, bn, stride=nj)
        hre_ref[rows, :] = hr
        him_ref[rows, :] = hi
        return ar * hr - ai * hi + sre_ref[rows, :], ar * hi + ai * hr + sim_ref[rows, :]

    hr, hi = lax.fori_loop(0, nj, body, (h0re_ref[...], h0im_ref[...]), unroll=math.gcd(nj, S5_CARRY_UNROLL))
    lre_ref[...] = hr
    lim_ref[...] = hi


def _s5_out_kernel(us_ref, k_ref, hre_ref, him_ref, q_ref, y_ref):
    blocks = range(S5_HALF // LANES)
    hin = jnp.concatenate([hre_ref[c] for c in blocks] + [him_ref[c] for c in blocks], axis=1).astype(BF16)
    for t in range(S5_CHUNK):
        steps = jnp.concatenate([us_ref[s] for s in range(t + 1)], axis=1)
        lags = k_ref[(S5_CHUNK - 1 - t) * S5_ROW:, :]
        y_ref[t] = _dot(steps, lags) + _dot_nt(hin, q_ref[t])


def s5_mix(us, h0re, h0im, tiles, bn, nj):
    kt, pt, qq, ar, ai = tiles
    r = bn * nj
    nstate = S5_GROUPS * S5_STATE
    us4 = us.reshape(S5_CHUNK, 2, r, S5_ROW)
    half_once = lambda *shape: pl.BlockSpec((None,) + shape, lambda h, i: (h,) + (0,) * len(shape),
                                            pipeline_mode=pl.Buffered(1))
    rows = min(r, 512)
    lanes = S5_CARRY_LANES
    sre, sim = pl.pallas_call(
        _s5_state_kernel,
        grid=(2, r // rows),
        in_specs=[pl.BlockSpec((S5_CHUNK, None, rows, S5_ROW), lambda h, i: (0, h, i, 0)),
                  half_once(S5_CHUNK * S5_ROW, 2 * S5_HALF)],
        out_specs=[pl.BlockSpec((S5_HALF // lanes, rows, lanes), lambda h, i: (h, i, 0))] * 2,
        out_shape=[jax.ShapeDtypeStruct((nstate // lanes, r, lanes), F32)] * 2,
        compiler_params=_cparams(("arbitrary", "arbitrary")),
        name="s5_state",
    )(us4, pt)
    col = lambda n: pl.BlockSpec((n, lanes), lambda i: (0, i))
    blk = pl.BlockSpec((None, r, lanes), lambda i: (i, 0, 0))
    hre, him, lre, lim = pl.pallas_call(
        functools.partial(_s5_carry_kernel, bn=bn, nj=nj),
        grid=(nstate // lanes,),
        in_specs=[blk, blk, col(bn), col(bn), col(1), col(1)],
        out_specs=[blk, blk, col(bn), col(bn)],
        out_shape=[jax.ShapeDtypeStruct((nstate // lanes, r, lanes), F32)] * 2
                  + [jax.ShapeDtypeStruct((bn, nstate), F32)] * 2,
        compiler_params=_cparams(("parallel",)),
        name="s5_carry",
    )(sre, sim, h0re, h0im, ar, ai)
    rows = min(r, 256)
    slabs = pl.BlockSpec((S5_CHUNK, None, rows, S5_ROW), lambda h, i: (0, h, i, 0))
    y = pl.pallas_call(
        _s5_out_kernel,
        grid=(2, r // rows),
        in_specs=[slabs, half_once(S5_CHUNK * S5_ROW, S5_ROW),
                  pl.BlockSpec((S5_HALF // lanes, rows, lanes), lambda h, i: (h, i, 0)),
                  pl.BlockSpec((S5_HALF // lanes, rows, lanes), lambda h, i: (h, i, 0)),
                  half_once(S5_CHUNK, S5_ROW, 2 * S5_HALF)],
        out_specs=slabs,
        out_shape=jax.ShapeDtypeStruct((S5_CHUNK, 2, r, S5_ROW), F32),
        compiler_params=_cparams(("arbitrary", "arbitrary")),
        name="s5_out",
    )(us4, kt, hre, him, qq)
    return y.reshape(S5_SLABS, r, S5_ROW), lre, lim


SB_TK = 256
LOG2E = math.log2(math.e)
SB_SKIP_LOG2 = 152.0
SB_SUBS = 4


def _sb_chains(tasks, ones, keys_on_lanes=False):
    zs = [_dot(q, k) if keys_on_lanes else _dot_nt(q, k) for q, k, _, _, _ in tasks]
    tails = [jnp.log2(1.0 + jnp.exp2(-jnp.abs(z))) for z in zs]
    sps = [jnp.maximum(z, 0.0) + t for z, t in zip(zs, tails)]
    sps = [sp if task[3] is None else jnp.where(task[3], sp, 0.0) for sp, task in zip(sps, tasks)]
    mass = [jnp.sum(sp, axis=-1, keepdims=True) for sp in sps]
    suffixes = [_dot(sp.astype(BF16), ones) for sp in sps]
    outs = []
    for i, (_, _, v, mask, carry) in enumerate(tasks):
        logw = jnp.minimum(zs[i], 0.0) - tails[i] - suffixes[i]
        if carry is not None:
            logw = logw - (mass[carry] if isinstance(carry, int) else carry)
        w = jnp.exp2(logw)
        if mask is not None:
            w = jnp.where(mask, w, 0.0)
        wb = w.astype(BF16)
        outs.append(_dot_nt(wb, v) if keys_on_lanes else _dot(wb, v))
    return outs, mass


def _sb_block(q2, k, v, ones, carry, mask, keys_on_lanes=False):
    half = q2.shape[0] // 2
    halves = (slice(0, half), slice(half, 2 * half))
    outs, mass = _sb_chains([(q2[h], k, v, None if mask is None else mask[h],
                              None if carry is None else carry[h]) for h in halves], ones, keys_on_lanes)
    return jnp.concatenate(outs, axis=0), jnp.concatenate(mass, axis=0)


def _sb_walk(q2, load_kv, ones, blocks, acc, carry, keys_on_lanes=False):
    def cond(s):
        return jnp.logical_and(s[0] < blocks, s[1] < SB_SKIP_LOG2)

    def body(s):
        i, _, acc, carry = s
        rows = pl.ds(pl.multiple_of((blocks - 1 - i) * SB_TK, SB_TK), SB_TK)
        k, v = load_kv(rows)
        d, m = _sb_block(q2, k, v, ones, carry, None, keys_on_lanes)
        carry = carry + m
        return i + 1, jnp.min(carry), acc + d, carry

    return lax.while_loop(cond, body, (jnp.int32(0), jnp.min(carry), acc, carry))[2]


def _sb_prompt_kernel(q_ref, kd_ref, vd_ref, kp_ref, vp_ref, ones_ref, o_ref):
    i = pl.program_id(2)
    t, n = SB_TK, SB_SUBS
    lanes = 2 * SB_HEAD_DIM
    q = q_ref[...]
    lane_lo = lax.broadcasted_iota(jnp.int32, (n * t, lanes), 1) < SB_HEAD_DIM
    zero = jnp.zeros_like(q)
    heads = (jnp.where(lane_lo, q, zero), jnp.where(lane_lo, zero, q))
    row = lax.broadcasted_iota(jnp.int32, (t, t), 0)
    col = lax.broadcasted_iota(jnp.int32, (t, t), 1)
    earlier = col < row
    ones = ones_ref[...]
    kd = [kd_ref[s * t:(s + 1) * t, :] for s in range(n)]
    vd = [vd_ref[s * t:(s + 1) * t, :] for s in range(n)]
    qs = [[h[s * t:(s + 1) * t, :] for h in heads] for s in range(n)]

    def front(with_past):
        tasks = [(qs[s][h], kd[s], vd[s], earlier, None) for s in range(n) for h in range(2)]
        past_of = {}
        for s in range(1, n):
            past_of[s] = len(tasks)
            tasks += [(qs[s][h], kd[s - 1], vd[s - 1], None, 2 * s + h) for h in range(2)]
        if with_past:
            rows = pl.ds(pl.multiple_of((n * i - 1) * t, t), t)
            past_of[0] = len(tasks)
            tasks += [(qs[0][h], kp_ref[rows, :], vp_ref[rows, :], None, h) for h in range(2)]
        outs, mass = _sb_chains(tasks, ones)
        result = []
        for s in range(n):
            parts = [(outs[2 * s + h], mass[2 * s + h]) for h in range(2)]
            if s in past_of:
                parts = [(o + outs[past_of[s] + h], m + mass[past_of[s] + h]) for h, (o, m) in enumerate(parts)]
            result += [jnp.concatenate([parts[0][0], parts[1][0]], axis=0),
                       jnp.concatenate([parts[0][1], parts[1][1]], axis=0)]
        lightest = [jnp.minimum(jnp.min(result[2 * s + 1][:t]), jnp.min(result[2 * s + 1][t:])) for s in range(n)]
        return tuple(result) + tuple(lightest)

    state = lax.cond(i > 0, lambda: front(True), lambda: front(False))
    accs, masses, lightest = state[0:2 * n:2], state[1:2 * n:2], state[2 * n:]
    older = [jnp.maximum(n * i + s - 1, 0) for s in range(n)]
    load_kv = lambda rows: (kp_ref[rows, :], vp_ref[rows, :])
    first = lax.broadcasted_iota(jnp.int32, (t, lanes), 1) < SB_HEAD_DIM

    def emit(accs):
        for s in range(n):
            o_ref[s * t:(s + 1) * t, :] = jnp.where(first, accs[s][:t], accs[s][t:])

    def walk_all():
        emit([_sb_walk(jnp.concatenate(qs[s], axis=0), load_kv, ones, older[s], accs[s], masses[s])
              for s in range(n)])

    unfinished = functools.reduce(jnp.logical_or, [jnp.logical_and(lightest[s] < SB_SKIP_LOG2, older[s] > 0)
                                                   for s in range(n)])
    lax.cond(unfinished, walk_all, lambda: emit(accs))


def _sb_cached_kernel(q_ref, kd_ref, vd_ref, kp_ref, vp_ref, ud_ref, up_ref, o_ref, *, tq, blocks):
    lanes = 2 * SB_HEAD_DIM
    pairs = SB_HEADS // 2
    lane_lo = lax.broadcasted_iota(jnp.int32, (tq, lanes), 1) < SB_HEAD_DIM
    row = lax.broadcasted_iota(jnp.int32, (tq, tq), 0)
    col = lax.broadcasted_iota(jnp.int32, (tq, tq), 1)
    earlier = col < row
    cols = [slice(p * lanes, (p + 1) * lanes) for p in range(pairs)]

    def load_kv(keys, pair):
        flat = lambda ref: ref[2 * pair:2 * pair + 2, :, keys].reshape(lanes, SB_TK).astype(BF16)
        return flat(kp_ref), flat(vp_ref)

    qs = []
    for p in range(pairs):
        q = q_ref[:, cols[p]]
        zero = jnp.zeros_like(q)
        qs += [jnp.where(lane_lo, q, zero), jnp.where(lane_lo, zero, q)]
    new = [(qs[2 * p + h], kd_ref[:, cols[p]], vd_ref[:, cols[p]], earlier, None)
           for p in range(pairs) for h in range(2)]
    outs_new, mass_new = _sb_chains(new, ud_ref[...])
    newest = slice((blocks - 1) * SB_TK, blocks * SB_TK)
    cached = []
    for p in range(pairs):
        k, v = load_kv(newest, p)
        cached += [(qs[2 * p + h], k, v, None, mass_new[2 * p + h]) for h in range(2)]
    outs_old, mass_old = _sb_chains(cached, up_ref[...], keys_on_lanes=True)
    accs = [jnp.concatenate([outs_new[2 * p + h] + outs_old[2 * p + h] for h in range(2)], axis=0)
            for p in range(pairs)]
    masses = [jnp.concatenate([mass_new[2 * p + h] + mass_old[2 * p + h] for h in range(2)], axis=0)
              for p in range(pairs)]

    def emit(accs):
        for p in range(pairs):
            o_ref[:, cols[p]] = jnp.where(lane_lo, accs[p][:tq], accs[p][tq:])

    def walk_all():
        emit([_sb_walk(jnp.concatenate(qs[2 * p:2 * p + 2], axis=0), functools.partial(load_kv, pair=p),
                       up_ref[...], blocks - 1, accs[p], masses[p], keys_on_lanes=True) for p in range(pairs)])

    lightest = functools.reduce(jnp.minimum, [jnp.min(m) for m in masses])
    lax.cond(jnp.logical_and(lightest < SB_SKIP_LOG2, blocks > 1), walk_all, lambda: emit(accs))


def _strict_lower_ones(n):
    r = lax.broadcasted_iota(jnp.int32, (n, n), 0)
    c = lax.broadcasted_iota(jnp.int32, (n, n), 1)
    return (r > c).astype(BF16)


def stick_breaking_fresh(q, k, v, bn, length):
    lanes = 2 * SB_HEAD_DIM
    tq = SB_SUBS * SB_TK
    assert length % tq == 0
    blk = pl.BlockSpec((None, tq, lanes), lambda b, h, i: (b, i, h))
    seq = pl.BlockSpec((None, length, lanes), lambda b, h, i: (b, 0, h))
    return pl.pallas_call(
        _sb_prompt_kernel,
        grid=(bn, SB_HEADS // 2, length // tq),
        in_specs=[blk, blk, blk, seq, seq, _full((SB_TK, SB_TK))],
        out_specs=blk,
        out_shape=jax.ShapeDtypeStruct((bn, length, SB_WIDTH), F32),
        compiler_params=_cparams(("parallel", "parallel", "arbitrary")),
        name="stick_breaking",
    )(q, k, v, k, v, _strict_lower_ones(SB_TK))


def stick_breaking_cached(q, k_new, v_new, k_past, v_past, bn, lq):
    lp = k_past.shape[3]
    assert lp % SB_TK == 0 and lp >= SB_TK
    blk = pl.BlockSpec((None, lq, SB_WIDTH), lambda b: (b, 0, 0))
    past = pl.BlockSpec((None, SB_HEADS, SB_HEAD_DIM, lp), lambda b: (b, 0, 0, 0))
    return pl.pallas_call(
        functools.partial(_sb_cached_kernel, tq=lq, blocks=lp // SB_TK),
        grid=(bn,),
        in_specs=[blk, blk, blk, past, past, _full((lq, lq)), _full((SB_TK, SB_TK))],
        out_specs=blk,
        out_shape=jax.ShapeDtypeStruct((bn, lq, SB_WIDTH), F32),
        compiler_params=_cparams(("parallel",)),
        name="stick_breaking_cached",
    )(q, k_new, v_new, k_past, v_past, _strict_lower_ones(lq), _strict_lower_ones(SB_TK))


def _gelu(x):
    return 0.5 * x * (1.0 + jnp.tanh(math.sqrt(2.0 / math.pi) * (x + 0.044715 * (x * x * x))))


def _mix0_out_kernel(x_ref, ys_ref, u_ref, sb_ref, d_ref, wglu_ref, bglu_ref, wo1_ref, wo2_ref, g_ref, b_ref,
                     wg_ref, wu_ref, wd_ref, g2_ref, b2_ref, o_ref, y_scr):
    x = x_ref[...]
    chunks = y_scr.shape[1] // S5_CHUNK
    per_slab = S5_ROW // LANES
    for s in range(S5_CHUNK):
        for half in range(2):
            for c in range(per_slab):
                y_scr[half * per_slab + c, pl.ds(s, chunks, stride=S5_CHUNK), :] = (
                    ys_ref[2 * s + half, :, c * LANES:(c + 1) * LANES])
    ys = jnp.concatenate([y_scr[c] for c in range(S5_WIDTH // LANES)], axis=1)
    gl = _gelu(ys + d_ref[...] * u_ref[...])
    gate = _sigmoid(_dot(gl.astype(BF16), wglu_ref[...]) + bglu_ref[...])
    m = _dot((gl * gate).astype(BF16), wo1_ref[...]) + _dot(sb_ref[...].astype(BF16), wo2_ref[...])
    x = _layer_norm(ALPHA * x + m, g_ref[...], b_ref[...])
    o_ref[...] = _ffn_sublayer(x, wg_ref, wu_ref, wd_ref, g2_ref, b2_ref)


def mix0_out_ffn(x, ys, u, sb, d, wglu, bglu, wo1, wo2, g, b, ffn):
    t = x.shape[0]
    tm = _row_tile(t)
    rowblk = lambda n: pl.BlockSpec((tm, n), lambda i: (i, 0))
    ffn_operands, ffn_specs = _ffn_operands(ffn)
    return pl.pallas_call(
        _mix0_out_kernel,
        grid=(t // tm,),
        in_specs=[rowblk(D_MODEL),
                  pl.BlockSpec((S5_SLABS, tm // S5_CHUNK, S5_ROW), lambda i: (0, i, 0)),
                  rowblk(S5_WIDTH), rowblk(SB_WIDTH),
                  _full((1, S5_WIDTH)), _full((S5_WIDTH, S5_WIDTH)), _full((1, S5_WIDTH)),
                  _full((S5_WIDTH, D_MODEL)), _full((SB_WIDTH, D_MODEL)),
                  _full((1, D_MODEL)), _full((1, D_MODEL))] + ffn_specs,
        out_specs=rowblk(D_MODEL),
        out_shape=jax.ShapeDtypeStruct((t, D_MODEL), F32),
        scratch_shapes=[pltpu.VMEM((S5_WIDTH // LANES, tm, LANES), F32)],
        compiler_params=_cparams(("parallel",)),
        name="mix0_out_ffn",
    )(x, ys, u, sb, d.reshape(1, -1), wglu, bglu.reshape(1, -1), wo1, wo2, g.reshape(1, -1), b.reshape(1, -1),
      *ffn_operands)


def _ssd_kernel(xbc_ref, z_ref, dt_ref, dtt_ref, conv0_ref, h0_ref, cw_ref, cb_ref, dtb_ref, dtbt_ref,
                alog_ref, alogt_ref, dsk_ref, ng_ref, tril_ref, triu_ref,
                y_ref, hl_ref, ext_ref, h_ref, *, lc, nc):
    c = pl.program_id(1)

    @pl.when(c == 0)
    def _():
        ext_ref[0:CONV_PAD, :] = conv0_ref[...]
        h_ref[...] = h0_ref[...]

    ext_ref[CONV_PAD:CONV_PAD + lc, :] = xbc_ref[...]
    ext = ext_ref[...]
    conv = cb_ref[...] + ext[CONV_PAD:, :] * cw_ref[SSD_CONV - 1:SSD_CONV, :]
    for back in range(1, SSD_CONV):
        tap = SSD_CONV - 1 - back
        conv = conv + pltpu.roll(ext, back, 0)[CONV_PAD:, :] * cw_ref[tap:tap + 1, :]
    ext_ref[0:CONV_PAD, :] = ext[lc:lc + CONV_PAD, :]
    act = conv * _sigmoid(conv)

    dt = _softplus(dt_ref[...] + dtb_ref[...])
    dtt = _softplus(dtt_ref[...] + dtbt_ref[...])
    cs = _sum_left(tril_ref[...], dt * (-LOG2E * jnp.exp(alog_ref[...])))
    cst = _sum_right(dtt * (-LOG2E * jnp.exp(alogt_ref[...])), triu_ref[...])
    cs_last = cs[lc - 1:lc, :]

    row = lax.broadcasted_iota(jnp.int32, (lc, lc), 0)
    col = lax.broadcasted_iota(jnp.int32, (lc, lc), 1)
    causal = col <= row
    lanes = 2 * SSD_HEAD_DIM
    lane_lo = lax.broadcasted_iota(jnp.int32, (lc, lanes), 1) < SSD_HEAD_DIM
    pairs_per_group = SSD_PAIRS // SSD_GROUPS

    for g in range(SSD_GROUPS):
        bm = act[:, SSD_INNER + g * SSD_STATE:SSD_INNER + (g + 1) * SSD_STATE].astype(BF16)
        cm = act[:, SSD_INNER + SSD_GN + g * SSD_STATE:SSD_INNER + SSD_GN + (g + 1) * SSD_STATE].astype(BF16)
        cb = _dot_nt(cm, bm)
        gated, sq = [], jnp.zeros((lc, lanes), F32)
        for k in range(g * pairs_per_group, (g + 1) * pairs_per_group):
            sl = slice(k * lanes, (k + 1) * lanes)
            xp = act[:, sl]
            heads = (2 * k, 2 * k + 1)
            pair = lambda v: jnp.where(lane_lo[:v[0].shape[0]], v[0], v[1])
            xdt = xp * pair([dt[:, h:h + 1] for h in heads])
            xdtb = xdt.astype(BF16)
            csb = [jnp.broadcast_to(cs[:, h:h + 1], (lc, lanes)) for h in heads]
            ys = []
            for h, csh in zip(heads, csb):
                seg = csh[:, :lc] - cst[h:h + 1, :]
                w = cb * jnp.exp2(jnp.where(causal, seg, -jnp.inf))
                ys.append(_dot(w.astype(BF16), xdtb))
            y = pair(ys)
            cs_pair = pair(csb)
            last_pair = pair([cs_last[:, h:h + 1] for h in heads])
            ht = h_ref[k]
            y = y + _dot(cm, ht.astype(BF16)) * jnp.exp2(cs_pair)
            xw = (xdt * jnp.exp2(last_pair - cs_pair)).astype(BF16)
            h_ref[k] = ht * jnp.exp2(last_pair) + _dot_tn(bm, xw)
            y = y + dsk_ref[:, sl] * xp
            zz = z_ref[:, sl]
            yg = y * (zz * _sigmoid(zz))
            sq = sq + yg * yg
            gated.append(yg)
        mean_sq = jnp.sum(sq, axis=-1, keepdims=True) * (1.0 / (pairs_per_group * lanes))
        scale = lax.rsqrt(mean_sq + RMS_EPS)
        for k, yg in zip(range(g * pairs_per_group, (g + 1) * pairs_per_group), gated):
            sl = slice(k * lanes, (k + 1) * lanes)
            y_ref[:, sl] = yg * scale * ng_ref[:, sl]

    @pl.when(c == nc - 1)
    def _():
        hl_ref[...] = h_ref[...]


def ssd_mix(xbc, z, dt_raw, conv0, h0, conv_w, conv_b, dt_bias, a_log, d_skip, norm_g, bn, length, lc):
    nc = length // lc
    t = bn * length
    lanes = 2 * SSD_HEAD_DIM
    dtt = jnp.swapaxes(dt_raw.reshape(bn * nc, lc, SSD_HEADS), 1, 2)
    conv0p = jnp.pad(conv0, ((0, 0), (CONV_PAD - (SSD_CONV - 1), 0), (0, 0)))
    a_log = a_log.astype(F32)
    r = lax.broadcasted_iota(jnp.int32, (lc, lc), 0)
    cc = lax.broadcasted_iota(jnp.int32, (lc, lc), 1)
    tril = (cc <= r).astype(BF16)
    triu = (r <= cc).astype(BF16)
    tok = lambda n: pl.BlockSpec((lc, n), lambda b, c: (b * nc + c, 0))
    y, hl = pl.pallas_call(
        functools.partial(_ssd_kernel, lc=lc, nc=nc),
        grid=(bn, nc),
        in_specs=[tok(SSD_CONV_DIM), tok(SSD_INNER), tok(SSD_HEADS),
                  pl.BlockSpec((None, SSD_HEADS, lc), lambda b, c: (b * nc + c, 0, 0)),
                  pl.BlockSpec((None, CONV_PAD, SSD_CONV_DIM), lambda b, c: (b, 0, 0)),
                  pl.BlockSpec((None, SSD_PAIRS, lanes, SSD_STATE), lambda b, c: (b, 0, 0, 0)),
                  _full((SSD_CONV, SSD_CONV_DIM)), _full((1, SSD_CONV_DIM)),
                  _full((1, SSD_HEADS)), _full((SSD_HEADS, 1)), _full((1, SSD_HEADS)), _full((SSD_HEADS, 1)),
                  _full((1, SSD_INNER)), _full((1, SSD_INNER)), _full((lc, lc)), _full((lc, lc))],
        out_specs=[tok(SSD_INNER),
                   pl.BlockSpec((None, SSD_PAIRS, lanes, SSD_STATE), lambda b, c: (b, 0, 0, 0))],
        out_shape=[jax.ShapeDtypeStruct((t, SSD_INNER), F32),
                   jax.ShapeDtypeStruct((bn, SSD_PAIRS, lanes, SSD_STATE), F32)],
        scratch_shapes=[pltpu.VMEM((CONV_PAD + lc, SSD_CONV_DIM), F32),
                        pltpu.VMEM((SSD_PAIRS, lanes, SSD_STATE), F32)],
        compiler_params=_cparams(("parallel", "arbitrary")),
        name="ssd_mix",
    )(xbc, z, dt_raw, dtt, conv0p, jnp.swapaxes(h0.reshape(bn, SSD_PAIRS, lanes, SSD_STATE), 2, 3),
      conv_w, conv_b.reshape(1, -1), dt_bias.reshape(1, -1), dt_bias.reshape(-1, 1),
      a_log.reshape(1, -1), a_log.reshape(-1, 1), jnp.repeat(d_skip, SSD_HEAD_DIM).reshape(1, -1),
      norm_g.reshape(1, -1), tril, triu)
    return y, jnp.swapaxes(hl, 2, 3).reshape(bn, SSD_HEADS, SSD_HEAD_DIM, SSD_STATE)


def _out_ffn_kernel(x_ref, a_ref, w_ref, g_ref, b_ref, wg_ref, wu_ref, wd_ref, g2_ref, b2_ref, o_ref):
    m = _dot(a_ref[...].astype(BF16), w_ref[...])
    x = _layer_norm(ALPHA * x_ref[...] + m, g_ref[...], b_ref[...])
    o_ref[...] = _ffn_sublayer(x, wg_ref, wu_ref, wd_ref, g2_ref, b2_ref)


def out_ffn(x, a, w, g, b, ffn):
    t = x.shape[0]
    tm = _row_tile(t)
    k = a.shape[1]
    ffn_operands, ffn_specs = _ffn_operands(ffn)
    return pl.pallas_call(
        _out_ffn_kernel,
        grid=(t // tm,),
        in_specs=[pl.BlockSpec((tm, D_MODEL), lambda i: (i, 0)), pl.BlockSpec((tm, k), lambda i: (i, 0)),
                  _full((k, D_MODEL)), _full((1, D_MODEL)), _full((1, D_MODEL))] + ffn_specs,
        out_specs=pl.BlockSpec((tm, D_MODEL), lambda i: (i, 0)),
        out_shape=jax.ShapeDtypeStruct((t, D_MODEL), F32),
        compiler_params=_cparams(("parallel",)),
        name="out_ffn",
    )(x, a, w, g.reshape(1, -1), b.reshape(1, -1), *ffn_operands)


def _trunk(x, bn, length, states, w):
    t = bn * length
    x = x.reshape(t, D_MODEL)
    ffn = lambda l, s: (w["wg"][l][s], w["wu"][l][s], w["wd"][l][s], w["ln_g"][l, 2 * s], w["ln_b"][l, 2 * s])
    x = ffn_ln(x, ffn(0, 0))

    u, u_slabs, q_bf, k, k_bf, v, v_bf = mix0_in(x, w["mix0_in"], SB_HEAD_DIM ** -0.5 * LOG2E)
    nj = length // S5_CHUNK
    nstate = S5_GROUPS * S5_STATE
    if states is None:
        h0re = h0im = jnp.zeros((bn, nstate), F32)
    else:
        h0re = states[0][0].astype(F32).reshape(bn, nstate)
        h0im = states[1][0].astype(F32).reshape(bn, nstate)
    y_s5, h_re, h_im = s5_mix(u_slabs, h0re, h0im, w["s5_tiles"], bn, nj)
    s5_re = h_re.reshape(1, bn, S5_GROUPS, S5_STATE)
    s5_im = h_im.reshape(1, bn, S5_GROUPS, S5_STATE)

    seq = lambda a: a.reshape(bn, length, SB_WIDTH)
    if states is None:
        sb = stick_breaking_fresh(seq(q_bf), seq(k_bf), seq(v_bf), bn, length)
    else:
        cache = lambda a: jnp.transpose(a[0], (0, 2, 3, 1))
        sb = stick_breaking_cached(seq(q_bf), seq(k_bf), seq(v_bf), cache(states[2]), cache(states[3]), bn, length)
    x = mix0_out_ffn(x, y_s5, u, sb.reshape(t, SB_WIDTH), w["s5_d"], w["s5_wglu"], w["s5_bglu"],
                     w["mix0_out"][:S5_WIDTH], w["mix0_out"][S5_WIDTH:], w["ln_g"][0, 1], w["ln_b"][0, 1],
                     ffn(0, 1))

    x = ffn_ln(x, ffn(1, 0))
    one = ((F32, 1.0),)
    z, xbc, dt_raw = proj(x, w["ssd_in"], (
        (0, SSD_INNER, one), (SSD_INNER, SSD_INNER + SSD_CONV_DIM, one),
        (SSD_INNER + SSD_CONV_DIM, SSD_INNER + SSD_CONV_DIM + SSD_HEADS, one)))
    if states is None:
        conv0 = jnp.zeros((bn, SSD_CONV - 1, SSD_CONV_DIM), F32)
        hs0 = jnp.zeros((bn, SSD_HEADS, SSD_HEAD_DIM, SSD_STATE), F32)
    else:
        conv0 = states[5][0].astype(F32)
        hs0 = states[4][0].astype(F32)
    lc = min(128, length)
    yg, h_ssd = ssd_mix(xbc, z, dt_raw, conv0, hs0, w["ssd_conv_w"], w["ssd_conv_b"], w["ssd_dt_bias"],
                        w["ssd_a_log"], w["ssd_d"], w["ssd_norm_g"], bn, length, lc)
    ext = jnp.concatenate([conv0, xbc.reshape(bn, length, SSD_CONV_DIM)[:, length - (SSD_CONV - 1):]], axis=1)
    new_conv = ext[:, ext.shape[1] - (SSD_CONV - 1):]
    x = out_ffn(x, yg, w["ssd_out"], w["ln_g"][1, 1], w["ln_b"][1, 1], ffn(1, 1))

    return (x.reshape(bn, length, D_MODEL), s5_re, s5_im,
            k.reshape(1, bn, length, SB_HEADS, SB_HEAD_DIM), v.reshape(1, bn, length, SB_HEADS, SB_HEAD_DIM),
            h_ssd[None], new_conv[None])


def kernel(x_prompt, x_sample, state_s5_re, state_s5_im, cache_sb_k, cache_sb_v, state_ssd, state_conv, ln_g, ln_b, ffn_w_gate, ffn_w_up, ffn_w_down, mix0_w_in, s5_a_re, s5_a_im, s5_log_dt, s5_b_re, s5_b_im, s5_c_re, s5_c_im, s5_d, s5_w_glu, s5_b_glu, mix0_w_out, ssd_w_in, ssd_conv_w, ssd_conv_b, ssd_dt_bias, ssd_a_log, ssd_d, ssd_norm_g, ssd_w_out):
    bf = lambda a: a.astype(BF16)
    w = {
        "wg": [[bf(ffn_w_gate[l, s]) for s in range(2)] for l in range(DEPTH)],
        "wu": [[bf(ffn_w_up[l, s]) for s in range(2)] for l in range(DEPTH)],
        "wd": [[bf(ffn_w_down[l, s]) for s in range(2)] for l in range(DEPTH)],
        "ln_g": ln_g, "ln_b": ln_b,
        "mix0_in": bf(mix0_w_in[0]),
        "s5_tiles": s5_tiles(s5_a_re[0], s5_a_im[0], s5_log_dt[0], s5_b_re[0], s5_b_im[0], s5_c_re[0], s5_c_im[0]),
        "s5_d": s5_d[0], "s5_wglu": bf(s5_w_glu[0]), "s5_bglu": s5_b_glu[0], "mix0_out": bf(mix0_w_out[0]),
        "ssd_in": bf(ssd_w_in[0]), "ssd_conv_w": ssd_conv_w[0], "ssd_conv_b": ssd_conv_b[0],
        "ssd_dt_bias": ssd_dt_bias[0], "ssd_a_log": ssd_a_log[0], "ssd_d": ssd_d[0],
        "ssd_norm_g": ssd_norm_g[0], "ssd_out": bf(ssd_w_out[0]),
    }
    bp, lp = x_prompt.shape[0], x_prompt.shape[1]
    bs, ls = x_sample.shape[0], x_sample.shape[1]
    yp, s5rp, s5ip, kp, vp, ssdp, convp = _trunk(x_prompt, bp, lp, None, w)
    ys, s5rs, s5is, ks, vs, ssds, convs = _trunk(
        x_sample, bs, ls, (state_s5_re, state_s5_im, cache_sb_k, cache_sb_v, state_ssd, state_conv), w)
    return (yp, ys, s5rp, s5ip, kp, vp, ssdp, convp, s5rs, s5is, ks, vs, ssds, convs)
```

```python
import functools
import math

import jax
import jax.numpy as jnp
from jax import lax
from jax.experimental import pallas as pl
from jax.experimental.pallas import tpu as pltpu

F32 = jnp.float32
BF16 = jnp.bfloat16

D_MODEL = 1024
DEPTH = 2
ALPHA = (2.0 * DEPTH) ** 0.25
LN_EPS = 1e-5
RMS_EPS = 1e-5
D_FF = 2816

S5_WIDTH = 512
S5_GROUP = 16
S5_GROUPS = 32
S5_STATE = 64
S5_CHUNK = 16
S5_ROW = S5_CHUNK * S5_GROUP
SB_HEADS = 8
SB_HEAD_DIM = 64
SB_WIDTH = 512
MIX0_IN = 2048

SSD_INNER = 2048
SSD_HEAD_DIM = 64
SSD_HEADS = 32
SSD_GROUPS = 4
SSD_STATE = 128
SSD_CONV = 4
SSD_GN = 512
SSD_CONV_DIM = 3072
SSD_PAIRS = SSD_HEADS // 2
CONV_PAD = 8

LANES = 128
V7X_VMEM_BYTES = 64 * 1024 * 1024
VMEM_LIMIT = V7X_VMEM_BYTES * 7 // 8


def _cparams(sem):
    return pltpu.CompilerParams(dimension_semantics=sem, vmem_limit_bytes=VMEM_LIMIT)


def _sigmoid(x):
    return 1.0 / (1.0 + jnp.exp2(x * -math.log2(math.e)))


def _softplus(x):
    return jnp.maximum(x, 0.0) + jnp.log1p(jnp.exp(-jnp.abs(x)))


def _layer_norm(y, g, b):
    mu = jnp.mean(y, axis=-1, keepdims=True)
    d = y - mu
    var = jnp.mean(d * d, axis=-1, keepdims=True)
    return d * lax.rsqrt(var + LN_EPS) * g + b


def _dot(a, b):
    return jnp.dot(a, b, preferred_element_type=F32)


def _dot_nt(a, b):
    return lax.dot_general(a, b, (((1,), (1,)), ((), ())), preferred_element_type=F32)


def _dot_tn(a, b):
    return lax.dot_general(a, b, (((0,), (0,)), ((), ())), preferred_element_type=F32)


def _split3(x):
    hi = x.astype(BF16)
    r = x - hi.astype(F32)
    mid = r.astype(BF16)
    lo = (r - mid.astype(F32)).astype(BF16)
    return hi, mid, lo


def _sum_right(x, ones_mat):
    hi, mid, lo = _split3(x)
    return _dot(hi, ones_mat) + _dot(mid, ones_mat) + _dot(lo, ones_mat)


def _sum_left(ones_mat, x):
    hi, mid, lo = _split3(x)
    return _dot(ones_mat, hi) + _dot(ones_mat, mid) + _dot(ones_mat, lo)


def _row_tile(t):
    for tm in (512, 256, 128, 64, 32, 16, 8):
        if t % tm == 0:
            return tm
    raise ValueError(f"token count {t} is not a multiple of 8")


def _full(shape):
    return pl.BlockSpec(shape, lambda *_: (0,) * len(shape), pipeline_mode=pl.Buffered(1))


MXU_TILE = 256
FFN_SPLIT = (D_FF // MXU_TILE // 2) * MXU_TILE


def _ffn_sublayer(x, wg_ref, wu_ref, wd_ref, g_ref, b_ref):
    xb = x.astype(BF16)
    acc = None
    for sl in (slice(0, FFN_SPLIT), slice(FFN_SPLIT, D_FF)):
        gate = _dot(xb, wg_ref[:, sl])
        up = _dot(xb, wu_ref[:, sl])
        h = (gate * _sigmoid(gate) * up).astype(BF16)
        part = _dot(h, wd_ref[sl, :])
        acc = part if acc is None else acc + part
    return _layer_norm(ALPHA * x + 0.5 * acc, g_ref[...], b_ref[...])


def _ffn_kernel(x_ref, wg_ref, wu_ref, wd_ref, g_ref, b_ref, o_ref):
    o_ref[...] = _ffn_sublayer(x_ref[...], wg_ref, wu_ref, wd_ref, g_ref, b_ref)


def _ffn_operands(ffn):
    wg, wu, wd, g, b = ffn
    specs = [_full((D_MODEL, D_FF)), _full((D_MODEL, D_FF)), _full((D_FF, D_MODEL)),
             _full((1, D_MODEL)), _full((1, D_MODEL))]
    return [wg, wu, wd, g.reshape(1, D_MODEL), b.reshape(1, D_MODEL)], specs


def ffn_ln(x, ffn):
    t = x.shape[0]
    tm = _row_tile(t)
    operands, specs = _ffn_operands(ffn)
    return pl.pallas_call(
        _ffn_kernel,
        grid=(t // tm,),
        in_specs=[pl.BlockSpec((tm, D_MODEL), lambda i: (i, 0))] + specs,
        out_specs=pl.BlockSpec((tm, D_MODEL), lambda i: (i, 0)),
        out_shape=jax.ShapeDtypeStruct((t, D_MODEL), F32),
        compiler_params=_cparams(("parallel",)),
        name="ffn_ln",
    )(x, *operands)


def _proj_kernel(x_ref, w_ref, *o_refs, plan):
    xb = x_ref[...].astype(BF16)
    refs = iter(o_refs)
    for lo, hi, outs in plan:
        y = _dot(xb, w_ref[:, lo:hi])
        for _, scale in outs:
            o_ref = next(refs)
            o_ref[...] = (y if scale == 1.0 else y * scale).astype(o_ref.dtype)


def proj(x, w, plan):
    t = x.shape[0]
    tm = _row_tile(t)
    n = w.shape[1]
    flat = [(hi - lo, dt) for lo, hi, outs in plan for dt, _ in outs]
    return pl.pallas_call(
        functools.partial(_proj_kernel, plan=plan),
        grid=(t // tm,),
        in_specs=[pl.BlockSpec((tm, D_MODEL), lambda i: (i, 0)), _full((D_MODEL, n))],
        out_specs=[pl.BlockSpec((tm, width), lambda i: (i, 0)) for width, _ in flat],
        out_shape=[jax.ShapeDtypeStruct((t, width), dt) for width, dt in flat],
        compiler_params=_cparams(("parallel",)),
        name="proj",
    )(x, w)


def _mix0_in_kernel(x_ref, w_ref, u_ref, us_ref, q_ref, k_ref, kb_ref, v_ref, vb_ref, u_scr, *, q_scale):
    xb = x_ref[...].astype(BF16)
    sw = S5_WIDTH
    u = _dot(xb, w_ref[:, 0:sw])
    u_ref[...] = u
    for c in range(sw // LANES):
        u_scr[c] = u[:, c * LANES:(c + 1) * LANES]
    chunks = u.shape[0] // S5_CHUNK
    per_slab = S5_ROW // LANES
    for s in range(S5_CHUNK):
        for half in range(2):
            for c in range(per_slab):
                rows = u_scr[half * per_slab + c, pl.ds(s, chunks, stride=S5_CHUNK), :]
                us_ref[2 * s + half, :, c * LANES:(c + 1) * LANES] = rows.astype(BF16)
    q_ref[...] = (_dot(xb, w_ref[:, sw:2 * sw]) * q_scale).astype(BF16)
    k = _dot(xb, w_ref[:, 2 * sw:3 * sw])
    k_ref[...] = k.reshape(k.shape[0], SB_HEADS, SB_HEAD_DIM)
    kb_ref[...] = k.astype(BF16)
    v = _dot(xb, w_ref[:, 3 * sw:4 * sw])
    v_ref[...] = v.reshape(v.shape[0], SB_HEADS, SB_HEAD_DIM)
    vb_ref[...] = v.astype(BF16)


def mix0_in(x, w, q_scale):
    t = x.shape[0]
    tm = _row_tile(t)
    sw = S5_WIDTH
    tok = pl.BlockSpec((tm, sw), lambda i: (i, 0))
    heads = pl.BlockSpec((tm, SB_HEADS, SB_HEAD_DIM), lambda i: (i, 0, 0))
    return pl.pallas_call(
        functools.partial(_mix0_in_kernel, q_scale=q_scale),
        grid=(t // tm,),
        in_specs=[pl.BlockSpec((tm, D_MODEL), lambda i: (i, 0)), _full((D_MODEL, MIX0_IN))],
        out_specs=[tok, pl.BlockSpec((S5_SLABS, tm // S5_CHUNK, S5_ROW), lambda i: (0, i, 0)),
                   tok, heads, tok, heads, tok],
        out_shape=[jax.ShapeDtypeStruct((t, sw), F32),
                   jax.ShapeDtypeStruct((S5_SLABS, t // S5_CHUNK, S5_ROW), BF16),
                   jax.ShapeDtypeStruct((t, sw), BF16), jax.ShapeDtypeStruct((t, SB_HEADS, SB_HEAD_DIM), F32),
                   jax.ShapeDtypeStruct((t, sw), BF16), jax.ShapeDtypeStruct((t, SB_HEADS, SB_HEAD_DIM), F32),
                   jax.ShapeDtypeStruct((t, sw), BF16)],
        scratch_shapes=[pltpu.VMEM((sw // LANES, tm, LANES), F32)],
        compiler_params=_cparams(("parallel",)),
        name="mix0_in",
    )(x, w)


S5_SLABS = 2 * S5_CHUNK
S5_GH = S5_GROUPS // 2
S5_HALF = S5_GH * S5_STATE
S5_CARRY_LANES = LANES
S5_CARRY_UNROLL = 8


def _s5_tiles_kernel(a_re_ref, a_im_ref, ldt_ref, bt_re_ref, bt_im_ref, cq_re_ref, cq_im_ref,
                     p_ref, qt_ref, k_ref, a16r_ref, a16i_ref):
    s = pl.program_id(1).astype(F32)

    def cpow(m, ar, ai, dt):
        mag = jnp.exp(m * (ar * dt))
        ang = m * (ai * dt)
        return mag * jnp.cos(ang), mag * jnp.sin(ang)

    def zoh(ar, ai, dt):
        abr, abi = cpow(1.0, ar, ai, dt)
        nr, ni, den = abr - 1.0, abi, ar * ar + ai * ai
        return (nr * ar + ni * ai) / den, (ni * ar - nr * ai) / den

    ar, ai, dt = a_re_ref[...], a_im_ref[...], jnp.exp(ldt_ref[...])
    fr, fi = zoh(ar, ai, dt)
    pwr, pwi = cpow(float(S5_CHUNK - 1) - s, ar, ai, dt)
    cr, ci = pwr * fr - pwi * fi, pwr * fi + pwi * fr
    shape = (S5_ROW, S5_HALF)
    same = (jnp.right_shift(lax.broadcasted_iota(jnp.int32, shape, 0), S5_GROUP.bit_length() - 1)
            == jnp.right_shift(lax.broadcasted_iota(jnp.int32, shape, 1), S5_STATE.bit_length() - 1))
    btr, bti = bt_re_ref[...], bt_im_ref[...]
    p_re = jnp.where(same, cr * btr - ci * bti, 0.0).astype(BF16)
    p_im = jnp.where(same, cr * bti + ci * btr, 0.0).astype(BF16)
    p_ref[:, :S5_HALF] = p_re
    p_ref[:, S5_HALF:] = p_im
    qwr, qwi = cpow(s + 1.0, ar, ai, dt)
    cqr, cqi = cq_re_ref[...], cq_im_ref[...]
    qt_ref[:, :S5_HALF] = jnp.where(same, cqr * qwr - cqi * qwi, 0.0).astype(BF16)
    qt_ref[:, S5_HALF:] = jnp.where(same, -(cqr * qwi + cqi * qwr), 0.0).astype(BF16)
    a16r_ref[...], a16i_ref[...] = cpow(float(S5_CHUNK), ar, ai, dt)

    c0 = jnp.concatenate([jnp.where(same, cqr, 0.0), jnp.where(same, -cqi, 0.0)], axis=1).astype(BF16)
    k_ref[...] = _dot_nt(jnp.concatenate([p_re, p_im], axis=1), c0).astype(BF16)


def s5_tiles(a_re, a_im, log_dt, b_re, b_im, c_re, c_im):
    gh, n, pch = S5_GH, S5_STATE, S5_GROUP
    halves = lambda v: v.reshape((2, gh) + v.shape[1:])
    lane = lambda v: halves(v).reshape(2, 1, S5_HALF)
    ldt = jnp.broadcast_to(log_dt[:, None], (S5_GROUPS, n))
    bt = lambda b: halves(jnp.swapaxes(b, 1, 2)).reshape(2, S5_ROW, n)
    wide = lambda v: jnp.tile(v, (1, 1, gh))
    cq = lambda c: jnp.tile(jnp.transpose(halves(c), (0, 2, 1, 3)).reshape(2, pch, S5_HALF), (1, gh, 1))
    half = lambda *shape: pl.BlockSpec((None,) + shape, lambda h, s: (h,) + (0,) * len(shape))
    pt, qt, kt, a16r, a16i = pl.pallas_call(
        _s5_tiles_kernel,
        grid=(2, S5_CHUNK),
        in_specs=[half(1, S5_HALF)] * 3 + [half(S5_ROW, S5_HALF)] * 4,
        out_specs=[pl.BlockSpec((None, S5_ROW, 2 * S5_HALF), lambda h, s: (h, s, 0)),
                   pl.BlockSpec((None, None, S5_ROW, 2 * S5_HALF), lambda h, s: (h, s, 0, 0)),
                   pl.BlockSpec((None, S5_ROW, S5_ROW), lambda h, s: (h, s, 0)),
                   half(1, S5_HALF), half(1, S5_HALF)],
        out_shape=[jax.ShapeDtypeStruct((2, S5_CHUNK * S5_ROW, 2 * S5_HALF), BF16),
                   jax.ShapeDtypeStruct((2, S5_CHUNK, S5_ROW, 2 * S5_HALF), BF16),
                   jax.ShapeDtypeStruct((2, S5_CHUNK * S5_ROW, S5_ROW), BF16),
                   jax.ShapeDtypeStruct((2, 1, S5_HALF), F32), jax.ShapeDtypeStruct((2, 1, S5_HALF), F32)],
        compiler_params=_cparams(("arbitrary", "arbitrary")),
        name="s5_tiles",
    )(lane(a_re), lane(a_im), lane(ldt), wide(bt(b_re)), wide(bt(b_im)), cq(c_re), cq(c_im))
    nstate = S5_GROUPS * S5_STATE
    return kt, pt, qt, a16r.reshape(1, nstate), a16i.reshape(1, nstate)


def _s5_state_kernel(us_ref, p_ref, sre_ref, sim_ref):
    steps = jnp.concatenate([us_ref[s] for s in range(S5_CHUNK)], axis=1)
    part = _dot(steps, p_ref[...])
    for c in range(S5_HALF // LANES):
        sre_ref[c] = part[:, c * LANES:(c + 1) * LANES]
        sim_ref[c] = part[:, S5_HALF + c * LANES:S5_HALF + (c + 1) * LANES]


def _s5_carry_kernel(sre_ref, sim_ref, h0re_ref, h0im_ref, ar_ref, ai_ref,
                     hre_ref, him_ref, lre_ref, lim_ref, *, bn, nj):
    ar, ai = ar_ref[...], ai_ref[...]

    def body(j, carry):
        hr, hi = carry
        rows = pl.ds(j, bn, stride=nj)
        hre_ref[rows, :] = hr
        him_ref[rows, :] = hi
        return ar * hr - ai * hi + sre_ref[rows, :], ar * hi + ai * hr + sim_ref[rows, :]

    hr, hi = lax.fori_loop(0, nj, body, (h0re_ref[...], h0im_ref[...]), unroll=math.gcd(nj, S5_CARRY_UNROLL))
    lre_ref[...] = hr
    lim_ref[...] = hi


def _s5_out_kernel(us_ref, k_ref, hre_ref, him_ref, q_ref, y_ref):
    blocks = range(S5_HALF // LANES)
    hin = jnp.concatenate([hre_ref[c] for c in blocks] + [him_ref[c] for c in blocks], axis=1).astype(BF16)
    for t in range(S5_CHUNK):
        steps = jnp.concatenate([us_ref[s] for s in range(t + 1)], axis=1)
        lags = k_ref[(S5_CHUNK - 1 - t) * S5_ROW:, :]
        y_ref[t] = _dot(steps, lags) + _dot_nt(hin, q_ref[t])


def s5_mix(us, h0re, h0im, tiles, bn, nj):
    kt, pt, qq, ar, ai = tiles
    r = bn * nj
    nstate = S5_GROUPS * S5_STATE
    us4 = us.reshape(S5_CHUNK, 2, r, S5_ROW)
    half_once = lambda *shape: pl.BlockSpec((None,) + shape, lambda h, i: (h,) + (0,) * len(shape),
                                            pipeline_mode=pl.Buffered(1))
    rows = min(r, 512)
    lanes = S5_CARRY_LANES
    sre, sim = pl.pallas_call(
        _s5_state_kernel,
        grid=(2, r // rows),
        in_specs=[pl.BlockSpec((S5_CHUNK, None, rows, S5_ROW), lambda h, i: (0, h, i, 0)),
                  half_once(S5_CHUNK * S5_ROW, 2 * S5_HALF)],
        out_specs=[pl.BlockSpec((S5_HALF // lanes, rows, lanes), lambda h, i: (h, i, 0))] * 2,
        out_shape=[jax.ShapeDtypeStruct((nstate // lanes, r, lanes), F32)] * 2,
        compiler_params=_cparams(("arbitrary", "arbitrary")),
        name="s5_state",
    )(us4, pt)
    col = lambda n: pl.BlockSpec((n, lanes), lambda i: (0, i))
    blk = pl.BlockSpec((None, r, lanes), lambda i: (i, 0, 0))
    hre, him, lre, lim = pl.pallas_call(
        functools.partial(_s5_carry_kernel, bn=bn, nj=nj),
        grid=(nstate // lanes,),
        in_specs=[blk, blk, col(bn), col(bn), col(1), col(1)],
        out_specs=[blk, blk, col(bn), col(bn)],
        out_shape=[jax.ShapeDtypeStruct((nstate // lanes, r, lanes), F32)] * 2
                  + [jax.ShapeDtypeStruct((bn, nstate), F32)] * 2,
        compiler_params=_cparams(("parallel",)),
        name="s5_carry",
    )(sre, sim, h0re, h0im, ar, ai)
    rows = min(r, 256)
    slabs = pl.BlockSpec((S5_CHUNK, None, rows, S5_ROW), lambda h, i: (0, h, i, 0))
    y = pl.pallas_call(
        _s5_out_kernel,
        grid=(2, r // rows),
        in_specs=[slabs, half_once(S5_CHUNK * S5_ROW, S5_ROW),
                  pl.BlockSpec((S5_HALF // lanes, rows, lanes), lambda h, i: (h, i, 0)),
                  pl.BlockSpec((S5_HALF // lanes, rows, lanes), lambda h, i: (h, i, 0)),
                  half_once(S5_CHUNK, S5_ROW, 2 * S5_HALF)],
        out_specs=slabs,
        out_shape=jax.ShapeDtypeStruct((S5_CHUNK, 2, r, S5_ROW), F32),
        compiler_params=_cparams(("arbitrary", "arbitrary")),
        name="s5_out",
    )(us4, kt, hre, him, qq)
    return y.reshape(S5_SLABS, r, S5_ROW), lre, lim


SB_TK = 256
LOG2E = math.log2(math.e)
SB_SKIP_LOG2 = 152.0
SB_SUBS = 4


def _sb_chains(tasks, ones, keys_on_lanes=False):
    zs = [_dot(q, k) if keys_on_lanes else _dot_nt(q, k) for q, k, _, _, _ in tasks]
    tails = [jnp.log2(1.0 + jnp.exp2(-jnp.abs(z))) for z in zs]
    sps = [jnp.maximum(z, 0.0) + t for z, t in zip(zs, tails)]
    sps = [sp if task[3] is None else jnp.where(task[3], sp, 0.0) for sp, task in zip(sps, tasks)]
    mass = [jnp.sum(sp, axis=-1, keepdims=True) for sp in sps]
    suffixes = [_dot(sp.astype(BF16), ones) for sp in sps]
    outs = []
    for i, (_, _, v, mask, carry) in enumerate(tasks):
        logw = jnp.minimum(zs[i], 0.0) - tails[i] - suffixes[i]
        if carry is not None:
            logw = logw - (mass[carry] if isinstance(carry, int) else carry)
        w = jnp.exp2(logw)
        if mask is not None:
            w = jnp.where(mask, w, 0.0)
        wb = w.astype(BF16)
        outs.append(_dot_nt(wb, v) if keys_on_lanes else _dot(wb, v))
    return outs, mass


def _sb_block(q2, k, v, ones, carry, mask, keys_on_lanes=False):
    half = q2.shape[0] // 2
    halves = (slice(0, half), slice(half, 2 * half))
    outs, mass = _sb_chains([(q2[h], k, v, None if mask is None else mask[h],
                              None if carry is None else carry[h]) for h in halves], ones, keys_on_lanes)
    return jnp.concatenate(outs, axis=0), jnp.concatenate(mass, axis=0)


def _sb_walk(q2, load_kv, ones, blocks, acc, carry, keys_on_lanes=False):
    def cond(s):
        return jnp.logical_and(s[0] < blocks, s[1] < SB_SKIP_LOG2)

    def body(s):
        i, _, acc, carry = s
        rows = pl.ds(pl.multiple_of((blocks - 1 - i) * SB_TK, SB_TK), SB_TK)
        k, v = load_kv(rows)
        d, m = _sb_block(q2, k, v, ones, carry, None, keys_on_lanes)
        carry = carry + m
        return i + 1, jnp.min(carry), acc + d, carry

    return lax.while_loop(cond, body, (jnp.int32(0), jnp.min(carry), acc, carry))[2]


def _sb_prompt_kernel(q_ref, kd_ref, vd_ref, kp_ref, vp_ref, ones_ref, o_ref):
    i = pl.program_id(2)
    t, n = SB_TK, SB_SUBS
    lanes = 2 * SB_HEAD_DIM
    q = q_ref[...]
    lane_lo = lax.broadcasted_iota(jnp.int32, (n * t, lanes), 1) < SB_HEAD_DIM
    zero = jnp.zeros_like(q)
    heads = (jnp.where(lane_lo, q, zero), jnp.where(lane_lo, zero, q))
    row = lax.broadcasted_iota(jnp.int32, (t, t), 0)
    col = lax.broadcasted_iota(jnp.int32, (t, t), 1)
    earlier = col < row
    ones = ones_ref[...]
    kd = [kd_ref[s * t:(s + 1) * t, :] for s in range(n)]
    vd = [vd_ref[s * t:(s + 1) * t, :] for s in range(n)]
    qs = [[h[s * t:(s + 1) * t, :] for h in heads] for s in range(n)]

    def front(with_past):
        tasks = [(qs[s][h], kd[s], vd[s], earlier, None) for s in range(n) for h in range(2)]
        past_of = {}
        for s in range(1, n):
            past_of[s] = len(tasks)
            tasks += [(qs[s][h], kd[s - 1], vd[s - 1], None, 2 * s + h) for h in range(2)]
        if with_past:
            rows = pl.ds(pl.multiple_of((n * i - 1) * t, t), t)
            past_of[0] = len(tasks)
            tasks += [(qs[0][h], kp_ref[rows, :], vp_ref[rows, :], None, h) for h in range(2)]
        outs, mass = _sb_chains(tasks, ones)
        result = []
        for s in range(n):
            parts = [(outs[2 * s + h], mass[2 * s + h]) for h in range(2)]
            if s in past_of:
                parts = [(o + outs[past_of[s] + h], m + mass[past_of[s] + h]) for h, (o, m) in enumerate(parts)]
            result += [jnp.concatenate([parts[0][0], parts[1][0]], axis=0),
                       jnp.concatenate([parts[0][1], parts[1][1]], axis=0)]
        lightest = [jnp.minimum(jnp.min(result[2 * s + 1][:t]), jnp.min(result[2 * s + 1][t:])) for s in range(n)]
        return tuple(result) + tuple(lightest)

    state = lax.cond(i > 0, lambda: front(True), lambda: front(False))
    accs, masses, lightest = state[0:2 * n:2], state[1:2 * n:2], state[2 * n:]
    older = [jnp.maximum(n * i + s - 1, 0) for s in range(n)]
    load_kv = lambda rows: (kp_ref[rows, :], vp_ref[rows, :])
    first = lax.broadcasted_iota(jnp.int32, (t, lanes), 1) < SB_HEAD_DIM

    def emit(accs):
        for s in range(n):
            o_ref[s * t:(s + 1) * t, :] = jnp.where(first, accs[s][:t], accs[s][t:])

    def walk_all():
        emit([_sb_walk(jnp.concatenate(qs[s], axis=0), load_kv, ones, older[s], accs[s], masses[s])
              for s in range(n)])

    unfinished = functools.reduce(jnp.logical_or, [jnp.logical_and(lightest[s] < SB_SKIP_LOG2, older[s] > 0)
                                                   for s in range(n)])
    lax.cond(unfinished, walk_all, lambda: emit(accs))


def _sb_cached_kernel(q_ref, kd_ref, vd_ref, kp_ref, vp_ref, ud_ref, up_ref, o_ref, low_ref, *, tq, blocks):
    lanes = 2 * SB_HEAD_DIM
    pairs = SB_HEADS // 2
    lane_lo = lax.broadcasted_iota(jnp.int32, (tq, lanes), 1) < SB_HEAD_DIM
    row = lax.broadcasted_iota(jnp.int32, (tq, tq), 0)
    col = lax.broadcasted_iota(jnp.int32, (tq, tq), 1)
    earlier = col < row
    cols = [slice(p * lanes, (p + 1) * lanes) for p in range(pairs)]

    def load_kv(keys, pair):
        flat = lambda ref: ref[2 * pair:2 * pair + 2, :, keys].reshape(lanes, SB_TK).astype(BF16)
        return flat(kp_ref), flat(vp_ref)

    qs = []
    for p in range(pairs):
        q = q_ref[:, cols[p]]
        zero = jnp.zeros_like(q)
        qs += [jnp.where(lane_lo, q, zero), jnp.where(lane_lo, zero, q)]
    new = [(qs[2 * p + h], kd_ref[:, cols[p]], vd_ref[:, cols[p]], earlier, None)
           for p in range(pairs) for h in range(2)]
    outs_new, mass_new = _sb_chains(new, ud_ref[...])
    newest = slice((blocks - 1) * SB_TK, blocks * SB_TK)
    cached = []
    for p in range(pairs):
        k, v = load_kv(newest, p)
        cached += [(qs[2 * p + h], k, v, None, mass_new[2 * p + h]) for h in range(2)]
    outs_old, mass_old = _sb_chains(cached, up_ref[...], keys_on_lanes=True)
    accs = [jnp.concatenate([outs_new[2 * p + h] + outs_old[2 * p + h] for h in range(2)], axis=0)
            for p in range(pairs)]
    masses = [jnp.concatenate([mass_new[2 * p + h] + mass_old[2 * p + h] for h in range(2)], axis=0)
              for p in range(pairs)]

    def emit(accs):
        for p in range(pairs):
            o_ref[:, cols[p]] = jnp.where(lane_lo, accs[p][:tq], accs[p][tq:])

    def walk_all():
        emit([_sb_walk(jnp.concatenate(qs[2 * p:2 * p + 2], axis=0), functools.partial(load_kv, pair=p),
                       up_ref[...], blocks - 1, accs[p], masses[p], keys_on_lanes=True) for p in range(pairs)])

    lightest = functools.reduce(jnp.minimum, [jnp.min(m) for m in masses])
    low_ref[...] = jnp.broadcast_to(lightest, low_ref.shape)
    lax.cond(jnp.logical_and(lightest < SB_SKIP_LOG2, blocks > 1), walk_all, lambda: emit(accs))


def _strict_lower_ones(n):
    r = lax.broadcasted_iota(jnp.int32, (n, n), 0)
    c = lax.broadcasted_iota(jnp.int32, (n, n), 1)
    return (r > c).astype(BF16)


def stick_breaking_fresh(q, k, v, bn, length):
    lanes = 2 * SB_HEAD_DIM
    tq = SB_SUBS * SB_TK
    assert length % tq == 0
    blk = pl.BlockSpec((None, tq, lanes), lambda b, h, i: (b, i, h))
    seq = pl.BlockSpec((None, length, lanes), lambda b, h, i: (b, 0, h))
    return pl.pallas_call(
        _sb_prompt_kernel,
        grid=(bn, SB_HEADS // 2, length // tq),
        in_specs=[blk, blk, blk, seq, seq, _full((SB_TK, SB_TK))],
        out_specs=blk,
        out_shape=jax.ShapeDtypeStruct((bn, length, SB_WIDTH), F32),
        compiler_params=_cparams(("parallel", "parallel", "arbitrary")),
        name="stick_breaking",
    )(q, k, v, k, v, _strict_lower_ones(SB_TK))


def stick_breaking_cached(q, k_new, v_new, k_past, v_past, bn, lq):
    lp = k_past.shape[3]
    assert lp % SB_TK == 0 and lp >= SB_TK
    blk = pl.BlockSpec((None, lq, SB_WIDTH), lambda b: (b, 0, 0))
    low = pl.BlockSpec((None, 1, LANES), lambda b: (b, 0, 0))

    def run(blocks):
        first = lp // SB_TK - blocks
        past = pl.BlockSpec((None, SB_HEADS, SB_HEAD_DIM, blocks * SB_TK), lambda b: (b, 0, 0, first // blocks))
        return pl.pallas_call(
            functools.partial(_sb_cached_kernel, tq=lq, blocks=blocks),
            grid=(bn,),
            in_specs=[blk, blk, blk, past, past, _full((lq, lq)), _full((SB_TK, SB_TK))],
            out_specs=[blk, low],
            out_shape=[jax.ShapeDtypeStruct((bn, lq, SB_WIDTH), F32), jax.ShapeDtypeStruct((bn, 1, LANES), F32)],
            compiler_params=_cparams(("parallel",)),
            name="stick_breaking_cached",
        )(q, k_new, v_new, k_past, v_past, _strict_lower_ones(lq), _strict_lower_ones(SB_TK))

    out, lightest = run(1)
    if lp == SB_TK:
        return out
    return lax.cond(jnp.min(lightest) >= SB_SKIP_LOG2, lambda: out, lambda: run(lp // SB_TK)[0])


def _gelu(x):
    return 0.5 * x * (1.0 + jnp.tanh(math.sqrt(2.0 / math.pi) * (x + 0.044715 * (x * x * x))))


def _mix0_out_kernel(x_ref, ys_ref, u_ref, sb_ref, d_ref, wglu_ref, bglu_ref, wo1_ref, wo2_ref, g_ref, b_ref,
                     wg_ref, wu_ref, wd_ref, g2_ref, b2_ref, o_ref, y_scr):
    x = x_ref[...]
    chunks = y_scr.shape[1] // S5_CHUNK
    per_slab = S5_ROW // LANES
    for s in range(S5_CHUNK):
        for half in range(2):
            for c in range(per_slab):
                y_scr[half * per_slab + c, pl.ds(s, chunks, stride=S5_CHUNK), :] = (
                    ys_ref[2 * s + half, :, c * LANES:(c + 1) * LANES])
    ys = jnp.concatenate([y_scr[c] for c in range(S5_WIDTH // LANES)], axis=1)
    gl = _gelu(ys + d_ref[...] * u_ref[...])
    gate = _sigmoid(_dot(gl.astype(BF16), wglu_ref[...]) + bglu_ref[...])
    m = _dot((gl * gate).astype(BF16), wo1_ref[...]) + _dot(sb_ref[...].astype(BF16), wo2_ref[...])
    x = _layer_norm(ALPHA * x + m, g_ref[...], b_ref[...])
    o_ref[...] = _ffn_sublayer(x, wg_ref, wu_ref, wd_ref, g2_ref, b2_ref)


def mix0_out_ffn(x, ys, u, sb, d, wglu, bglu, wo1, wo2, g, b, ffn):
    t = x.shape[0]
    tm = _row_tile(t)
    rowblk = lambda n: pl.BlockSpec((tm, n), lambda i: (i, 0))
    ffn_operands, ffn_specs = _ffn_operands(ffn)
    return pl.pallas_call(
        _mix0_out_kernel,
        grid=(t // tm,),
        in_specs=[rowblk(D_MODEL),
                  pl.BlockSpec((S5_SLABS, tm // S5_CHUNK, S5_ROW), lambda i: (0, i, 0)),
                  rowblk(S5_WIDTH), rowblk(SB_WIDTH),
                  _full((1, S5_WIDTH)), _full((S5_WIDTH, S5_WIDTH)), _full((1, S5_WIDTH)),
                  _full((S5_WIDTH, D_MODEL)), _full((SB_WIDTH, D_MODEL)),
                  _full((1, D_MODEL)), _full((1, D_MODEL))] + ffn_specs,
        out_specs=rowblk(D_MODEL),
        out_shape=jax.ShapeDtypeStruct((t, D_MODEL), F32),
        scratch_shapes=[pltpu.VMEM((S5_WIDTH // LANES, tm, LANES), F32)],
        compiler_params=_cparams(("parallel",)),
        name="mix0_out_ffn",
    )(x, ys, u, sb, d.reshape(1, -1), wglu, bglu.reshape(1, -1), wo1, wo2, g.reshape(1, -1), b.reshape(1, -1),
      *ffn_operands)


def _ssd_kernel(xbc_ref, z_ref, dt_ref, dtt_ref, conv0_ref, h0_ref, cw_ref, cb_ref, dtb_ref, dtbt_ref,
                alog_ref, alogt_ref, dsk_ref, ng_ref, tril_ref, triu_ref,
                y_ref, hl_ref, ext_ref, h_ref, *, lc, nc):
    c = pl.program_id(1)

    @pl.when(c == 0)
    def _():
        ext_ref[0:CONV_PAD, :] = conv0_ref[...]
        h_ref[...] = h0_ref[...]

    ext_ref[CONV_PAD:CONV_PAD + lc, :] = xbc_ref[...]
    ext = ext_ref[...]
    conv = cb_ref[...] + ext[CONV_PAD:, :] * cw_ref[SSD_CONV - 1:SSD_CONV, :]
    for back in range(1, SSD_CONV):
        tap = SSD_CONV - 1 - back
        conv = conv + pltpu.roll(ext, back, 0)[CONV_PAD:, :] * cw_ref[tap:tap + 1, :]
    ext_ref[0:CONV_PAD, :] = ext[lc:lc + CONV_PAD, :]
    act = conv * _sigmoid(conv)

    dt = _softplus(dt_ref[...] + dtb_ref[...])
    dtt = _softplus(dtt_ref[...] + dtbt_ref[...])
    cs = _sum_left(tril_ref[...], dt * (-LOG2E * jnp.exp(alog_ref[...])))
    cst = _sum_right(dtt * (-LOG2E * jnp.exp(alogt_ref[...])), triu_ref[...])
    cs_last = cs[lc - 1:lc, :]

    row = lax.broadcasted_iota(jnp.int32, (lc, lc), 0)
    col = lax.broadcasted_iota(jnp.int32, (lc, lc), 1)
    causal = col <= row
    lanes = 2 * SSD_HEAD_DIM
    lane_lo = lax.broadcasted_iota(jnp.int32, (lc, lanes), 1) < SSD_HEAD_DIM
    pairs_per_group = SSD_PAIRS // SSD_GROUPS

    for g in range(SSD_GROUPS):
        bm = act[:, SSD_INNER + g * SSD_STATE:SSD_INNER + (g + 1) * SSD_STATE].astype(BF16)
        cm = act[:, SSD_INNER + SSD_GN + g * SSD_STATE:SSD_INNER + SSD_GN + (g + 1) * SSD_STATE].astype(BF16)
        cb = _dot_nt(cm, bm)
        gated, sq = [], jnp.zeros((lc, lanes), F32)
        for k in range(g * pairs_per_group, (g + 1) * pairs_per_group):
            sl = slice(k * lanes, (k + 1) * lanes)
            xp = act[:, sl]
            heads = (2 * k, 2 * k + 1)
            pair = lambda v: jnp.where(lane_lo[:v[0].shape[0]], v[0], v[1])
            xdt = xp * pair([dt[:, h:h + 1] for h in heads])
            xdtb = xdt.astype(BF16)
            csb = [jnp.broadcast_to(cs[:, h:h + 1], (lc, lanes)) for h in heads]
            ys = []
            for h, csh in zip(heads, csb):
                seg = csh[:, :lc] - cst[h:h + 1, :]
                w = cb * jnp.exp2(jnp.where(causal, seg, -jnp.inf))
                ys.append(_dot(w.astype(BF16), xdtb))
            y = pair(ys)
            cs_pair = pair(csb)
            last_pair = pair([cs_last[:, h:h + 1] for h in heads])
            ht = h_ref[k]
            y = y + _dot(cm, ht.astype(BF16)) * jnp.exp2(cs_pair)
            xw = (xdt * jnp.exp2(last_pair - cs_pair)).astype(BF16)
            h_ref[k] = ht * jnp.exp2(last_pair) + _dot_tn(bm, xw)
            y = y + dsk_ref[:, sl] * xp
            zz = z_ref[:, sl]
            yg = y * (zz * _sigmoid(zz))
            sq = sq + yg * yg
            gated.append(yg)
        mean_sq = jnp.sum(sq, axis=-1, keepdims=True) * (1.0 / (pairs_per_group * lanes))
        scale = lax.rsqrt(mean_sq + RMS_EPS)
        for k, yg in zip(range(g * pairs_per_group, (g + 1) * pairs_per_group), gated):
            sl = slice(k * lanes, (k + 1) * lanes)
            y_ref[:, sl] = yg * scale * ng_ref[:, sl]

    @pl.when(c == nc - 1)
    def _():
        hl_ref[...] = h_ref[...]


def ssd_mix(xbc, z, dt_raw, conv0, h0, conv_w, conv_b, dt_bias, a_log, d_skip, norm_g, bn, length, lc):
    nc = length // lc
    t = bn * length
    lanes = 2 * SSD_HEAD_DIM
    dtt = jnp.swapaxes(dt_raw.reshape(bn * nc, lc, SSD_HEADS), 1, 2)
    conv0p = jnp.pad(conv0, ((0, 0), (CONV_PAD - (SSD_CONV - 1), 0), (0, 0)))
    a_log = a_log.astype(F32)
    r = lax.broadcasted_iota(jnp.int32, (lc, lc), 0)
    cc = lax.broadcasted_iota(jnp.int32, (lc, lc), 1)
    tril = (cc <= r).astype(BF16)
    triu = (r <= cc).astype(BF16)
    tok = lambda n: pl.BlockSpec((lc, n), lambda b, c: (b * nc + c, 0))
    y, hl = pl.pallas_call(
        functools.partial(_ssd_kernel, lc=lc, nc=nc),
        grid=(bn, nc),
        in_specs=[tok(SSD_CONV_DIM), tok(SSD_INNER), tok(SSD_HEADS),
                  pl.BlockSpec((None, SSD_HEADS, lc), lambda b, c: (b * nc + c, 0, 0)),
                  pl.BlockSpec((None, CONV_PAD, SSD_CONV_DIM), lambda b, c: (b, 0, 0)),
                  pl.BlockSpec((None, SSD_PAIRS, lanes, SSD_STATE), lambda b, c: (b, 0, 0, 0)),
                  _full((SSD_CONV, SSD_CONV_DIM)), _full((1, SSD_CONV_DIM)),
                  _full((1, SSD_HEADS)), _full((SSD_HEADS, 1)), _full((1, SSD_HEADS)), _full((SSD_HEADS, 1)),
                  _full((1, SSD_INNER)), _full((1, SSD_INNER)), _full((lc, lc)), _full((lc, lc))],
        out_specs=[tok(SSD_INNER),
                   pl.BlockSpec((None, SSD_PAIRS, lanes, SSD_STATE), lambda b, c: (b, 0, 0, 0))],
        out_shape=[jax.ShapeDtypeStruct((t, SSD_INNER), F32),
                   jax.ShapeDtypeStruct((bn, SSD_PAIRS, lanes, SSD_STATE), F32)],
        scratch_shapes=[pltpu.VMEM((CONV_PAD + lc, SSD_CONV_DIM), F32),
                        pltpu.VMEM((SSD_PAIRS, lanes, SSD_STATE), F32)],
        compiler_params=_cparams(("parallel", "arbitrary")),
        name="ssd_mix",
    )(xbc, z, dt_raw, dtt, conv0p, jnp.swapaxes(h0.reshape(bn, SSD_PAIRS, lanes, SSD_STATE), 2, 3),
      conv_w, conv_b.reshape(1, -1), dt_bias.reshape(1, -1), dt_bias.reshape(-1, 1),
      a_log.reshape(1, -1), a_log.reshape(-1, 1), jnp.repeat(d_skip, SSD_HEAD_DIM).reshape(1, -1),
      norm_g.reshape(1, -1), tril, triu)
    return y, jnp.swapaxes(hl, 2, 3).reshape(bn, SSD_HEADS, SSD_HEAD_DIM, SSD_STATE)


def _out_ffn_kernel(x_ref, a_ref, w_ref, g_ref, b_ref, wg_ref, wu_ref, wd_ref, g2_ref, b2_ref, o_ref):
    m = _dot(a_ref[...].astype(BF16), w_ref[...])
    x = _layer_norm(ALPHA * x_ref[...] + m, g_ref[...], b_ref[...])
    o_ref[...] = _ffn_sublayer(x, wg_ref, wu_ref, wd_ref, g2_ref, b2_ref)


def out_ffn(x, a, w, g, b, ffn):
    t = x.shape[0]
    tm = _row_tile(t)
    k = a.shape[1]
    ffn_operands, ffn_specs = _ffn_operands(ffn)
    return pl.pallas_call(
        _out_ffn_kernel,
        grid=(t // tm,),
        in_specs=[pl.BlockSpec((tm, D_MODEL), lambda i: (i, 0)), pl.BlockSpec((tm, k), lambda i: (i, 0)),
                  _full((k, D_MODEL)), _full((1, D_MODEL)), _full((1, D_MODEL))] + ffn_specs,
        out_specs=pl.BlockSpec((tm, D_MODEL), lambda i: (i, 0)),
        out_shape=jax.ShapeDtypeStruct((t, D_MODEL), F32),
        compiler_params=_cparams(("parallel",)),
        name="out_ffn",
    )(x, a, w, g.reshape(1, -1), b.reshape(1, -1), *ffn_operands)


def _trunk(x, bn, length, states, w):
    t = bn * length
    x = x.reshape(t, D_MODEL)
    ffn = lambda l, s: (w["wg"][l][s], w["wu"][l][s], w["wd"][l][s], w["ln_g"][l, 2 * s], w["ln_b"][l, 2 * s])
    x = ffn_ln(x, ffn(0, 0))

    u, u_slabs, q_bf, k, k_bf, v, v_bf = mix0_in(x, w["mix0_in"], SB_HEAD_DIM ** -0.5 * LOG2E)
    nj = length // S5_CHUNK
    nstate = S5_GROUPS * S5_STATE
    if states is None:
        h0re = h0im = jnp.zeros((bn, nstate), F32)
    else:
        h0re = states[0][0].astype(F32).reshape(bn, nstate)
        h0im = states[1][0].astype(F32).reshape(bn, nstate)
    y_s5, h_re, h_im = s5_mix(u_slabs, h0re, h0im, w["s5_tiles"], bn, nj)
    s5_re = h_re.reshape(1, bn, S5_GROUPS, S5_STATE)
    s5_im = h_im.reshape(1, bn, S5_GROUPS, S5_STATE)

    seq = lambda a: a.reshape(bn, length, SB_WIDTH)
    if states is None:
        sb = stick_breaking_fresh(seq(q_bf), seq(k_bf), seq(v_bf), bn, length)
    else:
        cache = lambda a: jnp.transpose(a[0], (0, 2, 3, 1))
        sb = stick_breaking_cached(seq(q_bf), seq(k_bf), seq(v_bf), cache(states[2]), cache(states[3]), bn, length)
    x = mix0_out_ffn(x, y_s5, u, sb.reshape(t, SB_WIDTH), w["s5_d"], w["s5_wglu"], w["s5_bglu"],
                     w["mix0_out"][:S5_WIDTH], w["mix0_out"][S5_WIDTH:], w["ln_g"][0, 1], w["ln_b"][0, 1],
                     ffn(0, 1))

    x = ffn_ln(x, ffn(1, 0))
    one = ((F32, 1.0),)
    z, xbc, dt_raw = proj(x, w["ssd_in"], (
        (0, SSD_INNER, one), (SSD_INNER, SSD_INNER + SSD_CONV_DIM, one),
        (SSD_INNER + SSD_CONV_DIM, SSD_INNER + SSD_CONV_DIM + SSD_HEADS, one)))
    if states is None:
        conv0 = jnp.zeros((bn, SSD_CONV - 1, SSD_CONV_DIM), F32)
        hs0 = jnp.zeros((bn, SSD_HEADS, SSD_HEAD_DIM, SSD_STATE), F32)
    else:
        conv0 = states[5][0].astype(F32)
        hs0 = states[4][0].astype(F32)
    lc = min(128, length)
    yg, h_ssd = ssd_mix(xbc, z, dt_raw, conv0, hs0, w["ssd_conv_w"], w["ssd_conv_b"], w["ssd_dt_bias"],
                        w["ssd_a_log"], w["ssd_d"], w["ssd_norm_g"], bn, length, lc)
    ext = jnp.concatenate([conv0, xbc.reshape(bn, length, SSD_CONV_DIM)[:, length - (SSD_CONV - 1):]], axis=1)
    new_conv = ext[:, ext.shape[1] - (SSD_CONV - 1):]
    x = out_ffn(x, yg, w["ssd_out"], w["ln_g"][1, 1], w["ln_b"][1, 1], ffn(1, 1))

    return (x.reshape(bn, length, D_MODEL), s5_re, s5_im,
            k.reshape(1, bn, length, SB_HEADS, SB_HEAD_DIM), v.reshape(1, bn, length, SB_HEADS, SB_HEAD_DIM),
            h_ssd[None], new_conv[None])


def kernel(x_prompt, x_sample, state_s5_re, state_s5_im, cache_sb_k, cache_sb_v, state_ssd, state_conv, ln_g, ln_b, ffn_w_gate, ffn_w_up, ffn_w_down, mix0_w_in, s5_a_re, s5_a_im, s5_log_dt, s5_b_re, s5_b_im, s5_c_re, s5_c_im, s5_d, s5_w_glu, s5_b_glu, mix0_w_out, ssd_w_in, ssd_conv_w, ssd_conv_b, ssd_dt_bias, ssd_a_log, ssd_d, ssd_norm_g, ssd_w_out):
    bf = lambda a: a.astype(BF16)
    w = {
        "wg": [[bf(ffn_w_gate[l, s]) for s in range(2)] for l in range(DEPTH)],
        "wu": [[bf(ffn_w_up[l, s]) for s in range(2)] for l in range(DEPTH)],
        "wd": [[bf(ffn_w_down[l, s]) for s in range(2)] for l in range(DEPTH)],
        "ln_g": ln_g, "ln_b": ln_b,
        "mix0_in": bf(mix0_w_in[0]),
        "s5_tiles": s5_tiles(s5_a_re[0], s5_a_im[0], s5_log_dt[0], s5_b_re[0], s5_b_im[0], s5_c_re[0], s5_c_im[0]),
        "s5_d": s5_d[0], "s5_wglu": bf(s5_w_glu[0]), "s5_bglu": s5_b_glu[0], "mix0_out": bf(mix0_w_out[0]),
        "ssd_in": bf(ssd_w_in[0]), "ssd_conv_w": ssd_conv_w[0], "ssd_conv_b": ssd_conv_b[0],
        "ssd_dt_bias": ssd_dt_bias[0], "ssd_a_log": ssd_a_log[0], "ssd_d": ssd_d[0],
        "ssd_norm_g": ssd_norm_g[0], "ssd_out": bf(ssd_w_out[0]),
    }
    bp, lp = x_prompt.shape[0], x_prompt.shape[1]
    bs, ls = x_sample.shape[0], x_sample.shape[1]
    yp, s5rp, s5ip, kp, vp, ssdp, convp = _trunk(x_prompt, bp, lp, None, w)
    ys, s5rs, s5is, ks, vs, ssds, convs = _trunk(
        x_sample, bs, ls, (state_s5_re, state_s5_im, cache_sb_k, cache_sb_v, state_ssd, state_conv), w)
    return (yp, ys, s5rp, s5ip, kp, vp, ssdp, convp, s5rs, s5is, ks, vs, ssds, convs)
```

```python
import functools
import math

import jax
import jax.numpy as jnp
from jax import lax
from jax.experimental import pallas as pl
from jax.experimental.pallas import tpu as pltpu

F32 = jnp.float32
BF16 = jnp.bfloat16

D_MODEL = 1024
DEPTH = 2
ALPHA = (2.0 * DEPTH) ** 0.25
LN_EPS = 1e-5
RMS_EPS = 1e-5
D_FF = 2816

S5_WIDTH = 512
S5_GROUP = 16
S5_GROUPS = 32
S5_STATE = 64
S5_CHUNK = 16
S5_ROW = S5_CHUNK * S5_GROUP
SB_HEADS = 8
SB_HEAD_DIM = 64
SB_WIDTH = 512
MIX0_IN = 2048

SSD_INNER = 2048
SSD_HEAD_DIM = 64
SSD_HEADS = 32
SSD_GROUPS = 4
SSD_STATE = 128
SSD_CONV = 4
SSD_GN = 512
SSD_CONV_DIM = 3072
SSD_PAIRS = SSD_HEADS // 2
CONV_PAD = 8

LANES = 128
V7X_VMEM_BYTES = 64 * 1024 * 1024
VMEM_LIMIT = V7X_VMEM_BYTES * 7 // 8


def _cparams(sem):
    return pltpu.CompilerParams(dimension_semantics=sem, vmem_limit_bytes=VMEM_LIMIT)


def _sigmoid(x):
    return 1.0 / (1.0 + jnp.exp2(x * -math.log2(math.e)))


def _softplus(x):
    return jnp.maximum(x, 0.0) + jnp.log1p(jnp.exp(-jnp.abs(x)))


def _layer_norm(y, g, b):
    mu = jnp.mean(y, axis=-1, keepdims=True)
    d = y - mu
    var = jnp.mean(d * d, axis=-1, keepdims=True)
    return d * lax.rsqrt(var + LN_EPS) * g + b


def _dot(a, b):
    return jnp.dot(a, b, preferred_element_type=F32)


def _dot_nt(a, b):
    return lax.dot_general(a, b, (((1,), (1,)), ((), ())), preferred_element_type=F32)


def _dot_tn(a, b):
    return lax.dot_general(a, b, (((0,), (0,)), ((), ())), preferred_element_type=F32)


def _split3(x):
    hi = x.astype(BF16)
    r = x - hi.astype(F32)
    mid = r.astype(BF16)
    lo = (r - mid.astype(F32)).astype(BF16)
    return hi, mid, lo


def _sum_right(x, ones_mat):
    hi, mid, lo = _split3(x)
    return _dot(hi, ones_mat) + _dot(mid, ones_mat) + _dot(lo, ones_mat)


def _sum_left(ones_mat, x):
    hi, mid, lo = _split3(x)
    return _dot(ones_mat, hi) + _dot(ones_mat, mid) + _dot(ones_mat, lo)


def _row_tile(t):
    for tm in (512, 256, 128, 64, 32, 16, 8):
        if t % tm == 0:
            return tm
    raise ValueError(f"token count {t} is not a multiple of 8")


def _full(shape):
    return pl.BlockSpec(shape, lambda *_: (0,) * len(shape), pipeline_mode=pl.Buffered(1))


MXU_TILE = 256
FFN_SPLIT = (D_FF // MXU_TILE // 2) * MXU_TILE


def _ffn_sublayer(x, wg_ref, wu_ref, wd_ref, g_ref, b_ref):
    xb = x.astype(BF16)
    acc = None
    for sl in (slice(0, FFN_SPLIT), slice(FFN_SPLIT, D_FF)):
        gate = _dot(xb, wg_ref[:, sl])
        up = _dot(xb, wu_ref[:, sl])
        h = (gate * _sigmoid(gate) * up).astype(BF16)
        part = _dot(h, wd_ref[sl, :])
        acc = part if acc is None else acc + part
    return _layer_norm(ALPHA * x + 0.5 * acc, g_ref[...], b_ref[...])


def _ffn_kernel(x_ref, wg_ref, wu_ref, wd_ref, g_ref, b_ref, o_ref):
    o_ref[...] = _ffn_sublayer(x_ref[...], wg_ref, wu_ref, wd_ref, g_ref, b_ref)


def _ffn_operands(ffn):
    wg, wu, wd, layer, slot, g, b = ffn
    pick = lambda rows, cols: pl.BlockSpec((None, None, rows, cols), lambda *_: (layer, slot, 0, 0),
                                           pipeline_mode=pl.Buffered(1))
    specs = [pick(D_MODEL, D_FF), pick(D_MODEL, D_FF), pick(D_FF, D_MODEL), _full((1, D_MODEL)), _full((1, D_MODEL))]
    return [wg, wu, wd, g.reshape(1, D_MODEL), b.reshape(1, D_MODEL)], specs


def ffn_ln(x, ffn):
    t = x.shape[0]
    tm = _row_tile(t)
    operands, specs = _ffn_operands(ffn)
    return pl.pallas_call(
        _ffn_kernel,
        grid=(t // tm,),
        in_specs=[pl.BlockSpec((tm, D_MODEL), lambda i: (i, 0))] + specs,
        out_specs=pl.BlockSpec((tm, D_MODEL), lambda i: (i, 0)),
        out_shape=jax.ShapeDtypeStruct((t, D_MODEL), F32),
        compiler_params=_cparams(("parallel",)),
        name="ffn_ln",
    )(x, *operands)


def _proj_kernel(x_ref, w_ref, *o_refs, plan):
    xb = x_ref[...].astype(BF16)
    refs = iter(o_refs)
    for lo, hi, outs in plan:
        y = _dot(xb, w_ref[:, lo:hi])
        for _, scale in outs:
            o_ref = next(refs)
            o_ref[...] = (y if scale == 1.0 else y * scale).astype(o_ref.dtype)


def proj(x, w, plan):
    t = x.shape[0]
    tm = _row_tile(t)
    n = w.shape[1]
    flat = [(hi - lo, dt) for lo, hi, outs in plan for dt, _ in outs]
    return pl.pallas_call(
        functools.partial(_proj_kernel, plan=plan),
        grid=(t // tm,),
        in_specs=[pl.BlockSpec((tm, D_MODEL), lambda i: (i, 0)), _full((D_MODEL, n))],
        out_specs=[pl.BlockSpec((tm, width), lambda i: (i, 0)) for width, _ in flat],
        out_shape=[jax.ShapeDtypeStruct((t, width), dt) for width, dt in flat],
        compiler_params=_cparams(("parallel",)),
        name="proj",
    )(x, w)


def _mix0_in_kernel(x_ref, w_ref, u_ref, us_ref, q_ref, k_ref, kb_ref, v_ref, vb_ref, u_scr, *, q_scale):
    xb = x_ref[...].astype(BF16)
    sw = S5_WIDTH
    u = _dot(xb, w_ref[:, 0:sw])
    u_ref[...] = u
    for c in range(sw // LANES):
        u_scr[c] = u[:, c * LANES:(c + 1) * LANES]
    chunks = u.shape[0] // S5_CHUNK
    per_slab = S5_ROW // LANES
    for s in range(S5_CHUNK):
        for half in range(2):
            for c in range(per_slab):
                rows = u_scr[half * per_slab + c, pl.ds(s, chunks, stride=S5_CHUNK), :]
                us_ref[2 * s + half, :, c * LANES:(c + 1) * LANES] = rows.astype(BF16)
    q_ref[...] = (_dot(xb, w_ref[:, sw:2 * sw]) * q_scale).astype(BF16)
    k = _dot(xb, w_ref[:, 2 * sw:3 * sw])
    k_ref[...] = k.reshape(k.shape[0], SB_HEADS, SB_HEAD_DIM)
    kb_ref[...] = k.astype(BF16)
    v = _dot(xb, w_ref[:, 3 * sw:4 * sw])
    v_ref[...] = v.reshape(v.shape[0], SB_HEADS, SB_HEAD_DIM)
    vb_ref[...] = v.astype(BF16)


def mix0_in(x, w, q_scale):
    t = x.shape[0]
    tm = _row_tile(t)
    sw = S5_WIDTH
    tok = pl.BlockSpec((tm, sw), lambda i: (i, 0))
    heads = pl.BlockSpec((tm, SB_HEADS, SB_HEAD_DIM), lambda i: (i, 0, 0))
    return pl.pallas_call(
        functools.partial(_mix0_in_kernel, q_scale=q_scale),
        grid=(t // tm,),
        in_specs=[pl.BlockSpec((tm, D_MODEL), lambda i: (i, 0)), _full((D_MODEL, MIX0_IN))],
        out_specs=[tok, pl.BlockSpec((S5_SLABS, tm // S5_CHUNK, S5_ROW), lambda i: (0, i, 0)),
                   tok, heads, tok, heads, tok],
        out_shape=[jax.ShapeDtypeStruct((t, sw), F32),
                   jax.ShapeDtypeStruct((S5_SLABS, t // S5_CHUNK, S5_ROW), BF16),
                   jax.ShapeDtypeStruct((t, sw), BF16), jax.ShapeDtypeStruct((t, SB_HEADS, SB_HEAD_DIM), F32),
                   jax.ShapeDtypeStruct((t, sw), BF16), jax.ShapeDtypeStruct((t, SB_HEADS, SB_HEAD_DIM), F32),
                   jax.ShapeDtypeStruct((t, sw), BF16)],
        scratch_shapes=[pltpu.VMEM((sw // LANES, tm, LANES), F32)],
        compiler_params=_cparams(("parallel",)),
        name="mix0_in",
    )(x, w)


S5_SLABS = 2 * S5_CHUNK
S5_GH = S5_GROUPS // 2
S5_HALF = S5_GH * S5_STATE
S5_CARRY_LANES = LANES
S5_CARRY_UNROLL = 8


def _s5_tiles_kernel(a_re_ref, a_im_ref, ldt_ref, bt_re_ref, bt_im_ref, cq_re_ref, cq_im_ref,
                     p_ref, qt_ref, k_ref, a16r_ref, a16i_ref):
    s = pl.program_id(1).astype(F32)

    def cpow(m, ar, ai, dt):
        mag = jnp.exp(m * (ar * dt))
        ang = m * (ai * dt)
        return mag * jnp.cos(ang), mag * jnp.sin(ang)

    def zoh(ar, ai, dt):
        abr, abi = cpow(1.0, ar, ai, dt)
        nr, ni, den = abr - 1.0, abi, ar * ar + ai * ai
        return (nr * ar + ni * ai) / den, (ni * ar - nr * ai) / den

    ar, ai, dt = a_re_ref[...], a_im_ref[...], jnp.exp(ldt_ref[...])
    fr, fi = zoh(ar, ai, dt)
    pwr, pwi = cpow(float(S5_CHUNK - 1) - s, ar, ai, dt)
    cr, ci = pwr * fr - pwi * fi, pwr * fi + pwi * fr
    shape = (S5_ROW, S5_HALF)
    same = (jnp.right_shift(lax.broadcasted_iota(jnp.int32, shape, 0), S5_GROUP.bit_length() - 1)
            == jnp.right_shift(lax.broadcasted_iota(jnp.int32, shape, 1), S5_STATE.bit_length() - 1))
    btr, bti = bt_re_ref[...], bt_im_ref[...]
    p_re = jnp.where(same, cr * btr - ci * bti, 0.0).astype(BF16)
    p_im = jnp.where(same, cr * bti + ci * btr, 0.0).astype(BF16)
    p_ref[:, :S5_HALF] = p_re
    p_ref[:, S5_HALF:] = p_im
    qwr, qwi = cpow(s + 1.0, ar, ai, dt)
    cqr, cqi = cq_re_ref[...], cq_im_ref[...]
    qt_ref[:, :S5_HALF] = jnp.where(same, cqr * qwr - cqi * qwi, 0.0).astype(BF16)
    qt_ref[:, S5_HALF:] = jnp.where(same, -(cqr * qwi + cqi * qwr), 0.0).astype(BF16)
    a16r_ref[...], a16i_ref[...] = cpow(float(S5_CHUNK), ar, ai, dt)

    c0 = jnp.concatenate([jnp.where(same, cqr, 0.0), jnp.where(same, -cqi, 0.0)], axis=1).astype(BF16)
    k_ref[...] = _dot_nt(jnp.concatenate([p_re, p_im], axis=1), c0).astype(BF16)


def s5_tiles(a_re, a_im, log_dt, b_re, b_im, c_re, c_im):
    gh, n, pch = S5_GH, S5_STATE, S5_GROUP
    halves = lambda v: v.reshape((2, gh) + v.shape[1:])
    lane = lambda v: halves(v).reshape(2, 1, S5_HALF)
    ldt = jnp.broadcast_to(log_dt[:, None], (S5_GROUPS, n))
    bt = lambda b: halves(jnp.swapaxes(b, 1, 2)).reshape(2, S5_ROW, n)
    wide = lambda v: jnp.tile(v, (1, 1, gh))
    cq = lambda c: jnp.tile(jnp.transpose(halves(c), (0, 2, 1, 3)).reshape(2, pch, S5_HALF), (1, gh, 1))
    half = lambda *shape: pl.BlockSpec((None,) + shape, lambda h, s: (h,) + (0,) * len(shape))
    pt, qt, kt, a16r, a16i = pl.pallas_call(
        _s5_tiles_kernel,
        grid=(2, S5_CHUNK),
        in_specs=[half(1, S5_HALF)] * 3 + [half(S5_ROW, S5_HALF)] * 4,
        out_specs=[pl.BlockSpec((None, S5_ROW, 2 * S5_HALF), lambda h, s: (h, s, 0)),
                   pl.BlockSpec((None, None, S5_ROW, 2 * S5_HALF), lambda h, s: (h, s, 0, 0)),
                   pl.BlockSpec((None, S5_ROW, S5_ROW), lambda h, s: (h, s, 0)),
                   half(1, S5_HALF), half(1, S5_HALF)],
        out_shape=[jax.ShapeDtypeStruct((2, S5_CHUNK * S5_ROW, 2 * S5_HALF), BF16),
                   jax.ShapeDtypeStruct((2, S5_CHUNK, S5_ROW, 2 * S5_HALF), BF16),
                   jax.ShapeDtypeStruct((2, S5_CHUNK * S5_ROW, S5_ROW), BF16),
                   jax.ShapeDtypeStruct((2, 1, S5_HALF), F32), jax.ShapeDtypeStruct((2, 1, S5_HALF), F32)],
        compiler_params=_cparams(("arbitrary", "arbitrary")),
        name="s5_tiles",
    )(lane(a_re), lane(a_im), lane(ldt), wide(bt(b_re)), wide(bt(b_im)), cq(c_re), cq(c_im))
    nstate = S5_GROUPS * S5_STATE
    return kt, pt, qt, a16r.reshape(1, nstate), a16i.reshape(1, nstate)


def _s5_state_kernel(us_ref, p_ref, sre_ref, sim_ref):
    steps = jnp.concatenate([us_ref[s] for s in range(S5_CHUNK)], axis=1)
    part = _dot(steps, p_ref[...])
    for c in range(S5_HALF // LANES):
        sre_ref[c] = part[:, c * LANES:(c + 1) * LANES]
        sim_ref[c] = part[:, S5_HALF + c * LANES:S5_HALF + (c + 1) * LANES]


def _s5_carry_kernel(sre_ref, sim_ref, h0re_ref, h0im_ref, ar_ref, ai_ref,
                     hre_ref, him_ref, lre_ref, lim_ref, *, bn, nj):
    ar, ai = ar_ref[...], ai_ref[...]

    def body(j, carry):
        hr, hi = carry
        rows = pl.ds(j, bn, stride=nj)
        hre_ref[rows, :] = hr
        him_ref[rows, :] = hi
        return ar * hr - ai * hi + sre_ref[rows, :], ar * hi + ai * hr + sim_ref[rows, :]

    hr, hi = lax.fori_loop(0, nj, body, (h0re_ref[...], h0im_ref[...]), unroll=math.gcd(nj, S5_CARRY_UNROLL))
    lre_ref[...] = hr
    lim_ref[...] = hi


def _s5_out_kernel(us_ref, k_ref, hre_ref, him_ref, q_ref, y_ref):
    blocks = range(S5_HALF // LANES)
    hin = jnp.concatenate([hre_ref[c] for c in blocks] + [him_ref[c] for c in blocks], axis=1).astype(BF16)
    for t in range(S5_CHUNK):
        steps = jnp.concatenate([us_ref[s] for s in range(t + 1)], axis=1)
        lags = k_ref[(S5_CHUNK - 1 - t) * S5_ROW:, :]
        y_ref[t] = _dot(steps, lags) + _dot_nt(hin, q_ref[t])


def s5_mix(us, h0re, h0im, tiles, bn, nj):
    kt, pt, qq, ar, ai = tiles
    r = bn * nj
    nstate = S5_GROUPS * S5_STATE
    us4 = us.reshape(S5_CHUNK, 2, r, S5_ROW)
    half_once = lambda *shape: pl.BlockSpec((None,) + shape, lambda h, i: (h,) + (0,) * len(shape),
                                            pipeline_mode=pl.Buffered(1))
    rows = min(r, 512)
    lanes = S5_CARRY_LANES
    sre, sim = pl.pallas_call(
        _s5_state_kernel,
        grid=(2, r // rows),
        in_specs=[pl.BlockSpec((S5_CHUNK, None, rows, S5_ROW), lambda h, i: (0, h, i, 0)),
                  half_once(S5_CHUNK * S5_ROW, 2 * S5_HALF)],
        out_specs=[pl.BlockSpec((S5_HALF // lanes, rows, lanes), lambda h, i: (h, i, 0))] * 2,
        out_shape=[jax.ShapeDtypeStruct((nstate // lanes, r, lanes), F32)] * 2,
        compiler_params=_cparams(("arbitrary", "arbitrary")),
        name="s5_state",
    )(us4, pt)
    col = lambda n: pl.BlockSpec((n, lanes), lambda i: (0, i))
    blk = pl.BlockSpec((None, r, lanes), lambda i: (i, 0, 0))
    hre, him, lre, lim = pl.pallas_call(
        functools.partial(_s5_carry_kernel, bn=bn, nj=nj),
        grid=(nstate // lanes,),
        in_specs=[blk, blk, col(bn), col(bn), col(1), col(1)],
        out_specs=[blk, blk, col(bn), col(bn)],
        out_shape=[jax.ShapeDtypeStruct((nstate // lanes, r, lanes), F32)] * 2
                  + [jax.ShapeDtypeStruct((bn, nstate), F32)] * 2,
        compiler_params=_cparams(("parallel",)),
        name="s5_carry",
    )(sre, sim, h0re, h0im, ar, ai)
    rows = min(r, 256)
    slabs = pl.BlockSpec((S5_CHUNK, None, rows, S5_ROW), lambda h, i: (0, h, i, 0))
    y = pl.pallas_call(
        _s5_out_kernel,
        grid=(2, r // rows),
        in_specs=[slabs, half_once(S5_CHUNK * S5_ROW, S5_ROW),
                  pl.BlockSpec((S5_HALF // lanes, rows, lanes), lambda h, i: (h, i, 0)),
                  pl.BlockSpec((S5_HALF // lanes, rows, lanes), lambda h, i: (h, i, 0)),
                  half_once(S5_CHUNK, S5_ROW, 2 * S5_HALF)],
        out_specs=slabs,
        out_shape=jax.ShapeDtypeStruct((S5_CHUNK, 2, r, S5_ROW), F32),
        compiler_params=_cparams(("arbitrary", "arbitrary")),
        name="s5_out",
    )(us4, kt, hre, him, qq)
    return y.reshape(S5_SLABS, r, S5_ROW), lre, lim


SB_TK = 256
LOG2E = math.log2(math.e)
SB_SKIP_LOG2 = 152.0
SB_SUBS = 4


def _sb_chains(tasks, ones, keys_on_lanes=False):
    zs = [_dot(q, k) if keys_on_lanes else _dot_nt(q, k) for q, k, _, _, _ in tasks]
    tails = [jnp.log2(1.0 + jnp.exp2(-jnp.abs(z))) for z in zs]
    sps = [jnp.maximum(z, 0.0) + t for z, t in zip(zs, tails)]
    sps = [sp if task[3] is None else jnp.where(task[3], sp, 0.0) for sp, task in zip(sps, tasks)]
    mass = [jnp.sum(sp, axis=-1, keepdims=True) for sp in sps]
    suffixes = [_dot(sp.astype(BF16), ones) for sp in sps]
    outs = []
    for i, (_, _, v, mask, carry) in enumerate(tasks):
        logw = jnp.minimum(zs[i], 0.0) - tails[i] - suffixes[i]
        if carry is not None:
            logw = logw - (mass[carry] if isinstance(carry, int) else carry)
        w = jnp.exp2(logw)
        if mask is not None:
            w = jnp.where(mask, w, 0.0)
        wb = w.astype(BF16)
        outs.append(_dot_nt(wb, v) if keys_on_lanes else _dot(wb, v))
    return outs, mass


def _sb_block(q2, k, v, ones, carry, mask, keys_on_lanes=False):
    half = q2.shape[0] // 2
    halves = (slice(0, half), slice(half, 2 * half))
    outs, mass = _sb_chains([(q2[h], k, v, None if mask is None else mask[h],
                              None if carry is None else carry[h]) for h in halves], ones, keys_on_lanes)
    return jnp.concatenate(outs, axis=0), jnp.concatenate(mass, axis=0)


def _sb_walk(q2, load_kv, ones, blocks, acc, carry, keys_on_lanes=False):
    def cond(s):
        return jnp.logical_and(s[0] < blocks, s[1] < SB_SKIP_LOG2)

    def body(s):
        i, _, acc, carry = s
        rows = pl.ds(pl.multiple_of((blocks - 1 - i) * SB_TK, SB_TK), SB_TK)
        k, v = load_kv(rows)
        d, m = _sb_block(q2, k, v, ones, carry, None, keys_on_lanes)
        carry = carry + m
        return i + 1, jnp.min(carry), acc + d, carry

    return lax.while_loop(cond, body, (jnp.int32(0), jnp.min(carry), acc, carry))[2]


def _sb_prompt_kernel(q_ref, kd_ref, vd_ref, kp_ref, vp_ref, ones_ref, o_ref):
    i = pl.program_id(2)
    t, n = SB_TK, SB_SUBS
    lanes = 2 * SB_HEAD_DIM
    q = q_ref[...]
    lane_lo = lax.broadcasted_iota(jnp.int32, (n * t, lanes), 1) < SB_HEAD_DIM
    zero = jnp.zeros_like(q)
    heads = (jnp.where(lane_lo, q, zero), jnp.where(lane_lo, zero, q))
    row = lax.broadcasted_iota(jnp.int32, (t, t), 0)
    col = lax.broadcasted_iota(jnp.int32, (t, t), 1)
    earlier = col < row
    ones = ones_ref[...]
    kd = [kd_ref[s * t:(s + 1) * t, :] for s in range(n)]
    vd = [vd_ref[s * t:(s + 1) * t, :] for s in range(n)]
    qs = [[h[s * t:(s + 1) * t, :] for h in heads] for s in range(n)]

    def front(with_past):
        tasks = [(qs[s][h], kd[s], vd[s], earlier, None) for s in range(n) for h in range(2)]
        past_of = {}
        for s in range(1, n):
            past_of[s] = len(tasks)
            tasks += [(qs[s][h], kd[s - 1], vd[s - 1], None, 2 * s + h) for h in range(2)]
        if with_past:
            rows = pl.ds(pl.multiple_of((n * i - 1) * t, t), t)
            past_of[0] = len(tasks)
            tasks += [(qs[0][h], kp_ref[rows, :], vp_ref[rows, :], None, h) for h in range(2)]
        outs, mass = _sb_chains(tasks, ones)
        result = []
        for s in range(n):
            parts = [(outs[2 * s + h], mass[2 * s + h]) for h in range(2)]
            if s in past_of:
                parts = [(o + outs[past_of[s] + h], m + mass[past_of[s] + h]) for h, (o, m) in enumerate(parts)]
            result += [jnp.concatenate([parts[0][0], parts[1][0]], axis=0),
                       jnp.concatenate([parts[0][1], parts[1][1]], axis=0)]
        lightest = [jnp.minimum(jnp.min(result[2 * s + 1][:t]), jnp.min(result[2 * s + 1][t:])) for s in range(n)]
        return tuple(result) + tuple(lightest)

    state = lax.cond(i > 0, lambda: front(True), lambda: front(False))
    accs, masses, lightest = state[0:2 * n:2], state[1:2 * n:2], state[2 * n:]
    older = [jnp.maximum(n * i + s - 1, 0) for s in range(n)]
    load_kv = lambda rows: (kp_ref[rows, :], vp_ref[rows, :])
    first = lax.broadcasted_iota(jnp.int32, (t, lanes), 1) < SB_HEAD_DIM

    def emit(accs):
        for s in range(n):
            o_ref[s * t:(s + 1) * t, :] = jnp.where(first, accs[s][:t], accs[s][t:])

    def walk_all():
        emit([_sb_walk(jnp.concatenate(qs[s], axis=0), load_kv, ones, older[s], accs[s], masses[s])
              for s in range(n)])

    unfinished = functools.reduce(jnp.logical_or, [jnp.logical_and(lightest[s] < SB_SKIP_LOG2, older[s] > 0)
                                                   for s in range(n)])
    lax.cond(unfinished, walk_all, lambda: emit(accs))


def _sb_cached_kernel(q_ref, kd_ref, vd_ref, kp_ref, vp_ref, ud_ref, up_ref, o_ref, low_ref, *, tq, blocks):
    lanes = 2 * SB_HEAD_DIM
    pairs = SB_HEADS // 2
    lane_lo = lax.broadcasted_iota(jnp.int32, (tq, lanes), 1) < SB_HEAD_DIM
    row = lax.broadcasted_iota(jnp.int32, (tq, tq), 0)
    col = lax.broadcasted_iota(jnp.int32, (tq, tq), 1)
    earlier = col < row
    cols = [slice(p * lanes, (p + 1) * lanes) for p in range(pairs)]

    def load_kv(keys, pair):
        flat = lambda ref: ref[2 * pair:2 * pair + 2, :, keys].reshape(lanes, SB_TK).astype(BF16)
        return flat(kp_ref), flat(vp_ref)

    qs = []
    for p in range(pairs):
        q = q_ref[:, cols[p]]
        zero = jnp.zeros_like(q)
        qs += [jnp.where(lane_lo, q, zero), jnp.where(lane_lo, zero, q)]
    new = [(qs[2 * p + h], kd_ref[:, cols[p]], vd_ref[:, cols[p]], earlier, None)
           for p in range(pairs) for h in range(2)]
    outs_new, mass_new = _sb_chains(new, ud_ref[...])
    newest = slice((blocks - 1) * SB_TK, blocks * SB_TK)
    cached = []
    for p in range(pairs):
        k, v = load_kv(newest, p)
        cached += [(qs[2 * p + h], k, v, None, mass_new[2 * p + h]) for h in range(2)]
    outs_old, mass_old = _sb_chains(cached, up_ref[...], keys_on_lanes=True)
    accs = [jnp.concatenate([outs_new[2 * p + h] + outs_old[2 * p + h] for h in range(2)], axis=0)
            for p in range(pairs)]
    masses = [jnp.concatenate([mass_new[2 * p + h] + mass_old[2 * p + h] for h in range(2)], axis=0)
              for p in range(pairs)]

    def emit(accs):
        for p in range(pairs):
            o_ref[:, cols[p]] = jnp.where(lane_lo, accs[p][:tq], accs[p][tq:])

    def walk_all():
        emit([_sb_walk(jnp.concatenate(qs[2 * p:2 * p + 2], axis=0), functools.partial(load_kv, pair=p),
                       up_ref[...], blocks - 1, accs[p], masses[p], keys_on_lanes=True) for p in range(pairs)])

    lightest = functools.reduce(jnp.minimum, [jnp.min(m) for m in masses])
    low_ref[...] = jnp.broadcast_to(lightest, low_ref.shape)
    lax.cond(jnp.logical_and(lightest < SB_SKIP_LOG2, blocks > 1), walk_all, lambda: emit(accs))


def _strict_lower_ones(n):
    r = lax.broadcasted_iota(jnp.int32, (n, n), 0)
    c = lax.broadcasted_iota(jnp.int32, (n, n), 1)
    return (r > c).astype(BF16)


def stick_breaking_fresh(q, k, v, bn, length):
    lanes = 2 * SB_HEAD_DIM
    tq = SB_SUBS * SB_TK
    assert length % tq == 0
    blk = pl.BlockSpec((None, tq, lanes), lambda b, h, i: (b, i, h))
    seq = pl.BlockSpec((None, length, lanes), lambda b, h, i: (b, 0, h))
    return pl.pallas_call(
        _sb_prompt_kernel,
        grid=(bn, SB_HEADS // 2, length // tq),
        in_specs=[blk, blk, blk, seq, seq, _full((SB_TK, SB_TK))],
        out_specs=blk,
        out_shape=jax.ShapeDtypeStruct((bn, length, SB_WIDTH), F32),
        compiler_params=_cparams(("parallel", "parallel", "arbitrary")),
        name="stick_breaking",
    )(q, k, v, k, v, _strict_lower_ones(SB_TK))


def stick_breaking_cached(q, k_new, v_new, k_past, v_past, bn, lq):
    lp = k_past.shape[3]
    assert lp % SB_TK == 0 and lp >= SB_TK
    blk = pl.BlockSpec((None, lq, SB_WIDTH), lambda b: (b, 0, 0))
    low = pl.BlockSpec((None, 1, LANES), lambda b: (b, 0, 0))

    def run(blocks):
        first = lp // SB_TK - blocks
        past = pl.BlockSpec((None, SB_HEADS, SB_HEAD_DIM, blocks * SB_TK), lambda b: (b, 0, 0, first // blocks))
        return pl.pallas_call(
            functools.partial(_sb_cached_kernel, tq=lq, blocks=blocks),
            grid=(bn,),
            in_specs=[blk, blk, blk, past, past, _full((lq, lq)), _full((SB_TK, SB_TK))],
            out_specs=[blk, low],
            out_shape=[jax.ShapeDtypeStruct((bn, lq, SB_WIDTH), F32), jax.ShapeDtypeStruct((bn, 1, LANES), F32)],
            compiler_params=_cparams(("parallel",)),
            name="stick_breaking_cached",
        )(q, k_new, v_new, k_past, v_past, _strict_lower_ones(lq), _strict_lower_ones(SB_TK))

    out, lightest = run(1)
    if lp == SB_TK:
        return out
    return lax.cond(jnp.min(lightest) >= SB_SKIP_LOG2, lambda: out, lambda: run(lp // SB_TK)[0])


def _gelu(x):
    return 0.5 * x * (1.0 + jnp.tanh(math.sqrt(2.0 / math.pi) * (x + 0.044715 * (x * x * x))))


def _mix0_out_kernel(x_ref, ys_ref, u_ref, sb_ref, d_ref, wglu_ref, bglu_ref, wo1_ref, wo2_ref, g_ref, b_ref,
                     wg_ref, wu_ref, wd_ref, g2_ref, b2_ref, o_ref, y_scr):
    x = x_ref[...]
    chunks = y_scr.shape[1] // S5_CHUNK
    per_slab = S5_ROW // LANES
    for s in range(S5_CHUNK):
        for half in range(2):
            for c in range(per_slab):
                y_scr[half * per_slab + c, pl.ds(s, chunks, stride=S5_CHUNK), :] = (
                    ys_ref[2 * s + half, :, c * LANES:(c + 1) * LANES])
    ys = jnp.concatenate([y_scr[c] for c in range(S5_WIDTH // LANES)], axis=1)
    gl = _gelu(ys + d_ref[...] * u_ref[...])
    gate = _sigmoid(_dot(gl.astype(BF16), wglu_ref[...]) + bglu_ref[...])
    m = _dot((gl * gate).astype(BF16), wo1_ref[...]) + _dot(sb_ref[...].astype(BF16), wo2_ref[...])
    x = _layer_norm(ALPHA * x + m, g_ref[...], b_ref[...])
    o_ref[...] = _ffn_sublayer(x, wg_ref, wu_ref, wd_ref, g2_ref, b2_ref)


def mix0_out_ffn(x, ys, u, sb, d, wglu, bglu, wo1, wo2, g, b, ffn):
    t = x.shape[0]
    tm = _row_tile(t)
    rowblk = lambda n: pl.BlockSpec((tm, n), lambda i: (i, 0))
    ffn_operands, ffn_specs = _ffn_operands(ffn)
    return pl.pallas_call(
        _mix0_out_kernel,
        grid=(t // tm,),
        in_specs=[rowblk(D_MODEL),
                  pl.BlockSpec((S5_SLABS, tm // S5_CHUNK, S5_ROW), lambda i: (0, i, 0)),
                  rowblk(S5_WIDTH), rowblk(SB_WIDTH),
                  _full((1, S5_WIDTH)), _full((S5_WIDTH, S5_WIDTH)), _full((1, S5_WIDTH)),
                  _full((S5_WIDTH, D_MODEL)), _full((SB_WIDTH, D_MODEL)),
                  _full((1, D_MODEL)), _full((1, D_MODEL))] + ffn_specs,
        out_specs=rowblk(D_MODEL),
        out_shape=jax.ShapeDtypeStruct((t, D_MODEL), F32),
        scratch_shapes=[pltpu.VMEM((S5_WIDTH // LANES, tm, LANES), F32)],
        compiler_params=_cparams(("parallel",)),
        name="mix0_out_ffn",
    )(x, ys, u, sb, d.reshape(1, -1), wglu, bglu.reshape(1, -1), wo1, wo2, g.reshape(1, -1), b.reshape(1, -1),
      *ffn_operands)


def _ssd_kernel(xbc_ref, z_ref, dt_ref, dtt_ref, conv0_ref, h0_ref, cw_ref, cb_ref, dtb_ref, dtbt_ref,
                alog_ref, alogt_ref, dsk_ref, ng_ref, tril_ref, triu_ref,
                y_ref, hl_ref, ext_ref, h_ref, *, lc, nc):
    c = pl.program_id(1)

    @pl.when(c == 0)
    def _():
        ext_ref[0:CONV_PAD, :] = conv0_ref[...]
        h_ref[...] = h0_ref[...]

    ext_ref[CONV_PAD:CONV_PAD + lc, :] = xbc_ref[...]
    ext = ext_ref[...]
    conv = cb_ref[...] + ext[CONV_PAD:, :] * cw_ref[SSD_CONV - 1:SSD_CONV, :]
    for back in range(1, SSD_CONV):
        tap = SSD_CONV - 1 - back
        conv = conv + pltpu.roll(ext, back, 0)[CONV_PAD:, :] * cw_ref[tap:tap + 1, :]
    ext_ref[0:CONV_PAD, :] = ext[lc:lc + CONV_PAD, :]
    act = conv * _sigmoid(conv)

    dt = _softplus(dt_ref[...] + dtb_ref[...])
    dtt = _softplus(dtt_ref[...] + dtbt_ref[...])
    cs = _sum_left(tril_ref[...], dt * (-LOG2E * jnp.exp(alog_ref[...])))
    cst = _sum_right(dtt * (-LOG2E * jnp.exp(alogt_ref[...])), triu_ref[...])
    cs_last = cs[lc - 1:lc, :]

    row = lax.broadcasted_iota(jnp.int32, (lc, lc), 0)
    col = lax.broadcasted_iota(jnp.int32, (lc, lc), 1)
    causal = col <= row
    lanes = 2 * SSD_HEAD_DIM
    lane_lo = lax.broadcasted_iota(jnp.int32, (lc, lanes), 1) < SSD_HEAD_DIM
    pairs_per_group = SSD_PAIRS // SSD_GROUPS

    for g in range(SSD_GROUPS):
        bm = act[:, SSD_INNER + g * SSD_STATE:SSD_INNER + (g + 1) * SSD_STATE].astype(BF16)
        cm = act[:, SSD_INNER + SSD_GN + g * SSD_STATE:SSD_INNER + SSD_GN + (g + 1) * SSD_STATE].astype(BF16)
        cb = _dot_nt(cm, bm)
        gated, sq = [], jnp.zeros((lc, lanes), F32)
        for k in range(g * pairs_per_group, (g + 1) * pairs_per_group):
            sl = slice(k * lanes, (k + 1) * lanes)
            xp = act[:, sl]
            heads = (2 * k, 2 * k + 1)
            pair = lambda v: jnp.where(lane_lo[:v[0].shape[0]], v[0], v[1])
            xdt = xp * pair([dt[:, h:h + 1] for h in heads])
            xdtb = xdt.astype(BF16)
            csb = [jnp.broadcast_to(cs[:, h:h + 1], (lc, lanes)) for h in heads]
            ys = []
            for h, csh in zip(heads, csb):
                seg = csh[:, :lc] - cst[h:h + 1, :]
                w = cb * jnp.exp2(jnp.where(causal, seg, -jnp.inf))
                ys.append(_dot(w.astype(BF16), xdtb))
            y = pair(ys)
            cs_pair = pair(csb)
            last_pair = pair([cs_last[:, h:h + 1] for h in heads])
            ht = h_ref[k]
            y = y + _dot(cm, ht.astype(BF16)) * jnp.exp2(cs_pair)
            xw = (xdt * jnp.exp2(last_pair - cs_pair)).astype(BF16)
            h_ref[k] = ht * jnp.exp2(last_pair) + _dot_tn(bm, xw)
            y = y + dsk_ref[:, sl] * xp
            zz = z_ref[:, sl]
            yg = y * (zz * _sigmoid(zz))
            sq = sq + yg * yg
            gated.append(yg)
        mean_sq = jnp.sum(sq, axis=-1, keepdims=True) * (1.0 / (pairs_per_group * lanes))
        scale = lax.rsqrt(mean_sq + RMS_EPS)
        for k, yg in zip(range(g * pairs_per_group, (g + 1) * pairs_per_group), gated):
            sl = slice(k * lanes, (k + 1) * lanes)
            y_ref[:, sl] = yg * scale * ng_ref[:, sl]

    @pl.when(c == nc - 1)
    def _():
        hl_ref[...] = h_ref[...]


def ssd_mix(xbc, z, dt_raw, conv0, h0, conv_w, conv_b, dt_bias, a_log, d_skip, norm_g, bn, length, lc):
    nc = length // lc
    t = bn * length
    lanes = 2 * SSD_HEAD_DIM
    dtt = jnp.swapaxes(dt_raw.reshape(bn * nc, lc, SSD_HEADS), 1, 2)
    conv0p = jnp.pad(conv0, ((0, 0), (CONV_PAD - (SSD_CONV - 1), 0), (0, 0)))
    a_log = a_log.astype(F32)
    r = lax.broadcasted_iota(jnp.int32, (lc, lc), 0)
    cc = lax.broadcasted_iota(jnp.int32, (lc, lc), 1)
    tril = (cc <= r).astype(BF16)
    triu = (r <= cc).astype(BF16)
    tok = lambda n: pl.BlockSpec((lc, n), lambda b, c: (b * nc + c, 0))
    y, hl = pl.pallas_call(
        functools.partial(_ssd_kernel, lc=lc, nc=nc),
        grid=(bn, nc),
        in_specs=[tok(SSD_CONV_DIM), tok(SSD_INNER), tok(SSD_HEADS),
                  pl.BlockSpec((None, SSD_HEADS, lc), lambda b, c: (b * nc + c, 0, 0)),
                  pl.BlockSpec((None, CONV_PAD, SSD_CONV_DIM), lambda b, c: (b, 0, 0)),
                  pl.BlockSpec((None, SSD_PAIRS, lanes, SSD_STATE), lambda b, c: (b, 0, 0, 0)),
                  _full((SSD_CONV, SSD_CONV_DIM)), _full((1, SSD_CONV_DIM)),
                  _full((1, SSD_HEADS)), _full((SSD_HEADS, 1)), _full((1, SSD_HEADS)), _full((SSD_HEADS, 1)),
                  _full((1, SSD_INNER)), _full((1, SSD_INNER)), _full((lc, lc)), _full((lc, lc))],
        out_specs=[tok(SSD_INNER),
                   pl.BlockSpec((None, SSD_PAIRS, lanes, SSD_STATE), lambda b, c: (b, 0, 0, 0))],
        out_shape=[jax.ShapeDtypeStruct((t, SSD_INNER), F32),
                   jax.ShapeDtypeStruct((bn, SSD_PAIRS, lanes, SSD_STATE), F32)],
        scratch_shapes=[pltpu.VMEM((CONV_PAD + lc, SSD_CONV_DIM), F32),
                        pltpu.VMEM((SSD_PAIRS, lanes, SSD_STATE), F32)],
        compiler_params=_cparams(("parallel", "arbitrary")),
        name="ssd_mix",
    )(xbc, z, dt_raw, dtt, conv0p, jnp.swapaxes(h0.reshape(bn, SSD_PAIRS, lanes, SSD_STATE), 2, 3),
      conv_w, conv_b.reshape(1, -1), dt_bias.reshape(1, -1), dt_bias.reshape(-1, 1),
      a_log.reshape(1, -1), a_log.reshape(-1, 1), jnp.repeat(d_skip, SSD_HEAD_DIM).reshape(1, -1),
      norm_g.reshape(1, -1), tril, triu)
    return y, jnp.swapaxes(hl, 2, 3).reshape(bn, SSD_HEADS, SSD_HEAD_DIM, SSD_STATE)


def _out_ffn_kernel(x_ref, a_ref, w_ref, g_ref, b_ref, wg_ref, wu_ref, wd_ref, g2_ref, b2_ref, o_ref):
    m = _dot(a_ref[...].astype(BF16), w_ref[...])
    x = _layer_norm(ALPHA * x_ref[...] + m, g_ref[...], b_ref[...])
    o_ref[...] = _ffn_sublayer(x, wg_ref, wu_ref, wd_ref, g2_ref, b2_ref)


def out_ffn(x, a, w, g, b, ffn):
    t = x.shape[0]
    tm = _row_tile(t)
    k = a.shape[1]
    ffn_operands, ffn_specs = _ffn_operands(ffn)
    return pl.pallas_call(
        _out_ffn_kernel,
        grid=(t // tm,),
        in_specs=[pl.BlockSpec((tm, D_MODEL), lambda i: (i, 0)), pl.BlockSpec((tm, k), lambda i: (i, 0)),
                  _full((k, D_MODEL)), _full((1, D_MODEL)), _full((1, D_MODEL))] + ffn_specs,
        out_specs=pl.BlockSpec((tm, D_MODEL), lambda i: (i, 0)),
        out_shape=jax.ShapeDtypeStruct((t, D_MODEL), F32),
        compiler_params=_cparams(("parallel",)),
        name="out_ffn",
    )(x, a, w, g.reshape(1, -1), b.reshape(1, -1), *ffn_operands)


def _trunk(x, bn, length, states, w):
    t = bn * length
    x = x.reshape(t, D_MODEL)
    ffn = lambda l, s: (w["wg"], w["wu"], w["wd"], l, s, w["ln_g"][l, 2 * s], w["ln_b"][l, 2 * s])
    x = ffn_ln(x, ffn(0, 0))

    u, u_slabs, q_bf, k, k_bf, v, v_bf = mix0_in(x, w["mix0_in"], SB_HEAD_DIM ** -0.5 * LOG2E)
    nj = length // S5_CHUNK
    nstate = S5_GROUPS * S5_STATE
    if states is None:
        h0re = h0im = jnp.zeros((bn, nstate), F32)
    else:
        h0re = states[0][0].astype(F32).reshape(bn, nstate)
        h0im = states[1][0].astype(F32).reshape(bn, nstate)
    y_s5, h_re, h_im = s5_mix(u_slabs, h0re, h0im, w["s5_tiles"], bn, nj)
    s5_re = h_re.reshape(1, bn, S5_GROUPS, S5_STATE)
    s5_im = h_im.reshape(1, bn, S5_GROUPS, S5_STATE)

    seq = lambda a: a.reshape(bn, length, SB_WIDTH)
    if states is None:
        sb = stick_breaking_fresh(seq(q_bf), seq(k_bf), seq(v_bf), bn, length)
    else:
        cache = lambda a: jnp.transpose(a[0], (0, 2, 3, 1))
        sb = stick_breaking_cached(seq(q_bf), seq(k_bf), seq(v_bf), cache(states[2]), cache(states[3]), bn, length)
    x = mix0_out_ffn(x, y_s5, u, sb.reshape(t, SB_WIDTH), w["s5_d"], w["s5_wglu"], w["s5_bglu"],
                     w["mix0_out"][:S5_WIDTH], w["mix0_out"][S5_WIDTH:], w["ln_g"][0, 1], w["ln_b"][0, 1],
                     ffn(0, 1))

    x = ffn_ln(x, ffn(1, 0))
    one = ((F32, 1.0),)
    z, xbc, dt_raw = proj(x, w["ssd_in"], (
        (0, SSD_INNER, one), (SSD_INNER, SSD_INNER + SSD_CONV_DIM, one),
        (SSD_INNER + SSD_CONV_DIM, SSD_INNER + SSD_CONV_DIM + SSD_HEADS, one)))
    if states is None:
        conv0 = jnp.zeros((bn, SSD_CONV - 1, SSD_CONV_DIM), F32)
        hs0 = jnp.zeros((bn, SSD_HEADS, SSD_HEAD_DIM, SSD_STATE), F32)
    else:
        conv0 = states[5][0].astype(F32)
        hs0 = states[4][0].astype(F32)
    lc = min(128, length)
    yg, h_ssd = ssd_mix(xbc, z, dt_raw, conv0, hs0, w["ssd_conv_w"], w["ssd_conv_b"], w["ssd_dt_bias"],
                        w["ssd_a_log"], w["ssd_d"], w["ssd_norm_g"], bn, length, lc)
    ext = jnp.concatenate([conv0, xbc.reshape(bn, length, SSD_CONV_DIM)[:, length - (SSD_CONV - 1):]], axis=1)
    new_conv = ext[:, ext.shape[1] - (SSD_CONV - 1):]
    x = out_ffn(x, yg, w["ssd_out"], w["ln_g"][1, 1], w["ln_b"][1, 1], ffn(1, 1))

    return (x.reshape(bn, length, D_MODEL), s5_re, s5_im,
            k.reshape(1, bn, length, SB_HEADS, SB_HEAD_DIM), v.reshape(1, bn, length, SB_HEADS, SB_HEAD_DIM),
            h_ssd[None], new_conv[None])


def kernel(x_prompt, x_sample, state_s5_re, state_s5_im, cache_sb_k, cache_sb_v, state_ssd, state_conv, ln_g, ln_b, ffn_w_gate, ffn_w_up, ffn_w_down, mix0_w_in, s5_a_re, s5_a_im, s5_log_dt, s5_b_re, s5_b_im, s5_c_re, s5_c_im, s5_d, s5_w_glu, s5_b_glu, mix0_w_out, ssd_w_in, ssd_conv_w, ssd_conv_b, ssd_dt_bias, ssd_a_log, ssd_d, ssd_norm_g, ssd_w_out):
    bf = lambda a: a.astype(BF16)
    w = {
        "wg": bf(ffn_w_gate), "wu": bf(ffn_w_up), "wd": bf(ffn_w_down),
        "ln_g": ln_g, "ln_b": ln_b,
        "mix0_in": bf(mix0_w_in[0]),
        "s5_tiles": s5_tiles(s5_a_re[0], s5_a_im[0], s5_log_dt[0], s5_b_re[0], s5_b_im[0], s5_c_re[0], s5_c_im[0]),
        "s5_d": s5_d[0], "s5_wglu": bf(s5_w_glu[0]), "s5_bglu": s5_b_glu[0], "mix0_out": bf(mix0_w_out[0]),
        "ssd_in": bf(ssd_w_in[0]), "ssd_conv_w": ssd_conv_w[0], "ssd_conv_b": ssd_conv_b[0],
        "ssd_dt_bias": ssd_dt_bias[0], "ssd_a_log": ssd_a_log[0], "ssd_d": ssd_d[0],
        "ssd_norm_g": ssd_norm_g[0], "ssd_out": bf(ssd_w_out[0]),
    }
    bp, lp = x_prompt.shape[0], x_prompt.shape[1]
    bs, ls = x_sample.shape[0], x_sample.shape[1]
    yp, s5rp, s5ip, kp, vp, ssdp, convp = _trunk(x_prompt, bp, lp, None, w)
    ys, s5rs, s5is, ks, vs, ssds, convs = _trunk(
        x_sample, bs, ls, (state_s5_re, state_s5_im, cache_sb_k, cache_sb_v, state_ssd, state_conv), w)
    return (yp, ys, s5rp, s5ip, kp, vp, ssdp, convp, s5rs, s5is, ks, vs, ssds, convs)
```

```python
import functools
import math

import jax
import jax.numpy as jnp
from jax import lax
from jax.experimental import pallas as pl
from jax.experimental.pallas import tpu as pltpu

F32 = jnp.float32
BF16 = jnp.bfloat16

D_MODEL = 1024
DEPTH = 2
ALPHA = (2.0 * DEPTH) ** 0.25
LN_EPS = 1e-5
RMS_EPS = 1e-5
D_FF = 2816

S5_WIDTH = 512
S5_GROUP = 16
S5_GROUPS = 32
S5_STATE = 64
S5_CHUNK = 16
S5_ROW = S5_CHUNK * S5_GROUP
SB_HEADS = 8
SB_HEAD_DIM = 64
SB_WIDTH = 512
MIX0_IN = 2048

SSD_INNER = 2048
SSD_HEAD_DIM = 64
SSD_HEADS = 32
SSD_GROUPS = 4
SSD_STATE = 128
SSD_CONV = 4
SSD_GN = 512
SSD_CONV_DIM = 3072
SSD_PAIRS = SSD_HEADS // 2
CONV_PAD = 8

LANES = 128
V7X_VMEM_BYTES = 64 * 1024 * 1024
VMEM_LIMIT = V7X_VMEM_BYTES * 7 // 8


def _cparams(sem):
    return pltpu.CompilerParams(dimension_semantics=sem, vmem_limit_bytes=VMEM_LIMIT)


def _sigmoid(x):
    return 1.0 / (1.0 + jnp.exp2(x * -math.log2(math.e)))


def _softplus(x):
    return jnp.maximum(x, 0.0) + jnp.log1p(jnp.exp(-jnp.abs(x)))


def _layer_norm(y, g, b):
    mu = jnp.mean(y, axis=-1, keepdims=True)
    d = y - mu
    var = jnp.mean(d * d, axis=-1, keepdims=True)
    return d * lax.rsqrt(var + LN_EPS) * g + b


def _dot(a, b):
    return jnp.dot(a, b, preferred_element_type=F32)


def _dot_nt(a, b):
    return lax.dot_general(a, b, (((1,), (1,)), ((), ())), preferred_element_type=F32)


def _dot_tn(a, b):
    return lax.dot_general(a, b, (((0,), (0,)), ((), ())), preferred_element_type=F32)


def _split3(x):
    hi = x.astype(BF16)
    r = x - hi.astype(F32)
    mid = r.astype(BF16)
    lo = (r - mid.astype(F32)).astype(BF16)
    return hi, mid, lo


def _sum_right(x, ones_mat):
    hi, mid, lo = _split3(x)
    return _dot(hi, ones_mat) + _dot(mid, ones_mat) + _dot(lo, ones_mat)


def _sum_left(ones_mat, x):
    hi, mid, lo = _split3(x)
    return _dot(ones_mat, hi) + _dot(ones_mat, mid) + _dot(ones_mat, lo)


def _row_tile(t):
    for tm in (512, 256, 128, 64, 32, 16, 8):
        if t % tm == 0:
            return tm
    raise ValueError(f"token count {t} is not a multiple of 8")


def _full(shape):
    return pl.BlockSpec(shape, lambda *_: (0,) * len(shape), pipeline_mode=pl.Buffered(1))


MXU_TILE = 256
FFN_SPLIT = (D_FF // MXU_TILE // 2) * MXU_TILE


def _ffn_sublayer(x, wg_ref, wu_ref, wd_ref, g_ref, b_ref):
    xb = x.astype(BF16)
    acc = None
    for sl in (slice(0, FFN_SPLIT), slice(FFN_SPLIT, D_FF)):
        gate = _dot(xb, wg_ref[:, sl])
        up = _dot(xb, wu_ref[:, sl])
        h = (gate * _sigmoid(gate) * up).astype(BF16)
        part = _dot(h, wd_ref[sl, :])
        acc = part if acc is None else acc + part
    return _layer_norm(ALPHA * x + 0.5 * acc, g_ref[...], b_ref[...])


def _ffn_kernel(x_ref, wg_ref, wu_ref, wd_ref, g_ref, b_ref, o_ref):
    o_ref[...] = _ffn_sublayer(x_ref[...], wg_ref, wu_ref, wd_ref, g_ref, b_ref)


def _ffn_operands(ffn):
    wg, wu, wd, layer, slot, g, b = ffn
    pick = lambda rows, cols: pl.BlockSpec((None, None, rows, cols), lambda *_: (layer, slot, 0, 0),
                                           pipeline_mode=pl.Buffered(1))
    specs = [pick(D_MODEL, D_FF), pick(D_MODEL, D_FF), pick(D_FF, D_MODEL), _full((1, D_MODEL)), _full((1, D_MODEL))]
    return [wg, wu, wd, g.reshape(1, D_MODEL), b.reshape(1, D_MODEL)], specs


def ffn_ln(x, ffn):
    t = x.shape[0]
    tm = _row_tile(t)
    operands, specs = _ffn_operands(ffn)
    return pl.pallas_call(
        _ffn_kernel,
        grid=(t // tm,),
        in_specs=[pl.BlockSpec((tm, D_MODEL), lambda i: (i, 0))] + specs,
        out_specs=pl.BlockSpec((tm, D_MODEL), lambda i: (i, 0)),
        out_shape=jax.ShapeDtypeStruct((t, D_MODEL), F32),
        compiler_params=_cparams(("parallel",)),
        name="ffn_ln",
    )(x, *operands)


def _proj_kernel(x_ref, w_ref, *o_refs, plan):
    xb = x_ref[...].astype(BF16)
    refs = iter(o_refs)
    for lo, hi, outs in plan:
        y = _dot(xb, w_ref[:, lo:hi])
        for _, scale in outs:
            o_ref = next(refs)
            o_ref[...] = (y if scale == 1.0 else y * scale).astype(o_ref.dtype)


def proj(x, w, plan):
    t = x.shape[0]
    tm = _row_tile(t)
    n = w.shape[1]
    flat = [(hi - lo, dt) for lo, hi, outs in plan for dt, _ in outs]
    return pl.pallas_call(
        functools.partial(_proj_kernel, plan=plan),
        grid=(t // tm,),
        in_specs=[pl.BlockSpec((tm, D_MODEL), lambda i: (i, 0)), _full((D_MODEL, n))],
        out_specs=[pl.BlockSpec((tm, width), lambda i: (i, 0)) for width, _ in flat],
        out_shape=[jax.ShapeDtypeStruct((t, width), dt) for width, dt in flat],
        compiler_params=_cparams(("parallel",)),
        name="proj",
    )(x, w)


def _mix0_in_kernel(x_ref, w_ref, u_ref, us_ref, q_ref, k_ref, kb_ref, v_ref, vb_ref, u_scr, *, q_scale):
    xb = x_ref[...].astype(BF16)
    sw = S5_WIDTH
    u = _dot(xb, w_ref[:, 0:sw])
    u_ref[...] = u
    for c in range(sw // LANES):
        u_scr[c] = u[:, c * LANES:(c + 1) * LANES]
    chunks = u.shape[0] // S5_CHUNK
    per_slab = S5_ROW // LANES
    for s in range(S5_CHUNK):
        for half in range(2):
            for c in range(per_slab):
                rows = u_scr[half * per_slab + c, pl.ds(s, chunks, stride=S5_CHUNK), :]
                us_ref[2 * s + half, :, c * LANES:(c + 1) * LANES] = rows.astype(BF16)
    q_ref[...] = (_dot(xb, w_ref[:, sw:2 * sw]) * q_scale).astype(BF16)
    k = _dot(xb, w_ref[:, 2 * sw:3 * sw])
    k_ref[...] = k.reshape(k.shape[0], SB_HEADS, SB_HEAD_DIM)
    kb_ref[...] = k.astype(BF16)
    v = _dot(xb, w_ref[:, 3 * sw:4 * sw])
    v_ref[...] = v.reshape(v.shape[0], SB_HEADS, SB_HEAD_DIM)
    vb_ref[...] = v.astype(BF16)


def mix0_in(x, w, q_scale):
    t = x.shape[0]
    tm = _row_tile(t)
    sw = S5_WIDTH
    tok = pl.BlockSpec((tm, sw), lambda i: (i, 0))
    heads = pl.BlockSpec((tm, SB_HEADS, SB_HEAD_DIM), lambda i: (i, 0, 0))
    return pl.pallas_call(
        functools.partial(_mix0_in_kernel, q_scale=q_scale),
        grid=(t // tm,),
        in_specs=[pl.BlockSpec((tm, D_MODEL), lambda i: (i, 0)), _full((D_MODEL, MIX0_IN))],
        out_specs=[tok, pl.BlockSpec((S5_SLABS, tm // S5_CHUNK, S5_ROW), lambda i: (0, i, 0)),
                   tok, heads, tok, heads, tok],
        out_shape=[jax.ShapeDtypeStruct((t, sw), F32),
                   jax.ShapeDtypeStruct((S5_SLABS, t // S5_CHUNK, S5_ROW), BF16),
                   jax.ShapeDtypeStruct((t, sw), BF16), jax.ShapeDtypeStruct((t, SB_HEADS, SB_HEAD_DIM), F32),
                   jax.ShapeDtypeStruct((t, sw), BF16), jax.ShapeDtypeStruct((t, SB_HEADS, SB_HEAD_DIM), F32),
                   jax.ShapeDtypeStruct((t, sw), BF16)],
        scratch_shapes=[pltpu.VMEM((sw // LANES, tm, LANES), F32)],
        compiler_params=_cparams(("parallel",)),
        name="mix0_in",
    )(x, w)


S5_SLABS = 2 * S5_CHUNK
S5_GH = S5_GROUPS // 2
S5_HALF = S5_GH * S5_STATE
S5_CARRY_LANES = LANES
S5_CARRY_UNROLL = 8


def _s5_tiles_kernel(a_re_ref, a_im_ref, ldt_ref, bt_re_ref, bt_im_ref, cq_re_ref, cq_im_ref,
                     p_ref, qt_ref, k_ref, a16r_ref, a16i_ref):
    s = pl.program_id(1).astype(F32)

    def cpow(m, ar, ai, dt):
        mag = jnp.exp(m * (ar * dt))
        ang = m * (ai * dt)
        return mag * jnp.cos(ang), mag * jnp.sin(ang)

    def zoh(ar, ai, dt):
        abr, abi = cpow(1.0, ar, ai, dt)
        nr, ni, den = abr - 1.0, abi, ar * ar + ai * ai
        return (nr * ar + ni * ai) / den, (ni * ar - nr * ai) / den

    ar, ai, dt = a_re_ref[...], a_im_ref[...], jnp.exp(ldt_ref[...])
    fr, fi = zoh(ar, ai, dt)
    pwr, pwi = cpow(float(S5_CHUNK - 1) - s, ar, ai, dt)
    cr, ci = pwr * fr - pwi * fi, pwr * fi + pwi * fr
    shape = (S5_ROW, S5_HALF)
    same = (jnp.right_shift(lax.broadcasted_iota(jnp.int32, shape, 0), S5_GROUP.bit_length() - 1)
            == jnp.right_shift(lax.broadcasted_iota(jnp.int32, shape, 1), S5_STATE.bit_length() - 1))
    btr, bti = bt_re_ref[...], bt_im_ref[...]
    p_re = jnp.where(same, cr * btr - ci * bti, 0.0).astype(BF16)
    p_im = jnp.where(same, cr * bti + ci * btr, 0.0).astype(BF16)
    p_ref[:, :S5_HALF] = p_re
    p_ref[:, S5_HALF:] = p_im
    qwr, qwi = cpow(s + 1.0, ar, ai, dt)
    cqr, cqi = cq_re_ref[...], cq_im_ref[...]
    qt_ref[:, :S5_HALF] = jnp.where(same, cqr * qwr - cqi * qwi, 0.0).astype(BF16)
    qt_ref[:, S5_HALF:] = jnp.where(same, -(cqr * qwi + cqi * qwr), 0.0).astype(BF16)
    a16r_ref[...], a16i_ref[...] = cpow(float(S5_CHUNK), ar, ai, dt)

    c0 = jnp.concatenate([jnp.where(same, cqr, 0.0), jnp.where(same, -cqi, 0.0)], axis=1).astype(BF16)
    k_ref[...] = _dot_nt(jnp.concatenate([p_re, p_im], axis=1), c0).astype(BF16)


def s5_tiles(a_re, a_im, log_dt, b_re, b_im, c_re, c_im):
    gh, n, pch = S5_GH, S5_STATE, S5_GROUP
    halves = lambda v: v.reshape((2, gh) + v.shape[1:])
    lane = lambda v: halves(v).reshape(2, 1, S5_HALF)
    ldt = jnp.broadcast_to(log_dt[:, None], (S5_GROUPS, n))
    bt = lambda b: halves(jnp.swapaxes(b, 1, 2)).reshape(2, S5_ROW, n)
    wide = lambda v: jnp.tile(v, (1, 1, gh))
    cq = lambda c: jnp.tile(jnp.transpose(halves(c), (0, 2, 1, 3)).reshape(2, pch, S5_HALF), (1, gh, 1))
    half = lambda *shape: pl.BlockSpec((None,) + shape, lambda h, s: (h,) + (0,) * len(shape))
    pt, qt, kt, a16r, a16i = pl.pallas_call(
        _s5_tiles_kernel,
        grid=(2, S5_CHUNK),
        in_specs=[half(1, S5_HALF)] * 3 + [half(S5_ROW, S5_HALF)] * 4,
        out_specs=[pl.BlockSpec((None, S5_ROW, 2 * S5_HALF), lambda h, s: (h, s, 0)),
                   pl.BlockSpec((None, None, S5_ROW, 2 * S5_HALF), lambda h, s: (h, s, 0, 0)),
                   pl.BlockSpec((None, S5_ROW, S5_ROW), lambda h, s: (h, s, 0)),
                   half(1, S5_HALF), half(1, S5_HALF)],
        out_shape=[jax.ShapeDtypeStruct((2, S5_CHUNK * S5_ROW, 2 * S5_HALF), BF16),
                   jax.ShapeDtypeStruct((2, S5_CHUNK, S5_ROW, 2 * S5_HALF), BF16),
                   jax.ShapeDtypeStruct((2, S5_CHUNK * S5_ROW, S5_ROW), BF16),
                   jax.ShapeDtypeStruct((2, 1, S5_HALF), F32), jax.ShapeDtypeStruct((2, 1, S5_HALF), F32)],
        compiler_params=_cparams(("arbitrary", "arbitrary")),
        name="s5_tiles",
    )(lane(a_re), lane(a_im), lane(ldt), wide(bt(b_re)), wide(bt(b_im)), cq(c_re), cq(c_im))
    nstate = S5_GROUPS * S5_STATE
    return kt, pt, qt, a16r.reshape(1, nstate), a16i.reshape(1, nstate)


def _s5_state_kernel(us_ref, p_ref, sre_ref, sim_ref):
    steps = jnp.concatenate([us_ref[s] for s in range(S5_CHUNK)], axis=1)
    part = _dot(steps, p_ref[...])
    for c in range(S5_HALF // LANES):
        sre_ref[c] = part[:, c * LANES:(c + 1) * LANES]
        sim_ref[c] = part[:, S5_HALF + c * LANES:S5_HALF + (c + 1) * LANES]


def _s5_carry_kernel(sre_ref, sim_ref, h0re_ref, h0im_ref, ar_ref, ai_ref,
                     hre_ref, him_ref, lre_ref, lim_ref, *, bn, nj):
    ar, ai = ar_ref[...], ai_ref[...]

    def body(j, carry):
        hr, hi = carry
        rows = pl.ds(j, bn, stride=nj)
        hre_ref[rows, :] = hr
        him_ref[rows, :] = hi
        return ar * hr - ai * hi + sre_ref[rows, :], ar * hi + ai * hr + sim_ref[rows, :]

    hr, hi = lax.fori_loop(0, nj, body, (h0re_ref[...], h0im_ref[...]), unroll=math.gcd(nj, S5_CARRY_UNROLL))
    lre_ref[...] = hr
    lim_ref[...] = hi


def _s5_out_kernel(us_ref, k_ref, hre_ref, him_ref, q_ref, y_ref):
    blocks = range(S5_HALF // LANES)
    hin = jnp.concatenate([hre_ref[c] for c in blocks] + [him_ref[c] for c in blocks], axis=1).astype(BF16)
    for t in range(S5_CHUNK):
        steps = jnp.concatenate([us_ref[s] for s in range(t + 1)], axis=1)
        lags = k_ref[(S5_CHUNK - 1 - t) * S5_ROW:, :]
        y_ref[t] = _dot(steps, lags) + _dot_nt(hin, q_ref[t])


def s5_mix(us, h0re, h0im, tiles, bn, nj):
    kt, pt, qq, ar, ai = tiles
    r = bn * nj
    nstate = S5_GROUPS * S5_STATE
    us4 = us.reshape(S5_CHUNK, 2, r, S5_ROW)
    half_once = lambda *shape: pl.BlockSpec((None,) + shape, lambda h, i: (h,) + (0,) * len(shape),
                                            pipeline_mode=pl.Buffered(1))
    rows = min(r, 512)
    lanes = S5_CARRY_LANES
    sre, sim = pl.pallas_call(
        _s5_state_kernel,
        grid=(2, r // rows),
        in_specs=[pl.BlockSpec((S5_CHUNK, None, rows, S5_ROW), lambda h, i: (0, h, i, 0)),
                  half_once(S5_CHUNK * S5_ROW, 2 * S5_HALF)],
        out_specs=[pl.BlockSpec((S5_HALF // lanes, rows, lanes), lambda h, i: (h, i, 0))] * 2,
        out_shape=[jax.ShapeDtypeStruct((nstate // lanes, r, lanes), F32)] * 2,
        compiler_params=_cparams(("arbitrary", "arbitrary")),
        name="s5_state",
    )(us4, pt)
    col = lambda n: pl.BlockSpec((n, lanes), lambda i: (0, i))
    blk = pl.BlockSpec((None, r, lanes), lambda i: (i, 0, 0))
    hre, him, lre, lim = pl.pallas_call(
        functools.partial(_s5_carry_kernel, bn=bn, nj=nj),
        grid=(nstate // lanes,),
        in_specs=[blk, blk, col(bn), col(bn), col(1), col(1)],
        out_specs=[blk, blk, col(bn), col(bn)],
        out_shape=[jax.ShapeDtypeStruct((nstate // lanes, r, lanes), F32)] * 2
                  + [jax.ShapeDtypeStruct((bn, nstate), F32)] * 2,
        compiler_params=_cparams(("parallel",)),
        name="s5_carry",
    )(sre, sim, h0re, h0im, ar, ai)
    rows = min(r, 256)
    slabs = pl.BlockSpec((S5_CHUNK, None, rows, S5_ROW), lambda h, i: (0, h, i, 0))
    y = pl.pallas_call(
        _s5_out_kernel,
        grid=(2, r // rows),
        in_specs=[slabs, half_once(S5_CHUNK * S5_ROW, S5_ROW),
                  pl.BlockSpec((S5_HALF // lanes, rows, lanes), lambda h, i: (h, i, 0)),
                  pl.BlockSpec((S5_HALF // lanes, rows, lanes), lambda h, i: (h, i, 0)),
                  half_once(S5_CHUNK, S5_ROW, 2 * S5_HALF)],
        out_specs=slabs,
        out_shape=jax.ShapeDtypeStruct((S5_CHUNK, 2, r, S5_ROW), F32),
        compiler_params=_cparams(("arbitrary", "arbitrary")),
        name="s5_out",
    )(us4, kt, hre, him, qq)
    return y.reshape(S5_SLABS, r, S5_ROW), lre, lim


SB_TK = 256
LOG2E = math.log2(math.e)
SB_SKIP_LOG2 = 152.0
SB_SUBS = 8


def _sb_chains(tasks, ones, keys_on_lanes=False):
    zs = [_dot(q, k) if keys_on_lanes else _dot_nt(q, k) for q, k, _, _, _ in tasks]
    tails = [jnp.log2(1.0 + jnp.exp2(-jnp.abs(z))) for z in zs]
    sps = [jnp.maximum(z, 0.0) + t for z, t in zip(zs, tails)]
    sps = [sp if task[3] is None else jnp.where(task[3], sp, 0.0) for sp, task in zip(sps, tasks)]
    mass = [jnp.sum(sp, axis=-1, keepdims=True) for sp in sps]
    suffixes = [_dot(sp.astype(BF16), ones) for sp in sps]
    outs = []
    for i, (_, _, v, mask, carry) in enumerate(tasks):
        logw = jnp.minimum(zs[i], 0.0) - tails[i] - suffixes[i]
        if carry is not None:
            logw = logw - (mass[carry] if isinstance(carry, int) else carry)
        w = jnp.exp2(logw)
        if mask is not None:
            w = jnp.where(mask, w, 0.0)
        wb = w.astype(BF16)
        outs.append(_dot_nt(wb, v) if keys_on_lanes else _dot(wb, v))
    return outs, mass


def _sb_block(q2, k, v, ones, carry, mask, keys_on_lanes=False):
    half = q2.shape[0] // 2
    halves = (slice(0, half), slice(half, 2 * half))
    outs, mass = _sb_chains([(q2[h], k, v, None if mask is None else mask[h],
                              None if carry is None else carry[h]) for h in halves], ones, keys_on_lanes)
    return jnp.concatenate(outs, axis=0), jnp.concatenate(mass, axis=0)


def _sb_walk(q2, load_kv, ones, blocks, acc, carry, keys_on_lanes=False):
    def cond(s):
        return jnp.logical_and(s[0] < blocks, s[1] < SB_SKIP_LOG2)

    def body(s):
        i, _, acc, carry = s
        rows = pl.ds(pl.multiple_of((blocks - 1 - i) * SB_TK, SB_TK), SB_TK)
        k, v = load_kv(rows)
        d, m = _sb_block(q2, k, v, ones, carry, None, keys_on_lanes)
        carry = carry + m
        return i + 1, jnp.min(carry), acc + d, carry

    return lax.while_loop(cond, body, (jnp.int32(0), jnp.min(carry), acc, carry))[2]


def _sb_prompt_kernel(q_ref, kd_ref, vd_ref, kp_ref, vp_ref, ones_ref, o_ref):
    i = pl.program_id(2)
    t, n = SB_TK, SB_SUBS
    lanes = 2 * SB_HEAD_DIM
    q = q_ref[...]
    lane_lo = lax.broadcasted_iota(jnp.int32, (n * t, lanes), 1) < SB_HEAD_DIM
    zero = jnp.zeros_like(q)
    heads = (jnp.where(lane_lo, q, zero), jnp.where(lane_lo, zero, q))
    row = lax.broadcasted_iota(jnp.int32, (t, t), 0)
    col = lax.broadcasted_iota(jnp.int32, (t, t), 1)
    earlier = col < row
    ones = ones_ref[...]
    kd = [kd_ref[s * t:(s + 1) * t, :] for s in range(n)]
    vd = [vd_ref[s * t:(s + 1) * t, :] for s in range(n)]
    qs = [[h[s * t:(s + 1) * t, :] for h in heads] for s in range(n)]

    def front(with_past):
        tasks = [(qs[s][h], kd[s], vd[s], earlier, None) for s in range(n) for h in range(2)]
        past_of = {}
        for s in range(1, n):
            past_of[s] = len(tasks)
            tasks += [(qs[s][h], kd[s - 1], vd[s - 1], None, 2 * s + h) for h in range(2)]
        if with_past:
            rows = pl.ds(pl.multiple_of((n * i - 1) * t, t), t)
            past_of[0] = len(tasks)
            tasks += [(qs[0][h], kp_ref[rows, :], vp_ref[rows, :], None, h) for h in range(2)]
        outs, mass = _sb_chains(tasks, ones)
        result = []
        for s in range(n):
            parts = [(outs[2 * s + h], mass[2 * s + h]) for h in range(2)]
            if s in past_of:
                parts = [(o + outs[past_of[s] + h], m + mass[past_of[s] + h]) for h, (o, m) in enumerate(parts)]
            result += [jnp.concatenate([parts[0][0], parts[1][0]], axis=0),
                       jnp.concatenate([parts[0][1], parts[1][1]], axis=0)]
        lightest = [jnp.minimum(jnp.min(result[2 * s + 1][:t]), jnp.min(result[2 * s + 1][t:])) for s in range(n)]
        return tuple(result) + tuple(lightest)

    state = lax.cond(i > 0, lambda: front(True), lambda: front(False))
    accs, masses, lightest = state[0:2 * n:2], state[1:2 * n:2], state[2 * n:]
    older = [jnp.maximum(n * i + s - 1, 0) for s in range(n)]
    load_kv = lambda rows: (kp_ref[rows, :], vp_ref[rows, :])
    first = lax.broadcasted_iota(jnp.int32, (t, lanes), 1) < SB_HEAD_DIM

    def emit(accs):
        for s in range(n):
            o_ref[s * t:(s + 1) * t, :] = jnp.where(first, accs[s][:t], accs[s][t:])

    def walk_all():
        emit([_sb_walk(jnp.concatenate(qs[s], axis=0), load_kv, ones, older[s], accs[s], masses[s])
              for s in range(n)])

    unfinished = functools.reduce(jnp.logical_or, [jnp.logical_and(lightest[s] < SB_SKIP_LOG2, older[s] > 0)
                                                   for s in range(n)])
    lax.cond(unfinished, walk_all, lambda: emit(accs))


def _sb_cached_kernel(q_ref, kd_ref, vd_ref, kp_ref, vp_ref, ud_ref, up_ref, o_ref, low_ref, *, tq, blocks):
    lanes = 2 * SB_HEAD_DIM
    pairs = SB_HEADS // 2
    lane_lo = lax.broadcasted_iota(jnp.int32, (tq, lanes), 1) < SB_HEAD_DIM
    row = lax.broadcasted_iota(jnp.int32, (tq, tq), 0)
    col = lax.broadcasted_iota(jnp.int32, (tq, tq), 1)
    earlier = col < row
    cols = [slice(p * lanes, (p + 1) * lanes) for p in range(pairs)]

    def load_kv(keys, pair):
        flat = lambda ref: ref[2 * pair:2 * pair + 2, :, keys].reshape(lanes, SB_TK).astype(BF16)
        return flat(kp_ref), flat(vp_ref)

    qs = []
    for p in range(pairs):
        q = q_ref[:, cols[p]]
        zero = jnp.zeros_like(q)
        qs += [jnp.where(lane_lo, q, zero), jnp.where(lane_lo, zero, q)]
    new = [(qs[2 * p + h], kd_ref[:, cols[p]], vd_ref[:, cols[p]], earlier, None)
           for p in range(pairs) for h in range(2)]
    outs_new, mass_new = _sb_chains(new, ud_ref[...])
    newest = slice((blocks - 1) * SB_TK, blocks * SB_TK)
    cached = []
    for p in range(pairs):
        k, v = load_kv(newest, p)
        cached += [(qs[2 * p + h], k, v, None, mass_new[2 * p + h]) for h in range(2)]
    outs_old, mass_old = _sb_chains(cached, up_ref[...], keys_on_lanes=True)
    accs = [jnp.concatenate([outs_new[2 * p + h] + outs_old[2 * p + h] for h in range(2)], axis=0)
            for p in range(pairs)]
    masses = [jnp.concatenate([mass_new[2 * p + h] + mass_old[2 * p + h] for h in range(2)], axis=0)
              for p in range(pairs)]

    def emit(accs):
        for p in range(pairs):
            o_ref[:, cols[p]] = jnp.where(lane_lo, accs[p][:tq], accs[p][tq:])

    def walk_all():
        emit([_sb_walk(jnp.concatenate(qs[2 * p:2 * p + 2], axis=0), functools.partial(load_kv, pair=p),
                       up_ref[...], blocks - 1, accs[p], masses[p], keys_on_lanes=True) for p in range(pairs)])

    lightest = functools.reduce(jnp.minimum, [jnp.min(m) for m in masses])
    low_ref[...] = jnp.broadcast_to(lightest, low_ref.shape)
    lax.cond(jnp.logical_and(lightest < SB_SKIP_LOG2, blocks > 1), walk_all, lambda: emit(accs))


def _strict_lower_ones(n):
    r = lax.broadcasted_iota(jnp.int32, (n, n), 0)
    c = lax.broadcasted_iota(jnp.int32, (n, n), 1)
    return (r > c).astype(BF16)


def stick_breaking_fresh(q, k, v, bn, length):
    lanes = 2 * SB_HEAD_DIM
    tq = SB_SUBS * SB_TK
    assert length % tq == 0
    blk = pl.BlockSpec((None, tq, lanes), lambda b, h, i: (b, i, h))
    seq = pl.BlockSpec((None, length, lanes), lambda b, h, i: (b, 0, h))
    return pl.pallas_call(
        _sb_prompt_kernel,
        grid=(bn, SB_HEADS // 2, length // tq),
        in_specs=[blk, blk, blk, seq, seq, _full((SB_TK, SB_TK))],
        out_specs=blk,
        out_shape=jax.ShapeDtypeStruct((bn, length, SB_WIDTH), F32),
        compiler_params=_cparams(("parallel", "parallel", "arbitrary")),
        name="stick_breaking",
    )(q, k, v, k, v, _strict_lower_ones(SB_TK))


def stick_breaking_cached(q, k_new, v_new, k_past, v_past, bn, lq):
    lp = k_past.shape[3]
    assert lp % SB_TK == 0 and lp >= SB_TK
    blk = pl.BlockSpec((None, lq, SB_WIDTH), lambda b: (b, 0, 0))
    low = pl.BlockSpec((None, 1, LANES), lambda b: (b, 0, 0))

    def run(blocks):
        first = lp // SB_TK - blocks
        past = pl.BlockSpec((None, SB_HEADS, SB_HEAD_DIM, blocks * SB_TK), lambda b: (b, 0, 0, first // blocks))
        return pl.pallas_call(
            functools.partial(_sb_cached_kernel, tq=lq, blocks=blocks),
            grid=(bn,),
            in_specs=[blk, blk, blk, past, past, _full((lq, lq)), _full((SB_TK, SB_TK))],
            out_specs=[blk, low],
            out_shape=[jax.ShapeDtypeStruct((bn, lq, SB_WIDTH), F32), jax.ShapeDtypeStruct((bn, 1, LANES), F32)],
            compiler_params=_cparams(("parallel",)),
            name="stick_breaking_cached",
        )(q, k_new, v_new, k_past, v_past, _strict_lower_ones(lq), _strict_lower_ones(SB_TK))

    out, lightest = run(1)
    if lp == SB_TK:
        return out
    return lax.cond(jnp.min(lightest) >= SB_SKIP_LOG2, lambda: out, lambda: run(lp // SB_TK)[0])


def _gelu(x):
    return 0.5 * x * (1.0 + jnp.tanh(math.sqrt(2.0 / math.pi) * (x + 0.044715 * (x * x * x))))


def _mix0_out_kernel(x_ref, ys_ref, u_ref, sb_ref, d_ref, wglu_ref, bglu_ref, wo1_ref, wo2_ref, g_ref, b_ref,
                     wg_ref, wu_ref, wd_ref, g2_ref, b2_ref, o_ref, y_scr):
    x = x_ref[...]
    chunks = y_scr.shape[1] // S5_CHUNK
    per_slab = S5_ROW // LANES
    for s in range(S5_CHUNK):
        for half in range(2):
            for c in range(per_slab):
                y_scr[half * per_slab + c, pl.ds(s, chunks, stride=S5_CHUNK), :] = (
                    ys_ref[2 * s + half, :, c * LANES:(c + 1) * LANES])
    ys = jnp.concatenate([y_scr[c] for c in range(S5_WIDTH // LANES)], axis=1)
    gl = _gelu(ys + d_ref[...] * u_ref[...])
    gate = _sigmoid(_dot(gl.astype(BF16), wglu_ref[...]) + bglu_ref[...])
    m = _dot((gl * gate).astype(BF16), wo1_ref[...]) + _dot(sb_ref[...].astype(BF16), wo2_ref[...])
    x = _layer_norm(ALPHA * x + m, g_ref[...], b_ref[...])
    o_ref[...] = _ffn_sublayer(x, wg_ref, wu_ref, wd_ref, g2_ref, b2_ref)


def mix0_out_ffn(x, ys, u, sb, d, wglu, bglu, wo1, wo2, g, b, ffn):
    t = x.shape[0]
    tm = _row_tile(t)
    rowblk = lambda n: pl.BlockSpec((tm, n), lambda i: (i, 0))
    ffn_operands, ffn_specs = _ffn_operands(ffn)
    return pl.pallas_call(
        _mix0_out_kernel,
        grid=(t // tm,),
        in_specs=[rowblk(D_MODEL),
                  pl.BlockSpec((S5_SLABS, tm // S5_CHUNK, S5_ROW), lambda i: (0, i, 0)),
                  rowblk(S5_WIDTH), rowblk(SB_WIDTH),
                  _full((1, S5_WIDTH)), _full((S5_WIDTH, S5_WIDTH)), _full((1, S5_WIDTH)),
                  _full((S5_WIDTH, D_MODEL)), _full((SB_WIDTH, D_MODEL)),
                  _full((1, D_MODEL)), _full((1, D_MODEL))] + ffn_specs,
        out_specs=rowblk(D_MODEL),
        out_shape=jax.ShapeDtypeStruct((t, D_MODEL), F32),
        scratch_shapes=[pltpu.VMEM((S5_WIDTH // LANES, tm, LANES), F32)],
        compiler_params=_cparams(("parallel",)),
        name="mix0_out_ffn",
    )(x, ys, u, sb, d.reshape(1, -1), wglu, bglu.reshape(1, -1), wo1, wo2, g.reshape(1, -1), b.reshape(1, -1),
      *ffn_operands)


def _ssd_kernel(xbc_ref, z_ref, dt_ref, dtt_ref, conv0_ref, h0_ref, cw_ref, cb_ref, dtb_ref, dtbt_ref,
                alog_ref, alogt_ref, dsk_ref, ng_ref, tril_ref, triu_ref,
                y_ref, hl_ref, ext_ref, h_ref, *, lc, nc):
    c = pl.program_id(1)

    @pl.when(c == 0)
    def _():
        ext_ref[0:CONV_PAD, :] = conv0_ref[...]
        h_ref[...] = h0_ref[...]

    ext_ref[CONV_PAD:CONV_PAD + lc, :] = xbc_ref[...]
    ext = ext_ref[...]
    conv = cb_ref[...] + ext[CONV_PAD:, :] * cw_ref[SSD_CONV - 1:SSD_CONV, :]
    for back in range(1, SSD_CONV):
        tap = SSD_CONV - 1 - back
        conv = conv + pltpu.roll(ext, back, 0)[CONV_PAD:, :] * cw_ref[tap:tap + 1, :]
    ext_ref[0:CONV_PAD, :] = ext[lc:lc + CONV_PAD, :]
    act = conv * _sigmoid(conv)

    dt = _softplus(dt_ref[...] + dtb_ref[...])
    dtt = _softplus(dtt_ref[...] + dtbt_ref[...])
    cs = _sum_left(tril_ref[...], dt * (-LOG2E * jnp.exp(alog_ref[...])))
    cst = _sum_right(dtt * (-LOG2E * jnp.exp(alogt_ref[...])), triu_ref[...])
    cs_last = cs[lc - 1:lc, :]

    row = lax.broadcasted_iota(jnp.int32, (lc, lc), 0)
    col = lax.broadcasted_iota(jnp.int32, (lc, lc), 1)
    causal = col <= row
    lanes = 2 * SSD_HEAD_DIM
    lane_lo = lax.broadcasted_iota(jnp.int32, (lc, lanes), 1) < SSD_HEAD_DIM
    pairs_per_group = SSD_PAIRS // SSD_GROUPS

    for g in range(SSD_GROUPS):
        bm = act[:, SSD_INNER + g * SSD_STATE:SSD_INNER + (g + 1) * SSD_STATE].astype(BF16)
        cm = act[:, SSD_INNER + SSD_GN + g * SSD_STATE:SSD_INNER + SSD_GN + (g + 1) * SSD_STATE].astype(BF16)
        cb = _dot_nt(cm, bm)
        gated, sq = [], jnp.zeros((lc, lanes), F32)
        for k in range(g * pairs_per_group, (g + 1) * pairs_per_group):
            sl = slice(k * lanes, (k + 1) * lanes)
            xp = act[:, sl]
            heads = (2 * k, 2 * k + 1)
            pair = lambda v: jnp.where(lane_lo[:v[0].shape[0]], v[0], v[1])
            xdt = xp * pair([dt[:, h:h + 1] for h in heads])
            xdtb = xdt.astype(BF16)
            csb = [jnp.broadcast_to(cs[:, h:h + 1], (lc, lanes)) for h in heads]
            ys = []
            for h, csh in zip(heads, csb):
                seg = csh[:, :lc] - cst[h:h + 1, :]
                w = cb * jnp.exp2(jnp.where(causal, seg, -jnp.inf))
                ys.append(_dot(w.astype(BF16), xdtb))
            y = pair(ys)
            cs_pair = pair(csb)
            last_pair = pair([cs_last[:, h:h + 1] for h in heads])
            ht = h_ref[k]
            y = y + _dot(cm, ht.astype(BF16)) * jnp.exp2(cs_pair)
            xw = (xdt * jnp.exp2(last_pair - cs_pair)).astype(BF16)
            h_ref[k] = ht * jnp.exp2(last_pair) + _dot_tn(bm, xw)
            y = y + dsk_ref[:, sl] * xp
            zz = z_ref[:, sl]
            yg = y * (zz * _sigmoid(zz))
            sq = sq + yg * yg
            gated.append(yg)
        mean_sq = jnp.sum(sq, axis=-1, keepdims=True) * (1.0 / (pairs_per_group * lanes))
        scale = lax.rsqrt(mean_sq + RMS_EPS)
        for k, yg in zip(range(g * pairs_per_group, (g + 1) * pairs_per_group), gated):
            sl = slice(k * lanes, (k + 1) * lanes)
            y_ref[:, sl] = yg * scale * ng_ref[:, sl]

    @pl.when(c == nc - 1)
    def _():
        hl_ref[...] = h_ref[...]


def ssd_mix(xbc, z, dt_raw, conv0, h0, conv_w, conv_b, dt_bias, a_log, d_skip, norm_g, bn, length, lc):
    nc = length // lc
    t = bn * length
    lanes = 2 * SSD_HEAD_DIM
    dtt = jnp.swapaxes(dt_raw.reshape(bn * nc, lc, SSD_HEADS), 1, 2)
    conv0p = jnp.pad(conv0, ((0, 0), (CONV_PAD - (SSD_CONV - 1), 0), (0, 0)))
    a_log = a_log.astype(F32)
    r = lax.broadcasted_iota(jnp.int32, (lc, lc), 0)
    cc = lax.broadcasted_iota(jnp.int32, (lc, lc), 1)
    tril = (cc <= r).astype(BF16)
    triu = (r <= cc).astype(BF16)
    tok = lambda n: pl.BlockSpec((lc, n), lambda b, c: (b * nc + c, 0))
    y, hl = pl.pallas_call(
        functools.partial(_ssd_kernel, lc=lc, nc=nc),
        grid=(bn, nc),
        in_specs=[tok(SSD_CONV_DIM), tok(SSD_INNER), tok(SSD_HEADS),
                  pl.BlockSpec((None, SSD_HEADS, lc), lambda b, c: (b * nc + c, 0, 0)),
                  pl.BlockSpec((None, CONV_PAD, SSD_CONV_DIM), lambda b, c: (b, 0, 0)),
                  pl.BlockSpec((None, SSD_PAIRS, lanes, SSD_STATE), lambda b, c: (b, 0, 0, 0)),
                  _full((SSD_CONV, SSD_CONV_DIM)), _full((1, SSD_CONV_DIM)),
                  _full((1, SSD_HEADS)), _full((SSD_HEADS, 1)), _full((1, SSD_HEADS)), _full((SSD_HEADS, 1)),
                  _full((1, SSD_INNER)), _full((1, SSD_INNER)), _full((lc, lc)), _full((lc, lc))],
        out_specs=[tok(SSD_INNER),
                   pl.BlockSpec((None, SSD_PAIRS, lanes, SSD_STATE), lambda b, c: (b, 0, 0, 0))],
        out_shape=[jax.ShapeDtypeStruct((t, SSD_INNER), F32),
                   jax.ShapeDtypeStruct((bn, SSD_PAIRS, lanes, SSD_STATE), F32)],
        scratch_shapes=[pltpu.VMEM((CONV_PAD + lc, SSD_CONV_DIM), F32),
                        pltpu.VMEM((SSD_PAIRS, lanes, SSD_STATE), F32)],
        compiler_params=_cparams(("parallel", "arbitrary")),
        name="ssd_mix",
    )(xbc, z, dt_raw, dtt, conv0p, jnp.swapaxes(h0.reshape(bn, SSD_PAIRS, lanes, SSD_STATE), 2, 3),
      conv_w, conv_b.reshape(1, -1), dt_bias.reshape(1, -1), dt_bias.reshape(-1, 1),
      a_log.reshape(1, -1), a_log.reshape(-1, 1), jnp.repeat(d_skip, SSD_HEAD_DIM).reshape(1, -1),
      norm_g.reshape(1, -1), tril, triu)
    return y, jnp.swapaxes(hl, 2, 3).reshape(bn, SSD_HEADS, SSD_HEAD_DIM, SSD_STATE)


def _out_ffn_kernel(x_ref, a_ref, w_ref, g_ref, b_ref, wg_ref, wu_ref, wd_ref, g2_ref, b2_ref, o_ref):
    m = _dot(a_ref[...].astype(BF16), w_ref[...])
    x = _layer_norm(ALPHA * x_ref[...] + m, g_ref[...], b_ref[...])
    o_ref[...] = _ffn_sublayer(x, wg_ref, wu_ref, wd_ref, g2_ref, b2_ref)


def out_ffn(x, a, w, g, b, ffn):
    t = x.shape[0]
    tm = _row_tile(t)
    k = a.shape[1]
    ffn_operands, ffn_specs = _ffn_operands(ffn)
    return pl.pallas_call(
        _out_ffn_kernel,
        grid=(t // tm,),
        in_specs=[pl.BlockSpec((tm, D_MODEL), lambda i: (i, 0)), pl.BlockSpec((tm, k), lambda i: (i, 0)),
                  _full((k, D_MODEL)), _full((1, D_MODEL)), _full((1, D_MODEL))] + ffn_specs,
        out_specs=pl.BlockSpec((tm, D_MODEL), lambda i: (i, 0)),
        out_shape=jax.ShapeDtypeStruct((t, D_MODEL), F32),
        compiler_params=_cparams(("parallel",)),
        name="out_ffn",
    )(x, a, w, g.reshape(1, -1), b.reshape(1, -1), *ffn_operands)


def _trunk(x, bn, length, states, w):
    t = bn * length
    x = x.reshape(t, D_MODEL)
    ffn = lambda l, s: (w["wg"], w["wu"], w["wd"], l, s, w["ln_g"][l, 2 * s], w["ln_b"][l, 2 * s])
    x = ffn_ln(x, ffn(0, 0))

    u, u_slabs, q_bf, k, k_bf, v, v_bf = mix0_in(x, w["mix0_in"], SB_HEAD_DIM ** -0.5 * LOG2E)
    nj = length // S5_CHUNK
    nstate = S5_GROUPS * S5_STATE
    if states is None:
        h0re = h0im = jnp.zeros((bn, nstate), F32)
    else:
        h0re = states[0][0].astype(F32).reshape(bn, nstate)
        h0im = states[1][0].astype(F32).reshape(bn, nstate)
    y_s5, h_re, h_im = s5_mix(u_slabs, h0re, h0im, w["s5_tiles"], bn, nj)
    s5_re = h_re.reshape(1, bn, S5_GROUPS, S5_STATE)
    s5_im = h_im.reshape(1, bn, S5_GROUPS, S5_STATE)

    seq = lambda a: a.reshape(bn, length, SB_WIDTH)
    if states is None:
        sb = stick_breaking_fresh(seq(q_bf), seq(k_bf), seq(v_bf), bn, length)
    else:
        cache = lambda a: jnp.transpose(a[0], (0, 2, 3, 1))
        sb = stick_breaking_cached(seq(q_bf), seq(k_bf), seq(v_bf), cache(states[2]), cache(states[3]), bn, length)
    x = mix0_out_ffn(x, y_s5, u, sb.reshape(t, SB_WIDTH), w["s5_d"], w["s5_wglu"], w["s5_bglu"],
                     w["mix0_out"][:S5_WIDTH], w["mix0_out"][S5_WIDTH:], w["ln_g"][0, 1], w["ln_b"][0, 1],
                     ffn(0, 1))

    x = ffn_ln(x, ffn(1, 0))
    one = ((F32, 1.0),)
    z, xbc, dt_raw = proj(x, w["ssd_in"], (
        (0, SSD_INNER, one), (SSD_INNER, SSD_INNER + SSD_CONV_DIM, one),
        (SSD_INNER + SSD_CONV_DIM, SSD_INNER + SSD_CONV_DIM + SSD_HEADS, one)))
    if states is None:
        conv0 = jnp.zeros((bn, SSD_CONV - 1, SSD_CONV_DIM), F32)
        hs0 = jnp.zeros((bn, SSD_HEADS, SSD_HEAD_DIM, SSD_STATE), F32)
    else:
        conv0 = states[5][0].astype(F32)
        hs0 = states[4][0].astype(F32)
    lc = min(128, length)
    yg, h_ssd = ssd_mix(xbc, z, dt_raw, conv0, hs0, w["ssd_conv_w"], w["ssd_conv_b"], w["ssd_dt_bias"],
                        w["ssd_a_log"], w["ssd_d"], w["ssd_norm_g"], bn, length, lc)
    ext = jnp.concatenate([conv0, xbc.reshape(bn, length, SSD_CONV_DIM)[:, length - (SSD_CONV - 1):]], axis=1)
    new_conv = ext[:, ext.shape[1] - (SSD_CONV - 1):]
    x = out_ffn(x, yg, w["ssd_out"], w["ln_g"][1, 1], w["ln_b"][1, 1], ffn(1, 1))

    return (x.reshape(bn, length, D_MODEL), s5_re, s5_im,
            k.reshape(1, bn, length, SB_HEADS, SB_HEAD_DIM), v.reshape(1, bn, length, SB_HEADS, SB_HEAD_DIM),
            h_ssd[None], new_conv[None])


def kernel(x_prompt, x_sample, state_s5_re, state_s5_im, cache_sb_k, cache_sb_v, state_ssd, state_conv, ln_g, ln_b, ffn_w_gate, ffn_w_up, ffn_w_down, mix0_w_in, s5_a_re, s5_a_im, s5_log_dt, s5_b_re, s5_b_im, s5_c_re, s5_c_im, s5_d, s5_w_glu, s5_b_glu, mix0_w_out, ssd_w_in, ssd_conv_w, ssd_conv_b, ssd_dt_bias, ssd_a_log, ssd_d, ssd_norm_g, ssd_w_out):
    bf = lambda a: a.astype(BF16)
    w = {
        "wg": bf(ffn_w_gate), "wu": bf(ffn_w_up), "wd": bf(ffn_w_down),
        "ln_g": ln_g, "ln_b": ln_b,
        "mix0_in": bf(mix0_w_in[0]),
        "s5_tiles": s5_tiles(s5_a_re[0], s5_a_im[0], s5_log_dt[0], s5_b_re[0], s5_b_im[0], s5_c_re[0], s5_c_im[0]),
        "s5_d": s5_d[0], "s5_wglu": bf(s5_w_glu[0]), "s5_bglu": s5_b_glu[0], "mix0_out": bf(mix0_w_out[0]),
        "ssd_in": bf(ssd_w_in[0]), "ssd_conv_w": ssd_conv_w[0], "ssd_conv_b": ssd_conv_b[0],
        "ssd_dt_bias": ssd_dt_bias[0], "ssd_a_log": ssd_a_log[0], "ssd_d": ssd_d[0],
        "ssd_norm_g": ssd_norm_g[0], "ssd_out": bf(ssd_w_out[0]),
    }
    bp, lp = x_prompt.shape[0], x_prompt.shape[1]
    bs, ls = x_sample.shape[0], x_sample.shape[1]
    yp, s5rp, s5ip, kp, vp, ssdp, convp = _trunk(x_prompt, bp, lp, None, w)
    ys, s5rs, s5is, ks, vs, ssds, convs = _trunk(
        x_sample, bs, ls, (state_s5_re, state_s5_im, cache_sb_k, cache_sb_v, state_ssd, state_conv), w)
    return (yp, ys, s5rp, s5ip, kp, vp, ssdp, convp, s5rs, s5is, ks, vs, ssds, convs)
```
